```python
import math
import jax, jax.numpy as jnp
from jax import lax
import numpy as np

D_MODEL = 2048
BATCH = 4
SEQ = 2048
DEPTH = 1
DEC_BATCH = 32
DEC_SEQ = 4
PAST_LEN = 8192
PAGE_SIZE = 128

SSM_WIDTH = D_MODEL // 2
SSM_GROUP = 16
SSM_GROUPS = SSM_WIDTH // SSM_GROUP
SSM_STATE = 64
DT_MIN = 0.001
DT_MAX = 0.1
N_HEADS = 8
HEAD_DIM = 128
N_KV_HEADS = 2
N_REP = N_HEADS // N_KV_HEADS
ATTN_WIDTH = N_HEADS * HEAD_DIM
KV_WIDTH = N_KV_HEADS * HEAD_DIM
ROT_DIM = HEAD_DIM // 4
ROPE_THETA = 500000.0
IDX_HEADS = 16
IDX_DIM = 64
IDX_ROT_DIM = IDX_DIM // 4
TOPK_MAX = 256
Q_BLOCK = 128
D_FF = 5632
FFN_RES = 0.5
EPS = 1e-6

IN_SIZES = [SSM_WIDTH, ATTN_WIDTH, KV_WIDTH, KV_WIDTH, IDX_HEADS * IDX_DIM, IDX_DIM, IDX_HEADS, D_MODEL, D_MODEL]
IN_WIDTH = sum(IN_SIZES)
IN_SPLITS = [sum(IN_SIZES[: i + 1]) for i in range(len(IN_SIZES) - 1)]

kernel_name = "hybrid_s5_dsa_macaron_step"


def _rms_norm(x, g):
    xf = x.astype(jnp.float32)
    y = xf * lax.rsqrt(jnp.mean(xf * xf, axis=-1, keepdims=True) + EPS)
    return (y * g.astype(jnp.float32)).astype(x.dtype)


def _swiglu(x, w_gate, w_up, w_down):
    return (jax.nn.silu(x @ w_gate) * (x @ w_up)) @ w_down


def _rope_partial(x, pos, rot_dim):
    half = rot_dim // 2
    freqs = ROPE_THETA ** (-jnp.arange(half, dtype=jnp.float32) * 2.0 / rot_dim)
    ang = pos.astype(jnp.float32)[:, None] * freqs[None, :]
    cos = jnp.cos(ang)[None, :, None, :]
    sin = jnp.sin(ang)[None, :, None, :]
    xr = x[..., :rot_dim].astype(jnp.float32)
    x1, x2 = xr[..., :half], xr[..., half:]
    rot = jnp.concatenate([x1 * cos - x2 * sin, x2 * cos + x1 * sin], axis=-1).astype(x.dtype)
    return jnp.concatenate([rot, x[..., rot_dim:]], axis=-1)


def _take_rows(t, idx):
    return jax.vmap(lambda tb, ib: tb[ib])(t, idx)


def _ssm_discretize(lam_re, lam_im, b_re, b_im, log_dt):
    f32 = jnp.float32
    lam = lax.complex(lam_re.astype(f32), lam_im.astype(f32))
    dt = jnp.exp(log_dt.astype(f32))[:, None]
    lam_bar = jnp.exp(lam * dt)
    b = lax.complex(b_re.astype(f32), b_im.astype(f32))
    b_bar = ((lam_bar - 1.0) / lam)[..., None] * b
    return lam_bar, b_bar


def _ssm_scan(u, h0, lam_bar, b_bar, c, d):
    uf = u.astype(jnp.float32)
    bu = jnp.einsum('gph,bsgh->bsgp', b_bar, uf.astype(jnp.complex64))
    bu = bu.at[:, 0].add(lam_bar[None] * h0)
    a = jnp.broadcast_to(lam_bar, bu.shape)

    def combine(e1, e2):
        a1, b1 = e1
        a2, b2 = e2
        return a1 * a2, a2 * b1 + b2

    _, h = lax.associative_scan(combine, (a, bu), axis=1)
    y = jnp.real(jnp.einsum('ghp,bsgp->bsgh', c, h)) + d.astype(jnp.float32) * uf
    return y.astype(u.dtype), h[:, -1]


def _indexer_select(qi, wi, ki, q_pos, top_k):
    s = jnp.einsum('bqhd,bld->bqhl', qi, ki).astype(jnp.float32) * (IDX_DIM ** -0.5)
    s = jnp.einsum('bqhl,bqh->bql', jax.nn.relu(s), wi.astype(jnp.float32))
    k_pos = jnp.arange(ki.shape[1])
    allowed = k_pos[None, None, :] <= q_pos[None, :, None]
    s = jnp.where(allowed, s, -jnp.inf)
    _, sel = lax.top_k(s, top_k)
    valid = sel <= q_pos[None, :, None]
    return sel, valid


def _sparse_attend(q, k_sel, v_sel, valid):
    b_, q_ = q.shape[:2]
    qg = q.reshape(b_, q_, N_KV_HEADS, N_REP, HEAD_DIM)
    logits = jnp.einsum('bqgrd,bqkgd->bqgrk', qg, k_sel).astype(jnp.float32) * (HEAD_DIM ** -0.5)
    logits = jnp.where(valid[:, :, None, None, :], logits, -jnp.inf)
    prob = jax.nn.softmax(logits, axis=-1).astype(v_sel.dtype)
    out = jnp.einsum('bqgrk,bqkgd->bqgrd', prob, v_sel)
    return out.reshape(b_, q_, ATTN_WIDTH)


def _prompt_attention(q, k, v, qi, wi, ki):
    b_, s_ = q.shape[:2]
    top_k = min(TOPK_MAX, s_ // 4)
    nb = s_ // Q_BLOCK

    def to_blocks(t):
        return jnp.moveaxis(t.reshape(b_, nb, Q_BLOCK, *t.shape[2:]), 1, 0)

    pos_blocks = jnp.arange(s_).reshape(nb, Q_BLOCK)

    def one_block(args):
        qb, qib, wib, pb = args
        sel, valid = _indexer_select(qib, wib, ki, pb, top_k)
        return _sparse_attend(qb, _take_rows(k, sel), _take_rows(v, sel), valid)

    out = lax.map(one_block, (to_blocks(q), to_blocks(qi), to_blocks(wi), pos_blocks))
    return jnp.moveaxis(out, 0, 1).reshape(b_, s_, ATTN_WIDTH)


def _sample_attention(q, k_new, v_new, qi, wi, ki_new, cache_k, cache_v, cache_idx_k, page_table):
    db, t_ = q.shape[:2]
    n_pages = page_table.shape[1]
    past = n_pages * PAGE_SIZE
    top_k = min(TOPK_MAX, (past + t_) // 4)
    ki_past = cache_idx_k[page_table].reshape(db, past, IDX_DIM)
    ki_all = jnp.concatenate([ki_past, ki_new.astype(ki_past.dtype)], axis=1)
    q_pos = past + jnp.arange(t_)
    sel, valid = _indexer_select(qi, wi, ki_all, q_pos, top_k)
    in_past = (sel < past)[..., None, None]
    page_ix = jnp.minimum(sel // PAGE_SIZE, n_pages - 1)
    phys = _take_rows(page_table, page_ix)
    off = sel % PAGE_SIZE
    new_ix = jnp.clip(sel - past, 0, t_ - 1)
    k_sel = jnp.where(in_past, cache_k[phys, off], _take_rows(k_new, new_ix).astype(cache_k.dtype))
    v_sel = jnp.where(in_past, cache_v[phys, off], _take_rows(v_new, new_ix).astype(cache_v.dtype))
    return _sparse_attend(q, k_sel.astype(q.dtype), v_sel.astype(q.dtype), valid)


def _token_mix(x, pos, h0, attend_fn, p, lam_bar, b_bar, c_ssm):
    b_, s_ = x.shape[:2]
    h = _rms_norm(x, p['mix_norm'])
    z = h @ p['w_in']
    u, q, k, v, qi, ki, wi, ga, gb = jnp.split(z, IN_SPLITS, axis=-1)
    u = u.reshape(b_, s_, SSM_GROUPS, SSM_GROUP)
    y_ssm, h_last = _ssm_scan(u, h0, lam_bar, b_bar, c_ssm, p['ssm_d'])
    za = jax.nn.gelu(y_ssm.reshape(b_, s_, SSM_WIDTH))
    a_out = za * jax.nn.sigmoid(za @ p['glu_w'] + p['glu_b'])
    q = _rope_partial(_rms_norm(q.reshape(b_, s_, N_HEADS, HEAD_DIM), p['q_norm']), pos, ROT_DIM)
    k = _rope_partial(_rms_norm(k.reshape(b_, s_, N_KV_HEADS, HEAD_DIM), p['k_norm']), pos, ROT_DIM)
    v = v.reshape(b_, s_, N_KV_HEADS, HEAD_DIM)
    qi = _rope_partial(qi.reshape(b_, s_, IDX_HEADS, IDX_DIM), pos, IDX_ROT_DIM)
    ki = _rope_partial(ki[:, :, None, :], pos, IDX_ROT_DIM)[:, :, 0, :]
    wi = wi * (IDX_HEADS ** -0.5)
    b_out = attend_fn(q, k, v, qi, wi, ki)
    merged = jax.nn.sigmoid(ga) * (a_out @ p['w_branch_a']) + jax.nn.sigmoid(gb) * (b_out @ p['w_branch_b'])
    return x + merged @ p['w_out'], (k, v, ki, h_last)


def _layer(x_p, x_s, ck, cv, cik, s_re, s_im, page_table, p):
    f32 = jnp.float32
    past = page_table.shape[1] * PAGE_SIZE
    s_len, t_len = x_p.shape[1], x_s.shape[1]
    x_p = x_p + FFN_RES * _swiglu(_rms_norm(x_p, p['ffn1_norm']), p['ffn1_w_gate'], p['ffn1_w_up'], p['ffn1_w_down'])
    x_s = x_s + FFN_RES * _swiglu(_rms_norm(x_s, p['ffn1_norm']), p['ffn1_w_gate'], p['ffn1_w_up'], p['ffn1_w_down'])
    lam_bar, b_bar = _ssm_discretize(p['ssm_lambda_re'], p['ssm_lambda_im'], p['ssm_b_re'], p['ssm_b_im'], p['ssm_log_dt'])
    c_ssm = lax.complex(p['ssm_c_re'].astype(f32), p['ssm_c_im'].astype(f32))
    h0_p = jnp.zeros((x_p.shape[0], SSM_GROUPS, SSM_STATE), jnp.complex64)
    h0_s = lax.complex(s_re.astype(f32), s_im.astype(f32))

    def attn_s(q, k, v, qi, wi, ki):
        return _sample_attention(q, k, v, qi, wi, ki, ck, cv, cik, page_table)

    x_p, (kp, vp, kip, hp) = _token_mix(x_p, jnp.arange(s_len), h0_p, _prompt_attention, p, lam_bar, b_bar, c_ssm)
    x_s, (ks, vs, kis, hs) = _token_mix(x_s, past + jnp.arange(t_len), h0_s, attn_s, p, lam_bar, b_bar, c_ssm)
    x_p = x_p + FFN_RES * _swiglu(_rms_norm(x_p, p['ffn2_norm']), p['ffn2_w_gate'], p['ffn2_w_up'], p['ffn2_w_down'])
    x_s = x_s + FFN_RES * _swiglu(_rms_norm(x_s, p['ffn2_norm']), p['ffn2_w_gate'], p['ffn2_w_up'], p['ffn2_w_down'])
    rows = (kp, vp, kip, jnp.real(hp), jnp.imag(hp), ks, vs, kis, jnp.real(hs), jnp.imag(hs))
    return x_p, x_s, rows


def setup_inputs(seed: int = 0) -> dict:
    key = jax.random.key(seed)
    k = jax.random.split(key, 33)
    f32 = jnp.float32
    n_pages = PAST_LEN // PAGE_SIZE
    n_phys = (DEC_BATCH * n_pages * 5) // 4

    def nrm(kk, shape, scale):
        return jax.random.normal(kk, shape, f32) * scale

    def gain(kk, n):
        return 1.0 + 0.02 * jax.random.normal(kk, (DEPTH, n), f32)

    page_table = jax.random.permutation(k[7], n_phys)[: DEC_BATCH * n_pages].reshape(DEC_BATCH, n_pages).astype(jnp.int32)
    lam_re = -0.5 + 0.01 * jax.random.normal(k[16], (DEPTH, SSM_GROUPS, SSM_STATE), f32)
    lam_im = math.pi * jnp.arange(SSM_STATE, dtype=f32)[None, None, :] + 0.01 * jax.random.normal(k[17], (DEPTH, SSM_GROUPS, SSM_STATE), f32)
    log_dt = jax.random.uniform(k[23], (DEPTH, SSM_GROUPS), f32, math.log(DT_MIN), math.log(DT_MAX))
    return {
        'x_prompt': nrm(k[0], (BATCH, SEQ, D_MODEL), 1.0),
        'x_sample': nrm(k[1], (DEC_BATCH, DEC_SEQ, D_MODEL), 1.0),
        'cache_k': nrm(k[2], (DEPTH, n_phys, PAGE_SIZE, N_KV_HEADS, HEAD_DIM), 1.0),
        'cache_v': nrm(k[3], (DEPTH, n_phys, PAGE_SIZE, N_KV_HEADS, HEAD_DIM), 1.0),
        'cache_idx_k': nrm(k[4], (DEPTH, n_phys, PAGE_SIZE, IDX_DIM), 1.0),
        'state_ssm_re': nrm(k[5], (DEPTH, DEC_BATCH, SSM_GROUPS, SSM_STATE), 0.5),
        'state_ssm_im': nrm(k[6], (DEPTH, DEC_BATCH, SSM_GROUPS, SSM_STATE), 0.5),
        'page_table': page_table,
        'ffn1_norm': gain(k[8], D_MODEL),
        'ffn1_w_gate': nrm(k[9], (DEPTH, D_MODEL, D_FF), D_MODEL ** -0.5),
        'ffn1_w_up': nrm(k[10], (DEPTH, D_MODEL, D_FF), D_MODEL ** -0.5),
        'ffn1_w_down': nrm(k[11], (DEPTH, D_FF, D_MODEL), D_FF ** -0.5),
        'mix_norm': gain(k[12], D_MODEL),
        'w_in': nrm(k[13], (DEPTH, D_MODEL, IN_WIDTH), D_MODEL ** -0.5),
        'q_norm': gain(k[14], HEAD_DIM),
        'k_norm': gain(k[15], HEAD_DIM),
        'ssm_lambda_re': lam_re,
        'ssm_lambda_im': lam_im,
        'ssm_b_re': nrm(k[18], (DEPTH, SSM_GROUPS, SSM_STATE, SSM_GROUP), (2 * SSM_GROUP) ** -0.5),
        'ssm_b_im': nrm(k[19], (DEPTH, SSM_GROUPS, SSM_STATE, SSM_GROUP), (2 * SSM_GROUP) ** -0.5),
        'ssm_c_re': nrm(k[20], (DEPTH, SSM_GROUPS, SSM_GROUP, SSM_STATE), SSM_STATE ** -0.5),
        'ssm_c_im': nrm(k[21], (DEPTH, SSM_GROUPS, SSM_GROUP, SSM_STATE), SSM_STATE ** -0.5),
        'ssm_d': nrm(k[22], (DEPTH, SSM_GROUPS, SSM_GROUP), 0.5),
        'ssm_log_dt': log_dt,
        'glu_w': nrm(k[24], (DEPTH, SSM_WIDTH, SSM_WIDTH), SSM_WIDTH ** -0.5),
        'glu_b': nrm(k[25], (DEPTH, SSM_WIDTH), 0.01),
        'w_branch_a': nrm(k[26], (DEPTH, SSM_WIDTH, D_MODEL), SSM_WIDTH ** -0.5),
        'w_branch_b': nrm(k[27], (DEPTH, ATTN_WIDTH, D_MODEL), ATTN_WIDTH ** -0.5),
        'w_out': nrm(k[28], (DEPTH, D_MODEL, D_MODEL), D_MODEL ** -0.5),
        'ffn2_norm': gain(k[29], D_MODEL),
        'ffn2_w_gate': nrm(k[30], (DEPTH, D_MODEL, D_FF), D_MODEL ** -0.5),
        'ffn2_w_up': nrm(k[31], (DEPTH, D_MODEL, D_FF), D_MODEL ** -0.5),
        'ffn2_w_down': nrm(k[32], (DEPTH, D_FF, D_MODEL), D_FF ** -0.5),
    }


def reference(x_prompt, x_sample, cache_k, cache_v, cache_idx_k, state_ssm_re, state_ssm_im, page_table,
              ffn1_norm, ffn1_w_gate, ffn1_w_up, ffn1_w_down, mix_norm, w_in, q_norm, k_norm,
              ssm_lambda_re, ssm_lambda_im, ssm_b_re, ssm_b_im, ssm_c_re, ssm_c_im, ssm_d, ssm_log_dt,
              glu_w, glu_b, w_branch_a, w_branch_b, w_out, ffn2_norm, ffn2_w_gate, ffn2_w_up, ffn2_w_down):
    y_p, y_s = x_prompt, x_sample
    new = [[] for _ in range(10)]
    for l in range(DEPTH):
        p = dict(
            ffn1_norm=ffn1_norm[l], ffn1_w_gate=ffn1_w_gate[l], ffn1_w_up=ffn1_w_up[l], ffn1_w_down=ffn1_w_down[l],
            mix_norm=mix_norm[l], w_in=w_in[l], q_norm=q_norm[l], k_norm=k_norm[l],
            ssm_lambda_re=ssm_lambda_re[l], ssm_lambda_im=ssm_lambda_im[l],
            ssm_b_re=ssm_b_re[l], ssm_b_im=ssm_b_im[l], ssm_c_re=ssm_c_re[l], ssm_c_im=ssm_c_im[l],
            ssm_d=ssm_d[l], ssm_log_dt=ssm_log_dt[l], glu_w=glu_w[l], glu_b=glu_b[l],
            w_branch_a=w_branch_a[l], w_branch_b=w_branch_b[l], w_out=w_out[l],
            ffn2_norm=ffn2_norm[l], ffn2_w_gate=ffn2_w_gate[l], ffn2_w_up=ffn2_w_up[l], ffn2_w_down=ffn2_w_down[l],
        )
        y_p, y_s, rows = _layer(y_p, y_s, cache_k[l], cache_v[l], cache_idx_k[l],
                                state_ssm_re[l], state_ssm_im[l], page_table, p)
        for lst, r in zip(new, rows):
            lst.append(r)
    k_p, v_p, ik_p, re_p, im_p, k_s, v_s, ik_s, re_s, im_s = [jnp.stack(lst) for lst in new]
    return (y_p, y_s, k_p, v_p, ik_p, re_p, im_p, k_s, v_s, ik_s, re_s, im_s)
```

```python
import functools
import math

import jax
import jax.numpy as jnp
from jax import lax
from jax.experimental import pallas as pl
from jax.experimental.pallas import tpu as pltpu

F32 = jnp.float32
BF16 = jnp.bfloat16

SSM_GROUP = 16
SSM_STATE = 64
N_HEADS = 8
HEAD_DIM = 128
N_KV_HEADS = 2
N_REP = N_HEADS // N_KV_HEADS
ROT_DIM = HEAD_DIM // 4
ROPE_THETA = 500000.0
IDX_HEADS = 16
IDX_DIM = 64
IDX_ROT_DIM = IDX_DIM // 4
TOPK_MAX = 256
PAGE_SIZE = 128
FFN_RES = 0.5
EPS = 1e-6

LANES = 128
SUBLANES = 8
VMEM_LIMIT_BYTES = 56 * 1024 * 1024

GROUPS_PER_BLOCK = LANES // SSM_GROUP
STATES_PER_BLOCK = GROUPS_PER_BLOCK * SSM_STATE
SSM_T = 8

INT_MIN = -(2 ** 31)


def _cparams(*sem):
    return pltpu.CompilerParams(dimension_semantics=sem, vmem_limit_bytes=VMEM_LIMIT_BYTES)


def _rms(x, g):
    return x * lax.rsqrt(jnp.mean(x * x, axis=-1, keepdims=True) + EPS) * g


def _dot(a, b):
    return jnp.dot(a, b, preferred_element_type=F32)


def _dot_nt(a, b):
    return lax.dot_general(a, b, (((1,), (1,)), ((), ())), preferred_element_type=F32)


def _ffn_kernel(*refs, nf, n_main, tm, tail, two_src, split_out, with_next_norm):
    refs = list(refs)
    x_main_ref = refs.pop(0)
    x_tail_ref = refs.pop(0) if two_src else x_main_ref
    g_ref, wg_ref, wu_ref, wd_ref = (refs.pop(0) for _ in range(4))
    g2_ref = refs.pop(0) if with_next_norm else None
    y_main_ref = refs.pop(0)
    y_tail_ref = refs.pop(0) if split_out else y_main_ref
    n2_ref = refs.pop(0) if with_next_norm else None
    xn_ref, acc_ref = refs
    i, f = pl.program_id(0), pl.program_id(1)

    def run(rows, x_ref, y_ref):
        @pl.when(f == 0)
        def _():
            xn_ref[:rows] = _rms(x_ref[:rows], g_ref[...]).astype(BF16)
            acc_ref[:rows] = jnp.zeros((rows, acc_ref.shape[1]), F32)

        xn = xn_ref[:rows]
        a = _dot(xn, wg_ref[...])
        b = _dot(xn, wu_ref[...])
        h = (a * jax.nn.sigmoid(a) * b).astype(BF16)
        acc_ref[:rows] += _dot(h, wd_ref[...])

        @pl.when(f == nf - 1)
        def _():
            y = x_ref[:rows] + FFN_RES * acc_ref[:rows]
            y_ref[:rows] = y
            if with_next_norm:
                n2_ref[:rows] = _rms(y, g2_ref[...]).astype(BF16)

    pl.when(i < n_main)(lambda: run(tm, x_main_ref, y_main_ref))
    pl.when(i == n_main)(lambda: run(tail, x_tail_ref, y_tail_ref))


def _ffn(x_main, x_tail, g, wg, wu, wd, g2=None, *, n_main_rows, tail, split_out, tm, tf):
    d = x_main.shape[1]
    nf = wg.shape[1] // tf
    n_main = n_main_rows // tm
    m = n_main_rows + tail
    two_src = x_tail is not None
    with_next = g2 is not None
    clamp = lambda i, f: (jnp.minimum(i, n_main - 1), 0)
    rows = lambda i, f: (i, 0)
    first = lambda i, f: (0, 0)
    sds = jax.ShapeDtypeStruct
    in_specs = [pl.BlockSpec((tm, d), clamp if two_src else rows)]
    args = [x_main]
    if two_src:
        in_specs.append(pl.BlockSpec((tail, d), first))
        args.append(x_tail)
    in_specs += [pl.BlockSpec((1, d), first), pl.BlockSpec((d, tf), lambda i, f: (0, f)),
                 pl.BlockSpec((d, tf), lambda i, f: (0, f)), pl.BlockSpec((tf, d), lambda i, f: (f, 0))]
    args += [g.reshape(1, d), wg, wu, wd]
    if with_next:
        in_specs.append(pl.BlockSpec((1, d), first))
        args.append(g2.reshape(1, d))
    if split_out:
        out_shape = [sds((n_main_rows, d), F32), sds((tail, d), F32)]
        out_specs = [pl.BlockSpec((tm, d), clamp), pl.BlockSpec((tail, d), first)]
    else:
        out_shape = [sds((m, d), F32)]
        out_specs = [pl.BlockSpec((tm, d), rows)]
    if with_next:
        out_shape.append(sds((m, d), BF16))
        out_specs.append(pl.BlockSpec((tm, d), rows))
    return pl.pallas_call(
        functools.partial(_ffn_kernel, nf=nf, n_main=n_main, tm=tm, tail=tail, two_src=two_src,
                          split_out=split_out, with_next_norm=with_next),
        grid=(n_main + 1, nf),
        in_specs=in_specs,
        out_specs=out_specs,
        out_shape=out_shape,
        scratch_shapes=[pltpu.VMEM((tm, d), BF16), pltpu.VMEM((tm, d), F32)],
        compiler_params=_cparams("arbitrary", "arbitrary"),
        name="ffn" + ("_norm" if with_next else ""),
    )(*args)


def _rope_tables(pos, rot_dim, width, tile):
    half = rot_dim // 2
    m = pos.shape[0]
    freqs = ROPE_THETA ** (-jnp.arange(half, dtype=F32) * 2.0 / rot_dim)
    ang = pos.astype(F32)[:, None] * freqs[None, :]
    cos, sin = jnp.cos(ang), jnp.sin(ang)
    zh = jnp.zeros((m, half), F32)
    rest = width - rot_dim
    c = jnp.concatenate([cos, cos, jnp.ones((m, rest), F32)], axis=1)
    s1 = jnp.concatenate([-sin, zh, jnp.zeros((m, rest), F32)], axis=1)
    s2 = jnp.concatenate([zh, sin, jnp.zeros((m, rest), F32)], axis=1)
    if tile:
        reps = LANES // width
        return tuple(jnp.tile(t, (1, reps)) for t in (c, s1, s2))
    pad = LANES - width
    return (jnp.pad(c, ((0, 0), (0, pad)), constant_values=1.0),
            jnp.pad(s1, ((0, 0), (0, pad))), jnp.pad(s2, ((0, 0), (0, pad))))


def _rope(x, c, s1, s2, half):
    return x * c + pltpu.roll(x, LANES - half, 1) * s1 + pltpu.roll(x, half, 1) * s2


def _u_proj_kernel(xn_ref, w_ref, u_ref):
    z = _dot(xn_ref[...], w_ref[...])
    for j in range(z.shape[1] // LANES):
        u_ref[j] = z[:, j * LANES:(j + 1) * LANES]


def _q_proj_kernel(xn_ref, w_ref, g_ref, c_ref, s1_ref, s2_ref, qn_ref, qt_ref):
    z = _dot(xn_ref[...], w_ref[...])
    c, s1, s2, g = c_ref[...], s1_ref[...], s2_ref[...], g_ref[...]
    heads = []
    for h in range(N_HEADS):
        x = _rms(z[:, h * HEAD_DIM:(h + 1) * HEAD_DIM], g)
        heads.append(_rope(x, c, s1, s2, ROT_DIM // 2) * (HEAD_DIM ** -0.5))
    q = jnp.concatenate(heads, axis=1)
    qn_ref[...] = q.astype(BF16)
    qt_ref[...] = q.T.astype(BF16)


def _kv_proj_kernel(xn_ref, w_ref, g_ref, c_ref, s1_ref, s2_ref, k_ref, kb_ref, v_ref, vt_ref):
    z = _dot(xn_ref[...], w_ref[...])
    c, s1, s2, g = c_ref[...], s1_ref[...], s2_ref[...], g_ref[...]
    kw = N_KV_HEADS * HEAD_DIM
    heads = []
    for h in range(N_KV_HEADS):
        x = _rms(z[:, h * HEAD_DIM:(h + 1) * HEAD_DIM], g)
        heads.append(_rope(x, c, s1, s2, ROT_DIM // 2))
    k = jnp.concatenate(heads, axis=1)
    v = z[:, kw:]
    k_ref[...] = k
    kb_ref[...] = k.astype(BF16)
    v_ref[...] = v
    vt_ref[...] = v.T.astype(BF16)


def _qi_proj_kernel(xn_ref, w_ref, c_ref, s1_ref, s2_ref, qn_ref, qt_ref):
    z = _dot(xn_ref[...], w_ref[...])
    c, s1, s2 = c_ref[...], s1_ref[...], s2_ref[...]
    cols = []
    for j in range(z.shape[1] // LANES):
        x = z[:, j * LANES:(j + 1) * LANES]
        cols.append(_rope(x, c, s1, s2, IDX_ROT_DIM // 2) * (IDX_DIM ** -0.5))
    q = jnp.concatenate(cols, axis=1)
    qn_ref[...] = q.astype(BF16)
    qt_ref[...] = q.T.astype(BF16)


def _kiwi_proj_kernel(xn_ref, w_ref, c_ref, s1_ref, s2_ref, scale_ref, o_ref, ot_ref, kb_ref):
    z = _dot(xn_ref[...], w_ref[...])
    y = _rope(z, c_ref[...], s1_ref[...], s2_ref[...], IDX_ROT_DIM // 2) * scale_ref[...]
    o_ref[...] = y
    ot_ref[...] = y.T
    kb_ref[...] = y[:, :IDX_DIM].astype(BF16)


def _gate_proj_kernel(xn_ref, w_ref, o_ref):
    o_ref[...] = jax.nn.sigmoid(_dot(xn_ref[...], w_ref[...])).astype(BF16)


def _row_spec(tm, n):
    return pl.BlockSpec((tm, n), lambda i: (i, 0))


def _full_spec(shape):
    return pl.BlockSpec(shape, lambda i: (0,) * len(shape))


def _projections(xn, w_u, w_q, w_kv, w_qi, w_kw, w_g, q_norm, k_norm, pos, *, tm):
    m, d = xn.shape
    grid = (m // tm,)
    hd_tabs = _rope_tables(pos, ROT_DIM, HEAD_DIM, True)
    ix_tabs = _rope_tables(pos, IDX_ROT_DIM, IDX_DIM, True)
    kw_tabs = _rope_tables(pos, IDX_ROT_DIM, IDX_DIM, False)
    tab_specs = [_row_spec(tm, LANES)] * 3
    xs = _row_spec(tm, d)
    sds = jax.ShapeDtypeStruct
    par = _cparams("parallel")

    nu = w_u.shape[1]
    u8 = pl.pallas_call(
        _u_proj_kernel, grid=grid, in_specs=[xs, _full_spec(w_u.shape)],
        out_specs=pl.BlockSpec((nu // LANES, tm, LANES), lambda i: (0, i, 0)),
        out_shape=sds((nu // LANES, m, LANES), F32), compiler_params=par, name="proj_u",
    )(xn, w_u)

    nq = w_q.shape[1]
    q_nat, q_t = pl.pallas_call(
        _q_proj_kernel, grid=grid,
        in_specs=[xs, _full_spec(w_q.shape), _full_spec((1, HEAD_DIM))] + tab_specs,
        out_specs=[_row_spec(tm, nq), pl.BlockSpec((nq, tm), lambda i: (0, i))],
        out_shape=[sds((m, nq), BF16), sds((nq, m), BF16)], compiler_params=par, name="proj_q",
    )(xn, w_q, q_norm.reshape(1, HEAD_DIM), *hd_tabs)

    kvw = N_KV_HEADS * HEAD_DIM
    k, kb, v, v_t = pl.pallas_call(
        _kv_proj_kernel, grid=grid,
        in_specs=[xs, _full_spec(w_kv.shape), _full_spec((1, HEAD_DIM))] + tab_specs,
        out_specs=[_row_spec(tm, kvw), _row_spec(tm, kvw), _row_spec(tm, kvw),
                   pl.BlockSpec((kvw, tm), lambda i: (0, i))],
        out_shape=[sds((m, kvw), F32), sds((m, kvw), BF16), sds((m, kvw), F32), sds((kvw, m), BF16)],
        compiler_params=par, name="proj_kv",
    )(xn, w_kv, k_norm.reshape(1, HEAD_DIM), *hd_tabs)

    nqi = w_qi.shape[1]
    qi_nat, qi_t = pl.pallas_call(
        _qi_proj_kernel, grid=grid,
        in_specs=[xs, _full_spec(w_qi.shape)] + tab_specs,
        out_specs=[_row_spec(tm, nqi), pl.BlockSpec((nqi, tm), lambda i: (0, i))],
        out_shape=[sds((m, nqi), BF16), sds((nqi, m), BF16)], compiler_params=par, name="proj_qi",
    )(xn, w_qi, *ix_tabs)

    lane = jnp.arange(LANES)
    kw_scale = jnp.where(lane < IDX_DIM, 1.0, IDX_HEADS ** -0.5).astype(F32).reshape(1, LANES)
    kiw, kiw_t, kib = pl.pallas_call(
        _kiwi_proj_kernel, grid=grid,
        in_specs=[xs, _full_spec(w_kw.shape)] + tab_specs + [_full_spec((1, LANES))],
        out_specs=[_row_spec(tm, LANES), pl.BlockSpec((LANES, tm), lambda i: (0, i)), _row_spec(tm, IDX_DIM)],
        out_shape=[sds((m, LANES), F32), sds((LANES, m), F32), sds((m, IDX_DIM), BF16)],
        compiler_params=par, name="proj_kiwi",
    )(xn, w_kw, *kw_tabs, kw_scale)

    ng = w_g.shape[1]
    tn = 1024
    gates = pl.pallas_call(
        _gate_proj_kernel, grid=(m // tm, ng // tn),
        in_specs=[pl.BlockSpec((tm, d), lambda i, j: (i, 0)), pl.BlockSpec((d, tn), lambda i, j: (0, j))],
        out_specs=pl.BlockSpec((tm, tn), lambda i, j: (i, j)),
        out_shape=sds((m, ng), BF16), compiler_params=_cparams("parallel", "parallel"), name="proj_gates",
    )(xn, w_g)
    return dict(u8=u8, q=q_nat, q_t=q_t, k=k, kb=kb, v=v, v_t=v_t, qi=qi_nat, qi_t=qi_t,
                kiw=kiw, kiw_t=kiw_t, kib=kib, gates=gates)


def _ssm_prep_kernel(lre_ref, lim_ref, ldt_ref, bre_ref, bim_ref, cre_ref, cim_ref,
                     k_ref, win_ref, wout_ref, lp_ref):
    ns = STATES_PER_BLOCK
    lre, lim = lre_ref[...], lim_ref[...]
    dt = jnp.exp(ldt_ref[...])
    a, th = lre * dt, lim * dt

    def power(l):
        mag = jnp.exp(a * float(l))
        return mag * jnp.cos(th * float(l)), mag * jnp.sin(th * float(l))

    pw = [power(l) for l in range(SSM_T + 1)]
    xr, xi = pw[1][0] - 1.0, pw[1][1]
    den = lre * lre + lim * lim
    cr, ci = (xr * lre + xi * lim) / den, (xi * lre - xr * lim) / den
    bre, bim = bre_ref[...], bim_ref[...]
    bbr, bbi = bre * cr - bim * ci, bre * ci + bim * cr
    cre, cim = cre_ref[...], cim_ref[...]
    hi = lax.Precision.HIGHEST
    nt = (((1,), (1,)), ((), ()))
    lag = []
    for l in range(SSM_T):
        pr, pi = pw[l]
        blr, bli = bbr * pr - bbi * pi, bbr * pi + bbi * pr
        m = (lax.dot_general(blr, cre, nt, precision=hi, preferred_element_type=F32)
             - lax.dot_general(bli, cim, nt, precision=hi, preferred_element_type=F32))
        lag.append(m.astype(BF16))
        t = SSM_T - 1 - l
        win_ref[t * LANES:(t + 1) * LANES, :ns] = blr.astype(BF16)
        win_ref[t * LANES:(t + 1) * LANES, ns:] = bli.astype(BF16)
    zero = jnp.zeros((LANES, LANES), BF16)
    for t in range(SSM_T):
        pr, pi = pw[t + 1]
        wout_ref[t * LANES:(t + 1) * LANES, :ns] = (cre * pr - cim * pi).astype(BF16)
        wout_ref[t * LANES:(t + 1) * LANES, ns:] = (-(cre * pi + cim * pr)).astype(BF16)
        lp_ref[t:t + 1, :ns] = pr
        lp_ref[t:t + 1, ns:] = pi
        for t2 in range(SSM_T):
            k_ref[t * LANES:(t + 1) * LANES, t2 * LANES:(t2 + 1) * LANES] = lag[t2 - t] if t2 >= t else zero


def _block_diag_groups(w):
    g, h, p = w.shape
    nb = g // GROUPS_PER_BLOCK
    w = w.reshape(nb, GROUPS_PER_BLOCK, h, p)
    eye = jnp.eye(GROUPS_PER_BLOCK, dtype=w.dtype)
    out = w[:, :, :, None, :] * eye[None, :, None, :, None]
    return out.reshape(nb, GROUPS_PER_BLOCK * h, GROUPS_PER_BLOCK * p)


def _ssm_prep(lam_re, lam_im, b_re, b_im, c_re, c_im, log_dt):
    g, p = lam_re.shape
    nb = g // GROUPS_PER_BLOCK
    ns = STATES_PER_BLOCK
    tl = SSM_T * LANES
    vec = lambda a: a.reshape(nb, 1, ns)
    ldt = vec(jnp.broadcast_to(log_dt[:, None], (g, p)))
    bt = lambda b: _block_diag_groups(jnp.swapaxes(b, 1, 2))
    vspec = pl.BlockSpec((None, 1, ns), lambda j: (j, 0, 0))
    mspec = pl.BlockSpec((None, LANES, ns), lambda j: (j, 0, 0))
    sds = jax.ShapeDtypeStruct
    return pl.pallas_call(
        _ssm_prep_kernel, grid=(nb,),
        in_specs=[vspec, vspec, vspec, mspec, mspec, mspec, mspec],
        out_specs=[pl.BlockSpec((None, tl, tl), lambda j: (j, 0, 0)),
                   pl.BlockSpec((None, tl, 2 * ns), lambda j: (j, 0, 0)),
                   pl.BlockSpec((None, tl, 2 * ns), lambda j: (j, 0, 0)),
                   pl.BlockSpec((None, SSM_T, 2 * ns), lambda j: (j, 0, 0))],
        out_shape=[sds((nb, tl, tl), BF16), sds((nb, tl, 2 * ns), BF16), sds((nb, tl, 2 * ns), BF16),
                   sds((nb, SSM_T, 2 * ns), F32)],
        compiler_params=_cparams("parallel"), name="ssm_prep",
    )(vec(lam_re), vec(lam_im), ldt, bt(b_re), bt(b_im), _block_diag_groups(c_re), _block_diag_groups(c_im))


def _ssm_kernel(u_ref, k_ref, win_ref, wout_ref, lp_ref, d_ref, h0_ref, *rest, nseq, seq, t_steps, chain):
    za_ref, hl_ref, x_scr, hs_scr = rest[-4:]
    ns = STATES_PER_BLOCK
    c_per = seq // t_steps

    def gather(b):
        return [u_ref[pl.ds(b * seq + t, c_per, stride=t_steps), :] for t in range(t_steps)]

    nk = ns // LANES
    for b in range(nseq):
        u = jnp.concatenate(gather(b), axis=1).astype(BF16)
        x = _dot(u, win_ref[...])
        for k in range(2 * nk):
            x_scr[k, b * c_per:(b + 1) * c_per, :] = x[:, k * LANES:(k + 1) * LANES]
    lam_t = [lp_ref[t_steps - 1:t_steps, k * LANES:(k + 1) * LANES] for k in range(2 * nk)]

    def advance(h, x):
        re = [lam_t[k] * h[k] - lam_t[nk + k] * h[nk + k] + x[k] for k in range(nk)]
        im = [lam_t[k] * h[nk + k] + lam_t[nk + k] * h[k] + x[nk + k] for k in range(nk)]
        return re + im

    split = lambda a: [a[:, k * LANES:(k + 1) * LANES] for k in range(2 * nk)]
    if chain:
        def step(c, h):
            rows = pl.ds(c, nseq, stride=c_per)
            for k in range(2 * nk):
                hs_scr.at[k][rows, :] = h[k]
            return tuple(advance(h, [x_scr.at[k][rows, :] for k in range(2 * nk)]))
        h_last = lax.fori_loop(0, c_per, step, tuple(split(h0_ref[...])))
    else:
        h0 = split(h0_ref[...])
        for k in range(2 * nk):
            hs_scr[k] = h0[k]
        h_last = advance(h0, [x_scr[k] for k in range(2 * nk)])
    hl_ref[...] = jnp.concatenate(list(h_last), axis=1)

    d = d_ref[...]
    for b in range(nseq):
        cols = gather(b)
        u = jnp.concatenate(cols, axis=1).astype(BF16)
        hs = jnp.concatenate([hs_scr[k, b * c_per:(b + 1) * c_per, :] for k in range(2 * nk)], axis=1).astype(BF16)
        y = _dot(u, k_ref[...]) + _dot_nt(hs, wout_ref[...])
        for t in range(t_steps):
            yt = y[:, t * LANES:(t + 1) * LANES] + d * cols[t]
            za_ref[pl.ds(b * seq + t, c_per, stride=t_steps), :] = jax.nn.gelu(yt, approximate=True)


def _ssm(u8, row0, kmat, win, wout, lp, d8, h0, *, nseq, seq, t_steps, chain, za_prev=None):
    nb, m_total, _ = u8.shape
    ns = STATES_PER_BLOCK
    ntok = nseq * seq
    assert row0 % ntok == 0
    rows = ntok // t_steps
    n_state_rows = nseq if chain else rows
    tl = t_steps * LANES
    win_blk = SSM_T // t_steps - 1
    sds = jax.ShapeDtypeStruct
    tok_spec = pl.BlockSpec((None, ntok, LANES), lambda j: (j, row0 // ntok, 0))
    in_specs = [tok_spec,
                pl.BlockSpec((None, tl, tl), lambda j: (j, 0, 0)),
                pl.BlockSpec((None, tl, 2 * ns), lambda j: (j, win_blk, 0)),
                pl.BlockSpec((None, tl, 2 * ns), lambda j: (j, 0, 0)),
                pl.BlockSpec((None, SSM_T, 2 * ns), lambda j: (j, 0, 0)),
                pl.BlockSpec((None, 1, LANES), lambda j: (j, 0, 0)),
                pl.BlockSpec((None, n_state_rows, 2 * ns), lambda j: (j, 0, 0))]
    args = [u8, kmat, win, wout, lp, d8, h0]
    aliases = {}
    if za_prev is not None:
        in_specs.append(pl.BlockSpec(memory_space=pl.ANY))
        args.append(za_prev)
        aliases = {len(args) - 1: 0}
    return pl.pallas_call(
        functools.partial(_ssm_kernel, nseq=nseq, seq=seq, t_steps=t_steps, chain=chain),
        grid=(nb,),
        in_specs=in_specs,
        out_specs=[tok_spec, pl.BlockSpec((None, n_state_rows, 2 * ns), lambda j: (j, 0, 0))],
        out_shape=[sds((nb, m_total, LANES), F32), sds((nb, n_state_rows, 2 * ns), F32)],
        scratch_shapes=[pltpu.VMEM((2 * ns // LANES, rows, LANES), F32)] * 2,
        input_output_aliases=aliases,
        compiler_params=_cparams("parallel"), name="ssm_chain" if chain else "ssm_single",
    )(*args)


def _state_to_blocks(re, im):
    n, g, p = re.shape
    nb = g // GROUPS_PER_BLOCK
    f = lambda a: jnp.transpose(a.reshape(n, nb, GROUPS_PER_BLOCK * p), (1, 0, 2))
    return jnp.concatenate([f(re), f(im)], axis=-1)


def _blocks_to_state(h):
    nb, n, w = h.shape
    ns = w // 2
    f = lambda a: jnp.transpose(a, (1, 0, 2)).reshape(n, nb * GROUPS_PER_BLOCK, ns // GROUPS_PER_BLOCK)
    return f(h[..., :ns]), f(h[..., ns:])


def _key_to_float(c):
    return lax.bitcast_convert_type(c ^ ((c >> 31) & 0x7FFFFFFF), F32)


def _kth_largest(count_ge, shape, top_k):
    def body(it, u):
        bit = jnp.left_shift(jnp.int32(1), 31 - it)
        cand = u | bit
        return jnp.where(count_ge(_key_to_float(cand ^ INT_MIN)) >= top_k, cand, u)
    return _key_to_float(lax.fori_loop(0, 32, body, jnp.zeros(shape, jnp.int32)) ^ INT_MIN)


def _attn_prompt_kernel(qt_ref, kb_ref, vt_ref, qit_ref, kib_ref, wt_ref, o_ref, s_scr, bias_scr,
                        *, seq, top_k, key_chunk):
    qb = pl.program_id(1)
    tq = o_ref.shape[0]

    def run(nk):
        kib = kib_ref[:nk]
        s = jnp.zeros((nk, tq), F32)
        for h in range(IDX_HEADS):
            d = _dot(kib, qit_ref[h * IDX_DIM:(h + 1) * IDX_DIM, :])
            s = s + jnp.maximum(d, 0.0) * wt_ref[IDX_DIM + h:IDX_DIM + h + 1, :]
        kpos = lax.broadcasted_iota(jnp.int32, (nk, tq), 0)
        qpos = qb * tq + lax.broadcasted_iota(jnp.int32, (nk, tq), 1)
        allowed = kpos <= qpos
        s_scr[:nk] = jnp.where(allowed, s, -jnp.inf)

        def count_ge(t):
            return jnp.sum((s_scr[:nk] >= t).astype(jnp.int32), axis=0, keepdims=True)

        thr = _kth_largest(count_ge, (1, tq), top_k)
        few = qpos < top_k - 1
        bias_scr[:nk] = jnp.where(allowed & ((s_scr[:nk] >= thr) | few), 0.0, -jnp.inf)

        outs = []
        for g in range(N_KV_HEADS):
            kg = kb_ref[:nk, g * HEAD_DIM:(g + 1) * HEAD_DIM]
            vtg = vt_ref[g * HEAD_DIM:(g + 1) * HEAD_DIM, :nk]
            for r in range(N_REP):
                h = g * N_REP + r
                lg = _dot(kg, qt_ref[h * HEAD_DIM:(h + 1) * HEAD_DIM, :]) + bias_scr[:nk]
                p = jnp.exp(lg - jnp.max(lg, axis=0, keepdims=True))
                den = jnp.sum(p, axis=0, keepdims=True)
                outs.append(_dot(vtg, p.astype(BF16)) / den)
        o_ref[...] = jnp.concatenate(outs, axis=0).T.astype(BF16)

    n_var = seq // key_chunk
    need = (qb * tq + tq + key_chunk - 1) // key_chunk
    for v in range(1, n_var + 1):
        pl.when(need == v)(functools.partial(run, v * key_chunk))


def _attn_prompt(q_t, kb, v_t, qi_t, kib, kiw_t, *, batch, seq, m_out, tq=128, key_chunk=256):
    top_k = min(TOPK_MAX, seq // 4)
    nq = seq // tq
    assert seq % key_chunk == 0 and key_chunk % tq == 0
    aw = N_HEADS * HEAD_DIM
    qcol = lambda n: pl.BlockSpec((n, tq), lambda b, i: (0, b * nq + i))
    return pl.pallas_call(
        functools.partial(_attn_prompt_kernel, seq=seq, top_k=top_k, key_chunk=key_chunk),
        grid=(batch, nq),
        in_specs=[qcol(aw),
                  pl.BlockSpec((seq, N_KV_HEADS * HEAD_DIM), lambda b, i: (b, 0)),
                  pl.BlockSpec((N_KV_HEADS * HEAD_DIM, seq), lambda b, i: (0, b)),
                  qcol(IDX_HEADS * IDX_DIM),
                  pl.BlockSpec((seq, IDX_DIM), lambda b, i: (b, 0)),
                  qcol(LANES)],
        out_specs=pl.BlockSpec((tq, aw), lambda b, i: (b * nq + i, 0)),
        out_shape=jax.ShapeDtypeStruct((m_out, aw), BF16),
        scratch_shapes=[pltpu.VMEM((seq, tq), F32), pltpu.VMEM((seq, tq), F32)],
        compiler_params=_cparams("parallel", "parallel"), name="attn_prompt",
    )(q_t, kb, v_t, qi_t, kib, kiw_t)


def _merge_kernel(za_ref, bo_ref, g_ref, x_ref, gw_ref, gb_ref, wa_ref, wb_ref, wo_ref, o_ref):
    za = jnp.concatenate([za_ref[j] for j in range(za_ref.shape[0])], axis=1)
    a_out = za * jax.nn.sigmoid(_dot(za.astype(BF16), gw_ref[...]) + gb_ref[...])
    d = x_ref.shape[1]
    merged = (g_ref[:, :d] * _dot(a_out.astype(BF16), wa_ref[...])
              + g_ref[:, d:] * _dot(bo_ref[...], wb_ref[...]))
    o_ref[...] = x_ref[...] + _dot(merged.astype(BF16), wo_ref[...])


def _merge(za8, b_out, gates, x, glu_w, glu_b, wa, wb, wo, *, tm):
    m, d = x.shape
    nb = za8.shape[0]
    aw = b_out.shape[1]
    resident = lambda shape: pl.BlockSpec(shape, lambda i: (0,) * len(shape), pipeline_mode=pl.Buffered(1))
    return pl.pallas_call(
        _merge_kernel, grid=(m // tm,),
        in_specs=[pl.BlockSpec((nb, tm, LANES), lambda i: (0, i, 0)), _row_spec(tm, aw), _row_spec(tm, 2 * d),
                  _row_spec(tm, d), resident(glu_w.shape), resident((1, glu_w.shape[1])),
                  resident(wa.shape), resident(wb.shape), resident(wo.shape)],
        out_specs=_row_spec(tm, d),
        out_shape=jax.ShapeDtypeStruct((m, d), F32),
        compiler_params=_cparams("parallel"), name="merge",
    )(za8, b_out, gates, x, glu_w, glu_b.reshape(1, -1), wa, wb, wo)


QPAD = SUBLANES
IDX_PAGES_PER_STEP = 16
KV_PAGES_PER_STEP = 8


def _sample_scores_kernel(pt_ref, qs_ref, ws_ref, kn_ref, *refs):
    pages, (sp_ref, sn_ref) = refs[:IDX_PAGES_PER_STEP], refs[IDX_PAGES_PER_STEP:]
    qs, ws = qs_ref[...], ws_ref[...]

    def scores(d):
        r = jnp.maximum(d, 0.0) * ws
        return r.reshape(IDX_HEADS, QPAD, r.shape[1]).sum(axis=0)

    for kk in range(IDX_PAGES_PER_STEP):
        sp_ref[:, kk * PAGE_SIZE:(kk + 1) * PAGE_SIZE] = scores(_dot(qs, pages[kk][...].astype(BF16)))
    sn_ref[...] = scores(_dot_nt(qs, kn_ref[...]))


def _sample_thr_kernel(sp_ref, sn_ref, thr_ref, s_scr, *, past, top_k):
    s = jnp.concatenate([sp_ref[...], sn_ref[...]], axis=1)
    shape = s.shape
    kpos = lax.broadcasted_iota(jnp.int32, shape, 1)
    qidx = lax.broadcasted_iota(jnp.int32, shape, 0) % QPAD
    s_scr[...] = jnp.where(kpos <= past + qidx, s, -jnp.inf)

    def count_ge(t):
        return jnp.sum((s_scr[...] >= t).astype(jnp.int32), axis=1, keepdims=True)

    thr = _kth_largest(count_ge, (shape[0], 1), top_k)
    thr_ref[...] = jnp.broadcast_to(thr, thr_ref.shape)


def _sample_attn_kernel(pt_ref, q_ref, sp_ref, sn_ref, thr_ref, kn_ref, vn_ref, *refs):
    npg = KV_PAGES_PER_STEP
    kpages, vpages = refs[:npg], refs[npg:2 * npg]
    o_ref, m_scr, l_scr, acc_scr = refs[2 * npg:]
    pg = pl.program_id(1)
    rows_g = N_REP * QPAD

    @pl.when(pg == 0)
    def _():
        m_scr[...] = jnp.full_like(m_scr, -jnp.inf)
        l_scr[...] = jnp.zeros_like(l_scr)
        acc_scr[...] = jnp.zeros_like(acc_scr)

    thr = thr_ref[:, :1]
    q = q_ref[...]

    def update(scores, allowed, nblk, kget, vget):
        sel = scores >= thr
        if allowed is not None:
            sel = sel & allowed
        bias = jnp.where(sel, 0.0, -jnp.inf)
        bias = jnp.concatenate([bias] * N_HEADS, axis=0)
        lg = jnp.concatenate([
            jnp.concatenate([_dot_nt(q[g * rows_g:(g + 1) * rows_g], kget(kk, g)) for kk in range(nblk)], axis=1)
            for g in range(N_KV_HEADS)], axis=0) + bias
        m_old = m_scr[...]
        m_new = jnp.maximum(m_old, jnp.max(lg, axis=1, keepdims=True))
        m_safe = jnp.where(m_new == -jnp.inf, 0.0, m_new)
        alpha = jnp.exp(m_old - m_safe)
        p = jnp.exp(lg - m_safe)
        l_scr[...] = alpha * l_scr[...] + jnp.sum(p, axis=1, keepdims=True)
        m_scr[...] = m_new
        pb = p.astype(BF16)
        pv = jnp.concatenate([
            sum(_dot(pb[g * rows_g:(g + 1) * rows_g, kk * PAGE_SIZE:(kk + 1) * PAGE_SIZE], vget(kk, g))
                for kk in range(nblk))
            for g in range(N_KV_HEADS)], axis=0)
        acc_scr[...] = alpha * acc_scr[...] + pv

    head_rows = lambda g: pl.ds(g, PAGE_SIZE, stride=N_KV_HEADS)
    update(sp_ref[...], None, npg,
           lambda kk, g: kpages[kk][head_rows(g), :].astype(BF16),
           lambda kk, g: vpages[kk][head_rows(g), :].astype(BF16))

    @pl.when(pg == pl.num_programs(1) - 1)
    def _():
        shape = sn_ref.shape
        allowed = lax.broadcasted_iota(jnp.int32, shape, 1) <= lax.broadcasted_iota(jnp.int32, shape, 0)
        update(sn_ref[...], allowed, 1,
               lambda kk, g: kn_ref[:, g * HEAD_DIM:(g + 1) * HEAD_DIM],
               lambda kk, g: vn_ref[:, g * HEAD_DIM:(g + 1) * HEAD_DIM])
        o_ref[...] = acc_scr[...] / l_scr[...]


def _pad_queries(x, t_len):
    n, h, d = x.shape
    x = jnp.transpose(x.reshape(n // t_len, t_len, h, d), (0, 2, 1, 3))
    x = jnp.pad(x, ((0, 0), (0, 0), (0, QPAD - t_len), (0, 0)))
    return x.reshape(n // t_len, h * QPAD, d)


def _pad_new_keys(x, t_len):
    n, w = x.shape
    return jnp.pad(x.reshape(n // t_len, t_len, w), ((0, 0), (0, PAGE_SIZE - t_len), (0, 0)))


def _attn_sample(q_s, k_new, v_new, qi_s, wi_s, ki_new, cache_k, cache_v, cache_idx_k, page_table, page_base,
                 *, t_len):
    db, n_pages = page_table.shape
    past = n_pages * PAGE_SIZE
    top_k = min(TOPK_MAX, (past + t_len) // 4)
    kvw = N_KV_HEADS * HEAD_DIM
    sds = jax.ShapeDtypeStruct
    q = _pad_queries(q_s.reshape(-1, N_HEADS, HEAD_DIM), t_len)
    qi = _pad_queries(qi_s.reshape(-1, IDX_HEADS, IDX_DIM), t_len)
    ws = _pad_queries(wi_s.reshape(-1, IDX_HEADS, 1), t_len)
    ws = jnp.broadcast_to(ws, ws.shape[:2] + (PAGE_SIZE,))
    kn, vn, kin = (_pad_new_keys(a, t_len) for a in (k_new, v_new, ki_new))

    def idx_page_spec(per_step, kk):
        return pl.BlockSpec((None, IDX_DIM, PAGE_SIZE),
                            lambda s, g, pt: (page_base + pt[s, g * per_step + kk], 0, 0))

    def kv_page_spec(per_step, kk):
        return pl.BlockSpec((N_KV_HEADS * PAGE_SIZE, HEAD_DIM),
                            lambda s, g, pt: (page_base + pt[s, g * per_step + kk], 0))

    seq_spec = lambda r, c: pl.BlockSpec((None, r, c), lambda s, g, pt: (s, 0, 0))
    npi = IDX_PAGES_PER_STEP
    sp, sn = pl.pallas_call(
        _sample_scores_kernel,
        grid_spec=pltpu.PrefetchScalarGridSpec(
            num_scalar_prefetch=1, grid=(db, n_pages // npi),
            in_specs=[seq_spec(IDX_HEADS * QPAD, IDX_DIM), seq_spec(IDX_HEADS * QPAD, PAGE_SIZE),
                      seq_spec(PAGE_SIZE, IDX_DIM)] + [idx_page_spec(npi, kk) for kk in range(npi)],
            out_specs=[pl.BlockSpec((None, QPAD, npi * PAGE_SIZE), lambda s, g, pt: (s, 0, g)),
                       seq_spec(QPAD, PAGE_SIZE)]),
        out_shape=[sds((db, QPAD, past), F32), sds((db, QPAD, PAGE_SIZE), F32)],
        compiler_params=_cparams("parallel", "arbitrary"), name="sample_scores",
    )(page_table, qi, ws, kin, *([cache_idx_k] * npi))

    rows = db * QPAD
    rblk = min(rows, 8 * QPAD)
    assert rows % rblk == 0
    thr = pl.pallas_call(
        functools.partial(_sample_thr_kernel, past=past, top_k=top_k),
        grid=(rows // rblk,),
        in_specs=[_row_spec(rblk, past), _row_spec(rblk, PAGE_SIZE)],
        out_specs=_row_spec(rblk, LANES),
        out_shape=sds((rows, LANES), F32),
        scratch_shapes=[pltpu.VMEM((rblk, past + PAGE_SIZE), F32)],
        compiler_params=_cparams("parallel"), name="sample_threshold",
    )(sp.reshape(rows, past), sn.reshape(rows, PAGE_SIZE))

    npa = KV_PAGES_PER_STEP
    hq = N_HEADS * QPAD
    out = pl.pallas_call(
        _sample_attn_kernel,
        grid_spec=pltpu.PrefetchScalarGridSpec(
            num_scalar_prefetch=1, grid=(db, n_pages // npa),
            in_specs=[seq_spec(hq, HEAD_DIM),
                      pl.BlockSpec((None, QPAD, npa * PAGE_SIZE), lambda s, g, pt: (s, 0, g)),
                      seq_spec(QPAD, PAGE_SIZE), seq_spec(QPAD, LANES),
                      seq_spec(PAGE_SIZE, kvw), seq_spec(PAGE_SIZE, kvw)]
                     + [kv_page_spec(npa, kk) for kk in range(npa)] * 2,
            out_specs=seq_spec(hq, HEAD_DIM),
            scratch_shapes=[pltpu.VMEM((hq, 1), F32), pltpu.VMEM((hq, 1), F32), pltpu.VMEM((hq, HEAD_DIM), F32)]),
        out_shape=sds((db, hq, HEAD_DIM), F32),
        compiler_params=_cparams("parallel", "arbitrary"), name="sample_attn",
    )(page_table, q, sp, sn, thr.reshape(db, QPAD, LANES), kn, vn, *([cache_k] * npa), *([cache_v] * npa))
    out = out.reshape(db, N_HEADS, QPAD, HEAD_DIM)[:, :, :t_len]
    return jnp.transpose(out, (0, 2, 1, 3)).reshape(db * t_len, N_HEADS * HEAD_DIM)


ROW_TILE = 640
FFN_ROW_TILE = 512
FF_TILE = 512
MERGE_ROW_TILE = 320


def _layer(x_p, x_s, pos, dims, ck, cv, cik, page_base, s_re, s_im, page_table, p):
    b, s, db, t = dims
    mp, ms = b * s, db * t
    m, d = mp + ms, x_p.shape[1]
    bf = lambda w: w.astype(BF16)
    ssm_w = d // 2
    attn_w = N_HEADS * HEAD_DIM
    kv_w = N_KV_HEADS * HEAD_DIM
    sizes = [ssm_w, attn_w, kv_w, kv_w, IDX_HEADS * IDX_DIM, IDX_DIM, IDX_HEADS, d, d]
    off = [0]
    for n in sizes:
        off.append(off[-1] + n)
    w_in = p["w_in"]
    col = lambda a, z: bf(w_in[:, off[a]:off[z]])
    w_kw = jnp.pad(col(5, 7), ((0, 0), (0, LANES - IDX_DIM - IDX_HEADS)))

    x1, xn = _ffn(x_p, x_s, p["ffn1_norm"], bf(p["ffn1_w_gate"]), bf(p["ffn1_w_up"]), bf(p["ffn1_w_down"]),
                  p["mix_norm"], n_main_rows=mp, tail=ms, split_out=False, tm=FFN_ROW_TILE, tf=FF_TILE)
    pr = _projections(xn, col(0, 1), col(1, 2), col(2, 4), col(4, 5), w_kw, col(7, 9),
                      p["q_norm"], p["k_norm"], pos, tm=ROW_TILE)

    kmat, win, wout, lp = _ssm_prep(p["ssm_lambda_re"], p["ssm_lambda_im"], p["ssm_b_re"], p["ssm_b_im"],
                                    p["ssm_c_re"], p["ssm_c_im"], p["ssm_log_dt"])
    nblk = ssm_w // LANES
    d8 = p["ssm_d"].reshape(nblk, 1, LANES)
    za8, hl_p = _ssm(pr["u8"], 0, kmat, win, wout, lp, d8, jnp.zeros((nblk, b, 2 * STATES_PER_BLOCK), F32),
                     nseq=b, seq=s, t_steps=SSM_T, chain=True)
    za8, hl_s = _ssm(pr["u8"], mp, kmat, win, wout, lp, d8, _state_to_blocks(s_re, s_im),
                     nseq=1, seq=ms, t_steps=t, chain=False, za_prev=za8)

    b_out = _attn_prompt(pr["q_t"], pr["kb"], pr["v_t"], pr["qi_t"], pr["kib"], pr["kiw_t"],
                         batch=b, seq=s, m_out=m)
    b_s = _attn_sample(pr["q"][mp:], pr["kb"][mp:], bf(pr["v"][mp:]), pr["qi"][mp:],
                       pr["kiw"][mp:, IDX_DIM:IDX_DIM + IDX_HEADS], pr["kib"][mp:],
                       ck, cv, cik, page_table, page_base, t_len=t)
    b_out = lax.dynamic_update_slice(b_out, bf(b_s), (mp, 0))

    x2 = _merge(za8, b_out, pr["gates"], x1, bf(p["glu_w"]), p["glu_b"], bf(p["w_branch_a"]),
                bf(p["w_branch_b"]), bf(p["w_out"]), tm=MERGE_ROW_TILE)
    y_p, y_s = _ffn(x2, None, p["ffn2_norm"], bf(p["ffn2_w_gate"]), bf(p["ffn2_w_up"]), bf(p["ffn2_w_down"]),
                    n_main_rows=mp, tail=ms, split_out=True, tm=FFN_ROW_TILE, tf=FF_TILE)

    hp_re, hp_im = _blocks_to_state(hl_p)
    hs_re, hs_im = _blocks_to_state(hl_s)
    k, v, ki = pr["k"], pr["v"], pr["kiw"][:, :IDX_DIM]
    kvs = (N_KV_HEADS, HEAD_DIM)
    rows = (k[:mp].reshape(b, s, *kvs), v[:mp].reshape(b, s, *kvs), ki[:mp].reshape(b, s, IDX_DIM), hp_re, hp_im,
            k[mp:].reshape(db, t, *kvs), v[mp:].reshape(db, t, *kvs), ki[mp:].reshape(db, t, IDX_DIM), hs_re, hs_im)
    return y_p, y_s, rows


def kernel(x_prompt, x_sample, cache_k, cache_v, cache_idx_k, state_ssm_re, state_ssm_im, page_table,
           ffn1_norm, ffn1_w_gate, ffn1_w_up, ffn1_w_down, mix_norm, w_in, q_norm, k_norm,
           ssm_lambda_re, ssm_lambda_im, ssm_b_re, ssm_b_im, ssm_c_re, ssm_c_im, ssm_d, ssm_log_dt,
           glu_w, glu_b, w_branch_a, w_branch_b, w_out, ffn2_norm, ffn2_w_gate, ffn2_w_up, ffn2_w_down):
    b, s, d = x_prompt.shape
    db, t, _ = x_sample.shape
    depth, n_phys = cache_k.shape[:2]
    past = page_table.shape[1] * PAGE_SIZE
    x_p, x_s = x_prompt.reshape(b * s, d), x_sample.reshape(db * t, d)
    pos = jnp.concatenate([jnp.tile(jnp.arange(s), b), jnp.tile(past + jnp.arange(t), db)])
    ck = cache_k.reshape(-1, HEAD_DIM)
    cv = cache_v.reshape(-1, HEAD_DIM)
    cik = jnp.swapaxes(cache_idx_k, 2, 3).reshape(depth * n_phys, IDX_DIM, PAGE_SIZE)
    params = dict(
        ffn1_norm=ffn1_norm, ffn1_w_gate=ffn1_w_gate, ffn1_w_up=ffn1_w_up, ffn1_w_down=ffn1_w_down,
        mix_norm=mix_norm, w_in=w_in, q_norm=q_norm, k_norm=k_norm,
        ssm_lambda_re=ssm_lambda_re, ssm_lambda_im=ssm_lambda_im, ssm_b_re=ssm_b_re, ssm_b_im=ssm_b_im,
        ssm_c_re=ssm_c_re, ssm_c_im=ssm_c_im, ssm_d=ssm_d, ssm_log_dt=ssm_log_dt, glu_w=glu_w, glu_b=glu_b,
        w_branch_a=w_branch_a, w_branch_b=w_branch_b, w_out=w_out,
        ffn2_norm=ffn2_norm, ffn2_w_gate=ffn2_w_gate, ffn2_w_up=ffn2_w_up, ffn2_w_down=ffn2_w_down)
    new = [[] for _ in range(10)]
    for l in range(depth):
        p = {name: w[l] for name, w in params.items()}
        x_p, x_s, rows = _layer(x_p, x_s, pos, (b, s, db, t), ck, cv, cik, l * n_phys,
                                state_ssm_re[l], state_ssm_im[l], page_table, p)
        for lst, r in zip(new, rows):
            lst.append(r)
    return (x_p.reshape(b, s, d), x_s.reshape(db, t, d)) + tuple(jnp.stack(lst) for lst in new)
```

```python
import functools
import math

import jax
import jax.numpy as jnp
from jax import lax
from jax.experimental import pallas as pl
from jax.experimental.pallas import tpu as pltpu

F32 = jnp.float32
BF16 = jnp.bfloat16

SSM_GROUP = 16
SSM_STATE = 64
N_HEADS = 8
HEAD_DIM = 128
N_KV_HEADS = 2
N_REP = N_HEADS // N_KV_HEADS
ROT_DIM = HEAD_DIM // 4
ROPE_THETA = 500000.0
IDX_HEADS = 16
IDX_DIM = 64
IDX_ROT_DIM = IDX_DIM // 4
TOPK_MAX = 256
PAGE_SIZE = 128
FFN_RES = 0.5
EPS = 1e-6

LANES = 128
SUBLANES = 8
VMEM_LIMIT_BYTES = 56 * 1024 * 1024

GROUPS_PER_BLOCK = LANES // SSM_GROUP
STATES_PER_BLOCK = GROUPS_PER_BLOCK * SSM_STATE
SSM_T = 8

INT_MIN = -(2 ** 31)


def _cparams(*sem):
    return pltpu.CompilerParams(dimension_semantics=sem, vmem_limit_bytes=VMEM_LIMIT_BYTES)


def _rms(x, g):
    return x * lax.rsqrt(jnp.mean(x * x, axis=-1, keepdims=True) + EPS) * g


def _dot(a, b):
    return jnp.dot(a, b, preferred_element_type=F32)


def _dot_nt(a, b):
    return lax.dot_general(a, b, (((1,), (1,)), ((), ())), preferred_element_type=F32)


def _ffn_kernel(*refs, nf, n_main, tm, tail, two_src, split_out, with_next_norm):
    refs = list(refs)
    x_main_ref = refs.pop(0)
    x_tail_ref = refs.pop(0) if two_src else x_main_ref
    g_ref, wg_ref, wu_ref, wd_ref = (refs.pop(0) for _ in range(4))
    g2_ref = refs.pop(0) if with_next_norm else None
    y_main_ref = refs.pop(0)
    y_tail_ref = refs.pop(0) if split_out else y_main_ref
    n2_ref = refs.pop(0) if with_next_norm else None
    xn_ref, acc_ref = refs
    i, f = pl.program_id(0), pl.program_id(1)

    def run(rows, x_ref, y_ref):
        @pl.when(f == 0)
        def _():
            xn_ref[:rows] = _rms(x_ref[:rows], g_ref[...]).astype(BF16)
            acc_ref[:rows] = jnp.zeros((rows, acc_ref.shape[1]), F32)

        xn = xn_ref[:rows]
        a = _dot(xn, wg_ref[...])
        b = _dot(xn, wu_ref[...])
        h = (a * jax.nn.sigmoid(a) * b).astype(BF16)
        acc_ref[:rows] += _dot(h, wd_ref[...])

        @pl.when(f == nf - 1)
        def _():
            y = x_ref[:rows] + FFN_RES * acc_ref[:rows]
            y_ref[:rows] = y
            if with_next_norm:
                n2_ref[:rows] = _rms(y, g2_ref[...]).astype(BF16)

    pl.when(i < n_main)(lambda: run(tm, x_main_ref, y_main_ref))
    pl.when(i == n_main)(lambda: run(tail, x_tail_ref, y_tail_ref))


def _ffn(x_main, x_tail, g, wg, wu, wd, g2=None, *, n_main_rows, tail, split_out, tm, tf):
    d = x_main.shape[1]
    nf = wg.shape[1] // tf
    n_main = n_main_rows // tm
    m = n_main_rows + tail
    two_src = x_tail is not None
    with_next = g2 is not None
    clamp = lambda i, f: (jnp.minimum(i, n_main - 1), 0)
    rows = lambda i, f: (i, 0)
    first = lambda i, f: (0, 0)
    sds = jax.ShapeDtypeStruct
    in_specs = [pl.BlockSpec((tm, d), clamp if two_src else rows)]
    args = [x_main]
    if two_src:
        in_specs.append(pl.BlockSpec((tail, d), first))
        args.append(x_tail)
    in_specs += [pl.BlockSpec((1, d), first), pl.BlockSpec((d, tf), lambda i, f: (0, f)),
                 pl.BlockSpec((d, tf), lambda i, f: (0, f)), pl.BlockSpec((tf, d), lambda i, f: (f, 0))]
    args += [g.reshape(1, d), wg, wu, wd]
    if with_next:
        in_specs.append(pl.BlockSpec((1, d), first))
        args.append(g2.reshape(1, d))
    if split_out:
        out_shape = [sds((n_main_rows, d), F32), sds((tail, d), F32)]
        out_specs = [pl.BlockSpec((tm, d), clamp), pl.BlockSpec((tail, d), first)]
    else:
        out_shape = [sds((m, d), F32)]
        out_specs = [pl.BlockSpec((tm, d), rows)]
    if with_next:
        out_shape.append(sds((m, d), BF16))
        out_specs.append(pl.BlockSpec((tm, d), rows))
    return pl.pallas_call(
        functools.partial(_ffn_kernel, nf=nf, n_main=n_main, tm=tm, tail=tail, two_src=two_src,
                          split_out=split_out, with_next_norm=with_next),
        grid=(n_main + 1, nf),
        in_specs=in_specs,
        out_specs=out_specs,
        out_shape=out_shape,
        scratch_shapes=[pltpu.VMEM((tm, d), BF16), pltpu.VMEM((tm, d), F32)],
        compiler_params=_cparams("arbitrary", "arbitrary"),
        name="ffn" + ("_norm" if with_next else ""),
    )(*args)


def _rope_tables(pos, rot_dim, width, tile):
    half = rot_dim // 2
    m = pos.shape[0]
    freqs = ROPE_THETA ** (-jnp.arange(half, dtype=F32) * 2.0 / rot_dim)
    ang = pos.astype(F32)[:, None] * freqs[None, :]
    cos, sin = jnp.cos(ang), jnp.sin(ang)
    zh = jnp.zeros((m, half), F32)
    rest = width - rot_dim
    c = jnp.concatenate([cos, cos, jnp.ones((m, rest), F32)], axis=1)
    s1 = jnp.concatenate([-sin, zh, jnp.zeros((m, rest), F32)], axis=1)
    s2 = jnp.concatenate([zh, sin, jnp.zeros((m, rest), F32)], axis=1)
    if tile:
        reps = LANES // width
        return tuple(jnp.tile(t, (1, reps)) for t in (c, s1, s2))
    pad = LANES - width
    return (jnp.pad(c, ((0, 0), (0, pad)), constant_values=1.0),
            jnp.pad(s1, ((0, 0), (0, pad))), jnp.pad(s2, ((0, 0), (0, pad))))


def _rope(x, c, s1, s2, half):
    return x * c + pltpu.roll(x, LANES - half, 1) * s1 + pltpu.roll(x, half, 1) * s2


def _u_proj_kernel(xn_ref, w_ref, u_ref):
    z = _dot(xn_ref[...], w_ref[...])
    for j in range(z.shape[1] // LANES):
        u_ref[j] = z[:, j * LANES:(j + 1) * LANES]


def _q_proj_kernel(xn_ref, w_ref, g_ref, c_ref, s1_ref, s2_ref, qn_ref, qt_ref):
    z = _dot(xn_ref[...], w_ref[...])
    c, s1, s2, g = c_ref[...], s1_ref[...], s2_ref[...], g_ref[...]
    heads = []
    for h in range(N_HEADS):
        x = _rms(z[:, h * HEAD_DIM:(h + 1) * HEAD_DIM], g)
        heads.append(_rope(x, c, s1, s2, ROT_DIM // 2) * (HEAD_DIM ** -0.5))
    q = jnp.concatenate(heads, axis=1)
    qn_ref[...] = q.astype(BF16)
    qt_ref[...] = q.T.astype(BF16)


def _kv_proj_kernel(xn_ref, w_ref, g_ref, c_ref, s1_ref, s2_ref, k_ref, kb_ref, v_ref, vt_ref):
    z = _dot(xn_ref[...], w_ref[...])
    c, s1, s2, g = c_ref[...], s1_ref[...], s2_ref[...], g_ref[...]
    kw = N_KV_HEADS * HEAD_DIM
    heads = []
    for h in range(N_KV_HEADS):
        x = _rms(z[:, h * HEAD_DIM:(h + 1) * HEAD_DIM], g)
        heads.append(_rope(x, c, s1, s2, ROT_DIM // 2))
    k = jnp.concatenate(heads, axis=1)
    v = z[:, kw:]
    k_ref[...] = k
    kb_ref[...] = k.astype(BF16)
    v_ref[...] = v
    vt_ref[...] = v.T.astype(BF16)


def _qi_proj_kernel(xn_ref, w_ref, c_ref, s1_ref, s2_ref, qn_ref, qt_ref):
    z = _dot(xn_ref[...], w_ref[...])
    c, s1, s2 = c_ref[...], s1_ref[...], s2_ref[...]
    cols = []
    for j in range(z.shape[1] // LANES):
        x = z[:, j * LANES:(j + 1) * LANES]
        cols.append(_rope(x, c, s1, s2, IDX_ROT_DIM // 2) * (IDX_DIM ** -0.5))
    q = jnp.concatenate(cols, axis=1)
    qn_ref[...] = q.astype(BF16)
    qt_ref[...] = q.T.astype(BF16)


def _kiwi_proj_kernel(xn_ref, w_ref, c_ref, s1_ref, s2_ref, scale_ref, o_ref, ot_ref, kb_ref):
    z = _dot(xn_ref[...], w_ref[...])
    y = _rope(z, c_ref[...], s1_ref[...], s2_ref[...], IDX_ROT_DIM // 2) * scale_ref[...]
    o_ref[...] = y
    ot_ref[...] = y.T
    kb_ref[...] = y[:, :IDX_DIM].astype(BF16)


def _gate_proj_kernel(xn_ref, w_ref, o_ref):
    o_ref[...] = jax.nn.sigmoid(_dot(xn_ref[...], w_ref[...])).astype(BF16)


def _row_spec(tm, n):
    return pl.BlockSpec((tm, n), lambda i: (i, 0))


def _full_spec(shape):
    return pl.BlockSpec(shape, lambda i: (0,) * len(shape))


def _projections(xn, w_u, w_q, w_kv, w_qi, w_kw, w_g, q_norm, k_norm, pos, *, tm):
    m, d = xn.shape
    grid = (m // tm,)
    hd_tabs = _rope_tables(pos, ROT_DIM, HEAD_DIM, True)
    ix_tabs = _rope_tables(pos, IDX_ROT_DIM, IDX_DIM, True)
    kw_tabs = _rope_tables(pos, IDX_ROT_DIM, IDX_DIM, False)
    tab_specs = [_row_spec(tm, LANES)] * 3
    xs = _row_spec(tm, d)
    sds = jax.ShapeDtypeStruct
    par = _cparams("parallel")

    nu = w_u.shape[1]
    u8 = pl.pallas_call(
        _u_proj_kernel, grid=grid, in_specs=[xs, _full_spec(w_u.shape)],
        out_specs=pl.BlockSpec((nu // LANES, tm, LANES), lambda i: (0, i, 0)),
        out_shape=sds((nu // LANES, m, LANES), F32), compiler_params=par, name="proj_u",
    )(xn, w_u)

    nq = w_q.shape[1]
    q_nat, q_t = pl.pallas_call(
        _q_proj_kernel, grid=grid,
        in_specs=[xs, _full_spec(w_q.shape), _full_spec((1, HEAD_DIM))] + tab_specs,
        out_specs=[_row_spec(tm, nq), pl.BlockSpec((nq, tm), lambda i: (0, i))],
        out_shape=[sds((m, nq), BF16), sds((nq, m), BF16)], compiler_params=par, name="proj_q",
    )(xn, w_q, q_norm.reshape(1, HEAD_DIM), *hd_tabs)

    kvw = N_KV_HEADS * HEAD_DIM
    k, kb, v, v_t = pl.pallas_call(
        _kv_proj_kernel, grid=grid,
        in_specs=[xs, _full_spec(w_kv.shape), _full_spec((1, HEAD_DIM))] + tab_specs,
        out_specs=[_row_spec(tm, kvw), _row_spec(tm, kvw), _row_spec(tm, kvw),
                   pl.BlockSpec((kvw, tm), lambda i: (0, i))],
        out_shape=[sds((m, kvw), F32), sds((m, kvw), BF16), sds((m, kvw), F32), sds((kvw, m), BF16)],
        compiler_params=par, name="proj_kv",
    )(xn, w_kv, k_norm.reshape(1, HEAD_DIM), *hd_tabs)

    nqi = w_qi.shape[1]
    qi_nat, qi_t = pl.pallas_call(
        _qi_proj_kernel, grid=grid,
        in_specs=[xs, _full_spec(w_qi.shape)] + tab_specs,
        out_specs=[_row_spec(tm, nqi), pl.BlockSpec((nqi, tm), lambda i: (0, i))],
        out_shape=[sds((m, nqi), BF16), sds((nqi, m), BF16)], compiler_params=par, name="proj_qi",
    )(xn, w_qi, *ix_tabs)

    lane = jnp.arange(LANES)
    kw_scale = jnp.where(lane < IDX_DIM, 1.0, IDX_HEADS ** -0.5).astype(F32).reshape(1, LANES)
    kiw, kiw_t, kib = pl.pallas_call(
        _kiwi_proj_kernel, grid=grid,
        in_specs=[xs, _full_spec(w_kw.shape)] + tab_specs + [_full_spec((1, LANES))],
        out_specs=[_row_spec(tm, LANES), pl.BlockSpec((LANES, tm), lambda i: (0, i)), _row_spec(tm, IDX_DIM)],
        out_shape=[sds((m, LANES), F32), sds((LANES, m), F32), sds((m, IDX_DIM), BF16)],
        compiler_params=par, name="proj_kiwi",
    )(xn, w_kw, *kw_tabs, kw_scale)

    ng = w_g.shape[1]
    tn = 1024
    gates = pl.pallas_call(
        _gate_proj_kernel, grid=(m // tm, ng // tn),
        in_specs=[pl.BlockSpec((tm, d), lambda i, j: (i, 0)), pl.BlockSpec((d, tn), lambda i, j: (0, j))],
        out_specs=pl.BlockSpec((tm, tn), lambda i, j: (i, j)),
        out_shape=sds((m, ng), BF16), compiler_params=_cparams("parallel", "parallel"), name="proj_gates",
    )(xn, w_g)
    return dict(u8=u8, q=q_nat, q_t=q_t, k=k, kb=kb, v=v, v_t=v_t, qi=qi_nat, qi_t=qi_t,
                kiw=kiw, kiw_t=kiw_t, kib=kib, gates=gates)


def _ssm_prep_kernel(lre_ref, lim_ref, ldt_ref, bre_ref, bim_ref, cre_ref, cim_ref,
                     k_ref, win_ref, wout_ref, lp_ref):
    ns = STATES_PER_BLOCK
    lre, lim = lre_ref[...], lim_ref[...]
    dt = jnp.exp(ldt_ref[...])
    a, th = lre * dt, lim * dt

    def power(l):
        mag = jnp.exp(a * float(l))
        return mag * jnp.cos(th * float(l)), mag * jnp.sin(th * float(l))

    pw = [power(l) for l in range(SSM_T + 1)]
    xr, xi = pw[1][0] - 1.0, pw[1][1]
    den = lre * lre + lim * lim
    cr, ci = (xr * lre + xi * lim) / den, (xi * lre - xr * lim) / den
    bre, bim = bre_ref[...], bim_ref[...]
    bbr, bbi = bre * cr - bim * ci, bre * ci + bim * cr
    cre, cim = cre_ref[...], cim_ref[...]
    hi = lax.Precision.HIGHEST
    nt = (((1,), (1,)), ((), ()))
    lag = []
    for l in range(SSM_T):
        pr, pi = pw[l]
        blr, bli = bbr * pr - bbi * pi, bbr * pi + bbi * pr
        m = (lax.dot_general(blr, cre, nt, precision=hi, preferred_element_type=F32)
             - lax.dot_general(bli, cim, nt, precision=hi, preferred_element_type=F32))
        lag.append(m.astype(BF16))
        t = SSM_T - 1 - l
        win_ref[t * LANES:(t + 1) * LANES, :ns] = blr.astype(BF16)
        win_ref[t * LANES:(t + 1) * LANES, ns:] = bli.astype(BF16)
    zero = jnp.zeros((LANES, LANES), BF16)
    for t in range(SSM_T):
        pr, pi = pw[t + 1]
        wout_ref[t * LANES:(t + 1) * LANES, :ns] = (cre * pr - cim * pi).astype(BF16)
        wout_ref[t * LANES:(t + 1) * LANES, ns:] = (-(cre * pi + cim * pr)).astype(BF16)
        lp_ref[t:t + 1, :ns] = pr
        lp_ref[t:t + 1, ns:] = pi
        for t2 in range(SSM_T):
            k_ref[t * LANES:(t + 1) * LANES, t2 * LANES:(t2 + 1) * LANES] = lag[t2 - t] if t2 >= t else zero


def _block_diag_groups(w):
    g, h, p = w.shape
    nb = g // GROUPS_PER_BLOCK
    w = w.reshape(nb, GROUPS_PER_BLOCK, h, p)
    eye = jnp.eye(GROUPS_PER_BLOCK, dtype=w.dtype)
    out = w[:, :, :, None, :] * eye[None, :, None, :, None]
    return out.reshape(nb, GROUPS_PER_BLOCK * h, GROUPS_PER_BLOCK * p)


def _ssm_prep(lam_re, lam_im, b_re, b_im, c_re, c_im, log_dt):
    g, p = lam_re.shape
    nb = g // GROUPS_PER_BLOCK
    ns = STATES_PER_BLOCK
    tl = SSM_T * LANES
    vec = lambda a: a.reshape(nb, 1, ns)
    ldt = vec(jnp.broadcast_to(log_dt[:, None], (g, p)))
    bt = lambda b: _block_diag_groups(jnp.swapaxes(b, 1, 2))
    vspec = pl.BlockSpec((None, 1, ns), lambda j: (j, 0, 0))
    mspec = pl.BlockSpec((None, LANES, ns), lambda j: (j, 0, 0))
    sds = jax.ShapeDtypeStruct
    return pl.pallas_call(
        _ssm_prep_kernel, grid=(nb,),
        in_specs=[vspec, vspec, vspec, mspec, mspec, mspec, mspec],
        out_specs=[pl.BlockSpec((None, tl, tl), lambda j: (j, 0, 0)),
                   pl.BlockSpec((None, tl, 2 * ns), lambda j: (j, 0, 0)),
                   pl.BlockSpec((None, tl, 2 * ns), lambda j: (j, 0, 0)),
                   pl.BlockSpec((None, SSM_T, 2 * ns), lambda j: (j, 0, 0))],
        out_shape=[sds((nb, tl, tl), BF16), sds((nb, tl, 2 * ns), BF16), sds((nb, tl, 2 * ns), BF16),
                   sds((nb, SSM_T, 2 * ns), F32)],
        compiler_params=_cparams("parallel"), name="ssm_prep",
    )(vec(lam_re), vec(lam_im), ldt, bt(b_re), bt(b_im), _block_diag_groups(c_re), _block_diag_groups(c_im))


def _ssm_kernel(u_ref, k_ref, win_ref, wout_ref, lp_ref, d_ref, h0s_ref, za_ref, hlp_ref, hls_ref, x_scr, hs_scr,
                *, nseq, seq, n_s, t_s):
    ns = STATES_PER_BLOCK
    nk = ns // LANES
    d = d_ref[...]
    split = lambda a: [a[:, k * LANES:(k + 1) * LANES] for k in range(2 * nk)]

    def gather(row0, n_rows, t_steps):
        return [u_ref[pl.ds(row0 + t, n_rows, stride=t_steps), :] for t in range(t_steps)]

    def advance(h, x, t_steps):
        lam = [lp_ref[t_steps - 1:t_steps, k * LANES:(k + 1) * LANES] for k in range(2 * nk)]
        re = [lam[k] * h[k] - lam[nk + k] * h[nk + k] + x[k] for k in range(nk)]
        im = [lam[k] * h[nk + k] + lam[nk + k] * h[k] + x[nk + k] for k in range(nk)]
        return re + im

    def emit(row0, n_rows, t_steps, cols, hs, kmat, wout):
        u = jnp.concatenate(cols, axis=1).astype(BF16)
        y = _dot(u, kmat) + _dot_nt(hs.astype(BF16), wout)
        for t in range(t_steps):
            yt = y[:, t * LANES:(t + 1) * LANES] + d * cols[t]
            za_ref[pl.ds(row0 + t, n_rows, stride=t_steps), :] = jax.nn.gelu(yt, approximate=True)

    c_per = seq // SSM_T
    for b in range(nseq):
        u = jnp.concatenate(gather(b * seq, c_per, SSM_T), axis=1).astype(BF16)
        x = _dot(u, win_ref[...])
        for k in range(2 * nk):
            x_scr[k, b * c_per:(b + 1) * c_per, :] = x[:, k * LANES:(k + 1) * LANES]

    def step(c, h):
        rows = pl.ds(c, nseq, stride=c_per)
        for k in range(2 * nk):
            hs_scr.at[k][rows, :] = h[k]
        return tuple(advance(h, [x_scr.at[k][rows, :] for k in range(2 * nk)], SSM_T))

    h_last = lax.fori_loop(0, c_per, step, tuple(jnp.zeros((nseq, LANES), F32) for _ in range(2 * nk)))
    hlp_ref[...] = jnp.concatenate(list(h_last), axis=1)
    for b in range(nseq):
        hs = jnp.concatenate([hs_scr[k, b * c_per:(b + 1) * c_per, :] for k in range(2 * nk)], axis=1)
        emit(b * seq, c_per, SSM_T, gather(b * seq, c_per, SSM_T), hs, k_ref[...], wout_ref[...])

    row0, tl = nseq * seq, t_s * LANES
    cols = gather(row0, n_s, t_s)
    u = jnp.concatenate(cols, axis=1).astype(BF16)
    x = _dot(u, win_ref[(SSM_T - t_s) * LANES:, :])
    h0 = h0s_ref[...]
    hls_ref[...] = jnp.concatenate(advance(split(h0), split(x), t_s), axis=1)
    emit(row0, n_s, t_s, cols, h0, k_ref[:tl, :tl], wout_ref[:tl, :])


def _ssm(u8, kmat, win, wout, lp, d8, h0s, *, nseq, seq, n_s, t_s):
    nb, m_total, _ = u8.shape
    ns = STATES_PER_BLOCK
    assert m_total == nseq * seq + n_s * t_s and seq % SSM_T == 0 and t_s <= SSM_T
    rows = nseq * seq // SSM_T
    tl = SSM_T * LANES
    sds = jax.ShapeDtypeStruct
    blk = lambda r, c: pl.BlockSpec((None, r, c), lambda j: (j, 0, 0))
    return pl.pallas_call(
        functools.partial(_ssm_kernel, nseq=nseq, seq=seq, n_s=n_s, t_s=t_s),
        grid=(nb,),
        in_specs=[blk(m_total, LANES), blk(tl, tl), blk(tl, 2 * ns), blk(tl, 2 * ns), blk(SSM_T, 2 * ns),
                  blk(1, LANES), blk(n_s, 2 * ns)],
        out_specs=[blk(m_total, LANES), blk(nseq, 2 * ns), blk(n_s, 2 * ns)],
        out_shape=[sds((nb, m_total, LANES), F32), sds((nb, nseq, 2 * ns), F32), sds((nb, n_s, 2 * ns), F32)],
        scratch_shapes=[pltpu.VMEM((2 * ns // LANES, rows, LANES), F32)] * 2,
        compiler_params=_cparams("parallel"), name="ssm",
    )(u8, kmat, win, wout, lp, d8, h0s)


def _state_to_blocks(re, im):
    n, g, p = re.shape
    nb = g // GROUPS_PER_BLOCK
    f = lambda a: jnp.transpose(a.reshape(n, nb, GROUPS_PER_BLOCK * p), (1, 0, 2))
    return jnp.concatenate([f(re), f(im)], axis=-1)


def _blocks_to_state(h):
    nb, n, w = h.shape
    ns = w // 2
    f = lambda a: jnp.transpose(a, (1, 0, 2)).reshape(n, nb * GROUPS_PER_BLOCK, ns // GROUPS_PER_BLOCK)
    return f(h[..., :ns]), f(h[..., ns:])


def _key_to_float(c):
    return lax.bitcast_convert_type(c ^ ((c >> 31) & 0x7FFFFFFF), F32)


def _kth_largest(count_ge, shape, top_k):
    def body(it, u):
        bit = jnp.left_shift(jnp.int32(1), 31 - it)
        cand = u | bit
        return jnp.where(count_ge(_key_to_float(cand ^ INT_MIN)) >= top_k, cand, u)
    return _key_to_float(lax.fori_loop(0, 32, body, jnp.zeros(shape, jnp.int32)) ^ INT_MIN)


def _attn_prompt_kernel(qt_ref, kb_ref, vt_ref, qit_ref, kib_ref, wt_ref, o_ref, s_scr, bias_scr,
                        *, seq, top_k, key_chunk):
    qb = pl.program_id(1)
    tq = o_ref.shape[0]

    def run(nk):
        kib = kib_ref[:nk]
        s = jnp.zeros((nk, tq), F32)
        for h in range(IDX_HEADS):
            d = _dot(kib, qit_ref[h * IDX_DIM:(h + 1) * IDX_DIM, :])
            s = s + jnp.maximum(d, 0.0) * wt_ref[IDX_DIM + h:IDX_DIM + h + 1, :]
        kpos = lax.broadcasted_iota(jnp.int32, (nk, tq), 0)
        qpos = qb * tq + lax.broadcasted_iota(jnp.int32, (nk, tq), 1)
        allowed = kpos <= qpos
        s_scr[:nk] = jnp.where(allowed, s, -jnp.inf)

        def count_ge(t):
            n_acc = 8
            acc = [jnp.zeros((SUBLANES, tq), jnp.int32) for _ in range(n_acc)]
            for j in range(nk // SUBLANES):
                rows = s_scr[j * SUBLANES:(j + 1) * SUBLANES]
                acc[j % n_acc] = acc[j % n_acc] + (rows >= t).astype(jnp.int32)
            while len(acc) > 1:
                acc = [a + b for a, b in zip(acc[::2], acc[1::2])]
            return jnp.sum(acc[0], axis=0, keepdims=True)

        thr = _kth_largest(count_ge, (1, tq), top_k)
        few = qpos < top_k - 1
        bias_scr[:nk] = jnp.where(allowed & ((s_scr[:nk] >= thr) | few), 0.0, -jnp.inf)

        outs = []
        for g in range(N_KV_HEADS):
            kg = kb_ref[:nk, g * HEAD_DIM:(g + 1) * HEAD_DIM]
            vtg = vt_ref[g * HEAD_DIM:(g + 1) * HEAD_DIM, :nk]
            for r in range(N_REP):
                h = g * N_REP + r
                lg = _dot(kg, qt_ref[h * HEAD_DIM:(h + 1) * HEAD_DIM, :]) + bias_scr[:nk]
                p = jnp.exp(lg - jnp.max(lg, axis=0, keepdims=True))
                den = jnp.sum(p, axis=0, keepdims=True)
                outs.append(_dot(vtg, p.astype(BF16)) / den)
        o_ref[...] = jnp.concatenate(outs, axis=0).T.astype(BF16)

    n_var = seq // key_chunk
    need = (qb * tq + tq + key_chunk - 1) // key_chunk
    for v in range(1, n_var + 1):
        pl.when(need == v)(functools.partial(run, v * key_chunk))


def _attn_prompt(q_t, kb, v_t, qi_t, kib, kiw_t, *, batch, seq, m_out, tq=128, key_chunk=256):
    top_k = min(TOPK_MAX, seq // 4)
    nq = seq // tq
    assert seq % key_chunk == 0 and key_chunk % tq == 0
    aw = N_HEADS * HEAD_DIM
    qcol = lambda n: pl.BlockSpec((n, tq), lambda b, i: (0, b * nq + i))
    return pl.pallas_call(
        functools.partial(_attn_prompt_kernel, seq=seq, top_k=top_k, key_chunk=key_chunk),
        grid=(batch, nq),
        in_specs=[qcol(aw),
                  pl.BlockSpec((seq, N_KV_HEADS * HEAD_DIM), lambda b, i: (b, 0)),
                  pl.BlockSpec((N_KV_HEADS * HEAD_DIM, seq), lambda b, i: (0, b)),
                  qcol(IDX_HEADS * IDX_DIM),
                  pl.BlockSpec((seq, IDX_DIM), lambda b, i: (b, 0)),
                  qcol(LANES)],
        out_specs=pl.BlockSpec((tq, aw), lambda b, i: (b * nq + i, 0)),
        out_shape=jax.ShapeDtypeStruct((m_out, aw), BF16),
        scratch_shapes=[pltpu.VMEM((seq, tq), F32), pltpu.VMEM((seq, tq), F32)],
        compiler_params=_cparams("parallel", "parallel"), name="attn_prompt",
    )(q_t, kb, v_t, qi_t, kib, kiw_t)


def _merge_kernel(za_ref, bm_ref, bt_ref, g_ref, x_ref, gw_ref, gb_ref, wa_ref, wb_ref, wo_ref, o_ref,
                  *, n_main, tm, tail):
    d = x_ref.shape[1]

    def run(rows, b_ref):
        za = jnp.concatenate([za_ref[j, :rows] for j in range(za_ref.shape[0])], axis=1)
        a_out = za * jax.nn.sigmoid(_dot(za.astype(BF16), gw_ref[...]) + gb_ref[...])
        merged = (g_ref[:rows, :d] * _dot(a_out.astype(BF16), wa_ref[...])
                  + g_ref[:rows, d:] * _dot(b_ref[:rows], wb_ref[...]))
        o_ref[:rows] = x_ref[:rows] + _dot(merged.astype(BF16), wo_ref[...])

    i = pl.program_id(0)
    pl.when(i < n_main)(lambda: run(tm, bm_ref))
    pl.when(i == n_main)(lambda: run(tail, bt_ref))


def _merge(za8, b_main, b_tail, gates, x, glu_w, glu_b, wa, wb, wo, *, tm):
    m, d = x.shape
    nb = za8.shape[0]
    n_main_rows, aw = b_main.shape
    tail = b_tail.shape[0]
    n_main = n_main_rows // tm
    assert n_main * tm == n_main_rows and n_main_rows + tail == m and tail <= tm
    resident = lambda shape: pl.BlockSpec(shape, lambda i: (0,) * len(shape), pipeline_mode=pl.Buffered(1))
    return pl.pallas_call(
        functools.partial(_merge_kernel, n_main=n_main, tm=tm, tail=tail), grid=(n_main + 1,),
        in_specs=[pl.BlockSpec((nb, tm, LANES), lambda i: (0, i, 0)),
                  pl.BlockSpec((tm, aw), lambda i: (jnp.minimum(i, n_main - 1), 0)),
                  pl.BlockSpec((tail, aw), lambda i: (0, 0)),
                  _row_spec(tm, 2 * d), _row_spec(tm, d), resident(glu_w.shape), resident((1, glu_w.shape[1])),
                  resident(wa.shape), resident(wb.shape), resident(wo.shape)],
        out_specs=_row_spec(tm, d),
        out_shape=jax.ShapeDtypeStruct((m, d), F32),
        compiler_params=_cparams("arbitrary"), name="merge",
    )(za8, b_main, b_tail, gates, x, glu_w, glu_b.reshape(1, -1), wa, wb, wo)


QPAD = SUBLANES
IDX_PAGES_PER_STEP = 16
KV_PAGES_PER_STEP = 8


def _sample_scores_kernel(pt_ref, qs_ref, ws_ref, kn_ref, *refs):
    pages, (sp_ref, sn_ref) = refs[:IDX_PAGES_PER_STEP], refs[IDX_PAGES_PER_STEP:]
    qs, ws = qs_ref[...], ws_ref[...]

    def scores(d):
        r = jnp.maximum(d, 0.0) * ws
        return r.reshape(IDX_HEADS, QPAD, r.shape[1]).sum(axis=0)

    for kk in range(IDX_PAGES_PER_STEP):
        sp_ref[:, kk * PAGE_SIZE:(kk + 1) * PAGE_SIZE] = scores(_dot(qs, pages[kk][...].astype(BF16)))
    sn_ref[...] = scores(_dot_nt(qs, kn_ref[...]))


def _sample_thr_kernel(sp_ref, sn_ref, thr_ref, s_scr, *, past, top_k):
    s = jnp.concatenate([sp_ref[...], sn_ref[...]], axis=1)
    shape = s.shape
    kpos = lax.broadcasted_iota(jnp.int32, shape, 1)
    qidx = lax.broadcasted_iota(jnp.int32, shape, 0) % QPAD
    s_scr[...] = jnp.where(kpos <= past + qidx, s, -jnp.inf)

    def count_ge(t):
        return jnp.sum((s_scr[...] >= t).astype(jnp.int32), axis=1, keepdims=True)

    thr = _kth_largest(count_ge, (shape[0], 1), top_k)
    thr_ref[...] = jnp.broadcast_to(thr, thr_ref.shape)


def _sample_attn_kernel(pt_ref, q_ref, sp_ref, sn_ref, thr_ref, kn_ref, vn_ref, *refs):
    npg = KV_PAGES_PER_STEP
    kpages, vpages = refs[:npg], refs[npg:2 * npg]
    o_ref, m_scr, l_scr, acc_scr = refs[2 * npg:]
    pg = pl.program_id(1)
    rows_g = N_REP * QPAD

    @pl.when(pg == 0)
    def _():
        m_scr[...] = jnp.full_like(m_scr, -jnp.inf)
        l_scr[...] = jnp.zeros_like(l_scr)
        acc_scr[...] = jnp.zeros_like(acc_scr)

    thr = thr_ref[:, :1]
    q = q_ref[...]

    def update(scores, allowed, nblk, kget, vget):
        sel = scores >= thr
        if allowed is not None:
            sel = sel & allowed
        bias = jnp.where(sel, 0.0, -jnp.inf)
        bias = jnp.concatenate([bias] * N_HEADS, axis=0)
        lg = jnp.concatenate([
            jnp.concatenate([_dot_nt(q[g * rows_g:(g + 1) * rows_g], kget(kk, g)) for kk in range(nblk)], axis=1)
            for g in range(N_KV_HEADS)], axis=0) + bias
        m_old = m_scr[...]
        m_new = jnp.maximum(m_old, jnp.max(lg, axis=1, keepdims=True))
        m_safe = jnp.where(m_new == -jnp.inf, 0.0, m_new)
        alpha = jnp.exp(m_old - m_safe)
        p = jnp.exp(lg - m_safe)
        l_scr[...] = alpha * l_scr[...] + jnp.sum(p, axis=1, keepdims=True)
        m_scr[...] = m_new
        pb = p.astype(BF16)
        pv = jnp.concatenate([
            sum(_dot(pb[g * rows_g:(g + 1) * rows_g, kk * PAGE_SIZE:(kk + 1) * PAGE_SIZE], vget(kk, g))
                for kk in range(nblk))
            for g in range(N_KV_HEADS)], axis=0)
        acc_scr[...] = alpha * acc_scr[...] + pv

    head_rows = lambda g: pl.ds(g, PAGE_SIZE, stride=N_KV_HEADS)
    update(sp_ref[...], None, npg,
           lambda kk, g: kpages[kk][head_rows(g), :].astype(BF16),
           lambda kk, g: vpages[kk][head_rows(g), :].astype(BF16))

    @pl.when(pg == pl.num_programs(1) - 1)
    def _():
        shape = sn_ref.shape
        allowed = lax.broadcasted_iota(jnp.int32, shape, 1) <= lax.broadcasted_iota(jnp.int32, shape, 0)
        update(sn_ref[...], allowed, 1,
               lambda kk, g: kn_ref[:, g * HEAD_DIM:(g + 1) * HEAD_DIM],
               lambda kk, g: vn_ref[:, g * HEAD_DIM:(g + 1) * HEAD_DIM])
        o_ref[...] = acc_scr[...] / l_scr[...]


def _pad_queries(x, t_len):
    n, h, d = x.shape
    x = jnp.transpose(x.reshape(n // t_len, t_len, h, d), (0, 2, 1, 3))
    x = jnp.pad(x, ((0, 0), (0, 0), (0, QPAD - t_len), (0, 0)))
    return x.reshape(n // t_len, h * QPAD, d)


def _pad_new_keys(x, t_len):
    n, w = x.shape
    return jnp.pad(x.reshape(n // t_len, t_len, w), ((0, 0), (0, PAGE_SIZE - t_len), (0, 0)))


def _attn_sample(q_s, k_new, v_new, qi_s, wi_s, ki_new, cache_k, cache_v, cache_idx_k, page_table, page_base,
                 *, t_len):
    db, n_pages = page_table.shape
    past = n_pages * PAGE_SIZE
    top_k = min(TOPK_MAX, (past + t_len) // 4)
    kvw = N_KV_HEADS * HEAD_DIM
    sds = jax.ShapeDtypeStruct
    q = _pad_queries(q_s.reshape(-1, N_HEADS, HEAD_DIM), t_len)
    qi = _pad_queries(qi_s.reshape(-1, IDX_HEADS, IDX_DIM), t_len)
    ws = _pad_queries(wi_s.reshape(-1, IDX_HEADS, 1), t_len)
    ws = jnp.broadcast_to(ws, ws.shape[:2] + (PAGE_SIZE,))
    kn, vn, kin = (_pad_new_keys(a, t_len) for a in (k_new, v_new, ki_new))

    def idx_page_spec(per_step, kk):
        return pl.BlockSpec((None, IDX_DIM, PAGE_SIZE),
                            lambda s, g, pt: (page_base + pt[s, g * per_step + kk], 0, 0))

    def kv_page_spec(per_step, kk):
        return pl.BlockSpec((N_KV_HEADS * PAGE_SIZE, HEAD_DIM),
                            lambda s, g, pt: (page_base + pt[s, g * per_step + kk], 0))

    seq_spec = lambda r, c: pl.BlockSpec((None, r, c), lambda s, g, pt: (s, 0, 0))
    npi = IDX_PAGES_PER_STEP
    sp, sn = pl.pallas_call(
        _sample_scores_kernel,
        grid_spec=pltpu.PrefetchScalarGridSpec(
            num_scalar_prefetch=1, grid=(db, n_pages // npi),
            in_specs=[seq_spec(IDX_HEADS * QPAD, IDX_DIM), seq_spec(IDX_HEADS * QPAD, PAGE_SIZE),
                      seq_spec(PAGE_SIZE, IDX_DIM)] + [idx_page_spec(npi, kk) for kk in range(npi)],
            out_specs=[pl.BlockSpec((None, QPAD, npi * PAGE_SIZE), lambda s, g, pt: (s, 0, g)),
                       seq_spec(QPAD, PAGE_SIZE)]),
        out_shape=[sds((db, QPAD, past), F32), sds((db, QPAD, PAGE_SIZE), F32)],
        compiler_params=_cparams("parallel", "arbitrary"), name="sample_scores",
    )(page_table, qi, ws, kin, *([cache_idx_k] * npi))

    rows = db * QPAD
    rblk = min(rows, 8 * QPAD)
    assert rows % rblk == 0
    thr = pl.pallas_call(
        functools.partial(_sample_thr_kernel, past=past, top_k=top_k),
        grid=(rows // rblk,),
        in_specs=[_row_spec(rblk, past), _row_spec(rblk, PAGE_SIZE)],
        out_specs=_row_spec(rblk, LANES),
        out_shape=sds((rows, LANES), F32),
        scratch_shapes=[pltpu.VMEM((rblk, past + PAGE_SIZE), F32)],
        compiler_params=_cparams("parallel"), name="sample_threshold",
    )(sp.reshape(rows, past), sn.reshape(rows, PAGE_SIZE))

    npa = KV_PAGES_PER_STEP
    hq = N_HEADS * QPAD
    out = pl.pallas_call(
        _sample_attn_kernel,
        grid_spec=pltpu.PrefetchScalarGridSpec(
            num_scalar_prefetch=1, grid=(db, n_pages // npa),
            in_specs=[seq_spec(hq, HEAD_DIM),
                      pl.BlockSpec((None, QPAD, npa * PAGE_SIZE), lambda s, g, pt: (s, 0, g)),
                      seq_spec(QPAD, PAGE_SIZE), seq_spec(QPAD, LANES),
                      seq_spec(PAGE_SIZE, kvw), seq_spec(PAGE_SIZE, kvw)]
                     + [kv_page_spec(npa, kk) for kk in range(npa)] * 2,
            out_specs=seq_spec(hq, HEAD_DIM),
            scratch_shapes=[pltpu.VMEM((hq, 1), F32), pltpu.VMEM((hq, 1), F32), pltpu.VMEM((hq, HEAD_DIM), F32)]),
        out_shape=sds((db, hq, HEAD_DIM), F32),
        compiler_params=_cparams("parallel", "arbitrary"), name="sample_attn",
    )(page_table, q, sp, sn, thr.reshape(db, QPAD, LANES), kn, vn, *([cache_k] * npa), *([cache_v] * npa))
    out = out.reshape(db, N_HEADS, QPAD, HEAD_DIM)[:, :, :t_len]
    return jnp.transpose(out, (0, 2, 1, 3)).reshape(db * t_len, N_HEADS * HEAD_DIM)


ROW_TILE = 640
FFN_ROW_TILE = 512
FF_TILE = 512
MERGE_ROW_TILE = 256


def _layer(x_p, x_s, pos, dims, ck, cv, cik, page_base, s_re, s_im, page_table, p):
    b, s, db, t = dims
    mp, ms = b * s, db * t
    m, d = mp + ms, x_p.shape[1]
    bf = lambda w: w.astype(BF16)
    ssm_w = d // 2
    attn_w = N_HEADS * HEAD_DIM
    kv_w = N_KV_HEADS * HEAD_DIM
    sizes = [ssm_w, attn_w, kv_w, kv_w, IDX_HEADS * IDX_DIM, IDX_DIM, IDX_HEADS, d, d]
    off = [0]
    for n in sizes:
        off.append(off[-1] + n)
    w_in = p["w_in"]
    col = lambda a, z: bf(w_in[:, off[a]:off[z]])
    w_kw = jnp.pad(col(5, 7), ((0, 0), (0, LANES - IDX_DIM - IDX_HEADS)))

    x1, xn = _ffn(x_p, x_s, p["ffn1_norm"], bf(p["ffn1_w_gate"]), bf(p["ffn1_w_up"]), bf(p["ffn1_w_down"]),
                  p["mix_norm"], n_main_rows=mp, tail=ms, split_out=False, tm=FFN_ROW_TILE, tf=FF_TILE)
    pr = _projections(xn, col(0, 1), col(1, 2), col(2, 4), col(4, 5), w_kw, col(7, 9),
                      p["q_norm"], p["k_norm"], pos, tm=ROW_TILE)

    kmat, win, wout, lp = _ssm_prep(p["ssm_lambda_re"], p["ssm_lambda_im"], p["ssm_b_re"], p["ssm_b_im"],
                                    p["ssm_c_re"], p["ssm_c_im"], p["ssm_log_dt"])
    nblk = ssm_w // LANES
    d8 = p["ssm_d"].reshape(nblk, 1, LANES)
    za8, hl_p, hl_s = _ssm(pr["u8"], kmat, win, wout, lp, d8, _state_to_blocks(s_re, s_im),
                           nseq=b, seq=s, n_s=db, t_s=t)

    b_p = _attn_prompt(pr["q_t"], pr["kb"], pr["v_t"], pr["qi_t"], pr["kib"], pr["kiw_t"],
                       batch=b, seq=s, m_out=mp)
    b_s = _attn_sample(pr["q"][mp:], pr["kb"][mp:], bf(pr["v"][mp:]), pr["qi"][mp:],
                       pr["kiw"][mp:, IDX_DIM:IDX_DIM + IDX_HEADS], pr["kib"][mp:],
                       ck, cv, cik, page_table, page_base, t_len=t)

    x2 = _merge(za8, b_p, bf(b_s), pr["gates"], x1, bf(p["glu_w"]), p["glu_b"], bf(p["w_branch_a"]),
                bf(p["w_branch_b"]), bf(p["w_out"]), tm=MERGE_ROW_TILE)
    y_p, y_s = _ffn(x2, None, p["ffn2_norm"], bf(p["ffn2_w_gate"]), bf(p["ffn2_w_up"]), bf(p["ffn2_w_down"]),
                    n_main_rows=mp, tail=ms, split_out=True, tm=FFN_ROW_TILE, tf=FF_TILE)

    hp_re, hp_im = _blocks_to_state(hl_p)
    hs_re, hs_im = _blocks_to_state(hl_s)
    k, v, ki = pr["k"], pr["v"], pr["kiw"][:, :IDX_DIM]
    kvs = (N_KV_HEADS, HEAD_DIM)
    rows = (k[:mp].reshape(b, s, *kvs), v[:mp].reshape(b, s, *kvs), ki[:mp].reshape(b, s, IDX_DIM), hp_re, hp_im,
            k[mp:].reshape(db, t, *kvs), v[mp:].reshape(db, t, *kvs), ki[mp:].reshape(db, t, IDX_DIM), hs_re, hs_im)
    return y_p, y_s, rows


def kernel(x_prompt, x_sample, cache_k, cache_v, cache_idx_k, state_ssm_re, state_ssm_im, page_table,
           ffn1_norm, ffn1_w_gate, ffn1_w_up, ffn1_w_down, mix_norm, w_in, q_norm, k_norm,
           ssm_lambda_re, ssm_lambda_im, ssm_b_re, ssm_b_im, ssm_c_re, ssm_c_im, ssm_d, ssm_log_dt,
           glu_w, glu_b, w_branch_a, w_branch_b, w_out, ffn2_norm, ffn2_w_gate, ffn2_w_up, ffn2_w_down):
    b, s, d = x_prompt.shape
    db, t, _ = x_sample.shape
    depth, n_phys = cache_k.shape[:2]
    past = page_table.shape[1] * PAGE_SIZE
    x_p, x_s = x_prompt.reshape(b * s, d), x_sample.reshape(db * t, d)
    pos = jnp.concatenate([jnp.tile(jnp.arange(s), b), jnp.tile(past + jnp.arange(t), db)])
    ck = cache_k.reshape(-1, HEAD_DIM)
    cv = cache_v.reshape(-1, HEAD_DIM)
    cik = jnp.swapaxes(cache_idx_k, 2, 3).reshape(depth * n_phys, IDX_DIM, PAGE_SIZE)
    params = dict(
        ffn1_norm=ffn1_norm, ffn1_w_gate=ffn1_w_gate, ffn1_w_up=ffn1_w_up, ffn1_w_down=ffn1_w_down,
        mix_norm=mix_norm, w_in=w_in, q_norm=q_norm, k_norm=k_norm,
        ssm_lambda_re=ssm_lambda_re, ssm_lambda_im=ssm_lambda_im, ssm_b_re=ssm_b_re, ssm_b_im=ssm_b_im,
        ssm_c_re=ssm_c_re, ssm_c_im=ssm_c_im, ssm_d=ssm_d, ssm_log_dt=ssm_log_dt, glu_w=glu_w, glu_b=glu_b,
        w_branch_a=w_branch_a, w_branch_b=w_branch_b, w_out=w_out,
        ffn2_norm=ffn2_norm, ffn2_w_gate=ffn2_w_gate, ffn2_w_up=ffn2_w_up, ffn2_w_down=ffn2_w_down)
    new = [[] for _ in range(10)]
    for l in range(depth):
        p = {name: w[l] for name, w in params.items()}
        x_p, x_s, rows = _layer(x_p, x_s, pos, (b, s, db, t), ck, cv, cik, l * n_phys,
                                state_ssm_re[l], state_ssm_im[l], page_table, p)
        for lst, r in zip(new, rows):
            lst.append(r)
    return (x_p.reshape(b, s, d), x_s.reshape(db, t, d)) + tuple(jnp.stack(lst) for lst in new)
```

```python
import functools
import math

import jax
import jax.numpy as jnp
from jax import lax
from jax.experimental import pallas as pl
from jax.experimental.pallas import tpu as pltpu

F32 = jnp.float32
BF16 = jnp.bfloat16

SSM_GROUP = 16
SSM_STATE = 64
N_HEADS = 8
HEAD_DIM = 128
N_KV_HEADS = 2
N_REP = N_HEADS // N_KV_HEADS
ROT_DIM = HEAD_DIM // 4
ROPE_THETA = 500000.0
IDX_HEADS = 16
IDX_DIM = 64
IDX_ROT_DIM = IDX_DIM // 4
TOPK_MAX = 256
PAGE_SIZE = 128
FFN_RES = 0.5
EPS = 1e-6

LANES = 128
SUBLANES = 8
VMEM_LIMIT_BYTES = 56 * 1024 * 1024

GROUPS_PER_BLOCK = LANES // SSM_GROUP
STATES_PER_BLOCK = GROUPS_PER_BLOCK * SSM_STATE
SSM_T = 8

INT_MIN = -(2 ** 31)


def _cparams(*sem):
    return pltpu.CompilerParams(dimension_semantics=sem, vmem_limit_bytes=VMEM_LIMIT_BYTES)


def _rms(x, g):
    return x * lax.rsqrt(jnp.mean(x * x, axis=-1, keepdims=True) + EPS) * g


def _dot(a, b):
    return jnp.dot(a, b, preferred_element_type=F32)


def _dot_nt(a, b):
    return lax.dot_general(a, b, (((1,), (1,)), ((), ())), preferred_element_type=F32)


def _ffn_kernel(*refs, nf, n_main, tm, tail, two_src, split_out, with_next_norm):
    refs = list(refs)
    x_main_ref = refs.pop(0)
    x_tail_ref = refs.pop(0) if two_src else x_main_ref
    g_ref, wg_ref, wu_ref, wd_ref = (refs.pop(0) for _ in range(4))
    g2_ref = refs.pop(0) if with_next_norm else None
    y_main_ref = refs.pop(0)
    y_tail_ref = refs.pop(0) if split_out else y_main_ref
    n2_ref = refs.pop(0) if with_next_norm else None
    xn_ref, acc_ref = refs
    i, f = pl.program_id(0), pl.program_id(1)

    def run(rows, x_ref, y_ref):
        @pl.when(f == 0)
        def _():
            xn_ref[:rows] = _rms(x_ref[:rows], g_ref[...]).astype(BF16)
            acc_ref[:rows] = jnp.zeros((rows, acc_ref.shape[1]), F32)

        xn = xn_ref[:rows]
        a = _dot(xn, wg_ref[...])
        b = _dot(xn, wu_ref[...])
        h = (a * jax.nn.sigmoid(a) * b).astype(BF16)
        acc_ref[:rows] += _dot(h, wd_ref[...])

        @pl.when(f == nf - 1)
        def _():
            y = x_ref[:rows] + FFN_RES * acc_ref[:rows]
            y_ref[:rows] = y
            if with_next_norm:
                n2_ref[:rows] = _rms(y, g2_ref[...]).astype(BF16)

    pl.when(i < n_main)(lambda: run(tm, x_main_ref, y_main_ref))
    pl.when(i == n_main)(lambda: run(tail, x_tail_ref, y_tail_ref))


def _ffn(x_main, x_tail, g, wg, wu, wd, g2=None, *, n_main_rows, tail, split_out, tm, tf):
    d = x_main.shape[1]
    nf = wg.shape[1] // tf
    n_main = n_main_rows // tm
    m = n_main_rows + tail
    two_src = x_tail is not None
    with_next = g2 is not None
    clamp = lambda i, f: (jnp.minimum(i, n_main - 1), 0)
    rows = lambda i, f: (i, 0)
    first = lambda i, f: (0, 0)
    sds = jax.ShapeDtypeStruct
    in_specs = [pl.BlockSpec((tm, d), clamp if two_src else rows)]
    args = [x_main]
    if two_src:
        in_specs.append(pl.BlockSpec((tail, d), first))
        args.append(x_tail)
    in_specs += [pl.BlockSpec((1, d), first), pl.BlockSpec((d, tf), lambda i, f: (0, f)),
                 pl.BlockSpec((d, tf), lambda i, f: (0, f)), pl.BlockSpec((tf, d), lambda i, f: (f, 0))]
    args += [g.reshape(1, d), wg, wu, wd]
    if with_next:
        in_specs.append(pl.BlockSpec((1, d), first))
        args.append(g2.reshape(1, d))
    if split_out:
        out_shape = [sds((n_main_rows, d), F32), sds((tail, d), F32)]
        out_specs = [pl.BlockSpec((tm, d), clamp), pl.BlockSpec((tail, d), first)]
    else:
        out_shape = [sds((m, d), F32)]
        out_specs = [pl.BlockSpec((tm, d), rows)]
    if with_next:
        out_shape.append(sds((m, d), BF16))
        out_specs.append(pl.BlockSpec((tm, d), rows))
    return pl.pallas_call(
        functools.partial(_ffn_kernel, nf=nf, n_main=n_main, tm=tm, tail=tail, two_src=two_src,
                          split_out=split_out, with_next_norm=with_next),
        grid=(n_main + 1, nf),
        in_specs=in_specs,
        out_specs=out_specs,
        out_shape=out_shape,
        scratch_shapes=[pltpu.VMEM((tm, d), BF16), pltpu.VMEM((tm, d), F32)],
        compiler_params=_cparams("arbitrary", "arbitrary"),
        name="ffn" + ("_norm" if with_next else ""),
    )(*args)


def _rope_tables(pos, rot_dim, width, tile):
    half = rot_dim // 2
    m = pos.shape[0]
    freqs = ROPE_THETA ** (-jnp.arange(half, dtype=F32) * 2.0 / rot_dim)
    ang = pos.astype(F32)[:, None] * freqs[None, :]
    cos, sin = jnp.cos(ang), jnp.sin(ang)
    zh = jnp.zeros((m, half), F32)
    rest = width - rot_dim
    c = jnp.concatenate([cos, cos, jnp.ones((m, rest), F32)], axis=1)
    s1 = jnp.concatenate([-sin, zh, jnp.zeros((m, rest), F32)], axis=1)
    s2 = jnp.concatenate([zh, sin, jnp.zeros((m, rest), F32)], axis=1)
    if tile:
        reps = LANES // width
        return tuple(jnp.tile(t, (1, reps)) for t in (c, s1, s2))
    pad = LANES - width
    return (jnp.pad(c, ((0, 0), (0, pad)), constant_values=1.0),
            jnp.pad(s1, ((0, 0), (0, pad))), jnp.pad(s2, ((0, 0), (0, pad))))


def _rope(x, c, s1, s2, half):
    return x * c + pltpu.roll(x, LANES - half, 1) * s1 + pltpu.roll(x, half, 1) * s2


def _proj_rows(xn_ref, w_refs, wbf_ref, n_main, tm, tail, chunk, epilogue):
    i = pl.program_id(0)

    @pl.when(i == 0)
    def _():
        off = 0
        for w_ref in w_refs:
            n = w_ref.shape[1]
            wbf_ref[:, off:off + n] = w_ref[...].astype(BF16)
            off += n

    def run(rows):
        for r0 in range(0, rows, chunk):
            rs = slice(r0, min(r0 + chunk, rows))
            epilogue(_dot(xn_ref[rs, :], wbf_ref[...]), rs)

    pl.when(i < n_main)(lambda: run(tm))
    pl.when(i == n_main)(lambda: run(tail))


def _u_proj_kernel(xn_ref, w_ref, u_ref, wbf_ref, **kw):
    def epilogue(z, rs):
        for j in range(z.shape[1] // LANES):
            u_ref[j, rs, :] = z[:, j * LANES:(j + 1) * LANES]
    _proj_rows(xn_ref, [w_ref], wbf_ref, epilogue=epilogue, **kw)


def _q_proj_kernel(xn_ref, w_ref, g_ref, c_ref, s1_ref, s2_ref, qn_ref, qt_ref, wbf_ref, **kw):
    def epilogue(z, rs):
        c, s1, s2, g = c_ref[rs, :], s1_ref[rs, :], s2_ref[rs, :], g_ref[...]
        heads = []
        for h in range(N_HEADS):
            x = _rms(z[:, h * HEAD_DIM:(h + 1) * HEAD_DIM], g)
            heads.append(_rope(x, c, s1, s2, ROT_DIM // 2) * (HEAD_DIM ** -0.5))
        q = jnp.concatenate(heads, axis=1)
        qn_ref[rs, :] = q.astype(BF16)
        qt_ref[:, rs] = q.T.astype(BF16)
    _proj_rows(xn_ref, [w_ref], wbf_ref, epilogue=epilogue, **kw)


def _kv_proj_kernel(xn_ref, w_ref, g_ref, c_ref, s1_ref, s2_ref, k_ref, kb_ref, v_ref, vt_ref, wbf_ref, **kw):
    def epilogue(z, rs):
        c, s1, s2, g = c_ref[rs, :], s1_ref[rs, :], s2_ref[rs, :], g_ref[...]
        heads = []
        for h in range(N_KV_HEADS):
            x = _rms(z[:, h * HEAD_DIM:(h + 1) * HEAD_DIM], g)
            heads.append(_rope(x, c, s1, s2, ROT_DIM // 2))
        k = jnp.concatenate(heads, axis=1)
        v = z[:, N_KV_HEADS * HEAD_DIM:]
        k_ref[rs, :] = k
        kb_ref[rs, :] = k.astype(BF16)
        v_ref[rs, :] = v
        vt_ref[:, rs] = v.T.astype(BF16)
    _proj_rows(xn_ref, [w_ref], wbf_ref, epilogue=epilogue, **kw)


def _qi_proj_kernel(xn_ref, wa_ref, wb_ref, c_ref, s1_ref, s2_ref, qn_ref, qt_ref, wbf_ref, **kw):
    def epilogue(z, rs):
        c, s1, s2 = c_ref[rs, :], s1_ref[rs, :], s2_ref[rs, :]
        cols = []
        for j in range(z.shape[1] // LANES):
            x = z[:, j * LANES:(j + 1) * LANES]
            cols.append(_rope(x, c, s1, s2, IDX_ROT_DIM // 2) * (IDX_DIM ** -0.5))
        q = jnp.concatenate(cols, axis=1)
        qn_ref[rs, :] = q.astype(BF16)
        qt_ref[:, rs] = q.T.astype(BF16)
    _proj_rows(xn_ref, [wa_ref, wb_ref], wbf_ref, epilogue=epilogue, **kw)


def _kiwi_proj_kernel(xn_ref, w_ref, c_ref, s1_ref, s2_ref, scale_ref, o_ref, ot_ref, kb_ref, wbf_ref, **kw):
    def epilogue(z, rs):
        y = _rope(z, c_ref[rs, :], s1_ref[rs, :], s2_ref[rs, :], IDX_ROT_DIM // 2) * scale_ref[...]
        o_ref[rs, :] = y
        ot_ref[:, rs] = y.T
        kb_ref[rs, :] = y[:, :IDX_DIM].astype(BF16)
    _proj_rows(xn_ref, [w_ref], wbf_ref, epilogue=epilogue, **kw)


def _gate_proj_kernel(xn_ref, w_ref, o_ref):
    o_ref[...] = jax.nn.sigmoid(_dot(xn_ref[...], w_ref[...])).astype(BF16)


def _row_spec(tm, n):
    return pl.BlockSpec((tm, n), lambda i: (i, 0))


def _full_spec(shape):
    return pl.BlockSpec(shape, lambda i: (0,) * len(shape))


PROJ_CHUNK = 256


def _projections(xn, w_in, layer, w_g, q_norm, k_norm, pos, *, n_main_rows, tail, tm, tm_gates):
    m, d = xn.shape
    n_main = n_main_rows // tm
    assert n_main * tm == n_main_rows and n_main_rows + tail == m and tail <= tm
    grid = (n_main + 1,)
    ssm_w, attn_w, kv_w, idx_w = d // 2, N_HEADS * HEAD_DIM, N_KV_HEADS * HEAD_DIM, IDX_HEADS * IDX_DIM
    hd_tabs = _rope_tables(pos, ROT_DIM, HEAD_DIM, True)
    ix_tabs = _rope_tables(pos, IDX_ROT_DIM, IDX_DIM, True)
    kw_tabs = _rope_tables(pos, IDX_ROT_DIM, IDX_DIM, False)
    tab_specs = [_row_spec(tm, LANES)] * 3
    xs = _row_spec(tm, d)
    sds = jax.ShapeDtypeStruct
    seq = _cparams("arbitrary")
    kw = dict(n_main=n_main, tm=tm, tail=tail, chunk=PROJ_CHUNK)

    def wcols(start, width):
        assert start % width == 0
        return pl.BlockSpec((None, d, width), lambda i: (layer, 0, start // width), pipeline_mode=pl.Buffered(1))

    def call(kernel, name, w_specs, n_w, extra_specs, extra_args, out_specs, out_shape):
        return pl.pallas_call(
            functools.partial(kernel, **kw), grid=grid,
            in_specs=[xs] + w_specs + extra_specs, out_specs=out_specs, out_shape=out_shape,
            scratch_shapes=[pltpu.VMEM((d, n_w), BF16)], compiler_params=seq, name=name,
        )(xn, *([w_in] * len(w_specs)), *extra_args)

    tcol = lambda n: pl.BlockSpec((n, tm), lambda i: (0, i))
    norm_spec = _full_spec((1, HEAD_DIM))
    u8 = call(_u_proj_kernel, "proj_u", [wcols(0, ssm_w)], ssm_w, [], [],
              pl.BlockSpec((ssm_w // LANES, tm, LANES), lambda i: (0, i, 0)), sds((ssm_w // LANES, m, LANES), F32))
    q_nat, q_t = call(_q_proj_kernel, "proj_q", [wcols(ssm_w, attn_w)], attn_w,
                      [norm_spec] + tab_specs, [q_norm.reshape(1, HEAD_DIM), *hd_tabs],
                      [_row_spec(tm, attn_w), tcol(attn_w)], [sds((m, attn_w), BF16), sds((attn_w, m), BF16)])
    off_kv = ssm_w + attn_w
    k, kb, v, v_t = call(_kv_proj_kernel, "proj_kv", [wcols(off_kv, 2 * kv_w)], 2 * kv_w,
                         [norm_spec] + tab_specs, [k_norm.reshape(1, HEAD_DIM), *hd_tabs],
                         [_row_spec(tm, kv_w), _row_spec(tm, kv_w), _row_spec(tm, kv_w), tcol(kv_w)],
                         [sds((m, kv_w), F32), sds((m, kv_w), BF16), sds((m, kv_w), F32), sds((kv_w, m), BF16)])
    off_qi = off_kv + 2 * kv_w
    half = idx_w // 2
    qi_nat, qi_t = call(_qi_proj_kernel, "proj_qi", [wcols(off_qi, half), wcols(off_qi + half, half)], idx_w,
                        tab_specs, ix_tabs,
                        [_row_spec(tm, idx_w), tcol(idx_w)], [sds((m, idx_w), BF16), sds((idx_w, m), BF16)])
    off_kw = off_qi + idx_w
    lane = jnp.arange(LANES)
    kw_scale = jnp.where(lane < IDX_DIM, 1.0, jnp.where(lane < IDX_DIM + IDX_HEADS, IDX_HEADS ** -0.5, 0.0))
    kiw, kiw_t, kib = call(_kiwi_proj_kernel, "proj_kiwi", [wcols(off_kw, LANES)], LANES,
                           tab_specs + [_full_spec((1, LANES))], [*kw_tabs, kw_scale.astype(F32).reshape(1, LANES)],
                           [_row_spec(tm, LANES), tcol(LANES), _row_spec(tm, IDX_DIM)],
                           [sds((m, LANES), F32), sds((LANES, m), F32), sds((m, IDX_DIM), BF16)])

    ng = w_g.shape[1]
    tn = 1024
    gates = pl.pallas_call(
        _gate_proj_kernel, grid=(m // tm_gates, ng // tn),
        in_specs=[pl.BlockSpec((tm_gates, d), lambda i, j: (i, 0)), pl.BlockSpec((d, tn), lambda i, j: (0, j))],
        out_specs=pl.BlockSpec((tm_gates, tn), lambda i, j: (i, j)),
        out_shape=sds((m, ng), BF16), compiler_params=_cparams("parallel", "parallel"), name="proj_gates",
    )(xn, w_g)
    return dict(u8=u8, q=q_nat, q_t=q_t, k=k, kb=kb, v=v, v_t=v_t, qi=qi_nat, qi_t=qi_t,
                kiw=kiw, kiw_t=kiw_t, kib=kib, gates=gates)


def _ssm_prep_kernel(lre_ref, lim_ref, ldt_ref, bre_ref, bim_ref, cre_ref, cim_ref,
                     k_ref, win_ref, wout_ref, lp_ref):
    ns = STATES_PER_BLOCK
    lre, lim = lre_ref[...], lim_ref[...]
    dt = jnp.exp(ldt_ref[...])
    a, th = lre * dt, lim * dt

    def power(l):
        mag = jnp.exp(a * float(l))
        return mag * jnp.cos(th * float(l)), mag * jnp.sin(th * float(l))

    pw = [power(l) for l in range(SSM_T + 1)]
    xr, xi = pw[1][0] - 1.0, pw[1][1]
    den = lre * lre + lim * lim
    cr, ci = (xr * lre + xi * lim) / den, (xi * lre - xr * lim) / den
    bre, bim = bre_ref[...], bim_ref[...]
    bbr, bbi = bre * cr - bim * ci, bre * ci + bim * cr
    cre, cim = cre_ref[...], cim_ref[...]
    hi = lax.Precision.HIGHEST
    nt = (((1,), (1,)), ((), ()))
    lag = []
    for l in range(SSM_T):
        pr, pi = pw[l]
        blr, bli = bbr * pr - bbi * pi, bbr * pi + bbi * pr
        m = (lax.dot_general(blr, cre, nt, precision=hi, preferred_element_type=F32)
             - lax.dot_general(bli, cim, nt, precision=hi, preferred_element_type=F32))
        lag.append(m.astype(BF16))
        t = SSM_T - 1 - l
        win_ref[t * LANES:(t + 1) * LANES, :ns] = blr.astype(BF16)
        win_ref[t * LANES:(t + 1) * LANES, ns:] = bli.astype(BF16)
    zero = jnp.zeros((LANES, LANES), BF16)
    for t in range(SSM_T):
        pr, pi = pw[t + 1]
        wout_ref[t * LANES:(t + 1) * LANES, :ns] = (cre * pr - cim * pi).astype(BF16)
        wout_ref[t * LANES:(t + 1) * LANES, ns:] = (-(cre * pi + cim * pr)).astype(BF16)
        lp_ref[t:t + 1, :ns] = pr
        lp_ref[t:t + 1, ns:] = pi
        for t2 in range(SSM_T):
            k_ref[t * LANES:(t + 1) * LANES, t2 * LANES:(t2 + 1) * LANES] = lag[t2 - t] if t2 >= t else zero


def _block_diag_groups(w):
    g, h, p = w.shape
    nb = g // GROUPS_PER_BLOCK
    w = w.reshape(nb, GROUPS_PER_BLOCK, h, p)
    eye = jnp.eye(GROUPS_PER_BLOCK, dtype=w.dtype)
    out = w[:, :, :, None, :] * eye[None, :, None, :, None]
    return out.reshape(nb, GROUPS_PER_BLOCK * h, GROUPS_PER_BLOCK * p)


def _ssm_prep(lam_re, lam_im, b_re, b_im, c_re, c_im, log_dt):
    g, p = lam_re.shape
    nb = g // GROUPS_PER_BLOCK
    ns = STATES_PER_BLOCK
    tl = SSM_T * LANES
    vec = lambda a: a.reshape(nb, 1, ns)
    ldt = vec(jnp.broadcast_to(log_dt[:, None], (g, p)))
    bt = lambda b: _block_diag_groups(jnp.swapaxes(b, 1, 2))
    vspec = pl.BlockSpec((None, 1, ns), lambda j: (j, 0, 0))
    mspec = pl.BlockSpec((None, LANES, ns), lambda j: (j, 0, 0))
    sds = jax.ShapeDtypeStruct
    return pl.pallas_call(
        _ssm_prep_kernel, grid=(nb,),
        in_specs=[vspec, vspec, vspec, mspec, mspec, mspec, mspec],
        out_specs=[pl.BlockSpec((None, tl, tl), lambda j: (j, 0, 0)),
                   pl.BlockSpec((None, tl, 2 * ns), lambda j: (j, 0, 0)),
                   pl.BlockSpec((None, tl, 2 * ns), lambda j: (j, 0, 0)),
                   pl.BlockSpec((None, SSM_T, 2 * ns), lambda j: (j, 0, 0))],
        out_shape=[sds((nb, tl, tl), BF16), sds((nb, tl, 2 * ns), BF16), sds((nb, tl, 2 * ns), BF16),
                   sds((nb, SSM_T, 2 * ns), F32)],
        compiler_params=_cparams("parallel"), name="ssm_prep",
    )(vec(lam_re), vec(lam_im), ldt, bt(b_re), bt(b_im), _block_diag_groups(c_re), _block_diag_groups(c_im))


def _ssm_kernel(u_ref, k_ref, win_ref, wout_ref, lp_ref, d_ref, h0s_ref, za_ref, hlp_ref, hls_ref, x_scr, hs_scr,
                *, nseq, seq, n_s, t_s):
    ns = STATES_PER_BLOCK
    nk = ns // LANES
    d = d_ref[...]
    split = lambda a: [a[:, k * LANES:(k + 1) * LANES] for k in range(2 * nk)]

    def gather(row0, n_rows, t_steps):
        return [u_ref[pl.ds(row0 + t, n_rows, stride=t_steps), :] for t in range(t_steps)]

    def advance(h, x, t_steps):
        lam = [lp_ref[t_steps - 1:t_steps, k * LANES:(k + 1) * LANES] for k in range(2 * nk)]
        re = [lam[k] * h[k] - lam[nk + k] * h[nk + k] + x[k] for k in range(nk)]
        im = [lam[k] * h[nk + k] + lam[nk + k] * h[k] + x[nk + k] for k in range(nk)]
        return re + im

    def emit(row0, n_rows, t_steps, cols, hs, kmat, wout):
        u = jnp.concatenate(cols, axis=1).astype(BF16)
        y = _dot(u, kmat) + _dot_nt(hs.astype(BF16), wout)
        for t in range(t_steps):
            yt = y[:, t * LANES:(t + 1) * LANES] + d * cols[t]
            za_ref[pl.ds(row0 + t, n_rows, stride=t_steps), :] = jax.nn.gelu(yt, approximate=True)

    c_per = seq // SSM_T
    for b in range(nseq):
        u = jnp.concatenate(gather(b * seq, c_per, SSM_T), axis=1).astype(BF16)
        x = _dot(u, win_ref[...])
        for k in range(2 * nk):
            x_scr[k, b * c_per:(b + 1) * c_per, :] = x[:, k * LANES:(k + 1) * LANES]

    def step(c, h):
        rows = pl.ds(c, nseq, stride=c_per)
        for k in range(2 * nk):
            hs_scr.at[k][rows, :] = h[k]
        return tuple(advance(h, [x_scr.at[k][rows, :] for k in range(2 * nk)], SSM_T))

    h_last = lax.fori_loop(0, c_per, step, tuple(jnp.zeros((nseq, LANES), F32) for _ in range(2 * nk)))
    hlp_ref[...] = jnp.concatenate(list(h_last), axis=1)
    for b in range(nseq):
        hs = jnp.concatenate([hs_scr[k, b * c_per:(b + 1) * c_per, :] for k in range(2 * nk)], axis=1)
        emit(b * seq, c_per, SSM_T, gather(b * seq, c_per, SSM_T), hs, k_ref[...], wout_ref[...])

    row0, tl = nseq * seq, t_s * LANES
    cols = gather(row0, n_s, t_s)
    u = jnp.concatenate(cols, axis=1).astype(BF16)
    x = _dot(u, win_ref[(SSM_T - t_s) * LANES:, :])
    h0 = h0s_ref[...]
    hls_ref[...] = jnp.concatenate(advance(split(h0), split(x), t_s), axis=1)
    emit(row0, n_s, t_s, cols, h0, k_ref[:tl, :tl], wout_ref[:tl, :])


def _ssm(u8, kmat, win, wout, lp, d8, h0s, *, nseq, seq, n_s, t_s):
    nb, m_total, _ = u8.shape
    ns = STATES_PER_BLOCK
    assert m_total == nseq * seq + n_s * t_s and seq % SSM_T == 0 and t_s <= SSM_T
    rows = nseq * seq // SSM_T
    tl = SSM_T * LANES
    sds = jax.ShapeDtypeStruct
    blk = lambda r, c: pl.BlockSpec((None, r, c), lambda j: (j, 0, 0))
    return pl.pallas_call(
        functools.partial(_ssm_kernel, nseq=nseq, seq=seq, n_s=n_s, t_s=t_s),
        grid=(nb,),
        in_specs=[blk(m_total, LANES), blk(tl, tl), blk(tl, 2 * ns), blk(tl, 2 * ns), blk(SSM_T, 2 * ns),
                  blk(1, LANES), blk(n_s, 2 * ns)],
        out_specs=[blk(m_total, LANES), blk(nseq, 2 * ns), blk(n_s, 2 * ns)],
        out_shape=[sds((nb, m_total, LANES), F32), sds((nb, nseq, 2 * ns), F32), sds((nb, n_s, 2 * ns), F32)],
        scratch_shapes=[pltpu.VMEM((2 * ns // LANES, rows, LANES), F32)] * 2,
        compiler_params=_cparams("parallel"), name="ssm",
    )(u8, kmat, win, wout, lp, d8, h0s)


def _state_to_blocks(re, im):
    n, g, p = re.shape
    nb = g // GROUPS_PER_BLOCK
    f = lambda a: jnp.transpose(a.reshape(n, nb, GROUPS_PER_BLOCK * p), (1, 0, 2))
    return jnp.concatenate([f(re), f(im)], axis=-1)


def _blocks_to_state(h):
    nb, n, w = h.shape
    ns = w // 2
    f = lambda a: jnp.transpose(a, (1, 0, 2)).reshape(n, nb * GROUPS_PER_BLOCK, ns // GROUPS_PER_BLOCK)
    return f(h[..., :ns]), f(h[..., ns:])


def _key_to_float(c):
    return lax.bitcast_convert_type(c ^ ((c >> 31) & 0x7FFFFFFF), F32)


def _kth_largest(count_ge, shape, top_k):
    def body(it, u):
        bit = jnp.left_shift(jnp.int32(1), 31 - it)
        cand = u | bit
        return jnp.where(count_ge(_key_to_float(cand ^ INT_MIN)) >= top_k, cand, u)
    return _key_to_float(lax.fori_loop(0, 32, body, jnp.zeros(shape, jnp.int32)) ^ INT_MIN)


def _attn_prompt_kernel(qt_ref, kb_ref, vt_ref, qit_ref, kib_ref, wt_ref, o_ref, s_scr, bias_scr,
                        *, seq, top_k, key_chunk):
    qb = pl.program_id(1)
    tq = o_ref.shape[0]

    def run(nk):
        kib = kib_ref[:nk]
        s = jnp.zeros((nk, tq), F32)
        for h in range(IDX_HEADS):
            d = _dot(kib, qit_ref[h * IDX_DIM:(h + 1) * IDX_DIM, :])
            s = s + jnp.maximum(d, 0.0) * wt_ref[IDX_DIM + h:IDX_DIM + h + 1, :]
        kpos = lax.broadcasted_iota(jnp.int32, (nk, tq), 0)
        qpos = qb * tq + lax.broadcasted_iota(jnp.int32, (nk, tq), 1)
        allowed = kpos <= qpos
        s_scr[:nk] = jnp.where(allowed, s, -jnp.inf)

        def count_ge(t):
            n_acc = 8
            acc = [jnp.zeros((SUBLANES, tq), jnp.int32) for _ in range(n_acc)]
            for j in range(nk // SUBLANES):
                rows = s_scr[j * SUBLANES:(j + 1) * SUBLANES]
                acc[j % n_acc] = acc[j % n_acc] + (rows >= t).astype(jnp.int32)
            while len(acc) > 1:
                acc = [a + b for a, b in zip(acc[::2], acc[1::2])]
            return jnp.sum(acc[0], axis=0, keepdims=True)

        thr = _kth_largest(count_ge, (1, tq), top_k)
        few = qpos < top_k - 1
        bias_scr[:nk] = jnp.where(allowed & ((s_scr[:nk] >= thr) | few), 0.0, -jnp.inf)

        outs = []
        for g in range(N_KV_HEADS):
            kg = kb_ref[:nk, g * HEAD_DIM:(g + 1) * HEAD_DIM]
            vtg = vt_ref[g * HEAD_DIM:(g + 1) * HEAD_DIM, :nk]
            for r in range(N_REP):
                h = g * N_REP + r
                lg = _dot(kg, qt_ref[h * HEAD_DIM:(h + 1) * HEAD_DIM, :]) + bias_scr[:nk]
                p = jnp.exp(lg - jnp.max(lg, axis=0, keepdims=True))
                den = jnp.sum(p, axis=0, keepdims=True)
                outs.append(_dot(vtg, p.astype(BF16)) / den)
        o_ref[...] = jnp.concatenate(outs, axis=0).T.astype(BF16)

    n_var = seq // key_chunk
    need = (qb * tq + tq + key_chunk - 1) // key_chunk
    for v in range(1, n_var + 1):
        pl.when(need == v)(functools.partial(run, v * key_chunk))


def _attn_prompt(q_t, kb, v_t, qi_t, kib, kiw_t, *, batch, seq, m_out, tq=128, key_chunk=256):
    top_k = min(TOPK_MAX, seq // 4)
    nq = seq // tq
    assert seq % key_chunk == 0 and key_chunk % tq == 0
    aw = N_HEADS * HEAD_DIM
    qcol = lambda n: pl.BlockSpec((n, tq), lambda b, i: (0, b * nq + i))
    return pl.pallas_call(
        functools.partial(_attn_prompt_kernel, seq=seq, top_k=top_k, key_chunk=key_chunk),
        grid=(batch, nq),
        in_specs=[qcol(aw),
                  pl.BlockSpec((seq, N_KV_HEADS * HEAD_DIM), lambda b, i: (b, 0)),
                  pl.BlockSpec((N_KV_HEADS * HEAD_DIM, seq), lambda b, i: (0, b)),
                  qcol(IDX_HEADS * IDX_DIM),
                  pl.BlockSpec((seq, IDX_DIM), lambda b, i: (b, 0)),
                  qcol(LANES)],
        out_specs=pl.BlockSpec((tq, aw), lambda b, i: (b * nq + i, 0)),
        out_shape=jax.ShapeDtypeStruct((m_out, aw), BF16),
        scratch_shapes=[pltpu.VMEM((seq, tq), F32), pltpu.VMEM((seq, tq), F32)],
        compiler_params=_cparams("parallel", "parallel"), name="attn_prompt",
    )(q_t, kb, v_t, qi_t, kib, kiw_t)


def _merge_kernel(za_ref, bm_ref, bt_ref, g_ref, x_ref, gw_ref, gb_ref, wa_ref, wb_ref, wo_ref, o_ref,
                  *, n_main, tm, tail):
    d = x_ref.shape[1]

    def run(rows, b_ref):
        za = jnp.concatenate([za_ref[j, :rows] for j in range(za_ref.shape[0])], axis=1)
        a_out = za * jax.nn.sigmoid(_dot(za.astype(BF16), gw_ref[...]) + gb_ref[...])
        merged = (g_ref[:rows, :d] * _dot(a_out.astype(BF16), wa_ref[...])
                  + g_ref[:rows, d:] * _dot(b_ref[:rows], wb_ref[...]))
        o_ref[:rows] = x_ref[:rows] + _dot(merged.astype(BF16), wo_ref[...])

    i = pl.program_id(0)
    pl.when(i < n_main)(lambda: run(tm, bm_ref))
    pl.when(i == n_main)(lambda: run(tail, bt_ref))


def _merge(za8, b_main, b_tail, gates, x, glu_w, glu_b, wa, wb, wo, *, tm):
    m, d = x.shape
    nb = za8.shape[0]
    n_main_rows, aw = b_main.shape
    tail = b_tail.shape[0]
    n_main = n_main_rows // tm
    assert n_main * tm == n_main_rows and n_main_rows + tail == m and tail <= tm
    resident = lambda shape: pl.BlockSpec(shape, lambda i: (0,) * len(shape), pipeline_mode=pl.Buffered(1))
    return pl.pallas_call(
        functools.partial(_merge_kernel, n_main=n_main, tm=tm, tail=tail), grid=(n_main + 1,),
        in_specs=[pl.BlockSpec((nb, tm, LANES), lambda i: (0, i, 0)),
                  pl.BlockSpec((tm, aw), lambda i: (jnp.minimum(i, n_main - 1), 0)),
                  pl.BlockSpec((tail, aw), lambda i: (0, 0)),
                  _row_spec(tm, 2 * d), _row_spec(tm, d), resident(glu_w.shape), resident((1, glu_w.shape[1])),
                  resident(wa.shape), resident(wb.shape), resident(wo.shape)],
        out_specs=_row_spec(tm, d),
        out_shape=jax.ShapeDtypeStruct((m, d), F32),
        compiler_params=_cparams("arbitrary"), name="merge",
    )(za8, b_main, b_tail, gates, x, glu_w, glu_b.reshape(1, -1), wa, wb, wo)


QPAD = SUBLANES
IDX_PAGES_PER_STEP = 16
KV_PAGES_PER_STEP = 8


def _sample_scores_kernel(pt_ref, qs_ref, ws_ref, kn_ref, *refs):
    pages, (sp_ref, sn_ref) = refs[:IDX_PAGES_PER_STEP], refs[IDX_PAGES_PER_STEP:]
    qs, ws = qs_ref[...], ws_ref[...]

    def scores(d):
        r = jnp.maximum(d, 0.0) * ws
        return r.reshape(IDX_HEADS, QPAD, r.shape[1]).sum(axis=0)

    for kk in range(IDX_PAGES_PER_STEP):
        sp_ref[:, kk * PAGE_SIZE:(kk + 1) * PAGE_SIZE] = scores(_dot(qs, pages[kk][...].astype(BF16)))
    sn_ref[...] = scores(_dot_nt(qs, kn_ref[...]))


def _sample_thr_kernel(sp_ref, sn_ref, thr_ref, s_scr, *, past, top_k):
    s = jnp.concatenate([sp_ref[...], sn_ref[...]], axis=1)
    shape = s.shape
    kpos = lax.broadcasted_iota(jnp.int32, shape, 1)
    qidx = lax.broadcasted_iota(jnp.int32, shape, 0) % QPAD
    s_scr[...] = jnp.where(kpos <= past + qidx, s, -jnp.inf)

    def count_ge(t):
        return jnp.sum((s_scr[...] >= t).astype(jnp.int32), axis=1, keepdims=True)

    thr = _kth_largest(count_ge, (shape[0], 1), top_k)
    thr_ref[...] = jnp.broadcast_to(thr, thr_ref.shape)


def _sample_attn_kernel(pt_ref, q_ref, sp_ref, sn_ref, thr_ref, kn_ref, vn_ref, *refs):
    npg = KV_PAGES_PER_STEP
    kpages, vpages = refs[:npg], refs[npg:2 * npg]
    o_ref, m_scr, l_scr, acc_scr = refs[2 * npg:]
    pg = pl.program_id(1)
    rows_g = N_REP * QPAD

    @pl.when(pg == 0)
    def _():
        m_scr[...] = jnp.full_like(m_scr, -jnp.inf)
        l_scr[...] = jnp.zeros_like(l_scr)
        acc_scr[...] = jnp.zeros_like(acc_scr)

    thr = thr_ref[:, :1]
    q = q_ref[...]

    def update(scores, allowed, nblk, kget, vget):
        sel = scores >= thr
        if allowed is not None:
            sel = sel & allowed
        bias = jnp.where(sel, 0.0, -jnp.inf)
        bias = jnp.concatenate([bias] * N_HEADS, axis=0)
        lg = jnp.concatenate([
            jnp.concatenate([_dot_nt(q[g * rows_g:(g + 1) * rows_g], kget(kk, g)) for kk in range(nblk)], axis=1)
            for g in range(N_KV_HEADS)], axis=0) + bias
        m_old = m_scr[...]
        m_new = jnp.maximum(m_old, jnp.max(lg, axis=1, keepdims=True))
        m_safe = jnp.where(m_new == -jnp.inf, 0.0, m_new)
        alpha = jnp.exp(m_old - m_safe)
        p = jnp.exp(lg - m_safe)
        l_scr[...] = alpha * l_scr[...] + jnp.sum(p, axis=1, keepdims=True)
        m_scr[...] = m_new
        pb = p.astype(BF16)
        pv = jnp.concatenate([
            sum(_dot(pb[g * rows_g:(g + 1) * rows_g, kk * PAGE_SIZE:(kk + 1) * PAGE_SIZE], vget(kk, g))
                for kk in range(nblk))
            for g in range(N_KV_HEADS)], axis=0)
        acc_scr[...] = alpha * acc_scr[...] + pv

    head_rows = lambda g: pl.ds(g, PAGE_SIZE, stride=N_KV_HEADS)
    update(sp_ref[...], None, npg,
           lambda kk, g: kpages[kk][head_rows(g), :].astype(BF16),
           lambda kk, g: vpages[kk][head_rows(g), :].astype(BF16))

    @pl.when(pg == pl.num_programs(1) - 1)
    def _():
        shape = sn_ref.shape
        allowed = lax.broadcasted_iota(jnp.int32, shape, 1) <= lax.broadcasted_iota(jnp.int32, shape, 0)
        update(sn_ref[...], allowed, 1,
               lambda kk, g: kn_ref[:, g * HEAD_DIM:(g + 1) * HEAD_DIM],
               lambda kk, g: vn_ref[:, g * HEAD_DIM:(g + 1) * HEAD_DIM])
        o_ref[...] = acc_scr[...] / l_scr[...]


def _pad_queries(x, t_len):
    n, h, d = x.shape
    x = jnp.transpose(x.reshape(n // t_len, t_len, h, d), (0, 2, 1, 3))
    x = jnp.pad(x, ((0, 0), (0, 0), (0, QPAD - t_len), (0, 0)))
    return x.reshape(n // t_len, h * QPAD, d)


def _pad_new_keys(x, t_len):
    n, w = x.shape
    return jnp.pad(x.reshape(n // t_len, t_len, w), ((0, 0), (0, PAGE_SIZE - t_len), (0, 0)))


def _attn_sample(q_s, k_new, v_new, qi_s, wi_s, ki_new, cache_k, cache_v, cache_idx_k, page_table, page_base,
                 *, t_len):
    db, n_pages = page_table.shape
    past = n_pages * PAGE_SIZE
    top_k = min(TOPK_MAX, (past + t_len) // 4)
    kvw = N_KV_HEADS * HEAD_DIM
    sds = jax.ShapeDtypeStruct
    q = _pad_queries(q_s.reshape(-1, N_HEADS, HEAD_DIM), t_len)
    qi = _pad_queries(qi_s.reshape(-1, IDX_HEADS, IDX_DIM), t_len)
    ws = _pad_queries(wi_s.reshape(-1, IDX_HEADS, 1), t_len)
    ws = jnp.broadcast_to(ws, ws.shape[:2] + (PAGE_SIZE,))
    kn, vn, kin = (_pad_new_keys(a, t_len) for a in (k_new, v_new, ki_new))

    def idx_page_spec(per_step, kk):
        return pl.BlockSpec((None, IDX_DIM, PAGE_SIZE),
                            lambda s, g, pt: (page_base + pt[s, g * per_step + kk], 0, 0))

    def kv_page_spec(per_step, kk):
        return pl.BlockSpec((N_KV_HEADS * PAGE_SIZE, HEAD_DIM),
                            lambda s, g, pt: (page_base + pt[s, g * per_step + kk], 0))

    seq_spec = lambda r, c: pl.BlockSpec((None, r, c), lambda s, g, pt: (s, 0, 0))
    npi = IDX_PAGES_PER_STEP
    sp, sn = pl.pallas_call(
        _sample_scores_kernel,
        grid_spec=pltpu.PrefetchScalarGridSpec(
            num_scalar_prefetch=1, grid=(db, n_pages // npi),
            in_specs=[seq_spec(IDX_HEADS * QPAD, IDX_DIM), seq_spec(IDX_HEADS * QPAD, PAGE_SIZE),
                      seq_spec(PAGE_SIZE, IDX_DIM)] + [idx_page_spec(npi, kk) for kk in range(npi)],
            out_specs=[pl.BlockSpec((None, QPAD, npi * PAGE_SIZE), lambda s, g, pt: (s, 0, g)),
                       seq_spec(QPAD, PAGE_SIZE)]),
        out_shape=[sds((db, QPAD, past), F32), sds((db, QPAD, PAGE_SIZE), F32)],
        compiler_params=_cparams("parallel", "arbitrary"), name="sample_scores",
    )(page_table, qi, ws, kin, *([cache_idx_k] * npi))

    rows = db * QPAD
    rblk = min(rows, 8 * QPAD)
    assert rows % rblk == 0
    thr = pl.pallas_call(
        functools.partial(_sample_thr_kernel, past=past, top_k=top_k),
        grid=(rows // rblk,),
        in_specs=[_row_spec(rblk, past), _row_spec(rblk, PAGE_SIZE)],
        out_specs=_row_spec(rblk, LANES),
        out_shape=sds((rows, LANES), F32),
        scratch_shapes=[pltpu.VMEM((rblk, past + PAGE_SIZE), F32)],
        compiler_params=_cparams("parallel"), name="sample_threshold",
    )(sp.reshape(rows, past), sn.reshape(rows, PAGE_SIZE))

    npa = KV_PAGES_PER_STEP
    hq = N_HEADS * QPAD
    out = pl.pallas_call(
        _sample_attn_kernel,
        grid_spec=pltpu.PrefetchScalarGridSpec(
            num_scalar_prefetch=1, grid=(db, n_pages // npa),
            in_specs=[seq_spec(hq, HEAD_DIM),
                      pl.BlockSpec((None, QPAD, npa * PAGE_SIZE), lambda s, g, pt: (s, 0, g)),
                      seq_spec(QPAD, PAGE_SIZE), seq_spec(QPAD, LANES),
                      seq_spec(PAGE_SIZE, kvw), seq_spec(PAGE_SIZE, kvw)]
                     + [kv_page_spec(npa, kk) for kk in range(npa)] * 2,
            out_specs=seq_spec(hq, HEAD_DIM),
            scratch_shapes=[pltpu.VMEM((hq, 1), F32), pltpu.VMEM((hq, 1), F32), pltpu.VMEM((hq, HEAD_DIM), F32)]),
        out_shape=sds((db, hq, HEAD_DIM), F32),
        compiler_params=_cparams("parallel", "arbitrary"), name="sample_attn",
    )(page_table, q, sp, sn, thr.reshape(db, QPAD, LANES), kn, vn, *([cache_k] * npa), *([cache_v] * npa))
    out = out.reshape(db, N_HEADS, QPAD, HEAD_DIM)[:, :, :t_len]
    return jnp.transpose(out, (0, 2, 1, 3)).reshape(db * t_len, N_HEADS * HEAD_DIM)


ROW_TILE = 640
PROJ_ROW_TILE = 512
FFN_ROW_TILE = 512
FF_TILE = 512
MERGE_ROW_TILE = 256


def _layer(x_p, x_s, pos, dims, layer, w_in_all, ck, cv, cik, page_base, s_re, s_im, page_table, p):
    b, s, db, t = dims
    mp, ms = b * s, db * t
    d = x_p.shape[1]
    bf = lambda w: w.astype(BF16)
    ssm_w = d // 2
    w_g = bf(w_in_all[layer, :, w_in_all.shape[2] - 2 * d:])

    x1, xn = _ffn(x_p, x_s, p["ffn1_norm"], bf(p["ffn1_w_gate"]), bf(p["ffn1_w_up"]), bf(p["ffn1_w_down"]),
                  p["mix_norm"], n_main_rows=mp, tail=ms, split_out=False, tm=FFN_ROW_TILE, tf=FF_TILE)
    pr = _projections(xn, w_in_all, layer, w_g, p["q_norm"], p["k_norm"], pos,
                      n_main_rows=mp, tail=ms, tm=PROJ_ROW_TILE, tm_gates=ROW_TILE)

    kmat, win, wout, lp = _ssm_prep(p["ssm_lambda_re"], p["ssm_lambda_im"], p["ssm_b_re"], p["ssm_b_im"],
                                    p["ssm_c_re"], p["ssm_c_im"], p["ssm_log_dt"])
    nblk = ssm_w // LANES
    d8 = p["ssm_d"].reshape(nblk, 1, LANES)
    za8, hl_p, hl_s = _ssm(pr["u8"], kmat, win, wout, lp, d8, _state_to_blocks(s_re, s_im),
                           nseq=b, seq=s, n_s=db, t_s=t)

    b_p = _attn_prompt(pr["q_t"], pr["kb"], pr["v_t"], pr["qi_t"], pr["kib"], pr["kiw_t"],
                       batch=b, seq=s, m_out=mp)
    b_s = _attn_sample(pr["q"][mp:], pr["kb"][mp:], bf(pr["v"][mp:]), pr["qi"][mp:],
                       pr["kiw"][mp:, IDX_DIM:IDX_DIM + IDX_HEADS], pr["kib"][mp:],
                       ck, cv, cik, page_table, page_base, t_len=t)

    x2 = _merge(za8, b_p, bf(b_s), pr["gates"], x1, bf(p["glu_w"]), p["glu_b"], bf(p["w_branch_a"]),
                bf(p["w_branch_b"]), bf(p["w_out"]), tm=MERGE_ROW_TILE)
    y_p, y_s = _ffn(x2, None, p["ffn2_norm"], bf(p["ffn2_w_gate"]), bf(p["ffn2_w_up"]), bf(p["ffn2_w_down"]),
                    n_main_rows=mp, tail=ms, split_out=True, tm=FFN_ROW_TILE, tf=FF_TILE)

    hp_re, hp_im = _blocks_to_state(hl_p)
    hs_re, hs_im = _blocks_to_state(hl_s)
    k, v, ki = pr["k"], pr["v"], pr["kiw"][:, :IDX_DIM]
    kvs = (N_KV_HEADS, HEAD_DIM)
    rows = (k[:mp].reshape(b, s, *kvs), v[:mp].reshape(b, s, *kvs), ki[:mp].reshape(b, s, IDX_DIM), hp_re, hp_im,
            k[mp:].reshape(db, t, *kvs), v[mp:].reshape(db, t, *kvs), ki[mp:].reshape(db, t, IDX_DIM), hs_re, hs_im)
    return y_p, y_s, rows


def kernel(x_prompt, x_sample, cache_k, cache_v, cache_idx_k, state_ssm_re, state_ssm_im, page_table,
           ffn1_norm, ffn1_w_gate, ffn1_w_up, ffn1_w_down, mix_norm, w_in, q_norm, k_norm,
           ssm_lambda_re, ssm_lambda_im, ssm_b_re, ssm_b_im, ssm_c_re, ssm_c_im, ssm_d, ssm_log_dt,
           glu_w, glu_b, w_branch_a, w_branch_b, w_out, ffn2_norm, ffn2_w_gate, ffn2_w_up, ffn2_w_down):
    b, s, d = x_prompt.shape
    db, t, _ = x_sample.shape
    depth, n_phys = cache_k.shape[:2]
    past = page_table.shape[1] * PAGE_SIZE
    x_p, x_s = x_prompt.reshape(b * s, d), x_sample.reshape(db * t, d)
    pos = jnp.concatenate([jnp.tile(jnp.arange(s), b), jnp.tile(past + jnp.arange(t), db)])
    ck = cache_k.reshape(-1, HEAD_DIM)
    cv = cache_v.reshape(-1, HEAD_DIM)
    cik = jnp.swapaxes(cache_idx_k, 2, 3).reshape(depth * n_phys, IDX_DIM, PAGE_SIZE)
    params = dict(
        ffn1_norm=ffn1_norm, ffn1_w_gate=ffn1_w_gate, ffn1_w_up=ffn1_w_up, ffn1_w_down=ffn1_w_down,
        mix_norm=mix_norm, w_in=w_in, q_norm=q_norm, k_norm=k_norm,
        ssm_lambda_re=ssm_lambda_re, ssm_lambda_im=ssm_lambda_im, ssm_b_re=ssm_b_re, ssm_b_im=ssm_b_im,
        ssm_c_re=ssm_c_re, ssm_c_im=ssm_c_im, ssm_d=ssm_d, ssm_log_dt=ssm_log_dt, glu_w=glu_w, glu_b=glu_b,
        w_branch_a=w_branch_a, w_branch_b=w_branch_b, w_out=w_out,
        ffn2_norm=ffn2_norm, ffn2_w_gate=ffn2_w_gate, ffn2_w_up=ffn2_w_up, ffn2_w_down=ffn2_w_down)
    new = [[] for _ in range(10)]
    for l in range(depth):
        p = {name: w[l] for name, w in params.items()}
        x_p, x_s, rows = _layer(x_p, x_s, pos, (b, s, db, t), l, w_in, ck, cv, cik, l * n_phys,
                                state_ssm_re[l], state_ssm_im[l], page_table, p)
        for lst, r in zip(new, rows):
            lst.append(r)
    return (x_p.reshape(b, s, d), x_s.reshape(db, t, d)) + tuple(jnp.stack(lst) for lst in new)
```

```python
import functools
import math

import jax
import jax.numpy as jnp
from jax import lax
from jax.experimental import pallas as pl
from jax.experimental.pallas import tpu as pltpu

F32 = jnp.float32
BF16 = jnp.bfloat16

SSM_GROUP = 16
SSM_STATE = 64
N_HEADS = 8
HEAD_DIM = 128
N_KV_HEADS = 2
N_REP = N_HEADS // N_KV_HEADS
ROT_DIM = HEAD_DIM // 4
ROPE_THETA = 500000.0
IDX_HEADS = 16
IDX_DIM = 64
IDX_ROT_DIM = IDX_DIM // 4
TOPK_MAX = 256
PAGE_SIZE = 128
FFN_RES = 0.5
EPS = 1e-6

LANES = 128
SUBLANES = 8
VMEM_LIMIT_BYTES = 56 * 1024 * 1024

GROUPS_PER_BLOCK = LANES // SSM_GROUP
STATES_PER_BLOCK = GROUPS_PER_BLOCK * SSM_STATE
SSM_T = 8

INT_MIN = -(2 ** 31)


def _cparams(*sem):
    return pltpu.CompilerParams(dimension_semantics=sem, vmem_limit_bytes=VMEM_LIMIT_BYTES)


def _rms(x, g):
    return x * lax.rsqrt(jnp.mean(x * x, axis=-1, keepdims=True) + EPS) * g


def _dot(a, b):
    return jnp.dot(a, b, preferred_element_type=F32)


def _dot_nt(a, b):
    return lax.dot_general(a, b, (((1,), (1,)), ((), ())), preferred_element_type=F32)


def _ffn_kernel(*refs, nf, n_main, tm, tail, two_src, split_out, with_next_norm):
    refs = list(refs)
    x_main_ref = refs.pop(0)
    x_tail_ref = refs.pop(0) if two_src else x_main_ref
    g_ref, wg_ref, wu_ref, wd_ref = (refs.pop(0) for _ in range(4))
    g2_ref = refs.pop(0) if with_next_norm else None
    y_main_ref = refs.pop(0)
    y_tail_ref = refs.pop(0) if split_out else y_main_ref
    n2_ref = refs.pop(0) if with_next_norm else None
    xn_ref, acc_ref = refs
    i, f = pl.program_id(0), pl.program_id(1)

    def run(rows, x_ref, y_ref):
        @pl.when(f == 0)
        def _():
            xn_ref[:rows] = _rms(x_ref[:rows], g_ref[...]).astype(BF16)
            acc_ref[:rows] = jnp.zeros((rows, acc_ref.shape[1]), F32)

        xn = xn_ref[:rows]
        a = _dot(xn, wg_ref[...])
        b = _dot(xn, wu_ref[...])
        h = (a * jax.nn.sigmoid(a) * b).astype(BF16)
        acc_ref[:rows] += _dot(h, wd_ref[...])

        @pl.when(f == nf - 1)
        def _():
            y = x_ref[:rows] + FFN_RES * acc_ref[:rows]
            y_ref[:rows] = y
            if with_next_norm:
                n2_ref[:rows] = _rms(y, g2_ref[...]).astype(BF16)

    pl.when(i < n_main)(lambda: run(tm, x_main_ref, y_main_ref))
    pl.when(i == n_main)(lambda: run(tail, x_tail_ref, y_tail_ref))


def _ffn(x_main, x_tail, g, wg, wu, wd, g2=None, *, n_main_rows, tail, split_out, tm, tf):
    d = x_main.shape[1]
    nf = wg.shape[1] // tf
    n_main = n_main_rows // tm
    m = n_main_rows + tail
    two_src = x_tail is not None
    with_next = g2 is not None
    clamp = lambda i, f: (jnp.minimum(i, n_main - 1), 0)
    rows = lambda i, f: (i, 0)
    first = lambda i, f: (0, 0)
    sds = jax.ShapeDtypeStruct
    in_specs = [pl.BlockSpec((tm, d), clamp if two_src else rows)]
    args = [x_main]
    if two_src:
        in_specs.append(pl.BlockSpec((tail, d), first))
        args.append(x_tail)
    in_specs += [pl.BlockSpec((1, d), first), pl.BlockSpec((d, tf), lambda i, f: (0, f)),
                 pl.BlockSpec((d, tf), lambda i, f: (0, f)), pl.BlockSpec((tf, d), lambda i, f: (f, 0))]
    args += [g.reshape(1, d), wg, wu, wd]
    if with_next:
        in_specs.append(pl.BlockSpec((1, d), first))
        args.append(g2.reshape(1, d))
    if split_out:
        out_shape = [sds((n_main_rows, d), F32), sds((tail, d), F32)]
        out_specs = [pl.BlockSpec((tm, d), clamp), pl.BlockSpec((tail, d), first)]
    else:
        out_shape = [sds((m, d), F32)]
        out_specs = [pl.BlockSpec((tm, d), rows)]
    if with_next:
        out_shape.append(sds((m, d), BF16))
        out_specs.append(pl.BlockSpec((tm, d), rows))
    return pl.pallas_call(
        functools.partial(_ffn_kernel, nf=nf, n_main=n_main, tm=tm, tail=tail, two_src=two_src,
                          split_out=split_out, with_next_norm=with_next),
        grid=(n_main + 1, nf),
        in_specs=in_specs,
        out_specs=out_specs,
        out_shape=out_shape,
        scratch_shapes=[pltpu.VMEM((tm, d), BF16), pltpu.VMEM((tm, d), F32)],
        compiler_params=_cparams("arbitrary", "arbitrary"),
        name="ffn" + ("_norm" if with_next else ""),
    )(*args)


def _rope_tables(pos, rot_dim, width, tile):
    half = rot_dim // 2
    m = pos.shape[0]
    freqs = ROPE_THETA ** (-jnp.arange(half, dtype=F32) * 2.0 / rot_dim)
    ang = pos.astype(F32)[:, None] * freqs[None, :]
    cos, sin = jnp.cos(ang), jnp.sin(ang)
    zh = jnp.zeros((m, half), F32)
    rest = width - rot_dim
    c = jnp.concatenate([cos, cos, jnp.ones((m, rest), F32)], axis=1)
    s1 = jnp.concatenate([-sin, zh, jnp.zeros((m, rest), F32)], axis=1)
    s2 = jnp.concatenate([zh, sin, jnp.zeros((m, rest), F32)], axis=1)
    if tile:
        reps = LANES // width
        return tuple(jnp.tile(t, (1, reps)) for t in (c, s1, s2))
    pad = LANES - width
    return (jnp.pad(c, ((0, 0), (0, pad)), constant_values=1.0),
            jnp.pad(s1, ((0, 0), (0, pad))), jnp.pad(s2, ((0, 0), (0, pad))))


def _rope(x, c, s1, s2, half):
    return x * c + pltpu.roll(x, LANES - half, 1) * s1 + pltpu.roll(x, half, 1) * s2


def _proj_rows(xn_ref, w_refs, wbf_ref, n_main, tm, tail, chunk, epilogue):
    i = pl.program_id(0)

    @pl.when(i == 0)
    def _():
        off = 0
        for w_ref in w_refs:
            n = w_ref.shape[0]
            wbf_ref[:, off:off + n] = w_ref[...].astype(F32).T.astype(BF16)
            off += n

    def run(rows):
        for r0 in range(0, rows, chunk):
            rs = slice(r0, min(r0 + chunk, rows))
            epilogue(_dot(xn_ref[rs, :], wbf_ref[...]), rs)

    pl.when(i < n_main)(lambda: run(tm))
    pl.when(i == n_main)(lambda: run(tail))


def _u_proj_kernel(xn_ref, w_ref, u_ref, wbf_ref, **kw):
    def epilogue(z, rs):
        for j in range(z.shape[1] // LANES):
            u_ref[j, rs, :] = z[:, j * LANES:(j + 1) * LANES]
    _proj_rows(xn_ref, [w_ref], wbf_ref, epilogue=epilogue, **kw)


def _q_proj_kernel(xn_ref, w_ref, g_ref, c_ref, s1_ref, s2_ref, qn_ref, qt_ref, wbf_ref, **kw):
    def epilogue(z, rs):
        c, s1, s2, g = c_ref[rs, :], s1_ref[rs, :], s2_ref[rs, :], g_ref[...]
        heads = []
        for h in range(N_HEADS):
            x = _rms(z[:, h * HEAD_DIM:(h + 1) * HEAD_DIM], g)
            heads.append(_rope(x, c, s1, s2, ROT_DIM // 2) * (HEAD_DIM ** -0.5))
        q = jnp.concatenate(heads, axis=1)
        qn_ref[rs, :] = q.astype(BF16)
        qt_ref[:, rs] = q.T.astype(BF16)
    _proj_rows(xn_ref, [w_ref], wbf_ref, epilogue=epilogue, **kw)


def _kv_proj_kernel(xn_ref, w_ref, g_ref, c_ref, s1_ref, s2_ref, k_ref, kb_ref, v_ref, vt_ref, wbf_ref, **kw):
    def epilogue(z, rs):
        c, s1, s2, g = c_ref[rs, :], s1_ref[rs, :], s2_ref[rs, :], g_ref[...]
        heads = []
        for h in range(N_KV_HEADS):
            x = _rms(z[:, h * HEAD_DIM:(h + 1) * HEAD_DIM], g)
            heads.append(_rope(x, c, s1, s2, ROT_DIM // 2))
        k = jnp.concatenate(heads, axis=1)
        v = z[:, N_KV_HEADS * HEAD_DIM:]
        k_ref[rs, :] = k
        kb_ref[rs, :] = k.astype(BF16)
        v_ref[rs, :] = v
        vt_ref[:, rs] = v.T.astype(BF16)
    _proj_rows(xn_ref, [w_ref], wbf_ref, epilogue=epilogue, **kw)


def _qi_proj_kernel(xn_ref, wa_ref, wb_ref, c_ref, s1_ref, s2_ref, qn_ref, qt_ref, wbf_ref, **kw):
    def epilogue(z, rs):
        c, s1, s2 = c_ref[rs, :], s1_ref[rs, :], s2_ref[rs, :]
        cols = []
        for j in range(z.shape[1] // LANES):
            x = z[:, j * LANES:(j + 1) * LANES]
            cols.append(_rope(x, c, s1, s2, IDX_ROT_DIM // 2) * (IDX_DIM ** -0.5))
        q = jnp.concatenate(cols, axis=1)
        qn_ref[rs, :] = q.astype(BF16)
        qt_ref[:, rs] = q.T.astype(BF16)
    _proj_rows(xn_ref, [wa_ref, wb_ref], wbf_ref, epilogue=epilogue, **kw)


def _kiwi_proj_kernel(xn_ref, w_ref, c_ref, s1_ref, s2_ref, scale_ref, o_ref, ot_ref, kb_ref, wbf_ref, **kw):
    def epilogue(z, rs):
        y = _rope(z, c_ref[rs, :], s1_ref[rs, :], s2_ref[rs, :], IDX_ROT_DIM // 2) * scale_ref[...]
        o_ref[rs, :] = y
        ot_ref[:, rs] = y.T
        kb_ref[rs, :] = y[:, :IDX_DIM].astype(BF16)
    _proj_rows(xn_ref, [w_ref], wbf_ref, epilogue=epilogue, **kw)


def _gate_proj_kernel(xn_ref, w_ref, o_ref):
    o_ref[...] = jax.nn.sigmoid(_dot(xn_ref[...], w_ref[...])).astype(BF16)


def _row_spec(tm, n):
    return pl.BlockSpec((tm, n), lambda i: (i, 0))


def _full_spec(shape):
    return pl.BlockSpec(shape, lambda i: (0,) * len(shape))


PROJ_CHUNK = 256


def _projections(xn, w_in_t, layer, w_g, q_norm, k_norm, pos, *, n_main_rows, tail, tm, tm_gates):
    m, d = xn.shape
    n_main = n_main_rows // tm
    assert n_main * tm == n_main_rows and n_main_rows + tail == m and tail <= tm
    grid = (n_main + 1,)
    ssm_w, attn_w, kv_w, idx_w = d // 2, N_HEADS * HEAD_DIM, N_KV_HEADS * HEAD_DIM, IDX_HEADS * IDX_DIM
    hd_tabs = _rope_tables(pos, ROT_DIM, HEAD_DIM, True)
    ix_tabs = _rope_tables(pos, IDX_ROT_DIM, IDX_DIM, True)
    kw_tabs = _rope_tables(pos, IDX_ROT_DIM, IDX_DIM, False)
    tab_specs = [_row_spec(tm, LANES)] * 3
    xs = _row_spec(tm, d)
    sds = jax.ShapeDtypeStruct
    seq = _cparams("arbitrary")
    kw = dict(n_main=n_main, tm=tm, tail=tail, chunk=PROJ_CHUNK)

    def wcols(start, width):
        assert start % width == 0
        return pl.BlockSpec((None, width, d), lambda i: (layer, start // width, 0), pipeline_mode=pl.Buffered(1))

    def call(kernel, name, w_specs, n_w, extra_specs, extra_args, out_specs, out_shape):
        return pl.pallas_call(
            functools.partial(kernel, **kw), grid=grid,
            in_specs=[xs] + w_specs + extra_specs, out_specs=out_specs, out_shape=out_shape,
            scratch_shapes=[pltpu.VMEM((d, n_w), BF16)], compiler_params=seq, name=name,
        )(xn, *([w_in_t] * len(w_specs)), *extra_args)

    tcol = lambda n: pl.BlockSpec((n, tm), lambda i: (0, i))
    norm_spec = _full_spec((1, HEAD_DIM))
    u8 = call(_u_proj_kernel, "proj_u", [wcols(0, ssm_w)], ssm_w, [], [],
              pl.BlockSpec((ssm_w // LANES, tm, LANES), lambda i: (0, i, 0)), sds((ssm_w // LANES, m, LANES), F32))
    q_nat, q_t = call(_q_proj_kernel, "proj_q", [wcols(ssm_w, attn_w)], attn_w,
                      [norm_spec] + tab_specs, [q_norm.reshape(1, HEAD_DIM), *hd_tabs],
                      [_row_spec(tm, attn_w), tcol(attn_w)], [sds((m, attn_w), BF16), sds((attn_w, m), BF16)])
    off_kv = ssm_w + attn_w
    k, kb, v, v_t = call(_kv_proj_kernel, "proj_kv", [wcols(off_kv, 2 * kv_w)], 2 * kv_w,
                         [norm_spec] + tab_specs, [k_norm.reshape(1, HEAD_DIM), *hd_tabs],
                         [_row_spec(tm, kv_w), _row_spec(tm, kv_w), _row_spec(tm, kv_w), tcol(kv_w)],
                         [sds((m, kv_w), F32), sds((m, kv_w), BF16), sds((m, kv_w), F32), sds((kv_w, m), BF16)])
    off_qi = off_kv + 2 * kv_w
    half = idx_w // 2
    qi_nat, qi_t = call(_qi_proj_kernel, "proj_qi", [wcols(off_qi, half), wcols(off_qi + half, half)], idx_w,
                        tab_specs, ix_tabs,
                        [_row_spec(tm, idx_w), tcol(idx_w)], [sds((m, idx_w), BF16), sds((idx_w, m), BF16)])
    off_kw = off_qi + idx_w
    lane = jnp.arange(LANES)
    kw_scale = jnp.where(lane < IDX_DIM, 1.0, jnp.where(lane < IDX_DIM + IDX_HEADS, IDX_HEADS ** -0.5, 0.0))
    kiw, kiw_t, kib = call(_kiwi_proj_kernel, "proj_kiwi", [wcols(off_kw, LANES)], LANES,
                           tab_specs + [_full_spec((1, LANES))], [*kw_tabs, kw_scale.astype(F32).reshape(1, LANES)],
                           [_row_spec(tm, LANES), tcol(LANES), _row_spec(tm, IDX_DIM)],
                           [sds((m, LANES), F32), sds((LANES, m), F32), sds((m, IDX_DIM), BF16)])

    ng = w_g.shape[1]
    tn = 1024
    gates = pl.pallas_call(
        _gate_proj_kernel, grid=(m // tm_gates, ng // tn),
        in_specs=[pl.BlockSpec((tm_gates, d), lambda i, j: (i, 0)), pl.BlockSpec((d, tn), lambda i, j: (0, j))],
        out_specs=pl.BlockSpec((tm_gates, tn), lambda i, j: (i, j)),
        out_shape=sds((m, ng), BF16), compiler_params=_cparams("parallel", "parallel"), name="proj_gates",
    )(xn, w_g)
    return dict(u8=u8, q=q_nat, q_t=q_t, k=k, kb=kb, v=v, v_t=v_t, qi=qi_nat, qi_t=qi_t,
                kiw=kiw, kiw_t=kiw_t, kib=kib, gates=gates)


def _ssm_prep_kernel(lre_ref, lim_ref, ldt_ref, bre_ref, bim_ref, cre_ref, cim_ref,
                     k_ref, win_ref, wout_ref, lp_ref):
    ns = STATES_PER_BLOCK
    lre, lim = lre_ref[...], lim_ref[...]
    dt = jnp.exp(ldt_ref[...])
    a, th = lre * dt, lim * dt

    def power(l):
        mag = jnp.exp(a * float(l))
        return mag * jnp.cos(th * float(l)), mag * jnp.sin(th * float(l))

    pw = [power(l) for l in range(SSM_T + 1)]
    xr, xi = pw[1][0] - 1.0, pw[1][1]
    den = lre * lre + lim * lim
    cr, ci = (xr * lre + xi * lim) / den, (xi * lre - xr * lim) / den
    bre, bim = bre_ref[...], bim_ref[...]
    bbr, bbi = bre * cr - bim * ci, bre * ci + bim * cr
    cre, cim = cre_ref[...], cim_ref[...]
    hi = lax.Precision.HIGHEST
    nt = (((1,), (1,)), ((), ()))
    lag = []
    for l in range(SSM_T):
        pr, pi = pw[l]
        blr, bli = bbr * pr - bbi * pi, bbr * pi + bbi * pr
        m = (lax.dot_general(blr, cre, nt, precision=hi, preferred_element_type=F32)
             - lax.dot_general(bli, cim, nt, precision=hi, preferred_element_type=F32))
        lag.append(m.astype(BF16))
        t = SSM_T - 1 - l
        win_ref[t * LANES:(t + 1) * LANES, :ns] = blr.astype(BF16)
        win_ref[t * LANES:(t + 1) * LANES, ns:] = bli.astype(BF16)
    zero = jnp.zeros((LANES, LANES), BF16)
    for t in range(SSM_T):
        pr, pi = pw[t + 1]
        wout_ref[t * LANES:(t + 1) * LANES, :ns] = (cre * pr - cim * pi).astype(BF16)
        wout_ref[t * LANES:(t + 1) * LANES, ns:] = (-(cre * pi + cim * pr)).astype(BF16)
        lp_ref[t:t + 1, :ns] = pr
        lp_ref[t:t + 1, ns:] = pi
        for t2 in range(SSM_T):
            k_ref[t * LANES:(t + 1) * LANES, t2 * LANES:(t2 + 1) * LANES] = lag[t2 - t] if t2 >= t else zero


def _block_diag_groups(w):
    g, h, p = w.shape
    nb = g // GROUPS_PER_BLOCK
    w = w.reshape(nb, GROUPS_PER_BLOCK, h, p)
    eye = jnp.eye(GROUPS_PER_BLOCK, dtype=w.dtype)
    out = w[:, :, :, None, :] * eye[None, :, None, :, None]
    return out.reshape(nb, GROUPS_PER_BLOCK * h, GROUPS_PER_BLOCK * p)


def _ssm_prep(lam_re, lam_im, b_re, b_im, c_re, c_im, log_dt):
    g, p = lam_re.shape
    nb = g // GROUPS_PER_BLOCK
    ns = STATES_PER_BLOCK
    tl = SSM_T * LANES
    vec = lambda a: a.reshape(nb, 1, ns)
    ldt = vec(jnp.broadcast_to(log_dt[:, None], (g, p)))
    bt = lambda b: _block_diag_groups(jnp.swapaxes(b, 1, 2))
    vspec = pl.BlockSpec((None, 1, ns), lambda j: (j, 0, 0))
    mspec = pl.BlockSpec((None, LANES, ns), lambda j: (j, 0, 0))
    sds = jax.ShapeDtypeStruct
    return pl.pallas_call(
        _ssm_prep_kernel, grid=(nb,),
        in_specs=[vspec, vspec, vspec, mspec, mspec, mspec, mspec],
        out_specs=[pl.BlockSpec((None, tl, tl), lambda j: (j, 0, 0)),
                   pl.BlockSpec((None, tl, 2 * ns), lambda j: (j, 0, 0)),
                   pl.BlockSpec((None, tl, 2 * ns), lambda j: (j, 0, 0)),
                   pl.BlockSpec((None, SSM_T, 2 * ns), lambda j: (j, 0, 0))],
        out_shape=[sds((nb, tl, tl), BF16), sds((nb, tl, 2 * ns), BF16), sds((nb, tl, 2 * ns), BF16),
                   sds((nb, SSM_T, 2 * ns), F32)],
        compiler_params=_cparams("parallel"), name="ssm_prep",
    )(vec(lam_re), vec(lam_im), ldt, bt(b_re), bt(b_im), _block_diag_groups(c_re), _block_diag_groups(c_im))


def _ssm_kernel(u_ref, k_ref, win_ref, wout_ref, lp_ref, d_ref, h0s_ref, za_ref, hlp_ref, hls_ref, x_scr, hs_scr,
                *, nseq, seq, n_s, t_s):
    ns = STATES_PER_BLOCK
    nk = ns // LANES
    d = d_ref[...]
    split = lambda a: [a[:, k * LANES:(k + 1) * LANES] for k in range(2 * nk)]

    def gather(row0, n_rows, t_steps):
        return [u_ref[pl.ds(row0 + t, n_rows, stride=t_steps), :] for t in range(t_steps)]

    def advance(h, x, t_steps):
        lam = [lp_ref[t_steps - 1:t_steps, k * LANES:(k + 1) * LANES] for k in range(2 * nk)]
        re = [lam[k] * h[k] - lam[nk + k] * h[nk + k] + x[k] for k in range(nk)]
        im = [lam[k] * h[nk + k] + lam[nk + k] * h[k] + x[nk + k] for k in range(nk)]
        return re + im

    def emit(row0, n_rows, t_steps, cols, hs, kmat, wout):
        u = jnp.concatenate(cols, axis=1).astype(BF16)
        y = _dot(u, kmat) + _dot_nt(hs.astype(BF16), wout)
        for t in range(t_steps):
            yt = y[:, t * LANES:(t + 1) * LANES] + d * cols[t]
            za_ref[pl.ds(row0 + t, n_rows, stride=t_steps), :] = jax.nn.gelu(yt, approximate=True)

    c_per = seq // SSM_T
    for b in range(nseq):
        u = jnp.concatenate(gather(b * seq, c_per, SSM_T), axis=1).astype(BF16)
        x = _dot(u, win_ref[...])
        for k in range(2 * nk):
            x_scr[k, b * c_per:(b + 1) * c_per, :] = x[:, k * LANES:(k + 1) * LANES]

    def step(c, h):
        rows = pl.ds(c, nseq, stride=c_per)
        for k in range(2 * nk):
            hs_scr.at[k][rows, :] = h[k]
        return tuple(advance(h, [x_scr.at[k][rows, :] for k in range(2 * nk)], SSM_T))

    h_last = lax.fori_loop(0, c_per, step, tuple(jnp.zeros((nseq, LANES), F32) for _ in range(2 * nk)),
                           unroll=4)
    hlp_ref[...] = jnp.concatenate(list(h_last), axis=1)
    for b in range(nseq):
        hs = jnp.concatenate([hs_scr[k, b * c_per:(b + 1) * c_per, :] for k in range(2 * nk)], axis=1)
        emit(b * seq, c_per, SSM_T, gather(b * seq, c_per, SSM_T), hs, k_ref[...], wout_ref[...])

    row0, tl = nseq * seq, t_s * LANES
    cols = gather(row0, n_s, t_s)
    u = jnp.concatenate(cols, axis=1).astype(BF16)
    x = _dot(u, win_ref[(SSM_T - t_s) * LANES:, :])
    h0 = h0s_ref[...]
    hls_ref[...] = jnp.concatenate(advance(split(h0), split(x), t_s), axis=1)
    emit(row0, n_s, t_s, cols, h0, k_ref[:tl, :tl], wout_ref[:tl, :])


def _ssm(u8, kmat, win, wout, lp, d8, h0s, *, nseq, seq, n_s, t_s):
    nb, m_total, _ = u8.shape
    ns = STATES_PER_BLOCK
    assert m_total == nseq * seq + n_s * t_s and seq % SSM_T == 0 and t_s <= SSM_T
    rows = nseq * seq // SSM_T
    tl = SSM_T * LANES
    sds = jax.ShapeDtypeStruct
    blk = lambda r, c: pl.BlockSpec((None, r, c), lambda j: (j, 0, 0))
    return pl.pallas_call(
        functools.partial(_ssm_kernel, nseq=nseq, seq=seq, n_s=n_s, t_s=t_s),
        grid=(nb,),
        in_specs=[blk(m_total, LANES), blk(tl, tl), blk(tl, 2 * ns), blk(tl, 2 * ns), blk(SSM_T, 2 * ns),
                  blk(1, LANES), blk(n_s, 2 * ns)],
        out_specs=[blk(m_total, LANES), blk(nseq, 2 * ns), blk(n_s, 2 * ns)],
        out_shape=[sds((nb, m_total, LANES), F32), sds((nb, nseq, 2 * ns), F32), sds((nb, n_s, 2 * ns), F32)],
        scratch_shapes=[pltpu.VMEM((2 * ns // LANES, rows, LANES), F32)] * 2,
        compiler_params=_cparams("parallel"), name="ssm",
    )(u8, kmat, win, wout, lp, d8, h0s)


def _state_to_blocks(re, im):
    n, g, p = re.shape
    nb = g // GROUPS_PER_BLOCK
    f = lambda a: jnp.transpose(a.reshape(n, nb, GROUPS_PER_BLOCK * p), (1, 0, 2))
    return jnp.concatenate([f(re), f(im)], axis=-1)


def _blocks_to_state(h):
    nb, n, w = h.shape
    ns = w // 2
    f = lambda a: jnp.transpose(a, (1, 0, 2)).reshape(n, nb * GROUPS_PER_BLOCK, ns // GROUPS_PER_BLOCK)
    return f(h[..., :ns]), f(h[..., ns:])


def _key_to_float(c):
    return lax.bitcast_convert_type(c ^ ((c >> 31) & 0x7FFFFFFF), F32)


def _kth_largest(count_ge, shape, top_k):
    def body(it, u):
        bit = jnp.left_shift(jnp.int32(1), 31 - it)
        cand = u | bit
        return jnp.where(count_ge(_key_to_float(cand ^ INT_MIN)) >= top_k, cand, u)
    return _key_to_float(lax.fori_loop(0, 32, body, jnp.zeros(shape, jnp.int32)) ^ INT_MIN)


def _attn_prompt_kernel(qt_ref, kb_ref, vt_ref, qit_ref, kib_ref, wt_ref, o_ref, s_scr, bias_scr,
                        *, seq, top_k, key_chunk):
    qb = pl.program_id(1)
    tq = o_ref.shape[0]

    def run(nk):
        kib = kib_ref[:nk]
        s = jnp.zeros((nk, tq), F32)
        for h in range(IDX_HEADS):
            d = _dot(kib, qit_ref[h * IDX_DIM:(h + 1) * IDX_DIM, :])
            s = s + jnp.maximum(d, 0.0) * wt_ref[IDX_DIM + h:IDX_DIM + h + 1, :]
        kpos = lax.broadcasted_iota(jnp.int32, (nk, tq), 0)
        qpos = qb * tq + lax.broadcasted_iota(jnp.int32, (nk, tq), 1)
        allowed = kpos <= qpos
        s_scr[:nk] = jnp.where(allowed, s, -jnp.inf)

        def count_ge(t):
            n_acc = 8
            acc = [jnp.zeros((SUBLANES, tq), jnp.int32) for _ in range(n_acc)]
            for j in range(nk // SUBLANES):
                rows = s_scr[j * SUBLANES:(j + 1) * SUBLANES]
                acc[j % n_acc] = acc[j % n_acc] + (rows >= t).astype(jnp.int32)
            while len(acc) > 1:
                acc = [a + b for a, b in zip(acc[::2], acc[1::2])]
            return jnp.sum(acc[0], axis=0, keepdims=True)

        thr = _kth_largest(count_ge, (1, tq), top_k)
        few = qpos < top_k - 1
        bias_scr[:nk] = jnp.where(allowed & ((s_scr[:nk] >= thr) | few), 0.0, -jnp.inf)

        outs = []
        for g in range(N_KV_HEADS):
            kg = kb_ref[:nk, g * HEAD_DIM:(g + 1) * HEAD_DIM]
            vtg = vt_ref[g * HEAD_DIM:(g + 1) * HEAD_DIM, :nk]
            for r in range(N_REP):
                h = g * N_REP + r
                lg = _dot(kg, qt_ref[h * HEAD_DIM:(h + 1) * HEAD_DIM, :]) + bias_scr[:nk]
                p = jnp.exp(lg - jnp.max(lg, axis=0, keepdims=True))
                den = jnp.sum(p, axis=0, keepdims=True)
                outs.append(_dot(vtg, p.astype(BF16)) / den)
        o_ref[...] = jnp.concatenate(outs, axis=0).T.astype(BF16)

    n_var = seq // key_chunk
    need = (qb * tq + tq + key_chunk - 1) // key_chunk
    for v in range(1, n_var + 1):
        pl.when(need == v)(functools.partial(run, v * key_chunk))


def _attn_prompt(q_t, kb, v_t, qi_t, kib, kiw_t, *, batch, seq, m_out, tq=128, key_chunk=256):
    top_k = min(TOPK_MAX, seq // 4)
    nq = seq // tq
    assert seq % key_chunk == 0 and key_chunk % tq == 0
    aw = N_HEADS * HEAD_DIM
    qcol = lambda n: pl.BlockSpec((n, tq), lambda b, i: (0, b * nq + i))
    return pl.pallas_call(
        functools.partial(_attn_prompt_kernel, seq=seq, top_k=top_k, key_chunk=key_chunk),
        grid=(batch, nq),
        in_specs=[qcol(aw),
                  pl.BlockSpec((seq, N_KV_HEADS * HEAD_DIM), lambda b, i: (b, 0)),
                  pl.BlockSpec((N_KV_HEADS * HEAD_DIM, seq), lambda b, i: (0, b)),
                  qcol(IDX_HEADS * IDX_DIM),
                  pl.BlockSpec((seq, IDX_DIM), lambda b, i: (b, 0)),
                  qcol(LANES)],
        out_specs=pl.BlockSpec((tq, aw), lambda b, i: (b * nq + i, 0)),
        out_shape=jax.ShapeDtypeStruct((m_out, aw), BF16),
        scratch_shapes=[pltpu.VMEM((seq, tq), F32), pltpu.VMEM((seq, tq), F32)],
        compiler_params=_cparams("parallel", "parallel"), name="attn_prompt",
    )(q_t, kb, v_t, qi_t, kib, kiw_t)


def _merge_kernel(za_ref, bm_ref, bt_ref, g_ref, x_ref, gw_ref, gb_ref, wa_ref, wb_ref, wo_ref, o_ref,
                  *, n_main, tm, tail):
    d = x_ref.shape[1]

    def run(rows, b_ref):
        za = jnp.concatenate([za_ref[j, :rows] for j in range(za_ref.shape[0])], axis=1)
        a_out = za * jax.nn.sigmoid(_dot(za.astype(BF16), gw_ref[...]) + gb_ref[...])
        merged = (g_ref[:rows, :d] * _dot(a_out.astype(BF16), wa_ref[...])
                  + g_ref[:rows, d:] * _dot(b_ref[:rows], wb_ref[...]))
        o_ref[:rows] = x_ref[:rows] + _dot(merged.astype(BF16), wo_ref[...])

    i = pl.program_id(0)
    pl.when(i < n_main)(lambda: run(tm, bm_ref))
    pl.when(i == n_main)(lambda: run(tail, bt_ref))


def _merge(za8, b_main, b_tail, gates, x, glu_w, glu_b, wa, wb, wo, *, tm):
    m, d = x.shape
    nb = za8.shape[0]
    n_main_rows, aw = b_main.shape
    tail = b_tail.shape[0]
    n_main = n_main_rows // tm
    assert n_main * tm == n_main_rows and n_main_rows + tail == m and tail <= tm
    resident = lambda shape: pl.BlockSpec(shape, lambda i: (0,) * len(shape), pipeline_mode=pl.Buffered(1))
    return pl.pallas_call(
        functools.partial(_merge_kernel, n_main=n_main, tm=tm, tail=tail), grid=(n_main + 1,),
        in_specs=[pl.BlockSpec((nb, tm, LANES), lambda i: (0, i, 0)),
                  pl.BlockSpec((tm, aw), lambda i: (jnp.minimum(i, n_main - 1), 0)),
                  pl.BlockSpec((tail, aw), lambda i: (0, 0)),
                  _row_spec(tm, 2 * d), _row_spec(tm, d), resident(glu_w.shape), resident((1, glu_w.shape[1])),
                  resident(wa.shape), resident(wb.shape), resident(wo.shape)],
        out_specs=_row_spec(tm, d),
        out_shape=jax.ShapeDtypeStruct((m, d), F32),
        compiler_params=_cparams("arbitrary"), name="merge",
    )(za8, b_main, b_tail, gates, x, glu_w, glu_b.reshape(1, -1), wa, wb, wo)


QPAD = SUBLANES
CHUNK_PAGES = 8


def _page_copy(pt_ref, src_ref, buf_ref, sem, seq, slot, p, page_base, rows):
    start = pl.multiple_of((page_base + pt_ref[seq, p]) * rows, rows)
    return pltpu.make_async_copy(src_ref.at[pl.ds(start, rows), :],
                                 buf_ref.at[slot, pl.ds(p * rows, rows), :], sem)


def _prefetch_pages(n_pages, copies):
    s, n_seq = pl.program_id(0), pl.num_programs(0)
    slot = s % 2

    def start_all(seq, sl):
        def body(p, carry):
            for copy in copies:
                copy(seq, sl, p).start()
            return carry
        lax.fori_loop(0, n_pages, body, 0)

    @pl.when(s == 0)
    def _():
        start_all(0, 0)

    @pl.when(s + 1 < n_seq)
    def _():
        start_all(s + 1, 1 - slot)

    def wait_body(p, carry):
        for copy in copies:
            copy(s, slot, p).wait()
        return carry
    lax.fori_loop(0, n_pages, wait_body, 0)
    return slot


def _sample_scores_kernel(pt_ref, qs_ref, ws_ref, kn_ref, cik_ref, sp_ref, sn_ref, buf_ref, sem_ref,
                          *, page_base, n_pages):
    slot = _prefetch_pages(n_pages, [
        lambda seq, sl, p: _page_copy(pt_ref, cik_ref, buf_ref, sem_ref.at[sl], seq, sl, p, page_base, IDX_DIM)])
    qs, ws = qs_ref[...], ws_ref[:, :1]

    def scores(d):
        r = jnp.maximum(d, 0.0) * ws
        return r.reshape(IDX_HEADS, QPAD, r.shape[1]).sum(axis=0)

    nkc = CHUNK_PAGES * PAGE_SIZE
    for c in range(n_pages // CHUNK_PAGES):
        keys_t = jnp.concatenate(
            [buf_ref[slot, (c * CHUNK_PAGES + kk) * IDX_DIM:(c * CHUNK_PAGES + kk + 1) * IDX_DIM, :]
             for kk in range(CHUNK_PAGES)], axis=1).astype(BF16)
        sp_ref[:, c * nkc:(c + 1) * nkc] = scores(_dot(qs, keys_t))
    sn_ref[...] = scores(_dot_nt(qs, kn_ref[...]))


def _sample_thr_kernel(sp_ref, sn_ref, thr_ref, s_scr, *, past, top_k):
    s = jnp.concatenate([sp_ref[...], sn_ref[...]], axis=1)
    shape = s.shape
    kpos = lax.broadcasted_iota(jnp.int32, shape, 1)
    qidx = lax.broadcasted_iota(jnp.int32, shape, 0) % QPAD
    s_scr[...] = jnp.where(kpos <= past + qidx, s, -jnp.inf)

    def count_ge(t):
        return jnp.sum((s_scr[...] >= t).astype(jnp.int32), axis=1, keepdims=True)

    thr = _kth_largest(count_ge, (shape[0], 1), top_k)
    thr_ref[...] = jnp.broadcast_to(thr, thr_ref.shape)


def _sample_attn_kernel(pt_ref, q_ref, sp_ref, sn_ref, thr_ref, kn_ref, vn_ref, ck_ref, cv_ref, o_ref,
                        kbuf, vbuf, sem_ref, lg_scr, *, page_base, n_pages):
    page_rows = N_KV_HEADS * PAGE_SIZE
    slot = _prefetch_pages(n_pages, [
        lambda seq, sl, p: _page_copy(pt_ref, ck_ref, kbuf, sem_ref.at[0, sl], seq, sl, p, page_base, page_rows),
        lambda seq, sl, p: _page_copy(pt_ref, cv_ref, vbuf, sem_ref.at[1, sl], seq, sl, p, page_base, page_rows)])
    rows_g = N_REP * QPAD
    nkc = CHUNK_PAGES * PAGE_SIZE
    n_chunks = n_pages // CHUNK_PAGES
    past = n_pages * PAGE_SIZE
    thr = thr_ref[:, :1]
    q = q_ref[...]
    qg = [q[g * rows_g:(g + 1) * rows_g] for g in range(N_KV_HEADS)]
    head_rows = lambda c, g: pl.ds(c * CHUNK_PAGES * page_rows + g, nkc, stride=N_KV_HEADS)

    def bias(scores, allowed=None):
        sel = scores >= thr
        if allowed is not None:
            sel = sel & allowed
        return jnp.concatenate([jnp.where(sel, 0.0, -jnp.inf)] * N_HEADS, axis=0)

    for c in range(n_chunks):
        lg = jnp.concatenate([_dot_nt(qg[g], kbuf.at[slot][head_rows(c, g), :].astype(BF16))
                              for g in range(N_KV_HEADS)], axis=0)
        lg_scr[:, c * nkc:(c + 1) * nkc] = lg + bias(sp_ref[:, c * nkc:(c + 1) * nkc])
    shape = sn_ref.shape
    allowed = lax.broadcasted_iota(jnp.int32, shape, 1) <= lax.broadcasted_iota(jnp.int32, shape, 0)
    lg = jnp.concatenate([_dot_nt(qg[g], kn_ref[:, g * HEAD_DIM:(g + 1) * HEAD_DIM])
                          for g in range(N_KV_HEADS)], axis=0)
    lg_scr[:, past:] = lg + bias(sn_ref[...], allowed)

    lg = lg_scr[...]
    p = jnp.exp(lg - jnp.max(lg, axis=1, keepdims=True))
    den = jnp.sum(p, axis=1, keepdims=True)
    pb = p.astype(BF16)
    acc = jnp.concatenate([_dot(pb[g * rows_g:(g + 1) * rows_g, past:], vn_ref[:, g * HEAD_DIM:(g + 1) * HEAD_DIM])
                           for g in range(N_KV_HEADS)], axis=0)
    for c in range(n_chunks):
        acc = acc + jnp.concatenate(
            [_dot(pb[g * rows_g:(g + 1) * rows_g, c * nkc:(c + 1) * nkc],
                  vbuf.at[slot][head_rows(c, g), :].astype(BF16)) for g in range(N_KV_HEADS)], axis=0)
    o_ref[...] = acc / den


def _pad_queries(x, t_len):
    n, h, d = x.shape
    x = jnp.transpose(x.reshape(n // t_len, t_len, h, d), (0, 2, 1, 3))
    x = jnp.pad(x, ((0, 0), (0, 0), (0, QPAD - t_len), (0, 0)))
    return x.reshape(n // t_len, h * QPAD, d)


def _pad_new_keys(x, t_len):
    n, w = x.shape
    return jnp.pad(x.reshape(n // t_len, t_len, w), ((0, 0), (0, PAGE_SIZE - t_len), (0, 0)))


def _attn_sample(q_s, k_new, v_new, qi_s, wi_s, ki_new, cache_k, cache_v, cache_idx_k, page_table, page_base,
                 *, t_len):
    db, n_pages = page_table.shape
    past = n_pages * PAGE_SIZE
    top_k = min(TOPK_MAX, (past + t_len) // 4)
    kvw = N_KV_HEADS * HEAD_DIM
    sds = jax.ShapeDtypeStruct
    q = _pad_queries(q_s.reshape(-1, N_HEADS, HEAD_DIM), t_len)
    qi = _pad_queries(qi_s.reshape(-1, IDX_HEADS, IDX_DIM), t_len)
    ws = _pad_queries(wi_s.reshape(-1, IDX_HEADS, 1), t_len)
    ws = jnp.broadcast_to(ws, ws.shape[:2] + (PAGE_SIZE,))
    kn, vn, kin = (_pad_new_keys(a, t_len) for a in (k_new, v_new, ki_new))

    assert n_pages % CHUNK_PAGES == 0
    seq_spec = lambda r, c: pl.BlockSpec((None, r, c), lambda s, pt: (s, 0, 0))
    hbm = pl.BlockSpec(memory_space=pl.ANY)
    dma = pltpu.SemaphoreType.DMA
    in_order = _cparams("arbitrary")
    sp, sn = pl.pallas_call(
        functools.partial(_sample_scores_kernel, page_base=page_base, n_pages=n_pages),
        grid_spec=pltpu.PrefetchScalarGridSpec(
            num_scalar_prefetch=1, grid=(db,),
            in_specs=[seq_spec(IDX_HEADS * QPAD, IDX_DIM), seq_spec(IDX_HEADS * QPAD, PAGE_SIZE),
                      seq_spec(PAGE_SIZE, IDX_DIM), hbm],
            out_specs=[seq_spec(QPAD, past), seq_spec(QPAD, PAGE_SIZE)],
            scratch_shapes=[pltpu.VMEM((2, n_pages * IDX_DIM, PAGE_SIZE), F32), dma((2,))]),
        out_shape=[sds((db, QPAD, past), F32), sds((db, QPAD, PAGE_SIZE), F32)],
        compiler_params=in_order, name="sample_scores",
    )(page_table, qi, ws, kin, cache_idx_k.reshape(-1, PAGE_SIZE))

    rows = db * QPAD
    rblk = min(rows, 8 * QPAD)
    assert rows % rblk == 0
    thr = pl.pallas_call(
        functools.partial(_sample_thr_kernel, past=past, top_k=top_k),
        grid=(rows // rblk,),
        in_specs=[_row_spec(rblk, past), _row_spec(rblk, PAGE_SIZE)],
        out_specs=_row_spec(rblk, LANES),
        out_shape=sds((rows, LANES), F32),
        scratch_shapes=[pltpu.VMEM((rblk, past + PAGE_SIZE), F32)],
        compiler_params=_cparams("parallel"), name="sample_threshold",
    )(sp.reshape(rows, past), sn.reshape(rows, PAGE_SIZE))

    hq = N_HEADS * QPAD
    page_rows = N_KV_HEADS * PAGE_SIZE
    out = pl.pallas_call(
        functools.partial(_sample_attn_kernel, page_base=page_base, n_pages=n_pages),
        grid_spec=pltpu.PrefetchScalarGridSpec(
            num_scalar_prefetch=1, grid=(db,),
            in_specs=[seq_spec(hq, HEAD_DIM), seq_spec(QPAD, past), seq_spec(QPAD, PAGE_SIZE), seq_spec(QPAD, LANES),
                      seq_spec(PAGE_SIZE, kvw), seq_spec(PAGE_SIZE, kvw), hbm, hbm],
            out_specs=seq_spec(hq, HEAD_DIM),
            scratch_shapes=[pltpu.VMEM((2, n_pages * page_rows, HEAD_DIM), F32),
                            pltpu.VMEM((2, n_pages * page_rows, HEAD_DIM), F32),
                            dma((2, 2)), pltpu.VMEM((hq, past + PAGE_SIZE), F32)]),
        out_shape=sds((db, hq, HEAD_DIM), F32),
        compiler_params=in_order, name="sample_attn",
    )(page_table, q, sp, sn, thr.reshape(db, QPAD, LANES), kn, vn, cache_k, cache_v)
    out = out.reshape(db, N_HEADS, QPAD, HEAD_DIM)[:, :, :t_len]
    return jnp.transpose(out, (0, 2, 1, 3)).reshape(db * t_len, N_HEADS * HEAD_DIM)


ROW_TILE = 640
PROJ_ROW_TILE = 512
FFN_ROW_TILE = 512
FF_TILE = 512
MERGE_ROW_TILE = 256


def _layer(x_p, x_s, pos, dims, layer, w_in_all, ck, cv, cik, page_base, s_re, s_im, page_table, p):
    b, s, db, t = dims
    mp, ms = b * s, db * t
    d = x_p.shape[1]
    bf = lambda w: w.astype(BF16)
    ssm_w = d // 2
    w_g = bf(w_in_all[layer, :, w_in_all.shape[2] - 2 * d:])

    x1, xn = _ffn(x_p, x_s, p["ffn1_norm"], bf(p["ffn1_w_gate"]), bf(p["ffn1_w_up"]), bf(p["ffn1_w_down"]),
                  p["mix_norm"], n_main_rows=mp, tail=ms, split_out=False, tm=FFN_ROW_TILE, tf=FF_TILE)
    pr = _projections(xn, jnp.swapaxes(w_in_all, 1, 2), layer, w_g, p["q_norm"], p["k_norm"], pos,
                      n_main_rows=mp, tail=ms, tm=PROJ_ROW_TILE, tm_gates=ROW_TILE)

    kmat, win, wout, lp = _ssm_prep(p["ssm_lambda_re"], p["ssm_lambda_im"], p["ssm_b_re"], p["ssm_b_im"],
                                    p["ssm_c_re"], p["ssm_c_im"], p["ssm_log_dt"])
    nblk = ssm_w // LANES
    d8 = p["ssm_d"].reshape(nblk, 1, LANES)
    za8, hl_p, hl_s = _ssm(pr["u8"], kmat, win, wout, lp, d8, _state_to_blocks(s_re, s_im),
                           nseq=b, seq=s, n_s=db, t_s=t)

    b_p = _attn_prompt(pr["q_t"], pr["kb"], pr["v_t"], pr["qi_t"], pr["kib"], pr["kiw_t"],
                       batch=b, seq=s, m_out=mp)
    b_s = _attn_sample(pr["q"][mp:], pr["kb"][mp:], bf(pr["v"][mp:]), pr["qi"][mp:],
                       pr["kiw"][mp:, IDX_DIM:IDX_DIM + IDX_HEADS], pr["kib"][mp:],
                       ck, cv, cik, page_table, page_base, t_len=t)

    x2 = _merge(za8, b_p, bf(b_s), pr["gates"], x1, bf(p["glu_w"]), p["glu_b"], bf(p["w_branch_a"]),
                bf(p["w_branch_b"]), bf(p["w_out"]), tm=MERGE_ROW_TILE)
    y_p, y_s = _ffn(x2, None, p["ffn2_norm"], bf(p["ffn2_w_gate"]), bf(p["ffn2_w_up"]), bf(p["ffn2_w_down"]),
                    n_main_rows=mp, tail=ms, split_out=True, tm=FFN_ROW_TILE, tf=FF_TILE)

    hp_re, hp_im = _blocks_to_state(hl_p)
    hs_re, hs_im = _blocks_to_state(hl_s)
    k, v, ki = pr["k"], pr["v"], pr["kiw"][:, :IDX_DIM]
    kvs = (N_KV_HEADS, HEAD_DIM)
    rows = (k[:mp].reshape(b, s, *kvs), v[:mp].reshape(b, s, *kvs), ki[:mp].reshape(b, s, IDX_DIM), hp_re, hp_im,
            k[mp:].reshape(db, t, *kvs), v[mp:].reshape(db, t, *kvs), ki[mp:].reshape(db, t, IDX_DIM), hs_re, hs_im)
    return y_p, y_s, rows


def kernel(x_prompt, x_sample, cache_k, cache_v, cache_idx_k, state_ssm_re, state_ssm_im, page_table,
           ffn1_norm, ffn1_w_gate, ffn1_w_up, ffn1_w_down, mix_norm, w_in, q_norm, k_norm,
           ssm_lambda_re, ssm_lambda_im, ssm_b_re, ssm_b_im, ssm_c_re, ssm_c_im, ssm_d, ssm_log_dt,
           glu_w, glu_b, w_branch_a, w_branch_b, w_out, ffn2_norm, ffn2_w_gate, ffn2_w_up, ffn2_w_down):
    b, s, d = x_prompt.shape
    db, t, _ = x_sample.shape
    depth, n_phys = cache_k.shape[:2]
    past = page_table.shape[1] * PAGE_SIZE
    x_p, x_s = x_prompt.reshape(b * s, d), x_sample.reshape(db * t, d)
    pos = jnp.concatenate([jnp.tile(jnp.arange(s), b), jnp.tile(past + jnp.arange(t), db)])
    ck = cache_k.reshape(-1, HEAD_DIM)
    cv = cache_v.reshape(-1, HEAD_DIM)
    cik = jnp.swapaxes(cache_idx_k, 2, 3).reshape(depth * n_phys, IDX_DIM, PAGE_SIZE)
    params = dict(
        ffn1_norm=ffn1_norm, ffn1_w_gate=ffn1_w_gate, ffn1_w_up=ffn1_w_up, ffn1_w_down=ffn1_w_down,
        mix_norm=mix_norm, w_in=w_in, q_norm=q_norm, k_norm=k_norm,
        ssm_lambda_re=ssm_lambda_re, ssm_lambda_im=ssm_lambda_im, ssm_b_re=ssm_b_re, ssm_b_im=ssm_b_im,
        ssm_c_re=ssm_c_re, ssm_c_im=ssm_c_im, ssm_d=ssm_d, ssm_log_dt=ssm_log_dt, glu_w=glu_w, glu_b=glu_b,
        w_branch_a=w_branch_a, w_branch_b=w_branch_b, w_out=w_out,
        ffn2_norm=ffn2_norm, ffn2_w_gate=ffn2_w_gate, ffn2_w_up=ffn2_w_up, ffn2_w_down=ffn2_w_down)
    new = [[] for _ in range(10)]
    for l in range(depth):
        p = {name: w[l] for name, w in params.items()}
        x_p, x_s, rows = _layer(x_p, x_s, pos, (b, s, db, t), l, w_in, ck, cv, cik, l * n_phys,
                                state_ssm_re[l], state_ssm_im[l], page_table, p)
        for lst, r in zip(new, rows):
            lst.append(r)
    return (x_p.reshape(b, s, d), x_s.reshape(db, t, d)) + tuple(jnp.stack(lst) for lst in new)
```

```python
import functools
import math

import jax
import jax.numpy as jnp
from jax import lax
from jax.experimental import pallas as pl
from jax.experimental.pallas import tpu as pltpu

F32 = jnp.float32
BF16 = jnp.bfloat16

SSM_GROUP = 16
SSM_STATE = 64
N_HEADS = 8
HEAD_DIM = 128
N_KV_HEADS = 2
N_REP = N_HEADS // N_KV_HEADS
ROT_DIM = HEAD_DIM // 4
ROPE_THETA = 500000.0
IDX_HEADS = 16
IDX_DIM = 64
IDX_ROT_DIM = IDX_DIM // 4
TOPK_MAX = 256
PAGE_SIZE = 128
FFN_RES = 0.5
EPS = 1e-6

LANES = 128
SUBLANES = 8
VMEM_LIMIT_BYTES = 56 * 1024 * 1024

GROUPS_PER_BLOCK = LANES // SSM_GROUP
STATES_PER_BLOCK = GROUPS_PER_BLOCK * SSM_STATE
SSM_T = 8

INT_MIN = -(2 ** 31)


def _cparams(*sem):
    return pltpu.CompilerParams(dimension_semantics=sem, vmem_limit_bytes=VMEM_LIMIT_BYTES)


def _rms(x, g):
    return x * lax.rsqrt(jnp.mean(x * x, axis=-1, keepdims=True) + EPS) * g


def _dot(a, b):
    return jnp.dot(a, b, preferred_element_type=F32)


def _dot_nt(a, b):
    return lax.dot_general(a, b, (((1,), (1,)), ((), ())), preferred_element_type=F32)


def _ffn_kernel(*refs, nf, n_main, tm, tail, two_src, split_out, with_next_norm):
    refs = list(refs)
    x_main_ref = refs.pop(0)
    x_tail_ref = refs.pop(0) if two_src else x_main_ref
    g_ref, wg_ref, wu_ref, wd_ref = (refs.pop(0) for _ in range(4))
    g2_ref = refs.pop(0) if with_next_norm else None
    y_main_ref = refs.pop(0)
    y_tail_ref = refs.pop(0) if split_out else y_main_ref
    n2_ref = refs.pop(0) if with_next_norm else None
    xn_ref, acc_ref = refs
    i, f = pl.program_id(0), pl.program_id(1)

    def run(rows, x_ref, y_ref):
        @pl.when(f == 0)
        def _():
            xn_ref[:rows] = _rms(x_ref[:rows], g_ref[...]).astype(BF16)
            acc_ref[:rows] = jnp.zeros((rows, acc_ref.shape[1]), F32)

        xn = xn_ref[:rows]
        a = _dot(xn, wg_ref[...])
        b = _dot(xn, wu_ref[...])
        h = (a * jax.nn.sigmoid(a) * b).astype(BF16)
        acc_ref[:rows] += _dot(h, wd_ref[...])

        @pl.when(f == nf - 1)
        def _():
            y = x_ref[:rows] + FFN_RES * acc_ref[:rows]
            y_ref[:rows] = y
            if with_next_norm:
                n2_ref[:rows] = _rms(y, g2_ref[...]).astype(BF16)

    pl.when(i < n_main)(lambda: run(tm, x_main_ref, y_main_ref))
    pl.when(i == n_main)(lambda: run(tail, x_tail_ref, y_tail_ref))


def _ffn(x_main, x_tail, g, wg, wu, wd, g2=None, *, n_main_rows, tail, split_out, tm, tf):
    d = x_main.shape[1]
    nf = wg.shape[1] // tf
    n_main = n_main_rows // tm
    m = n_main_rows + tail
    two_src = x_tail is not None
    with_next = g2 is not None
    clamp = lambda i, f: (jnp.minimum(i, n_main - 1), 0)
    rows = lambda i, f: (i, 0)
    first = lambda i, f: (0, 0)
    sds = jax.ShapeDtypeStruct
    in_specs = [pl.BlockSpec((tm, d), clamp if two_src else rows)]
    args = [x_main]
    if two_src:
        in_specs.append(pl.BlockSpec((tail, d), first))
        args.append(x_tail)
    in_specs += [pl.BlockSpec((1, d), first), pl.BlockSpec((d, tf), lambda i, f: (0, f)),
                 pl.BlockSpec((d, tf), lambda i, f: (0, f)), pl.BlockSpec((tf, d), lambda i, f: (f, 0))]
    args += [g.reshape(1, d), wg, wu, wd]
    if with_next:
        in_specs.append(pl.BlockSpec((1, d), first))
        args.append(g2.reshape(1, d))
    if split_out:
        out_shape = [sds((n_main_rows, d), F32), sds((tail, d), F32)]
        out_specs = [pl.BlockSpec((tm, d), clamp), pl.BlockSpec((tail, d), first)]
    else:
        out_shape = [sds((m, d), F32)]
        out_specs = [pl.BlockSpec((tm, d), rows)]
    if with_next:
        out_shape.append(sds((m, d), BF16))
        out_specs.append(pl.BlockSpec((tm, d), rows))
    return pl.pallas_call(
        functools.partial(_ffn_kernel, nf=nf, n_main=n_main, tm=tm, tail=tail, two_src=two_src,
                          split_out=split_out, with_next_norm=with_next),
        grid=(n_main + 1, nf),
        in_specs=in_specs,
        out_specs=out_specs,
        out_shape=out_shape,
        scratch_shapes=[pltpu.VMEM((tm, d), BF16), pltpu.VMEM((tm, d), F32)],
        compiler_params=_cparams("arbitrary", "arbitrary"),
        name="ffn" + ("_norm" if with_next else ""),
    )(*args)


def _rope_tables(pos, rot_dim, width, tile):
    half = rot_dim // 2
    m = pos.shape[0]
    freqs = ROPE_THETA ** (-jnp.arange(half, dtype=F32) * 2.0 / rot_dim)
    ang = pos.astype(F32)[:, None] * freqs[None, :]
    cos, sin = jnp.cos(ang), jnp.sin(ang)
    zh = jnp.zeros((m, half), F32)
    rest = width - rot_dim
    c = jnp.concatenate([cos, cos, jnp.ones((m, rest), F32)], axis=1)
    s1 = jnp.concatenate([-sin, zh, jnp.zeros((m, rest), F32)], axis=1)
    s2 = jnp.concatenate([zh, sin, jnp.zeros((m, rest), F32)], axis=1)
    if tile:
        reps = LANES // width
        return tuple(jnp.tile(t, (1, reps)) for t in (c, s1, s2))
    pad = LANES - width
    return (jnp.pad(c, ((0, 0), (0, pad)), constant_values=1.0),
            jnp.pad(s1, ((0, 0), (0, pad))), jnp.pad(s2, ((0, 0), (0, pad))))


def _rope(x, c, s1, s2, half):
    return x * c + pltpu.roll(x, LANES - half, 1) * s1 + pltpu.roll(x, half, 1) * s2


def _proj_rows(xn_ref, w_refs, wbf_ref, n_main, tm, tail, chunk, epilogue):
    i = pl.program_id(0)

    @pl.when(i == 0)
    def _():
        off = 0
        for w_ref in w_refs:
            n = w_ref.shape[0]
            wbf_ref[:, off:off + n] = w_ref[...].astype(F32).T.astype(BF16)
            off += n

    def run(rows):
        for r0 in range(0, rows, chunk):
            rs = slice(r0, min(r0 + chunk, rows))
            epilogue(_dot(xn_ref[rs, :], wbf_ref[...]), rs)

    pl.when(i < n_main)(lambda: run(tm))
    pl.when(i == n_main)(lambda: run(tail))


def _u_proj_kernel(xn_ref, w_ref, u_ref, wbf_ref, **kw):
    def epilogue(z, rs):
        for j in range(z.shape[1] // LANES):
            u_ref[j, rs, :] = z[:, j * LANES:(j + 1) * LANES]
    _proj_rows(xn_ref, [w_ref], wbf_ref, epilogue=epilogue, **kw)


def _q_proj_kernel(xn_ref, w_ref, g_ref, c_ref, s1_ref, s2_ref, qn_ref, qt_ref, wbf_ref, **kw):
    def epilogue(z, rs):
        c, s1, s2, g = c_ref[rs, :], s1_ref[rs, :], s2_ref[rs, :], g_ref[...]
        heads = []
        for h in range(N_HEADS):
            x = _rms(z[:, h * HEAD_DIM:(h + 1) * HEAD_DIM], g)
            heads.append(_rope(x, c, s1, s2, ROT_DIM // 2) * (HEAD_DIM ** -0.5))
        q = jnp.concatenate(heads, axis=1)
        qn_ref[rs, :] = q.astype(BF16)
        qt_ref[:, rs] = q.T.astype(BF16)
    _proj_rows(xn_ref, [w_ref], wbf_ref, epilogue=epilogue, **kw)


def _kv_proj_kernel(xn_ref, w_ref, g_ref, c_ref, s1_ref, s2_ref, k_ref, kb_ref, v_ref, vt_ref, wbf_ref, **kw):
    def epilogue(z, rs):
        c, s1, s2, g = c_ref[rs, :], s1_ref[rs, :], s2_ref[rs, :], g_ref[...]
        heads = []
        for h in range(N_KV_HEADS):
            x = _rms(z[:, h * HEAD_DIM:(h + 1) * HEAD_DIM], g)
            heads.append(_rope(x, c, s1, s2, ROT_DIM // 2))
        k = jnp.concatenate(heads, axis=1)
        v = z[:, N_KV_HEADS * HEAD_DIM:]
        k_ref[rs, :] = k
        kb_ref[rs, :] = k.astype(BF16)
        v_ref[rs, :] = v
        vt_ref[:, rs] = v.T.astype(BF16)
    _proj_rows(xn_ref, [w_ref], wbf_ref, epilogue=epilogue, **kw)


def _qi_proj_kernel(xn_ref, wa_ref, wb_ref, c_ref, s1_ref, s2_ref, qn_ref, qt_ref, wbf_ref, **kw):
    def epilogue(z, rs):
        c, s1, s2 = c_ref[rs, :], s1_ref[rs, :], s2_ref[rs, :]
        cols = []
        for j in range(z.shape[1] // LANES):
            x = z[:, j * LANES:(j + 1) * LANES]
            cols.append(_rope(x, c, s1, s2, IDX_ROT_DIM // 2) * (IDX_DIM ** -0.5))
        q = jnp.concatenate(cols, axis=1)
        qn_ref[rs, :] = q.astype(BF16)
        qt_ref[:, rs] = q.T.astype(BF16)
    _proj_rows(xn_ref, [wa_ref, wb_ref], wbf_ref, epilogue=epilogue, **kw)


def _kiwi_proj_kernel(xn_ref, w_ref, c_ref, s1_ref, s2_ref, scale_ref, o_ref, ot_ref, kb_ref, wbf_ref, **kw):
    def epilogue(z, rs):
        y = _rope(z, c_ref[rs, :], s1_ref[rs, :], s2_ref[rs, :], IDX_ROT_DIM // 2) * scale_ref[...]
        o_ref[rs, :] = y
        ot_ref[:, rs] = y.T
        kb_ref[rs, :] = y[:, :IDX_DIM].astype(BF16)
    _proj_rows(xn_ref, [w_ref], wbf_ref, epilogue=epilogue, **kw)


def _gate_proj_kernel(xn_ref, w_ref, o_ref):
    o_ref[...] = jax.nn.sigmoid(_dot(xn_ref[...], w_ref[...])).astype(BF16)


def _row_spec(tm, n):
    return pl.BlockSpec((tm, n), lambda i: (i, 0))


def _full_spec(shape):
    return pl.BlockSpec(shape, lambda i: (0,) * len(shape))


PROJ_CHUNK = 256


def _projections(xn, w_in_t, layer, w_g, q_norm, k_norm, pos, *, n_main_rows, seq, tail, tm, tm_gates):
    m, d = xn.shape
    n_main = n_main_rows // tm
    assert n_main * tm == n_main_rows and n_main_rows + tail == m and tail <= tm
    grid = (n_main + 1,)
    ssm_w, attn_w, kv_w, idx_w = d // 2, N_HEADS * HEAD_DIM, N_KV_HEADS * HEAD_DIM, IDX_HEADS * IDX_DIM
    hd_tabs = _rope_tables(pos, ROT_DIM, HEAD_DIM, True)
    ix_tabs = _rope_tables(pos, IDX_ROT_DIM, IDX_DIM, True)
    kw_tabs = _rope_tables(pos, IDX_ROT_DIM, IDX_DIM, False)
    assert seq % tm == 0 and pos.shape[0] == seq + tail
    tabs_per_seq = seq // tm
    tab_specs = [pl.BlockSpec((tm, LANES), lambda i: (jnp.where(i < n_main, i % tabs_per_seq, tabs_per_seq), 0))] * 3
    xs = _row_spec(tm, d)
    sds = jax.ShapeDtypeStruct
    seq = _cparams("arbitrary")
    kw = dict(n_main=n_main, tm=tm, tail=tail, chunk=PROJ_CHUNK)

    def wcols(start, width):
        assert start % width == 0
        return pl.BlockSpec((None, width, d), lambda i: (layer, start // width, 0), pipeline_mode=pl.Buffered(1))

    def call(kernel, name, w_specs, n_w, extra_specs, extra_args, out_specs, out_shape):
        return pl.pallas_call(
            functools.partial(kernel, **kw), grid=grid,
            in_specs=[xs] + w_specs + extra_specs, out_specs=out_specs, out_shape=out_shape,
            scratch_shapes=[pltpu.VMEM((d, n_w), BF16)], compiler_params=seq, name=name,
        )(xn, *([w_in_t] * len(w_specs)), *extra_args)

    tcol = lambda n: pl.BlockSpec((n, tm), lambda i: (0, i))
    norm_spec = _full_spec((1, HEAD_DIM))
    u8 = call(_u_proj_kernel, "proj_u", [wcols(0, ssm_w)], ssm_w, [], [],
              pl.BlockSpec((ssm_w // LANES, tm, LANES), lambda i: (0, i, 0)), sds((ssm_w // LANES, m, LANES), F32))
    q_nat, q_t = call(_q_proj_kernel, "proj_q", [wcols(ssm_w, attn_w)], attn_w,
                      [norm_spec] + tab_specs, [q_norm.reshape(1, HEAD_DIM), *hd_tabs],
                      [_row_spec(tm, attn_w), tcol(attn_w)], [sds((m, attn_w), BF16), sds((attn_w, m), BF16)])
    off_kv = ssm_w + attn_w
    k, kb, v, v_t = call(_kv_proj_kernel, "proj_kv", [wcols(off_kv, 2 * kv_w)], 2 * kv_w,
                         [norm_spec] + tab_specs, [k_norm.reshape(1, HEAD_DIM), *hd_tabs],
                         [_row_spec(tm, kv_w), _row_spec(tm, kv_w), _row_spec(tm, kv_w), tcol(kv_w)],
                         [sds((m, kv_w), F32), sds((m, kv_w), BF16), sds((m, kv_w), F32), sds((kv_w, m), BF16)])
    off_qi = off_kv + 2 * kv_w
    half = idx_w // 2
    qi_nat, qi_t = call(_qi_proj_kernel, "proj_qi", [wcols(off_qi, half), wcols(off_qi + half, half)], idx_w,
                        tab_specs, ix_tabs,
                        [_row_spec(tm, idx_w), tcol(idx_w)], [sds((m, idx_w), BF16), sds((idx_w, m), BF16)])
    off_kw = off_qi + idx_w
    lane = jnp.arange(LANES)
    kw_scale = jnp.where(lane < IDX_DIM, 1.0, jnp.where(lane < IDX_DIM + IDX_HEADS, IDX_HEADS ** -0.5, 0.0))
    kiw, kiw_t, kib = call(_kiwi_proj_kernel, "proj_kiwi", [wcols(off_kw, LANES)], LANES,
                           tab_specs + [_full_spec((1, LANES))], [*kw_tabs, kw_scale.astype(F32).reshape(1, LANES)],
                           [_row_spec(tm, LANES), tcol(LANES), _row_spec(tm, IDX_DIM)],
                           [sds((m, LANES), F32), sds((LANES, m), F32), sds((m, IDX_DIM), BF16)])

    ng = w_g.shape[1]
    tn = 1024
    gates = pl.pallas_call(
        _gate_proj_kernel, grid=(m // tm_gates, ng // tn),
        in_specs=[pl.BlockSpec((tm_gates, d), lambda i, j: (i, 0)), pl.BlockSpec((d, tn), lambda i, j: (0, j))],
        out_specs=pl.BlockSpec((tm_gates, tn), lambda i, j: (i, j)),
        out_shape=sds((m, ng), BF16), compiler_params=_cparams("parallel", "parallel"), name="proj_gates",
    )(xn, w_g)
    return dict(u8=u8, q=q_nat, q_t=q_t, k=k, kb=kb, v=v, v_t=v_t, qi=qi_nat, qi_t=qi_t,
                kiw=kiw, kiw_t=kiw_t, kib=kib, gates=gates)


def _ssm_prep_kernel(lre_ref, lim_ref, ldt_ref, bre_ref, bim_ref, cre_ref, cim_ref,
                     k_ref, win_ref, wout_ref, lp_ref):
    ns = STATES_PER_BLOCK
    lre, lim = lre_ref[...], lim_ref[...]
    dt = jnp.exp(ldt_ref[...])
    a, th = lre * dt, lim * dt

    def power(l):
        mag = jnp.exp(a * float(l))
        return mag * jnp.cos(th * float(l)), mag * jnp.sin(th * float(l))

    pw = [power(l) for l in range(SSM_T + 1)]
    xr, xi = pw[1][0] - 1.0, pw[1][1]
    den = lre * lre + lim * lim
    cr, ci = (xr * lre + xi * lim) / den, (xi * lre - xr * lim) / den
    bre, bim = bre_ref[...], bim_ref[...]
    bbr, bbi = bre * cr - bim * ci, bre * ci + bim * cr
    cre, cim = cre_ref[...], cim_ref[...]
    hi = lax.Precision.HIGHEST
    nt = (((1,), (1,)), ((), ()))
    lag = []
    for l in range(SSM_T):
        pr, pi = pw[l]
        blr, bli = bbr * pr - bbi * pi, bbr * pi + bbi * pr
        m = (lax.dot_general(blr, cre, nt, precision=hi, preferred_element_type=F32)
             - lax.dot_general(bli, cim, nt, precision=hi, preferred_element_type=F32))
        lag.append(m.astype(BF16))
        t = SSM_T - 1 - l
        win_ref[t * LANES:(t + 1) * LANES, :ns] = blr.astype(BF16)
        win_ref[t * LANES:(t + 1) * LANES, ns:] = bli.astype(BF16)
    zero = jnp.zeros((LANES, LANES), BF16)
    for t in range(SSM_T):
        pr, pi = pw[t + 1]
        wout_ref[t * LANES:(t + 1) * LANES, :ns] = (cre * pr - cim * pi).astype(BF16)
        wout_ref[t * LANES:(t + 1) * LANES, ns:] = (-(cre * pi + cim * pr)).astype(BF16)
        lp_ref[t:t + 1, :ns] = pr
        lp_ref[t:t + 1, ns:] = pi
        for t2 in range(SSM_T):
            k_ref[t * LANES:(t + 1) * LANES, t2 * LANES:(t2 + 1) * LANES] = lag[t2 - t] if t2 >= t else zero


def _block_diag_groups(w):
    g, h, p = w.shape
    nb = g // GROUPS_PER_BLOCK
    w = w.reshape(nb, GROUPS_PER_BLOCK, h, p)
    eye = jnp.eye(GROUPS_PER_BLOCK, dtype=w.dtype)
    out = w[:, :, :, None, :] * eye[None, :, None, :, None]
    return out.reshape(nb, GROUPS_PER_BLOCK * h, GROUPS_PER_BLOCK * p)


def _ssm_prep(lam_re, lam_im, b_re, b_im, c_re, c_im, log_dt):
    g, p = lam_re.shape
    nb = g // GROUPS_PER_BLOCK
    ns = STATES_PER_BLOCK
    tl = SSM_T * LANES
    vec = lambda a: a.reshape(nb, 1, ns)
    ldt = vec(jnp.broadcast_to(log_dt[:, None], (g, p)))
    bt = lambda b: _block_diag_groups(jnp.swapaxes(b, 1, 2))
    vspec = pl.BlockSpec((None, 1, ns), lambda j: (j, 0, 0))
    mspec = pl.BlockSpec((None, LANES, ns), lambda j: (j, 0, 0))
    sds = jax.ShapeDtypeStruct
    return pl.pallas_call(
        _ssm_prep_kernel, grid=(nb,),
        in_specs=[vspec, vspec, vspec, mspec, mspec, mspec, mspec],
        out_specs=[pl.BlockSpec((None, tl, tl), lambda j: (j, 0, 0)),
                   pl.BlockSpec((None, tl, 2 * ns), lambda j: (j, 0, 0)),
                   pl.BlockSpec((None, tl, 2 * ns), lambda j: (j, 0, 0)),
                   pl.BlockSpec((None, SSM_T, 2 * ns), lambda j: (j, 0, 0))],
        out_shape=[sds((nb, tl, tl), BF16), sds((nb, tl, 2 * ns), BF16), sds((nb, tl, 2 * ns), BF16),
                   sds((nb, SSM_T, 2 * ns), F32)],
        compiler_params=_cparams("parallel"), name="ssm_prep",
    )(vec(lam_re), vec(lam_im), ldt, bt(b_re), bt(b_im), _block_diag_groups(c_re), _block_diag_groups(c_im))


def _ssm_kernel(u_ref, k_ref, win_ref, wout_ref, lp_ref, d_ref, h0s_ref, za_ref, hlp_ref, hls_ref, x_scr, hs_scr,
                *, nseq, seq, n_s, t_s):
    ns = STATES_PER_BLOCK
    nk = ns // LANES
    d = d_ref[...]
    split = lambda a: [a[:, k * LANES:(k + 1) * LANES] for k in range(2 * nk)]

    def gather(row0, n_rows, t_steps):
        return [u_ref[pl.ds(row0 + t, n_rows, stride=t_steps), :] for t in range(t_steps)]

    def advance(h, x, t_steps):
        lam = [lp_ref[t_steps - 1:t_steps, k * LANES:(k + 1) * LANES] for k in range(2 * nk)]
        re = [lam[k] * h[k] - lam[nk + k] * h[nk + k] + x[k] for k in range(nk)]
        im = [lam[k] * h[nk + k] + lam[nk + k] * h[k] + x[nk + k] for k in range(nk)]
        return re + im

    def emit(row0, n_rows, t_steps, cols, hs, kmat, wout):
        u = jnp.concatenate(cols, axis=1).astype(BF16)
        y = _dot(u, kmat) + _dot_nt(hs.astype(BF16), wout)
        for t in range(t_steps):
            yt = y[:, t * LANES:(t + 1) * LANES] + d * cols[t]
            za_ref[pl.ds(row0 + t, n_rows, stride=t_steps), :] = jax.nn.gelu(yt, approximate=True)

    c_per = seq // SSM_T
    for b in range(nseq):
        u = jnp.concatenate(gather(b * seq, c_per, SSM_T), axis=1).astype(BF16)
        x = _dot(u, win_ref[...])
        for k in range(2 * nk):
            x_scr[k, b * c_per:(b + 1) * c_per, :] = x[:, k * LANES:(k + 1) * LANES]

    def step(c, h):
        rows = pl.ds(c, nseq, stride=c_per)
        for k in range(2 * nk):
            hs_scr.at[k][rows, :] = h[k]
        return tuple(advance(h, [x_scr.at[k][rows, :] for k in range(2 * nk)], SSM_T))

    h_last = lax.fori_loop(0, c_per, step, tuple(jnp.zeros((nseq, LANES), F32) for _ in range(2 * nk)),
                           unroll=4)
    hlp_ref[...] = jnp.concatenate(list(h_last), axis=1)
    for b in range(nseq):
        hs = jnp.concatenate([hs_scr[k, b * c_per:(b + 1) * c_per, :] for k in range(2 * nk)], axis=1)
        emit(b * seq, c_per, SSM_T, gather(b * seq, c_per, SSM_T), hs, k_ref[...], wout_ref[...])

    row0, tl = nseq * seq, t_s * LANES
    cols = gather(row0, n_s, t_s)
    u = jnp.concatenate(cols, axis=1).astype(BF16)
    x = _dot(u, win_ref[(SSM_T - t_s) * LANES:, :])
    h0 = h0s_ref[...]
    hls_ref[...] = jnp.concatenate(advance(split(h0), split(x), t_s), axis=1)
    emit(row0, n_s, t_s, cols, h0, k_ref[:tl, :tl], wout_ref[:tl, :])


def _ssm(u8, kmat, win, wout, lp, d8, h0s, *, nseq, seq, n_s, t_s):
    nb, m_total, _ = u8.shape
    ns = STATES_PER_BLOCK
    assert m_total == nseq * seq + n_s * t_s and seq % SSM_T == 0 and t_s <= SSM_T
    rows = nseq * seq // SSM_T
    tl = SSM_T * LANES
    sds = jax.ShapeDtypeStruct
    blk = lambda r, c: pl.BlockSpec((None, r, c), lambda j: (j, 0, 0))
    return pl.pallas_call(
        functools.partial(_ssm_kernel, nseq=nseq, seq=seq, n_s=n_s, t_s=t_s),
        grid=(nb,),
        in_specs=[blk(m_total, LANES), blk(tl, tl), blk(tl, 2 * ns), blk(tl, 2 * ns), blk(SSM_T, 2 * ns),
                  blk(1, LANES), blk(n_s, 2 * ns)],
        out_specs=[blk(m_total, LANES), blk(nseq, 2 * ns), blk(n_s, 2 * ns)],
        out_shape=[sds((nb, m_total, LANES), F32), sds((nb, nseq, 2 * ns), F32), sds((nb, n_s, 2 * ns), F32)],
        scratch_shapes=[pltpu.VMEM((2 * ns // LANES, rows, LANES), F32)] * 2,
        compiler_params=_cparams("parallel"), name="ssm",
    )(u8, kmat, win, wout, lp, d8, h0s)


def _state_to_blocks(re, im):
    n, g, p = re.shape
    nb = g // GROUPS_PER_BLOCK
    f = lambda a: jnp.transpose(a.reshape(n, nb, GROUPS_PER_BLOCK * p), (1, 0, 2))
    return jnp.concatenate([f(re), f(im)], axis=-1)


def _blocks_to_state(h):
    nb, n, w = h.shape
    ns = w // 2
    f = lambda a: jnp.transpose(a, (1, 0, 2)).reshape(n, nb * GROUPS_PER_BLOCK, ns // GROUPS_PER_BLOCK)
    return f(h[..., :ns]), f(h[..., ns:])


def _key_to_float(c):
    return lax.bitcast_convert_type(c ^ ((c >> 31) & 0x7FFFFFFF), F32)


def _kth_largest(count_ge, shape, top_k):
    def body(it, carry):
        u, cnt = carry
        bit = jnp.left_shift(jnp.int32(1), 31 - it)
        cand = u | bit
        c = count_ge(_key_to_float(cand ^ INT_MIN))
        take = c >= top_k
        return jnp.where(take, cand, u), jnp.where(take, c, cnt)
    u, cnt = lax.fori_loop(0, 32, body, (jnp.zeros(shape, jnp.int32), jnp.zeros(shape, jnp.int32)))
    return _key_to_float(u ^ INT_MIN), cnt


TIE_CHUNK = 256


def _tri_ones(n, dtype, lower):
    r = lax.broadcasted_iota(jnp.int32, (n, n), 0)
    c = lax.broadcasted_iota(jnp.int32, (n, n), 1)
    return jnp.where((r >= c) if lower else (r <= c), 1.0, 0.0).astype(dtype)


def _attn_prompt_kernel(qt_ref, kb_ref, vt_ref, qit_ref, kib_ref, wt_ref, o_ref, s_scr, bias_scr,
                        *, seq, top_k, key_chunk):
    qb = pl.program_id(1)
    tq = o_ref.shape[0]

    def run(nk):
        kib = kib_ref[:nk]
        s = jnp.zeros((nk, tq), F32)
        for h in range(IDX_HEADS):
            d = _dot(kib, qit_ref[h * IDX_DIM:(h + 1) * IDX_DIM, :])
            s = s + jnp.maximum(d, 0.0) * wt_ref[IDX_DIM + h:IDX_DIM + h + 1, :]
        kpos = lax.broadcasted_iota(jnp.int32, (nk, tq), 0)
        qpos = qb * tq + lax.broadcasted_iota(jnp.int32, (nk, tq), 1)
        allowed = kpos <= qpos
        s_scr[:nk] = jnp.where(allowed, s, -jnp.inf)

        def count_ge(t):
            n_acc = 8
            acc = [jnp.zeros((SUBLANES, tq), jnp.int32) for _ in range(n_acc)]
            for j in range(nk // SUBLANES):
                rows = s_scr[j * SUBLANES:(j + 1) * SUBLANES]
                acc[j % n_acc] = acc[j % n_acc] + (rows >= t).astype(jnp.int32)
            while len(acc) > 1:
                acc = [a + b for a, b in zip(acc[::2], acc[1::2])]
            return jnp.sum(acc[0], axis=0, keepdims=True)

        thr, cnt = _kth_largest(count_ge, (1, tq), top_k)
        few = qpos < top_k - 1
        q_row = qb * tq + lax.broadcasted_iota(jnp.int32, (1, tq), 1)
        tied = jnp.max(jnp.where((cnt > top_k) & (q_row >= top_k - 1), 1, 0)) > 0

        @pl.when(jnp.logical_not(tied))
        def _():
            bias_scr[:nk] = jnp.where(allowed & ((s_scr[:nk] >= thr) | few), 0.0, -jnp.inf)

        @pl.when(tied)
        def _():
            n_gt = jnp.sum((s_scr[:nk] > thr).astype(jnp.int32), axis=0, keepdims=True)
            need = (top_k - n_gt).astype(F32)
            tri = _tri_ones(TIE_CHUNK, BF16, lower=True)
            carry = jnp.zeros((1, tq), F32)
            for c in range(nk // TIE_CHUNK):
                rs = slice(c * TIE_CHUNK, (c + 1) * TIE_CHUNK)
                sc = s_scr[rs]
                eq = sc == thr
                rank = _dot(tri, jnp.where(eq, 1.0, 0.0).astype(BF16)) + carry
                keep = (sc > thr) | (eq & (rank <= need))
                k_row = c * TIE_CHUNK + lax.broadcasted_iota(jnp.int32, (TIE_CHUNK, tq), 0)
                keep_all = (q_row < top_k - 1) & (k_row <= q_row)
                bias_scr[rs] = jnp.where(keep | keep_all, 0.0, -jnp.inf)
                carry = rank[TIE_CHUNK - 1:, :]

        outs = []
        for g in range(N_KV_HEADS):
            kg = kb_ref[:nk, g * HEAD_DIM:(g + 1) * HEAD_DIM]
            vtg = vt_ref[g * HEAD_DIM:(g + 1) * HEAD_DIM, :nk]
            for r in range(N_REP):
                h = g * N_REP + r
                lg = _dot(kg, qt_ref[h * HEAD_DIM:(h + 1) * HEAD_DIM, :]) + bias_scr[:nk]
                p = jnp.exp(lg - jnp.max(lg, axis=0, keepdims=True))
                den = jnp.sum(p, axis=0, keepdims=True)
                outs.append(_dot(vtg, p.astype(BF16)) / den)
        o_ref[...] = jnp.concatenate(outs, axis=0).T.astype(BF16)

    n_var = seq // key_chunk
    need = (qb * tq + tq + key_chunk - 1) // key_chunk
    for v in range(1, n_var + 1):
        pl.when(need == v)(functools.partial(run, v * key_chunk))


def _attn_prompt(q_t, kb, v_t, qi_t, kib, kiw_t, *, batch, seq, m_out, tq=128, key_chunk=256):
    top_k = min(TOPK_MAX, seq // 4)
    nq = seq // tq
    assert seq % key_chunk == 0 and key_chunk % tq == 0
    aw = N_HEADS * HEAD_DIM
    qcol = lambda n: pl.BlockSpec((n, tq), lambda b, i: (0, b * nq + i))
    return pl.pallas_call(
        functools.partial(_attn_prompt_kernel, seq=seq, top_k=top_k, key_chunk=key_chunk),
        grid=(batch, nq),
        in_specs=[qcol(aw),
                  pl.BlockSpec((seq, N_KV_HEADS * HEAD_DIM), lambda b, i: (b, 0)),
                  pl.BlockSpec((N_KV_HEADS * HEAD_DIM, seq), lambda b, i: (0, b)),
                  qcol(IDX_HEADS * IDX_DIM),
                  pl.BlockSpec((seq, IDX_DIM), lambda b, i: (b, 0)),
                  qcol(LANES)],
        out_specs=pl.BlockSpec((tq, aw), lambda b, i: (b * nq + i, 0)),
        out_shape=jax.ShapeDtypeStruct((m_out, aw), BF16),
        scratch_shapes=[pltpu.VMEM((seq, tq), F32), pltpu.VMEM((seq, tq), F32)],
        compiler_params=_cparams("parallel", "parallel"), name="attn_prompt",
    )(q_t, kb, v_t, qi_t, kib, kiw_t)


def _merge_kernel(za_ref, bm_ref, bt_ref, g_ref, x_ref, gw_ref, gb_ref, wa_ref, wb_ref, wo_ref, o_ref,
                  *, n_main, tm, tail):
    d = x_ref.shape[1]

    def run(rows, b_ref):
        za = jnp.concatenate([za_ref[j, :rows] for j in range(za_ref.shape[0])], axis=1)
        a_out = za * jax.nn.sigmoid(_dot(za.astype(BF16), gw_ref[...]) + gb_ref[...])
        merged = (g_ref[:rows, :d] * _dot(a_out.astype(BF16), wa_ref[...])
                  + g_ref[:rows, d:] * _dot(b_ref[:rows], wb_ref[...]))
        o_ref[:rows] = x_ref[:rows] + _dot(merged.astype(BF16), wo_ref[...])

    i = pl.program_id(0)
    pl.when(i < n_main)(lambda: run(tm, bm_ref))
    pl.when(i == n_main)(lambda: run(tail, bt_ref))


def _merge(za8, b_main, b_tail, gates, x, glu_w, glu_b, wa, wb, wo, *, tm):
    m, d = x.shape
    nb = za8.shape[0]
    n_main_rows, aw = b_main.shape
    tail = b_tail.shape[0]
    n_main = n_main_rows // tm
    assert n_main * tm == n_main_rows and n_main_rows + tail == m and tail <= tm
    resident = lambda shape: pl.BlockSpec(shape, lambda i: (0,) * len(shape), pipeline_mode=pl.Buffered(1))
    return pl.pallas_call(
        functools.partial(_merge_kernel, n_main=n_main, tm=tm, tail=tail), grid=(n_main + 1,),
        in_specs=[pl.BlockSpec((nb, tm, LANES), lambda i: (0, i, 0)),
                  pl.BlockSpec((tm, aw), lambda i: (jnp.minimum(i, n_main - 1), 0)),
                  pl.BlockSpec((tail, aw), lambda i: (0, 0)),
                  _row_spec(tm, 2 * d), _row_spec(tm, d), resident(glu_w.shape), resident((1, glu_w.shape[1])),
                  resident(wa.shape), resident(wb.shape), resident(wo.shape)],
        out_specs=_row_spec(tm, d),
        out_shape=jax.ShapeDtypeStruct((m, d), F32),
        compiler_params=_cparams("arbitrary"), name="merge",
    )(za8, b_main, b_tail, gates, x, glu_w, glu_b.reshape(1, -1), wa, wb, wo)


QPAD = SUBLANES
CHUNK_PAGES = 8


def _page_copy(pt_ref, src_ref, buf_ref, sem, seq, slot, p, page_base, rows):
    start = pl.multiple_of((page_base + pt_ref[seq, p]) * rows, rows)
    return pltpu.make_async_copy(src_ref.at[pl.ds(start, rows), :],
                                 buf_ref.at[slot, pl.ds(p * rows, rows), :], sem)


def _prefetch_pages(n_pages, copies):
    s, n_seq = pl.program_id(0), pl.num_programs(0)
    slot = s % 2

    def start_all(seq, sl):
        def body(p, carry):
            for copy in copies:
                copy(seq, sl, p).start()
            return carry
        lax.fori_loop(0, n_pages, body, 0)

    @pl.when(s == 0)
    def _():
        start_all(0, 0)

    @pl.when(s + 1 < n_seq)
    def _():
        start_all(s + 1, 1 - slot)

    def wait_body(p, carry):
        for copy in copies:
            copy(s, slot, p).wait()
        return carry
    lax.fori_loop(0, n_pages, wait_body, 0)
    return slot


def _sample_scores_kernel(pt_ref, qs_ref, ws_ref, kn_ref, cik_ref, sp_ref, sn_ref, buf_ref, sem_ref,
                          *, page_base, n_pages):
    slot = _prefetch_pages(n_pages, [
        lambda seq, sl, p: _page_copy(pt_ref, cik_ref, buf_ref, sem_ref.at[sl], seq, sl, p, page_base, IDX_DIM)])
    qs, ws = qs_ref[...], ws_ref[:, :1]

    def scores(d):
        r = jnp.maximum(d, 0.0) * ws
        return r.reshape(IDX_HEADS, QPAD, r.shape[1]).sum(axis=0)

    nkc = CHUNK_PAGES * PAGE_SIZE
    for c in range(n_pages // CHUNK_PAGES):
        keys_t = jnp.concatenate(
            [buf_ref[slot, (c * CHUNK_PAGES + kk) * IDX_DIM:(c * CHUNK_PAGES + kk + 1) * IDX_DIM, :]
             for kk in range(CHUNK_PAGES)], axis=1).astype(BF16)
        sp_ref[:, c * nkc:(c + 1) * nkc] = scores(_dot(qs, keys_t))
    sn_ref[...] = scores(_dot_nt(qs, kn_ref[...]))


def _sample_thr_kernel(sp_ref, sn_ref, thr_ref, cut_ref, s_scr, *, past, top_k):
    s = jnp.concatenate([sp_ref[...], sn_ref[...]], axis=1)
    shape = s.shape
    kpos = lax.broadcasted_iota(jnp.int32, shape, 1)
    qidx = lax.broadcasted_iota(jnp.int32, shape, 0) % QPAD
    s_scr[...] = jnp.where(kpos <= past + qidx, s, -jnp.inf)

    def count_ge(t):
        return jnp.sum((s_scr[...] >= t).astype(jnp.int32), axis=1, keepdims=True)

    thr, cnt = _kth_largest(count_ge, (shape[0], 1), top_k)
    thr_ref[...] = jnp.broadcast_to(thr, thr_ref.shape)
    tied = jnp.max(jnp.where(cnt > top_k, 1, 0)) > 0

    @pl.when(jnp.logical_not(tied))
    def _():
        cut_ref[...] = jnp.full(cut_ref.shape, shape[1], jnp.int32)

    @pl.when(tied)
    def _():
        n_gt = jnp.sum((s_scr[...] > thr).astype(jnp.int32), axis=1, keepdims=True)
        need = (top_k - n_gt).astype(F32)
        tri = _tri_ones(LANES, BF16, lower=False)
        carry = jnp.zeros((shape[0], 1), F32)
        cut = jnp.full((shape[0], 1), -1, jnp.int32)
        lane = lax.broadcasted_iota(jnp.int32, (shape[0], LANES), 1)
        for c in range(shape[1] // LANES):
            eq = s_scr[:, c * LANES:(c + 1) * LANES] == thr
            rank = _dot(jnp.where(eq, 1.0, 0.0).astype(BF16), tri) + carry
            last = jnp.max(jnp.where(eq & (rank <= need), lane + c * LANES, -1), axis=1, keepdims=True)
            cut = jnp.maximum(cut, last)
            carry = rank[:, LANES - 1:]
        cut_ref[...] = jnp.broadcast_to(cut, cut_ref.shape)


def _sample_attn_kernel(pt_ref, q_ref, sp_ref, sn_ref, thr_ref, cut_ref, kn_ref, vn_ref, ck_ref, cv_ref, o_ref,
                        kbuf, vbuf, sem_ref, lg_scr, *, page_base, n_pages):
    page_rows = N_KV_HEADS * PAGE_SIZE
    slot = _prefetch_pages(n_pages, [
        lambda seq, sl, p: _page_copy(pt_ref, ck_ref, kbuf, sem_ref.at[0, sl], seq, sl, p, page_base, page_rows),
        lambda seq, sl, p: _page_copy(pt_ref, cv_ref, vbuf, sem_ref.at[1, sl], seq, sl, p, page_base, page_rows)])
    rows_g = N_REP * QPAD
    nkc = CHUNK_PAGES * PAGE_SIZE
    n_chunks = n_pages // CHUNK_PAGES
    past = n_pages * PAGE_SIZE
    thr, cut = thr_ref[:, :1], cut_ref[:, :1]
    q = q_ref[...]
    qg = [q[g * rows_g:(g + 1) * rows_g] for g in range(N_KV_HEADS)]
    head_rows = lambda c, g: pl.ds(c * CHUNK_PAGES * page_rows + g, nkc, stride=N_KV_HEADS)

    def bias(scores, pos0, allowed=None):
        kpos = pos0 + lax.broadcasted_iota(jnp.int32, scores.shape, 1)
        sel = (scores > thr) | ((scores == thr) & (kpos <= cut))
        if allowed is not None:
            sel = sel & allowed
        return jnp.concatenate([jnp.where(sel, 0.0, -jnp.inf)] * N_HEADS, axis=0)

    for c in range(n_chunks):
        lg = jnp.concatenate([_dot_nt(qg[g], kbuf.at[slot][head_rows(c, g), :].astype(BF16))
                              for g in range(N_KV_HEADS)], axis=0)
        lg_scr[:, c * nkc:(c + 1) * nkc] = lg + bias(sp_ref[:, c * nkc:(c + 1) * nkc], c * nkc)
    shape = sn_ref.shape
    allowed = lax.broadcasted_iota(jnp.int32, shape, 1) <= lax.broadcasted_iota(jnp.int32, shape, 0)
    lg = jnp.concatenate([_dot_nt(qg[g], kn_ref[:, g * HEAD_DIM:(g + 1) * HEAD_DIM])
                          for g in range(N_KV_HEADS)], axis=0)
    lg_scr[:, past:] = lg + bias(sn_ref[...], past, allowed)

    lg = lg_scr[...]
    p = jnp.exp(lg - jnp.max(lg, axis=1, keepdims=True))
    den = jnp.sum(p, axis=1, keepdims=True)
    pb = p.astype(BF16)
    acc = jnp.concatenate([_dot(pb[g * rows_g:(g + 1) * rows_g, past:], vn_ref[:, g * HEAD_DIM:(g + 1) * HEAD_DIM])
                           for g in range(N_KV_HEADS)], axis=0)
    for c in range(n_chunks):
        acc = acc + jnp.concatenate(
            [_dot(pb[g * rows_g:(g + 1) * rows_g, c * nkc:(c + 1) * nkc],
                  vbuf.at[slot][head_rows(c, g), :].astype(BF16)) for g in range(N_KV_HEADS)], axis=0)
    o_ref[...] = acc / den


def _pad_queries(x, t_len):
    n, h, d = x.shape
    x = jnp.transpose(x.reshape(n // t_len, t_len, h, d), (0, 2, 1, 3))
    x = jnp.pad(x, ((0, 0), (0, 0), (0, QPAD - t_len), (0, 0)))
    return x.reshape(n // t_len, h * QPAD, d)


def _pad_new_keys(x, t_len):
    n, w = x.shape
    return jnp.pad(x.reshape(n // t_len, t_len, w), ((0, 0), (0, PAGE_SIZE - t_len), (0, 0)))


def _attn_sample(q_s, k_new, v_new, qi_s, wi_s, ki_new, cache_k, cache_v, cache_idx_k, page_table, page_base,
                 *, t_len):
    db, n_pages = page_table.shape
    past = n_pages * PAGE_SIZE
    top_k = min(TOPK_MAX, (past + t_len) // 4)
    kvw = N_KV_HEADS * HEAD_DIM
    sds = jax.ShapeDtypeStruct
    q = _pad_queries(q_s.reshape(-1, N_HEADS, HEAD_DIM), t_len)
    qi = _pad_queries(qi_s.reshape(-1, IDX_HEADS, IDX_DIM), t_len)
    ws = _pad_queries(wi_s.reshape(-1, IDX_HEADS, 1), t_len)
    ws = jnp.broadcast_to(ws, ws.shape[:2] + (PAGE_SIZE,))
    kn, vn, kin = (_pad_new_keys(a, t_len) for a in (k_new, v_new, ki_new))

    assert n_pages % CHUNK_PAGES == 0
    seq_spec = lambda r, c: pl.BlockSpec((None, r, c), lambda s, pt: (s, 0, 0))
    hbm = pl.BlockSpec(memory_space=pl.ANY)
    dma = pltpu.SemaphoreType.DMA
    in_order = _cparams("arbitrary")
    sp, sn = pl.pallas_call(
        functools.partial(_sample_scores_kernel, page_base=page_base, n_pages=n_pages),
        grid_spec=pltpu.PrefetchScalarGridSpec(
            num_scalar_prefetch=1, grid=(db,),
            in_specs=[seq_spec(IDX_HEADS * QPAD, IDX_DIM), seq_spec(IDX_HEADS * QPAD, PAGE_SIZE),
                      seq_spec(PAGE_SIZE, IDX_DIM), hbm],
            out_specs=[seq_spec(QPAD, past), seq_spec(QPAD, PAGE_SIZE)],
            scratch_shapes=[pltpu.VMEM((2, n_pages * IDX_DIM, PAGE_SIZE), F32), dma((2,))]),
        out_shape=[sds((db, QPAD, past), F32), sds((db, QPAD, PAGE_SIZE), F32)],
        compiler_params=in_order, name="sample_scores",
    )(page_table, qi, ws, kin, cache_idx_k.reshape(-1, PAGE_SIZE))

    rows = db * QPAD
    rblk = min(rows, 8 * QPAD)
    assert rows % rblk == 0
    thr, cut = pl.pallas_call(
        functools.partial(_sample_thr_kernel, past=past, top_k=top_k),
        grid=(rows // rblk,),
        in_specs=[_row_spec(rblk, past), _row_spec(rblk, PAGE_SIZE)],
        out_specs=[_row_spec(rblk, LANES), _row_spec(rblk, LANES)],
        out_shape=[sds((rows, LANES), F32), sds((rows, LANES), jnp.int32)],
        scratch_shapes=[pltpu.VMEM((rblk, past + PAGE_SIZE), F32)],
        compiler_params=_cparams("parallel"), name="sample_threshold",
    )(sp.reshape(rows, past), sn.reshape(rows, PAGE_SIZE))

    hq = N_HEADS * QPAD
    page_rows = N_KV_HEADS * PAGE_SIZE
    out = pl.pallas_call(
        functools.partial(_sample_attn_kernel, page_base=page_base, n_pages=n_pages),
        grid_spec=pltpu.PrefetchScalarGridSpec(
            num_scalar_prefetch=1, grid=(db,),
            in_specs=[seq_spec(hq, HEAD_DIM), seq_spec(QPAD, past), seq_spec(QPAD, PAGE_SIZE), seq_spec(QPAD, LANES),
                      seq_spec(QPAD, LANES), seq_spec(PAGE_SIZE, kvw), seq_spec(PAGE_SIZE, kvw), hbm, hbm],
            out_specs=seq_spec(hq, HEAD_DIM),
            scratch_shapes=[pltpu.VMEM((2, n_pages * page_rows, HEAD_DIM), F32),
                            pltpu.VMEM((2, n_pages * page_rows, HEAD_DIM), F32),
                            dma((2, 2)), pltpu.VMEM((hq, past + PAGE_SIZE), F32)]),
        out_shape=sds((db, hq, HEAD_DIM), F32),
        compiler_params=in_order, name="sample_attn",
    )(page_table, q, sp, sn, thr.reshape(db, QPAD, LANES), cut.reshape(db, QPAD, LANES), kn, vn, cache_k, cache_v)
    out = out.reshape(db, N_HEADS, QPAD, HEAD_DIM)[:, :, :t_len]
    return jnp.transpose(out, (0, 2, 1, 3)).reshape(db * t_len, N_HEADS * HEAD_DIM)


ROW_TILE = 640
PROJ_ROW_TILE = 512
FFN_ROW_TILE = 512
FF_TILE = 512
MERGE_ROW_TILE = 256


def _layer(x_p, x_s, pos, dims, layer, w_in_all, ck, cv, cik, page_base, s_re, s_im, page_table, p):
    b, s, db, t = dims
    mp, ms = b * s, db * t
    d = x_p.shape[1]
    bf = lambda w: w.astype(BF16)
    ssm_w = d // 2
    w_g = bf(w_in_all[layer, :, w_in_all.shape[2] - 2 * d:])

    x1, xn = _ffn(x_p, x_s, p["ffn1_norm"], bf(p["ffn1_w_gate"]), bf(p["ffn1_w_up"]), bf(p["ffn1_w_down"]),
                  p["mix_norm"], n_main_rows=mp, tail=ms, split_out=False, tm=FFN_ROW_TILE, tf=FF_TILE)
    pr = _projections(xn, jnp.swapaxes(w_in_all, 1, 2), layer, w_g, p["q_norm"], p["k_norm"], pos,
                      n_main_rows=mp, seq=s, tail=ms, tm=PROJ_ROW_TILE, tm_gates=ROW_TILE)

    kmat, win, wout, lp = _ssm_prep(p["ssm_lambda_re"], p["ssm_lambda_im"], p["ssm_b_re"], p["ssm_b_im"],
                                    p["ssm_c_re"], p["ssm_c_im"], p["ssm_log_dt"])
    nblk = ssm_w // LANES
    d8 = p["ssm_d"].reshape(nblk, 1, LANES)
    za8, hl_p, hl_s = _ssm(pr["u8"], kmat, win, wout, lp, d8, _state_to_blocks(s_re, s_im),
                           nseq=b, seq=s, n_s=db, t_s=t)

    b_p = _attn_prompt(pr["q_t"], pr["kb"], pr["v_t"], pr["qi_t"], pr["kib"], pr["kiw_t"],
                       batch=b, seq=s, m_out=mp)
    b_s = _attn_sample(pr["q"][mp:], pr["kb"][mp:], bf(pr["v"][mp:]), pr["qi"][mp:],
                       pr["kiw"][mp:, IDX_DIM:IDX_DIM + IDX_HEADS], pr["kib"][mp:],
                       ck, cv, cik, page_table, page_base, t_len=t)

    x2 = _merge(za8, b_p, bf(b_s), pr["gates"], x1, bf(p["glu_w"]), p["glu_b"], bf(p["w_branch_a"]),
                bf(p["w_branch_b"]), bf(p["w_out"]), tm=MERGE_ROW_TILE)
    y_p, y_s = _ffn(x2, None, p["ffn2_norm"], bf(p["ffn2_w_gate"]), bf(p["ffn2_w_up"]), bf(p["ffn2_w_down"]),
                    n_main_rows=mp, tail=ms, split_out=True, tm=FFN_ROW_TILE, tf=FF_TILE)

    hp_re, hp_im = _blocks_to_state(hl_p)
    hs_re, hs_im = _blocks_to_state(hl_s)
    k, v, ki = pr["k"], pr["v"], pr["kiw"][:, :IDX_DIM]
    kvs = (N_KV_HEADS, HEAD_DIM)
    rows = (k[:mp].reshape(b, s, *kvs), v[:mp].reshape(b, s, *kvs), ki[:mp].reshape(b, s, IDX_DIM), hp_re, hp_im,
            k[mp:].reshape(db, t, *kvs), v[mp:].reshape(db, t, *kvs), ki[mp:].reshape(db, t, IDX_DIM), hs_re, hs_im)
    return y_p, y_s, rows


def kernel(x_prompt, x_sample, cache_k, cache_v, cache_idx_k, state_ssm_re, state_ssm_im, page_table,
           ffn1_norm, ffn1_w_gate, ffn1_w_up, ffn1_w_down, mix_norm, w_in, q_norm, k_norm,
           ssm_lambda_re, ssm_lambda_im, ssm_b_re, ssm_b_im, ssm_c_re, ssm_c_im, ssm_d, ssm_log_dt,
           glu_w, glu_b, w_branch_a, w_branch_b, w_out, ffn2_norm, ffn2_w_gate, ffn2_w_up, ffn2_w_down):
    b, s, d = x_prompt.shape
    db, t, _ = x_sample.shape
    depth, n_phys = cache_k.shape[:2]
    past = page_table.shape[1] * PAGE_SIZE
    x_p, x_s = x_prompt.reshape(b * s, d), x_sample.reshape(db * t, d)
    pos = jnp.concatenate([jnp.arange(s), jnp.tile(past + jnp.arange(t), db)])
    ck = cache_k.reshape(-1, HEAD_DIM)
    cv = cache_v.reshape(-1, HEAD_DIM)
    cik = jnp.swapaxes(cache_idx_k, 2, 3).reshape(depth * n_phys, IDX_DIM, PAGE_SIZE)
    params = dict(
        ffn1_norm=ffn1_norm, ffn1_w_gate=ffn1_w_gate, ffn1_w_up=ffn1_w_up, ffn1_w_down=ffn1_w_down,
        mix_norm=mix_norm, w_in=w_in, q_norm=q_norm, k_norm=k_norm,
        ssm_lambda_re=ssm_lambda_re, ssm_lambda_im=ssm_lambda_im, ssm_b_re=ssm_b_re, ssm_b_im=ssm_b_im,
        ssm_c_re=ssm_c_re, ssm_c_im=ssm_c_im, ssm_d=ssm_d, ssm_log_dt=ssm_log_dt, glu_w=glu_w, glu_b=glu_b,
        w_branch_a=w_branch_a, w_branch_b=w_branch_b, w_out=w_out,
        ffn2_norm=ffn2_norm, ffn2_w_gate=ffn2_w_gate, ffn2_w_up=ffn2_w_up, ffn2_w_down=ffn2_w_down)
    new = [[] for _ in range(10)]
    for l in range(depth):
        p = {name: w[l] for name, w in params.items()}
        x_p, x_s, rows = _layer(x_p, x_s, pos, (b, s, db, t), l, w_in, ck, cv, cik, l * n_phys,
                                state_ssm_re[l], state_ssm_im[l], page_table, p)
        for lst, r in zip(new, rows):
            lst.append(r)
    return (x_p.reshape(b, s, d), x_s.reshape(db, t, d)) + tuple(jnp.stack(lst) for lst in new)
```

```python
import functools
import math

import jax
import jax.numpy as jnp
from jax import lax
from jax.experimental import pallas as pl
from jax.experimental.pallas import tpu as pltpu

F32 = jnp.float32
BF16 = jnp.bfloat16

SSM_GROUP = 16
SSM_STATE = 64
N_HEADS = 8
HEAD_DIM = 128
N_KV_HEADS = 2
N_REP = N_HEADS // N_KV_HEADS
ROT_DIM = HEAD_DIM // 4
ROPE_THETA = 500000.0
IDX_HEADS = 16
IDX_DIM = 64
IDX_ROT_DIM = IDX_DIM // 4
TOPK_MAX = 256
PAGE_SIZE = 128
FFN_RES = 0.5
EPS = 1e-6

LANES = 128
SUBLANES = 8
VMEM_LIMIT_BYTES = 56 * 1024 * 1024

GROUPS_PER_BLOCK = LANES // SSM_GROUP
STATES_PER_BLOCK = GROUPS_PER_BLOCK * SSM_STATE
SSM_T = 8

INT_MIN = -(2 ** 31)


def _cparams(*sem):
    return pltpu.CompilerParams(dimension_semantics=sem, vmem_limit_bytes=VMEM_LIMIT_BYTES)


def _rms(x, g):
    return x * lax.rsqrt(jnp.mean(x * x, axis=-1, keepdims=True) + EPS) * g


def _dot(a, b):
    return jnp.dot(a, b, preferred_element_type=F32)


def _dot_nt(a, b):
    return lax.dot_general(a, b, (((1,), (1,)), ((), ())), preferred_element_type=F32)


def _ffn_kernel(*refs, nf, n_main, tm, tail, two_src, split_out, with_next_norm):
    refs = list(refs)
    x_main_ref = refs.pop(0)
    x_tail_ref = refs.pop(0) if two_src else x_main_ref
    g_ref, wg_ref, wu_ref, wd_ref = (refs.pop(0) for _ in range(4))
    g2_ref = refs.pop(0) if with_next_norm else None
    y_main_ref = refs.pop(0)
    y_tail_ref = refs.pop(0) if split_out else y_main_ref
    n2_ref = refs.pop(0) if with_next_norm else None
    xn_ref, acc_ref = refs
    i, f = pl.program_id(0), pl.program_id(1)

    def run(rows, x_ref, y_ref):
        @pl.when(f == 0)
        def _():
            xn_ref[:rows] = _rms(x_ref[:rows], g_ref[...]).astype(BF16)
            acc_ref[:rows] = jnp.zeros((rows, acc_ref.shape[1]), F32)

        xn = xn_ref[:rows]
        a = _dot(xn, wg_ref[...])
        b = _dot(xn, wu_ref[...])
        h = (a * jax.nn.sigmoid(a) * b).astype(BF16)
        acc_ref[:rows] += _dot(h, wd_ref[...])

        @pl.when(f == nf - 1)
        def _():
            y = x_ref[:rows] + FFN_RES * acc_ref[:rows]
            y_ref[:rows] = y
            if with_next_norm:
                n2_ref[:rows] = _rms(y, g2_ref[...]).astype(BF16)

    pl.when(i < n_main)(lambda: run(tm, x_main_ref, y_main_ref))
    pl.when(i == n_main)(lambda: run(tail, x_tail_ref, y_tail_ref))


def _ffn(x_main, x_tail, g, wg, wu, wd, g2=None, *, n_main_rows, tail, split_out, tm, tf):
    d = x_main.shape[1]
    nf = wg.shape[1] // tf
    n_main = n_main_rows // tm
    m = n_main_rows + tail
    two_src = x_tail is not None
    with_next = g2 is not None
    clamp = lambda i, f: (jnp.minimum(i, n_main - 1), 0)
    rows = lambda i, f: (i, 0)
    first = lambda i, f: (0, 0)
    sds = jax.ShapeDtypeStruct
    in_specs = [pl.BlockSpec((tm, d), clamp if two_src else rows)]
    args = [x_main]
    if two_src:
        in_specs.append(pl.BlockSpec((tail, d), first))
        args.append(x_tail)
    in_specs += [pl.BlockSpec((1, d), first), pl.BlockSpec((d, tf), lambda i, f: (0, f)),
                 pl.BlockSpec((d, tf), lambda i, f: (0, f)), pl.BlockSpec((tf, d), lambda i, f: (f, 0))]
    args += [g.reshape(1, d), wg, wu, wd]
    if with_next:
        in_specs.append(pl.BlockSpec((1, d), first))
        args.append(g2.reshape(1, d))
    if split_out:
        out_shape = [sds((n_main_rows, d), F32), sds((tail, d), F32)]
        out_specs = [pl.BlockSpec((tm, d), clamp), pl.BlockSpec((tail, d), first)]
    else:
        out_shape = [sds((m, d), F32)]
        out_specs = [pl.BlockSpec((tm, d), rows)]
    if with_next:
        out_shape.append(sds((m, d), BF16))
        out_specs.append(pl.BlockSpec((tm, d), rows))
    return pl.pallas_call(
        functools.partial(_ffn_kernel, nf=nf, n_main=n_main, tm=tm, tail=tail, two_src=two_src,
                          split_out=split_out, with_next_norm=with_next),
        grid=(n_main + 1, nf),
        in_specs=in_specs,
        out_specs=out_specs,
        out_shape=out_shape,
        scratch_shapes=[pltpu.VMEM((tm, d), BF16), pltpu.VMEM((tm, d), F32)],
        compiler_params=_cparams("arbitrary", "arbitrary"),
        name="ffn" + ("_norm" if with_next else ""),
    )(*args)


def _rope_tables(pos, rot_dim, width, tile):
    half = rot_dim // 2
    m = pos.shape[0]
    freqs = ROPE_THETA ** (-jnp.arange(half, dtype=F32) * 2.0 / rot_dim)
    ang = pos.astype(F32)[:, None] * freqs[None, :]
    cos, sin = jnp.cos(ang), jnp.sin(ang)
    zh = jnp.zeros((m, half), F32)
    rest = width - rot_dim
    c = jnp.concatenate([cos, cos, jnp.ones((m, rest), F32)], axis=1)
    s1 = jnp.concatenate([-sin, zh, jnp.zeros((m, rest), F32)], axis=1)
    s2 = jnp.concatenate([zh, sin, jnp.zeros((m, rest), F32)], axis=1)
    if tile:
        reps = LANES // width
        return tuple(jnp.tile(t, (1, reps)) for t in (c, s1, s2))
    pad = LANES - width
    return (jnp.pad(c, ((0, 0), (0, pad)), constant_values=1.0),
            jnp.pad(s1, ((0, 0), (0, pad))), jnp.pad(s2, ((0, 0), (0, pad))))


def _rope(x, c, s1, s2, half):
    return x * c + pltpu.roll(x, LANES - half, 1) * s1 + pltpu.roll(x, half, 1) * s2


def _proj_rows(xn_ref, w_refs, wbf_ref, n_main, tm, tail, chunk, epilogue):
    i = pl.program_id(0)

    @pl.when(i == 0)
    def _():
        off = 0
        for w_ref in w_refs:
            n = w_ref.shape[0]
            wbf_ref[:, off:off + n] = w_ref[...].astype(F32).T.astype(BF16)
            off += n

    def run(rows):
        for r0 in range(0, rows, chunk):
            rs = slice(r0, min(r0 + chunk, rows))
            epilogue(_dot(xn_ref[rs, :], wbf_ref[...]), rs)

    pl.when(i < n_main)(lambda: run(tm))
    pl.when(i == n_main)(lambda: run(tail))


def _u_proj_kernel(xn_ref, w_ref, u_ref, wbf_ref, **kw):
    def epilogue(z, rs):
        for j in range(z.shape[1] // LANES):
            u_ref[j, rs, :] = z[:, j * LANES:(j + 1) * LANES]
    _proj_rows(xn_ref, [w_ref], wbf_ref, epilogue=epilogue, **kw)


def _q_proj_kernel(xn_ref, w_ref, g_ref, c_ref, s1_ref, s2_ref, qn_ref, qt_ref, wbf_ref, **kw):
    def epilogue(z, rs):
        c, s1, s2, g = c_ref[rs, :], s1_ref[rs, :], s2_ref[rs, :], g_ref[...]
        heads = []
        for h in range(N_HEADS):
            x = _rms(z[:, h * HEAD_DIM:(h + 1) * HEAD_DIM], g)
            heads.append(_rope(x, c, s1, s2, ROT_DIM // 2) * (HEAD_DIM ** -0.5))
        q = jnp.concatenate(heads, axis=1)
        qn_ref[rs, :] = q.astype(BF16)
        qt_ref[:, rs] = q.T.astype(BF16)
    _proj_rows(xn_ref, [w_ref], wbf_ref, epilogue=epilogue, **kw)


def _kv_proj_kernel(xn_ref, w_ref, g_ref, c_ref, s1_ref, s2_ref, k_ref, kb_ref, v_ref, vt_ref, wbf_ref, **kw):
    def epilogue(z, rs):
        c, s1, s2, g = c_ref[rs, :], s1_ref[rs, :], s2_ref[rs, :], g_ref[...]
        heads = []
        for h in range(N_KV_HEADS):
            x = _rms(z[:, h * HEAD_DIM:(h + 1) * HEAD_DIM], g)
            heads.append(_rope(x, c, s1, s2, ROT_DIM // 2))
        k = jnp.concatenate(heads, axis=1)
        v = z[:, N_KV_HEADS * HEAD_DIM:]
        k_ref[rs, :] = k
        kb_ref[rs, :] = k.astype(BF16)
        v_ref[rs, :] = v
        vt_ref[:, rs] = v.T.astype(BF16)
    _proj_rows(xn_ref, [w_ref], wbf_ref, epilogue=epilogue, **kw)


def _qi_proj_kernel(xn_ref, wa_ref, wb_ref, c_ref, s1_ref, s2_ref, qn_ref, qt_ref, wbf_ref, **kw):
    def epilogue(z, rs):
        c, s1, s2 = c_ref[rs, :], s1_ref[rs, :], s2_ref[rs, :]
        cols = []
        for j in range(z.shape[1] // LANES):
            x = z[:, j * LANES:(j + 1) * LANES]
            cols.append(_rope(x, c, s1, s2, IDX_ROT_DIM // 2) * (IDX_DIM ** -0.5))
        q = jnp.concatenate(cols, axis=1)
        qn_ref[rs, :] = q.astype(BF16)
        qt_ref[:, rs] = q.T.astype(BF16)
    _proj_rows(xn_ref, [wa_ref, wb_ref], wbf_ref, epilogue=epilogue, **kw)


def _kiwi_proj_kernel(xn_ref, w_ref, c_ref, s1_ref, s2_ref, scale_ref, o_ref, ot_ref, kb_ref, wbf_ref, **kw):
    def epilogue(z, rs):
        y = _rope(z, c_ref[rs, :], s1_ref[rs, :], s2_ref[rs, :], IDX_ROT_DIM // 2) * scale_ref[...]
        o_ref[rs, :] = y
        ot_ref[:, rs] = y.T
        kb_ref[rs, :] = y[:, :IDX_DIM].astype(BF16)
    _proj_rows(xn_ref, [w_ref], wbf_ref, epilogue=epilogue, **kw)


def _gate_proj_kernel(xn_ref, w_ref, o_ref, *, chunk):
    for r0 in range(0, xn_ref.shape[0], chunk):
        rs = slice(r0, r0 + chunk)
        o_ref[rs, :] = jax.nn.sigmoid(_dot(xn_ref[rs, :], w_ref[...])).astype(BF16)


def _row_spec(tm, n):
    return pl.BlockSpec((tm, n), lambda i: (i, 0))


def _full_spec(shape):
    return pl.BlockSpec(shape, lambda i: (0,) * len(shape))


PROJ_CHUNK = 256


def _projections(xn, w_in_t, layer, w_g, q_norm, k_norm, pos, *, n_main_rows, seq, tail, tm, tm_gates):
    m, d = xn.shape
    n_main = n_main_rows // tm
    assert n_main * tm == n_main_rows and n_main_rows + tail == m and tail <= tm
    grid = (n_main + 1,)
    ssm_w, attn_w, kv_w, idx_w = d // 2, N_HEADS * HEAD_DIM, N_KV_HEADS * HEAD_DIM, IDX_HEADS * IDX_DIM
    hd_tabs = _rope_tables(pos, ROT_DIM, HEAD_DIM, True)
    ix_tabs = _rope_tables(pos, IDX_ROT_DIM, IDX_DIM, True)
    kw_tabs = _rope_tables(pos, IDX_ROT_DIM, IDX_DIM, False)
    assert seq % tm == 0 and pos.shape[0] == seq + tail
    tabs_per_seq = seq // tm
    tab_specs = [pl.BlockSpec((tm, LANES), lambda i: (jnp.where(i < n_main, i % tabs_per_seq, tabs_per_seq), 0))] * 3
    xs = _row_spec(tm, d)
    sds = jax.ShapeDtypeStruct
    seq = _cparams("arbitrary")
    kw = dict(n_main=n_main, tm=tm, tail=tail, chunk=PROJ_CHUNK)

    def wcols(start, width):
        assert start % width == 0
        return pl.BlockSpec((None, width, d), lambda i: (layer, start // width, 0), pipeline_mode=pl.Buffered(1))

    def call(kernel, name, w_specs, n_w, extra_specs, extra_args, out_specs, out_shape):
        return pl.pallas_call(
            functools.partial(kernel, **kw), grid=grid,
            in_specs=[xs] + w_specs + extra_specs, out_specs=out_specs, out_shape=out_shape,
            scratch_shapes=[pltpu.VMEM((d, n_w), BF16)], compiler_params=seq, name=name,
        )(xn, *([w_in_t] * len(w_specs)), *extra_args)

    tcol = lambda n: pl.BlockSpec((n, tm), lambda i: (0, i))
    norm_spec = _full_spec((1, HEAD_DIM))
    u8 = call(_u_proj_kernel, "proj_u", [wcols(0, ssm_w)], ssm_w, [], [],
              pl.BlockSpec((ssm_w // LANES, tm, LANES), lambda i: (0, i, 0)), sds((ssm_w // LANES, m, LANES), F32))
    q_nat, q_t = call(_q_proj_kernel, "proj_q", [wcols(ssm_w, attn_w)], attn_w,
                      [norm_spec] + tab_specs, [q_norm.reshape(1, HEAD_DIM), *hd_tabs],
                      [_row_spec(tm, attn_w), tcol(attn_w)], [sds((m, attn_w), BF16), sds((attn_w, m), BF16)])
    off_kv = ssm_w + attn_w
    k, kb, v, v_t = call(_kv_proj_kernel, "proj_kv", [wcols(off_kv, 2 * kv_w)], 2 * kv_w,
                         [norm_spec] + tab_specs, [k_norm.reshape(1, HEAD_DIM), *hd_tabs],
                         [_row_spec(tm, kv_w), _row_spec(tm, kv_w), _row_spec(tm, kv_w), tcol(kv_w)],
                         [sds((m, kv_w), F32), sds((m, kv_w), BF16), sds((m, kv_w), F32), sds((kv_w, m), BF16)])
    off_qi = off_kv + 2 * kv_w
    half = idx_w // 2
    qi_nat, qi_t = call(_qi_proj_kernel, "proj_qi", [wcols(off_qi, half), wcols(off_qi + half, half)], idx_w,
                        tab_specs, ix_tabs,
                        [_row_spec(tm, idx_w), tcol(idx_w)], [sds((m, idx_w), BF16), sds((idx_w, m), BF16)])
    off_kw = off_qi + idx_w
    lane = jnp.arange(LANES)
    kw_scale = jnp.where(lane < IDX_DIM, 1.0, jnp.where(lane < IDX_DIM + IDX_HEADS, IDX_HEADS ** -0.5, 0.0))
    kiw, kiw_t, kib = call(_kiwi_proj_kernel, "proj_kiwi", [wcols(off_kw, LANES)], LANES,
                           tab_specs + [_full_spec((1, LANES))], [*kw_tabs, kw_scale.astype(F32).reshape(1, LANES)],
                           [_row_spec(tm, LANES), tcol(LANES), _row_spec(tm, IDX_DIM)],
                           [sds((m, LANES), F32), sds((LANES, m), F32), sds((m, IDX_DIM), BF16)])

    ng = w_g.shape[1]
    tn = 1024
    gates = pl.pallas_call(
        functools.partial(_gate_proj_kernel, chunk=tm_gates // 2), grid=(m // tm_gates, ng // tn),
        in_specs=[pl.BlockSpec((tm_gates, d), lambda i, j: (i, 0)), pl.BlockSpec((d, tn), lambda i, j: (0, j))],
        out_specs=pl.BlockSpec((tm_gates, tn), lambda i, j: (i, j)),
        out_shape=sds((m, ng), BF16), compiler_params=_cparams("parallel", "parallel"), name="proj_gates",
    )(xn, w_g)
    return dict(u8=u8, q=q_nat, q_t=q_t, k=k, kb=kb, v=v, v_t=v_t, qi=qi_nat, qi_t=qi_t,
                kiw=kiw, kiw_t=kiw_t, kib=kib, gates=gates)


def _ssm_prep_kernel(lre_ref, lim_ref, ldt_ref, bre_ref, bim_ref, cre_ref, cim_ref,
                     k_ref, win_ref, wout_ref, lp_ref):
    ns = STATES_PER_BLOCK
    lre, lim = lre_ref[...], lim_ref[...]
    dt = jnp.exp(ldt_ref[...])
    a, th = lre * dt, lim * dt

    def power(l):
        mag = jnp.exp(a * float(l))
        return mag * jnp.cos(th * float(l)), mag * jnp.sin(th * float(l))

    pw = [power(l) for l in range(SSM_T + 1)]
    xr, xi = pw[1][0] - 1.0, pw[1][1]
    den = lre * lre + lim * lim
    cr, ci = (xr * lre + xi * lim) / den, (xi * lre - xr * lim) / den
    bre, bim = bre_ref[...], bim_ref[...]
    bbr, bbi = bre * cr - bim * ci, bre * ci + bim * cr
    cre, cim = cre_ref[...], cim_ref[...]
    hi = lax.Precision.HIGHEST
    nt = (((1,), (1,)), ((), ()))
    lag = []
    for l in range(SSM_T):
        pr, pi = pw[l]
        blr, bli = bbr * pr - bbi * pi, bbr * pi + bbi * pr
        m = (lax.dot_general(blr, cre, nt, precision=hi, preferred_element_type=F32)
             - lax.dot_general(bli, cim, nt, precision=hi, preferred_element_type=F32))
        lag.append(m.astype(BF16))
        t = SSM_T - 1 - l
        win_ref[t * LANES:(t + 1) * LANES, :ns] = blr.astype(BF16)
        win_ref[t * LANES:(t + 1) * LANES, ns:] = bli.astype(BF16)
    zero = jnp.zeros((LANES, LANES), BF16)
    for t in range(SSM_T):
        pr, pi = pw[t + 1]
        wout_ref[t * LANES:(t + 1) * LANES, :ns] = (cre * pr - cim * pi).astype(BF16)
        wout_ref[t * LANES:(t + 1) * LANES, ns:] = (-(cre * pi + cim * pr)).astype(BF16)
        lp_ref[t:t + 1, :ns] = pr
        lp_ref[t:t + 1, ns:] = pi
        for t2 in range(SSM_T):
            k_ref[t * LANES:(t + 1) * LANES, t2 * LANES:(t2 + 1) * LANES] = lag[t2 - t] if t2 >= t else zero


def _block_diag_groups(w):
    g, h, p = w.shape
    nb = g // GROUPS_PER_BLOCK
    w = w.reshape(nb, GROUPS_PER_BLOCK, h, p)
    eye = jnp.eye(GROUPS_PER_BLOCK, dtype=w.dtype)
    out = w[:, :, :, None, :] * eye[None, :, None, :, None]
    return out.reshape(nb, GROUPS_PER_BLOCK * h, GROUPS_PER_BLOCK * p)


def _ssm_prep(lam_re, lam_im, b_re, b_im, c_re, c_im, log_dt):
    g, p = lam_re.shape
    nb = g // GROUPS_PER_BLOCK
    ns = STATES_PER_BLOCK
    tl = SSM_T * LANES
    vec = lambda a: a.reshape(nb, 1, ns)
    ldt = vec(jnp.broadcast_to(log_dt[:, None], (g, p)))
    bt = lambda b: _block_diag_groups(jnp.swapaxes(b, 1, 2))
    vspec = pl.BlockSpec((None, 1, ns), lambda j: (j, 0, 0))
    mspec = pl.BlockSpec((None, LANES, ns), lambda j: (j, 0, 0))
    sds = jax.ShapeDtypeStruct
    return pl.pallas_call(
        _ssm_prep_kernel, grid=(nb,),
        in_specs=[vspec, vspec, vspec, mspec, mspec, mspec, mspec],
        out_specs=[pl.BlockSpec((None, tl, tl), lambda j: (j, 0, 0)),
                   pl.BlockSpec((None, tl, 2 * ns), lambda j: (j, 0, 0)),
                   pl.BlockSpec((None, tl, 2 * ns), lambda j: (j, 0, 0)),
                   pl.BlockSpec((None, SSM_T, 2 * ns), lambda j: (j, 0, 0))],
        out_shape=[sds((nb, tl, tl), BF16), sds((nb, tl, 2 * ns), BF16), sds((nb, tl, 2 * ns), BF16),
                   sds((nb, SSM_T, 2 * ns), F32)],
        compiler_params=_cparams("parallel"), name="ssm_prep",
    )(vec(lam_re), vec(lam_im), ldt, bt(b_re), bt(b_im), _block_diag_groups(c_re), _block_diag_groups(c_im))


def _ssm_kernel(u_ref, k_ref, win_ref, wout_ref, lp_ref, d_ref, h0s_ref, za_ref, hlp_ref, hls_ref, x_scr, hs_scr,
                *, nseq, seq, n_s, t_s):
    ns = STATES_PER_BLOCK
    nk = ns // LANES
    d = d_ref[...]
    split = lambda a: [a[:, k * LANES:(k + 1) * LANES] for k in range(2 * nk)]

    def gather(row0, n_rows, t_steps):
        return [u_ref[pl.ds(row0 + t, n_rows, stride=t_steps), :] for t in range(t_steps)]

    def advance(h, x, t_steps):
        lam = [lp_ref[t_steps - 1:t_steps, k * LANES:(k + 1) * LANES] for k in range(2 * nk)]
        re = [lam[k] * h[k] - lam[nk + k] * h[nk + k] + x[k] for k in range(nk)]
        im = [lam[k] * h[nk + k] + lam[nk + k] * h[k] + x[nk + k] for k in range(nk)]
        return re + im

    def emit(row0, n_rows, t_steps, cols, hs, kmat, wout):
        u = jnp.concatenate(cols, axis=1).astype(BF16)
        y = _dot(u, kmat) + _dot_nt(hs.astype(BF16), wout)
        for t in range(t_steps):
            yt = y[:, t * LANES:(t + 1) * LANES] + d * cols[t]
            za_ref[pl.ds(row0 + t, n_rows, stride=t_steps), :] = jax.nn.gelu(yt, approximate=True)

    c_per = seq // SSM_T
    for b in range(nseq):
        u = jnp.concatenate(gather(b * seq, c_per, SSM_T), axis=1).astype(BF16)
        x = _dot(u, win_ref[...])
        for k in range(2 * nk):
            x_scr[k, b * c_per:(b + 1) * c_per, :] = x[:, k * LANES:(k + 1) * LANES]

    def step(c, h):
        rows = pl.ds(c, nseq, stride=c_per)
        for k in range(2 * nk):
            hs_scr.at[k][rows, :] = h[k]
        return tuple(advance(h, [x_scr.at[k][rows, :] for k in range(2 * nk)], SSM_T))

    h_last = lax.fori_loop(0, c_per, step, tuple(jnp.zeros((nseq, LANES), F32) for _ in range(2 * nk)),
                           unroll=4)
    hlp_ref[...] = jnp.concatenate(list(h_last), axis=1)
    for b in range(nseq):
        hs = jnp.concatenate([hs_scr[k, b * c_per:(b + 1) * c_per, :] for k in range(2 * nk)], axis=1)
        emit(b * seq, c_per, SSM_T, gather(b * seq, c_per, SSM_T), hs, k_ref[...], wout_ref[...])

    row0, tl = nseq * seq, t_s * LANES
    cols = gather(row0, n_s, t_s)
    u = jnp.concatenate(cols, axis=1).astype(BF16)
    x = _dot(u, win_ref[(SSM_T - t_s) * LANES:, :])
    h0 = h0s_ref[...]
    hls_ref[...] = jnp.concatenate(advance(split(h0), split(x), t_s), axis=1)
    emit(row0, n_s, t_s, cols, h0, k_ref[:tl, :tl], wout_ref[:tl, :])


def _ssm(u8, kmat, win, wout, lp, d8, h0s, *, nseq, seq, n_s, t_s):
    nb, m_total, _ = u8.shape
    ns = STATES_PER_BLOCK
    assert m_total == nseq * seq + n_s * t_s and seq % SSM_T == 0 and t_s <= SSM_T
    rows = nseq * seq // SSM_T
    tl = SSM_T * LANES
    sds = jax.ShapeDtypeStruct
    blk = lambda r, c: pl.BlockSpec((None, r, c), lambda j: (j, 0, 0))
    return pl.pallas_call(
        functools.partial(_ssm_kernel, nseq=nseq, seq=seq, n_s=n_s, t_s=t_s),
        grid=(nb,),
        in_specs=[blk(m_total, LANES), blk(tl, tl), blk(tl, 2 * ns), blk(tl, 2 * ns), blk(SSM_T, 2 * ns),
                  blk(1, LANES), blk(n_s, 2 * ns)],
        out_specs=[blk(m_total, LANES), blk(nseq, 2 * ns), blk(n_s, 2 * ns)],
        out_shape=[sds((nb, m_total, LANES), F32), sds((nb, nseq, 2 * ns), F32), sds((nb, n_s, 2 * ns), F32)],
        scratch_shapes=[pltpu.VMEM((2 * ns // LANES, rows, LANES), F32)] * 2,
        compiler_params=_cparams("parallel"), name="ssm",
    )(u8, kmat, win, wout, lp, d8, h0s)


def _state_to_blocks(re, im):
    n, g, p = re.shape
    nb = g // GROUPS_PER_BLOCK
    f = lambda a: jnp.transpose(a.reshape(n, nb, GROUPS_PER_BLOCK * p), (1, 0, 2))
    return jnp.concatenate([f(re), f(im)], axis=-1)


def _blocks_to_state(h):
    nb, n, w = h.shape
    ns = w // 2
    f = lambda a: jnp.transpose(a, (1, 0, 2)).reshape(n, nb * GROUPS_PER_BLOCK, ns // GROUPS_PER_BLOCK)
    return f(h[..., :ns]), f(h[..., ns:])


def _key_to_float(c):
    return lax.bitcast_convert_type(c ^ ((c >> 31) & 0x7FFFFFFF), F32)


def _kth_largest(count_ge, shape, top_k):
    def body(it, carry):
        u, cnt = carry
        bit = jnp.left_shift(jnp.int32(1), 31 - it)
        cand = u | bit
        c = count_ge(_key_to_float(cand ^ INT_MIN))
        take = c >= top_k
        return jnp.where(take, cand, u), jnp.where(take, c, cnt)
    u, cnt = lax.fori_loop(0, 32, body, (jnp.zeros(shape, jnp.int32), jnp.zeros(shape, jnp.int32)))
    return _key_to_float(u ^ INT_MIN), cnt


TIE_CHUNK = 256


def _tri_ones(n, dtype, lower):
    r = lax.broadcasted_iota(jnp.int32, (n, n), 0)
    c = lax.broadcasted_iota(jnp.int32, (n, n), 1)
    return jnp.where((r >= c) if lower else (r <= c), 1.0, 0.0).astype(dtype)


def _attn_prompt_kernel(qt_ref, kb_ref, vt_ref, qit_ref, kib_ref, wt_ref, o_ref, s_scr, bias_scr,
                        *, seq, top_k, key_chunk):
    qb = pl.program_id(1)
    tq = o_ref.shape[0]

    def run(nk):
        kib = kib_ref[:nk]
        s = jnp.zeros((nk, tq), F32)
        for h in range(IDX_HEADS):
            d = _dot(kib, qit_ref[h * IDX_DIM:(h + 1) * IDX_DIM, :])
            s = s + jnp.maximum(d, 0.0) * wt_ref[IDX_DIM + h:IDX_DIM + h + 1, :]
        kpos = lax.broadcasted_iota(jnp.int32, (nk, tq), 0)
        qpos = qb * tq + lax.broadcasted_iota(jnp.int32, (nk, tq), 1)
        allowed = kpos <= qpos
        s_scr[:nk] = jnp.where(allowed, s, -jnp.inf)

        def count_ge(t):
            n_acc = 8
            acc = [jnp.zeros((SUBLANES, tq), jnp.int32) for _ in range(n_acc)]
            for j in range(nk // SUBLANES):
                rows = s_scr[j * SUBLANES:(j + 1) * SUBLANES]
                acc[j % n_acc] = acc[j % n_acc] + (rows >= t).astype(jnp.int32)
            while len(acc) > 1:
                acc = [a + b for a, b in zip(acc[::2], acc[1::2])]
            return jnp.sum(acc[0], axis=0, keepdims=True)

        thr, cnt = _kth_largest(count_ge, (1, tq), top_k)
        few = qpos < top_k - 1
        q_row = qb * tq + lax.broadcasted_iota(jnp.int32, (1, tq), 1)
        tied = jnp.max(jnp.where((cnt > top_k) & (q_row >= top_k - 1), 1, 0)) > 0

        @pl.when(jnp.logical_not(tied))
        def _():
            bias_scr[:nk] = jnp.where(allowed & ((s_scr[:nk] >= thr) | few), 0.0, -jnp.inf)

        @pl.when(tied)
        def _():
            n_gt = jnp.sum((s_scr[:nk] > thr).astype(jnp.int32), axis=0, keepdims=True)
            need = (top_k - n_gt).astype(F32)
            tri = _tri_ones(TIE_CHUNK, BF16, lower=True)
            carry = jnp.zeros((1, tq), F32)
            for c in range(nk // TIE_CHUNK):
                rs = slice(c * TIE_CHUNK, (c + 1) * TIE_CHUNK)
                sc = s_scr[rs]
                eq = sc == thr
                rank = _dot(tri, jnp.where(eq, 1.0, 0.0).astype(BF16)) + carry
                keep = (sc > thr) | (eq & (rank <= need))
                k_row = c * TIE_CHUNK + lax.broadcasted_iota(jnp.int32, (TIE_CHUNK, tq), 0)
                keep_all = (q_row < top_k - 1) & (k_row <= q_row)
                bias_scr[rs] = jnp.where(keep | keep_all, 0.0, -jnp.inf)
                carry = rank[TIE_CHUNK - 1:, :]

        outs = []
        for g in range(N_KV_HEADS):
            kg = kb_ref[:nk, g * HEAD_DIM:(g + 1) * HEAD_DIM]
            vtg = vt_ref[g * HEAD_DIM:(g + 1) * HEAD_DIM, :nk]
            for r in range(N_REP):
                h = g * N_REP + r
                lg = _dot(kg, qt_ref[h * HEAD_DIM:(h + 1) * HEAD_DIM, :]) + bias_scr[:nk]
                p = jnp.exp(lg - jnp.max(lg, axis=0, keepdims=True))
                den = jnp.sum(p, axis=0, keepdims=True)
                outs.append(_dot(vtg, p.astype(BF16)) / den)
        o_ref[...] = jnp.concatenate(outs, axis=0).T.astype(BF16)

    n_var = seq // key_chunk
    need = (qb * tq + tq + key_chunk - 1) // key_chunk
    for v in range(1, n_var + 1):
        pl.when(need == v)(functools.partial(run, v * key_chunk))


def _attn_prompt(q_t, kb, v_t, qi_t, kib, kiw_t, *, batch, seq, m_out, tq=128, key_chunk=256):
    top_k = min(TOPK_MAX, seq // 4)
    nq = seq // tq
    assert seq % key_chunk == 0 and key_chunk % tq == 0
    aw = N_HEADS * HEAD_DIM
    qcol = lambda n: pl.BlockSpec((n, tq), lambda b, i: (0, b * nq + i))
    return pl.pallas_call(
        functools.partial(_attn_prompt_kernel, seq=seq, top_k=top_k, key_chunk=key_chunk),
        grid=(batch, nq),
        in_specs=[qcol(aw),
                  pl.BlockSpec((seq, N_KV_HEADS * HEAD_DIM), lambda b, i: (b, 0)),
                  pl.BlockSpec((N_KV_HEADS * HEAD_DIM, seq), lambda b, i: (0, b)),
                  qcol(IDX_HEADS * IDX_DIM),
                  pl.BlockSpec((seq, IDX_DIM), lambda b, i: (b, 0)),
                  qcol(LANES)],
        out_specs=pl.BlockSpec((tq, aw), lambda b, i: (b * nq + i, 0)),
        out_shape=jax.ShapeDtypeStruct((m_out, aw), BF16),
        scratch_shapes=[pltpu.VMEM((seq, tq), F32), pltpu.VMEM((seq, tq), F32)],
        compiler_params=_cparams("parallel", "parallel"), name="attn_prompt",
    )(q_t, kb, v_t, qi_t, kib, kiw_t)


def _merge_kernel(za_ref, bm_ref, bt_ref, g_ref, x_ref, gw_ref, gb_ref, wa_ref, wb_ref, wo_ref, o_ref,
                  *, n_main, tm, tail):
    d = x_ref.shape[1]

    def run(rows, b_ref):
        za = jnp.concatenate([za_ref[j, :rows] for j in range(za_ref.shape[0])], axis=1)
        a_out = za * jax.nn.sigmoid(_dot(za.astype(BF16), gw_ref[...]) + gb_ref[...])
        merged = (g_ref[:rows, :d] * _dot(a_out.astype(BF16), wa_ref[...])
                  + g_ref[:rows, d:] * _dot(b_ref[:rows], wb_ref[...]))
        o_ref[:rows] = x_ref[:rows] + _dot(merged.astype(BF16), wo_ref[...])

    i = pl.program_id(0)
    pl.when(i < n_main)(lambda: run(tm, bm_ref))
    pl.when(i == n_main)(lambda: run(tail, bt_ref))


def _merge(za8, b_main, b_tail, gates, x, glu_w, glu_b, wa, wb, wo, *, tm):
    m, d = x.shape
    nb = za8.shape[0]
    n_main_rows, aw = b_main.shape
    tail = b_tail.shape[0]
    n_main = n_main_rows // tm
    assert n_main * tm == n_main_rows and n_main_rows + tail == m and tail <= tm
    resident = lambda shape: pl.BlockSpec(shape, lambda i: (0,) * len(shape), pipeline_mode=pl.Buffered(1))
    return pl.pallas_call(
        functools.partial(_merge_kernel, n_main=n_main, tm=tm, tail=tail), grid=(n_main + 1,),
        in_specs=[pl.BlockSpec((nb, tm, LANES), lambda i: (0, i, 0)),
                  pl.BlockSpec((tm, aw), lambda i: (jnp.minimum(i, n_main - 1), 0)),
                  pl.BlockSpec((tail, aw), lambda i: (0, 0)),
                  _row_spec(tm, 2 * d), _row_spec(tm, d), resident(glu_w.shape), resident((1, glu_w.shape[1])),
                  resident(wa.shape), resident(wb.shape), resident(wo.shape)],
        out_specs=_row_spec(tm, d),
        out_shape=jax.ShapeDtypeStruct((m, d), F32),
        compiler_params=_cparams("arbitrary"), name="merge",
    )(za8, b_main, b_tail, gates, x, glu_w, glu_b.reshape(1, -1), wa, wb, wo)


QPAD = SUBLANES
CHUNK_PAGES = 8


def _page_copy(pt_ref, src_ref, buf_ref, sem, seq, slot, p, page_base, rows):
    start = pl.multiple_of((page_base + pt_ref[seq, p]) * rows, rows)
    return pltpu.make_async_copy(src_ref.at[pl.ds(start, rows), :],
                                 buf_ref.at[slot, pl.ds(p * rows, rows), :], sem)


def _prefetch_pages(n_pages, copies):
    s, n_seq = pl.program_id(0), pl.num_programs(0)
    slot = s % 2

    def start_all(seq, sl):
        def body(p, carry):
            for copy in copies:
                copy(seq, sl, p).start()
            return carry
        lax.fori_loop(0, n_pages, body, 0)

    @pl.when(s == 0)
    def _():
        start_all(0, 0)

    @pl.when(s + 1 < n_seq)
    def _():
        start_all(s + 1, 1 - slot)

    def wait_body(p, carry):
        for copy in copies:
            copy(s, slot, p).wait()
        return carry
    lax.fori_loop(0, n_pages, wait_body, 0)
    return slot


def _sample_scores_kernel(pt_ref, qs_ref, ws_ref, kn_ref, cik_ref, sp_ref, sn_ref, buf_ref, sem_ref,
                          *, page_base, n_pages):
    slot = _prefetch_pages(n_pages, [
        lambda seq, sl, p: _page_copy(pt_ref, cik_ref, buf_ref, sem_ref.at[sl], seq, sl, p, page_base, IDX_DIM)])
    qs, ws = qs_ref[...], ws_ref[:, :1]

    def scores(d):
        r = jnp.maximum(d, 0.0) * ws
        return r.reshape(IDX_HEADS, QPAD, r.shape[1]).sum(axis=0)

    nkc = CHUNK_PAGES * PAGE_SIZE
    for c in range(n_pages // CHUNK_PAGES):
        keys_t = jnp.concatenate(
            [buf_ref[slot, (c * CHUNK_PAGES + kk) * IDX_DIM:(c * CHUNK_PAGES + kk + 1) * IDX_DIM, :]
             for kk in range(CHUNK_PAGES)], axis=1).astype(BF16)
        sp_ref[:, c * nkc:(c + 1) * nkc] = scores(_dot(qs, keys_t))
    sn_ref[...] = scores(_dot_nt(qs, kn_ref[...]))


def _sample_thr_kernel(sp_ref, sn_ref, thr_ref, cut_ref, s_scr, *, past, top_k, t_len):
    s = jnp.concatenate([sp_ref[...], sn_ref[...]], axis=1)
    shape = s.shape
    kpos = lax.broadcasted_iota(jnp.int32, shape, 1)
    qidx = lax.broadcasted_iota(jnp.int32, shape, 0) % QPAD
    s_scr[...] = jnp.where(kpos <= past + qidx, s, -jnp.inf)

    def count_ge(t):
        return jnp.sum((s_scr[...] >= t).astype(jnp.int32), axis=1, keepdims=True)

    thr, cnt = _kth_largest(count_ge, (shape[0], 1), top_k)
    thr_ref[...] = jnp.broadcast_to(thr, thr_ref.shape)
    real_row = lax.broadcasted_iota(jnp.int32, (shape[0], 1), 0) % QPAD < t_len
    tied = jnp.max(jnp.where((cnt > top_k) & real_row, 1, 0)) > 0

    @pl.when(jnp.logical_not(tied))
    def _():
        cut_ref[...] = jnp.full(cut_ref.shape, shape[1], jnp.int32)

    @pl.when(tied)
    def _():
        n_gt = jnp.sum((s_scr[...] > thr).astype(jnp.int32), axis=1, keepdims=True)
        need = (top_k - n_gt).astype(F32)
        tri = _tri_ones(LANES, BF16, lower=False)
        carry = jnp.zeros((shape[0], 1), F32)
        cut = jnp.full((shape[0], 1), -1, jnp.int32)
        lane = lax.broadcasted_iota(jnp.int32, (shape[0], LANES), 1)
        for c in range(shape[1] // LANES):
            eq = s_scr[:, c * LANES:(c + 1) * LANES] == thr
            rank = _dot(jnp.where(eq, 1.0, 0.0).astype(BF16), tri) + carry
            last = jnp.max(jnp.where(eq & (rank <= need), lane + c * LANES, -1), axis=1, keepdims=True)
            cut = jnp.maximum(cut, last)
            carry = rank[:, LANES - 1:]
        cut_ref[...] = jnp.broadcast_to(cut, cut_ref.shape)


def _sample_attn_kernel(pt_ref, q_ref, sp_ref, sn_ref, thr_ref, cut_ref, kn_ref, vn_ref, ck_ref, cv_ref, o_ref,
                        kbuf, vbuf, sem_ref, lg_scr, *, page_base, n_pages):
    page_rows = N_KV_HEADS * PAGE_SIZE
    slot = _prefetch_pages(n_pages, [
        lambda seq, sl, p: _page_copy(pt_ref, ck_ref, kbuf, sem_ref.at[0, sl], seq, sl, p, page_base, page_rows),
        lambda seq, sl, p: _page_copy(pt_ref, cv_ref, vbuf, sem_ref.at[1, sl], seq, sl, p, page_base, page_rows)])
    rows_g = N_REP * QPAD
    nkc = CHUNK_PAGES * PAGE_SIZE
    n_chunks = n_pages // CHUNK_PAGES
    past = n_pages * PAGE_SIZE
    thr, cut = thr_ref[:, :1], cut_ref[:, :1]
    q = q_ref[...]
    qg = [q[g * rows_g:(g + 1) * rows_g] for g in range(N_KV_HEADS)]
    head_rows = lambda c, g: pl.ds(c * CHUNK_PAGES * page_rows + g, nkc, stride=N_KV_HEADS)

    def bias(scores, pos0, allowed=None):
        kpos = pos0 + lax.broadcasted_iota(jnp.int32, scores.shape, 1)
        sel = (scores > thr) | ((scores == thr) & (kpos <= cut))
        if allowed is not None:
            sel = sel & allowed
        return jnp.concatenate([jnp.where(sel, 0.0, -jnp.inf)] * N_HEADS, axis=0)

    for c in range(n_chunks):
        lg = jnp.concatenate([_dot_nt(qg[g], kbuf.at[slot][head_rows(c, g), :].astype(BF16))
                              for g in range(N_KV_HEADS)], axis=0)
        lg_scr[:, c * nkc:(c + 1) * nkc] = lg + bias(sp_ref[:, c * nkc:(c + 1) * nkc], c * nkc)
    shape = sn_ref.shape
    allowed = lax.broadcasted_iota(jnp.int32, shape, 1) <= lax.broadcasted_iota(jnp.int32, shape, 0)
    lg = jnp.concatenate([_dot_nt(qg[g], kn_ref[:, g * HEAD_DIM:(g + 1) * HEAD_DIM])
                          for g in range(N_KV_HEADS)], axis=0)
    lg_scr[:, past:] = lg + bias(sn_ref[...], past, allowed)

    lg = lg_scr[...]
    p = jnp.exp(lg - jnp.max(lg, axis=1, keepdims=True))
    den = jnp.sum(p, axis=1, keepdims=True)
    pb = p.astype(BF16)
    acc = jnp.concatenate([_dot(pb[g * rows_g:(g + 1) * rows_g, past:], vn_ref[:, g * HEAD_DIM:(g + 1) * HEAD_DIM])
                           for g in range(N_KV_HEADS)], axis=0)
    for c in range(n_chunks):
        acc = acc + jnp.concatenate(
            [_dot(pb[g * rows_g:(g + 1) * rows_g, c * nkc:(c + 1) * nkc],
                  vbuf.at[slot][head_rows(c, g), :].astype(BF16)) for g in range(N_KV_HEADS)], axis=0)
    o_ref[...] = acc / den


def _pad_queries(x, t_len):
    n, h, d = x.shape
    x = jnp.transpose(x.reshape(n // t_len, t_len, h, d), (0, 2, 1, 3))
    x = jnp.pad(x, ((0, 0), (0, 0), (0, QPAD - t_len), (0, 0)))
    return x.reshape(n // t_len, h * QPAD, d)


def _pad_new_keys(x, t_len):
    n, w = x.shape
    return jnp.pad(x.reshape(n // t_len, t_len, w), ((0, 0), (0, PAGE_SIZE - t_len), (0, 0)))


def _attn_sample(q_s, k_new, v_new, qi_s, wi_s, ki_new, cache_k, cache_v, cache_idx_k, page_table, page_base,
                 *, t_len):
    db, n_pages = page_table.shape
    past = n_pages * PAGE_SIZE
    top_k = min(TOPK_MAX, (past + t_len) // 4)
    kvw = N_KV_HEADS * HEAD_DIM
    sds = jax.ShapeDtypeStruct
    q = _pad_queries(q_s.reshape(-1, N_HEADS, HEAD_DIM), t_len)
    qi = _pad_queries(qi_s.reshape(-1, IDX_HEADS, IDX_DIM), t_len)
    ws = _pad_queries(wi_s.reshape(-1, IDX_HEADS, 1), t_len)
    ws = jnp.broadcast_to(ws, ws.shape[:2] + (PAGE_SIZE,))
    kn, vn, kin = (_pad_new_keys(a, t_len) for a in (k_new, v_new, ki_new))

    assert n_pages % CHUNK_PAGES == 0
    seq_spec = lambda r, c: pl.BlockSpec((None, r, c), lambda s, pt: (s, 0, 0))
    hbm = pl.BlockSpec(memory_space=pl.ANY)
    dma = pltpu.SemaphoreType.DMA
    in_order = _cparams("arbitrary")
    sp, sn = pl.pallas_call(
        functools.partial(_sample_scores_kernel, page_base=page_base, n_pages=n_pages),
        grid_spec=pltpu.PrefetchScalarGridSpec(
            num_scalar_prefetch=1, grid=(db,),
            in_specs=[seq_spec(IDX_HEADS * QPAD, IDX_DIM), seq_spec(IDX_HEADS * QPAD, PAGE_SIZE),
                      seq_spec(PAGE_SIZE, IDX_DIM), hbm],
            out_specs=[seq_spec(QPAD, past), seq_spec(QPAD, PAGE_SIZE)],
            scratch_shapes=[pltpu.VMEM((2, n_pages * IDX_DIM, PAGE_SIZE), F32), dma((2,))]),
        out_shape=[sds((db, QPAD, past), F32), sds((db, QPAD, PAGE_SIZE), F32)],
        compiler_params=in_order, name="sample_scores",
    )(page_table, qi, ws, kin, cache_idx_k.reshape(-1, PAGE_SIZE))

    rows = db * QPAD
    rblk = min(rows, 8 * QPAD)
    assert rows % rblk == 0
    thr, cut = pl.pallas_call(
        functools.partial(_sample_thr_kernel, past=past, top_k=top_k, t_len=t_len),
        grid=(rows // rblk,),
        in_specs=[_row_spec(rblk, past), _row_spec(rblk, PAGE_SIZE)],
        out_specs=[_row_spec(rblk, LANES), _row_spec(rblk, LANES)],
        out_shape=[sds((rows, LANES), F32), sds((rows, LANES), jnp.int32)],
        scratch_shapes=[pltpu.VMEM((rblk, past + PAGE_SIZE), F32)],
        compiler_params=_cparams("parallel"), name="sample_threshold",
    )(sp.reshape(rows, past), sn.reshape(rows, PAGE_SIZE))

    hq = N_HEADS * QPAD
    page_rows = N_KV_HEADS * PAGE_SIZE
    out = pl.pallas_call(
        functools.partial(_sample_attn_kernel, page_base=page_base, n_pages=n_pages),
        grid_spec=pltpu.PrefetchScalarGridSpec(
            num_scalar_prefetch=1, grid=(db,),
            in_specs=[seq_spec(hq, HEAD_DIM), seq_spec(QPAD, past), seq_spec(QPAD, PAGE_SIZE), seq_spec(QPAD, LANES),
                      seq_spec(QPAD, LANES), seq_spec(PAGE_SIZE, kvw), seq_spec(PAGE_SIZE, kvw), hbm, hbm],
            out_specs=seq_spec(hq, HEAD_DIM),
            scratch_shapes=[pltpu.VMEM((2, n_pages * page_rows, HEAD_DIM), F32),
                            pltpu.VMEM((2, n_pages * page_rows, HEAD_DIM), F32),
                            dma((2, 2)), pltpu.VMEM((hq, past + PAGE_SIZE), F32)]),
        out_shape=sds((db, hq, HEAD_DIM), F32),
        compiler_params=in_order, name="sample_attn",
    )(page_table, q, sp, sn, thr.reshape(db, QPAD, LANES), cut.reshape(db, QPAD, LANES), kn, vn, cache_k, cache_v)
    out = out.reshape(db, N_HEADS, QPAD, HEAD_DIM)[:, :, :t_len]
    return jnp.transpose(out, (0, 2, 1, 3)).reshape(db * t_len, N_HEADS * HEAD_DIM)


ROW_TILE = 640
PROJ_ROW_TILE = 512
FFN_ROW_TILE = 512
FF_TILE = 512
MERGE_ROW_TILE = 256


def _layer(x_p, x_s, pos, dims, layer, w_in_all, ck, cv, cik, page_base, s_re, s_im, page_table, p):
    b, s, db, t = dims
    mp, ms = b * s, db * t
    d = x_p.shape[1]
    bf = lambda w: w.astype(BF16)
    ssm_w = d // 2
    w_g = bf(w_in_all[layer, :, w_in_all.shape[2] - 2 * d:])

    x1, xn = _ffn(x_p, x_s, p["ffn1_norm"], bf(p["ffn1_w_gate"]), bf(p["ffn1_w_up"]), bf(p["ffn1_w_down"]),
                  p["mix_norm"], n_main_rows=mp, tail=ms, split_out=False, tm=FFN_ROW_TILE, tf=FF_TILE)
    pr = _projections(xn, jnp.swapaxes(w_in_all, 1, 2), layer, w_g, p["q_norm"], p["k_norm"], pos,
                      n_main_rows=mp, seq=s, tail=ms, tm=PROJ_ROW_TILE, tm_gates=ROW_TILE)

    kmat, win, wout, lp = _ssm_prep(p["ssm_lambda_re"], p["ssm_lambda_im"], p["ssm_b_re"], p["ssm_b_im"],
                                    p["ssm_c_re"], p["ssm_c_im"], p["ssm_log_dt"])
    nblk = ssm_w // LANES
    d8 = p["ssm_d"].reshape(nblk, 1, LANES)
    za8, hl_p, hl_s = _ssm(pr["u8"], kmat, win, wout, lp, d8, _state_to_blocks(s_re, s_im),
                           nseq=b, seq=s, n_s=db, t_s=t)

    b_p = _attn_prompt(pr["q_t"], pr["kb"], pr["v_t"], pr["qi_t"], pr["kib"], pr["kiw_t"],
                       batch=b, seq=s, m_out=mp)
    b_s = _attn_sample(pr["q"][mp:], pr["kb"][mp:], bf(pr["v"][mp:]), pr["qi"][mp:],
                       pr["kiw"][mp:, IDX_DIM:IDX_DIM + IDX_HEADS], pr["kib"][mp:],
                       ck, cv, cik, page_table, page_base, t_len=t)

    x2 = _merge(za8, b_p, bf(b_s), pr["gates"], x1, bf(p["glu_w"]), p["glu_b"], bf(p["w_branch_a"]),
                bf(p["w_branch_b"]), bf(p["w_out"]), tm=MERGE_ROW_TILE)
    y_p, y_s = _ffn(x2, None, p["ffn2_norm"], bf(p["ffn2_w_gate"]), bf(p["ffn2_w_up"]), bf(p["ffn2_w_down"]),
                    n_main_rows=mp, tail=ms, split_out=True, tm=FFN_ROW_TILE, tf=FF_TILE)

    hp_re, hp_im = _blocks_to_state(hl_p)
    hs_re, hs_im = _blocks_to_state(hl_s)
    k, v, ki = pr["k"], pr["v"], pr["kiw"][:, :IDX_DIM]
    kvs = (N_KV_HEADS, HEAD_DIM)
    rows = (k[:mp].reshape(b, s, *kvs), v[:mp].reshape(b, s, *kvs), ki[:mp].reshape(b, s, IDX_DIM), hp_re, hp_im,
            k[mp:].reshape(db, t, *kvs), v[mp:].reshape(db, t, *kvs), ki[mp:].reshape(db, t, IDX_DIM), hs_re, hs_im)
    return y_p, y_s, rows


def kernel(x_prompt, x_sample, cache_k, cache_v, cache_idx_k, state_ssm_re, state_ssm_im, page_table,
           ffn1_norm, ffn1_w_gate, ffn1_w_up, ffn1_w_down, mix_norm, w_in, q_norm, k_norm,
           ssm_lambda_re, ssm_lambda_im, ssm_b_re, ssm_b_im, ssm_c_re, ssm_c_im, ssm_d, ssm_log_dt,
           glu_w, glu_b, w_branch_a, w_branch_b, w_out, ffn2_norm, ffn2_w_gate, ffn2_w_up, ffn2_w_down):
    b, s, d = x_prompt.shape
    db, t, _ = x_sample.shape
    depth, n_phys = cache_k.shape[:2]
    past = page_table.shape[1] * PAGE_SIZE
    x_p, x_s = x_prompt.reshape(b * s, d), x_sample.reshape(db * t, d)
    pos = jnp.concatenate([jnp.arange(s), jnp.tile(past + jnp.arange(t), db)])
    ck = cache_k.reshape(-1, HEAD_DIM)
    cv = cache_v.reshape(-1, HEAD_DIM)
    cik = jnp.swapaxes(cache_idx_k, 2, 3).reshape(depth * n_phys, IDX_DIM, PAGE_SIZE)
    params = dict(
        ffn1_norm=ffn1_norm, ffn1_w_gate=ffn1_w_gate, ffn1_w_up=ffn1_w_up, ffn1_w_down=ffn1_w_down,
        mix_norm=mix_norm, w_in=w_in, q_norm=q_norm, k_norm=k_norm,
        ssm_lambda_re=ssm_lambda_re, ssm_lambda_im=ssm_lambda_im, ssm_b_re=ssm_b_re, ssm_b_im=ssm_b_im,
        ssm_c_re=ssm_c_re, ssm_c_im=ssm_c_im, ssm_d=ssm_d, ssm_log_dt=ssm_log_dt, glu_w=glu_w, glu_b=glu_b,
        w_branch_a=w_branch_a, w_branch_b=w_branch_b, w_out=w_out,
        ffn2_norm=ffn2_norm, ffn2_w_gate=ffn2_w_gate, ffn2_w_up=ffn2_w_up, ffn2_w_down=ffn2_w_down)
    new = [[] for _ in range(10)]
    for l in range(depth):
        p = {name: w[l] for name, w in params.items()}
        x_p, x_s, rows = _layer(x_p, x_s, pos, (b, s, db, t), l, w_in, ck, cv, cik, l * n_phys,
                                state_ssm_re[l], state_ssm_im[l], page_table, p)
        for lst, r in zip(new, rows):
            lst.append(r)
    return (x_p.reshape(b, s, d), x_s.reshape(db, t, d)) + tuple(jnp.stack(lst) for lst in new)
```

```python
import functools
import math

import jax
import jax.numpy as jnp
from jax import lax
from jax.experimental import pallas as pl
from jax.experimental.pallas import tpu as pltpu

F32 = jnp.float32
BF16 = jnp.bfloat16

SSM_GROUP = 16
SSM_STATE = 64
N_HEADS = 8
HEAD_DIM = 128
N_KV_HEADS = 2
N_REP = N_HEADS // N_KV_HEADS
ROT_DIM = HEAD_DIM // 4
ROPE_THETA = 500000.0
IDX_HEADS = 16
IDX_DIM = 64
IDX_ROT_DIM = IDX_DIM // 4
TOPK_MAX = 256
PAGE_SIZE = 128
FFN_RES = 0.5
EPS = 1e-6

LANES = 128
SUBLANES = 8
VMEM_LIMIT_BYTES = 56 * 1024 * 1024

GROUPS_PER_BLOCK = LANES // SSM_GROUP
STATES_PER_BLOCK = GROUPS_PER_BLOCK * SSM_STATE
SSM_T = 8

INT_MIN = -(2 ** 31)


def _cparams(*sem):
    return pltpu.CompilerParams(dimension_semantics=sem, vmem_limit_bytes=VMEM_LIMIT_BYTES)


def _rms(x, g):
    return x * lax.rsqrt(jnp.mean(x * x, axis=-1, keepdims=True) + EPS) * g


def _dot(a, b):
    return jnp.dot(a, b, preferred_element_type=F32)


def _dot_nt(a, b):
    return lax.dot_general(a, b, (((1,), (1,)), ((), ())), preferred_element_type=F32)


def _ffn_kernel(*refs, nf, n_main, tm, tail, two_src, split_out, with_next_norm):
    refs = list(refs)
    x_main_ref = refs.pop(0)
    x_tail_ref = refs.pop(0) if two_src else x_main_ref
    g_ref, wg_ref, wu_ref, wd_ref = (refs.pop(0) for _ in range(4))
    g2_ref = refs.pop(0) if with_next_norm else None
    y_main_ref = refs.pop(0)
    y_tail_ref = refs.pop(0) if split_out else y_main_ref
    n2_ref = refs.pop(0) if with_next_norm else None
    xn_ref, acc_ref = refs
    i, f = pl.program_id(0), pl.program_id(1)

    def run(rows, x_ref, y_ref):
        @pl.when(f == 0)
        def _():
            xn_ref[:rows] = _rms(x_ref[:rows], g_ref[...]).astype(BF16)
            acc_ref[:rows] = jnp.zeros((rows, acc_ref.shape[1]), F32)

        xn = xn_ref[:rows]
        a = _dot(xn, wg_ref[...])
        b = _dot(xn, wu_ref[...])
        h = (a * jax.nn.sigmoid(a) * b).astype(BF16)
        acc_ref[:rows] += _dot(h, wd_ref[...])

        @pl.when(f == nf - 1)
        def _():
            y = x_ref[:rows] + FFN_RES * acc_ref[:rows]
            y_ref[:rows] = y
            if with_next_norm:
                n2_ref[:rows] = _rms(y, g2_ref[...]).astype(BF16)

    pl.when(i < n_main)(lambda: run(tm, x_main_ref, y_main_ref))
    pl.when(i == n_main)(lambda: run(tail, x_tail_ref, y_tail_ref))


def _ffn(x_main, x_tail, g, wg, wu, wd, g2=None, *, n_main_rows, tail, split_out, tm, tf):
    d = x_main.shape[1]
    nf = wg.shape[1] // tf
    n_main = n_main_rows // tm
    m = n_main_rows + tail
    two_src = x_tail is not None
    with_next = g2 is not None
    clamp = lambda i, f: (jnp.minimum(i, n_main - 1), 0)
    rows = lambda i, f: (i, 0)
    first = lambda i, f: (0, 0)
    sds = jax.ShapeDtypeStruct
    in_specs = [pl.BlockSpec((tm, d), clamp if two_src else rows)]
    args = [x_main]
    if two_src:
        in_specs.append(pl.BlockSpec((tail, d), first))
        args.append(x_tail)
    in_specs += [pl.BlockSpec((1, d), first), pl.BlockSpec((d, tf), lambda i, f: (0, f)),
                 pl.BlockSpec((d, tf), lambda i, f: (0, f)), pl.BlockSpec((tf, d), lambda i, f: (f, 0))]
    args += [g.reshape(1, d), wg, wu, wd]
    if with_next:
        in_specs.append(pl.BlockSpec((1, d), first))
        args.append(g2.reshape(1, d))
    if split_out:
        out_shape = [sds((n_main_rows, d), F32), sds((tail, d), F32)]
        out_specs = [pl.BlockSpec((tm, d), clamp), pl.BlockSpec((tail, d), first)]
    else:
        out_shape = [sds((m, d), F32)]
        out_specs = [pl.BlockSpec((tm, d), rows)]
    if with_next:
        out_shape.append(sds((m, d), BF16))
        out_specs.append(pl.BlockSpec((tm, d), rows))
    return pl.pallas_call(
        functools.partial(_ffn_kernel, nf=nf, n_main=n_main, tm=tm, tail=tail, two_src=two_src,
                          split_out=split_out, with_next_norm=with_next),
        grid=(n_main + 1, nf),
        in_specs=in_specs,
        out_specs=out_specs,
        out_shape=out_shape,
        scratch_shapes=[pltpu.VMEM((tm, d), BF16), pltpu.VMEM((tm, d), F32)],
        compiler_params=_cparams("arbitrary", "arbitrary"),
        name="ffn" + ("_norm" if with_next else ""),
    )(*args)


def _rope_tables(pos, rot_dim, width, tile):
    half = rot_dim // 2
    m = pos.shape[0]
    freqs = ROPE_THETA ** (-jnp.arange(half, dtype=F32) * 2.0 / rot_dim)
    ang = pos.astype(F32)[:, None] * freqs[None, :]
    cos, sin = jnp.cos(ang), jnp.sin(ang)
    zh = jnp.zeros((m, half), F32)
    rest = width - rot_dim
    c = jnp.concatenate([cos, cos, jnp.ones((m, rest), F32)], axis=1)
    s1 = jnp.concatenate([-sin, zh, jnp.zeros((m, rest), F32)], axis=1)
    s2 = jnp.concatenate([zh, sin, jnp.zeros((m, rest), F32)], axis=1)
    if tile:
        reps = LANES // width
        return tuple(jnp.tile(t, (1, reps)) for t in (c, s1, s2))
    pad = LANES - width
    return (jnp.pad(c, ((0, 0), (0, pad)), constant_values=1.0),
            jnp.pad(s1, ((0, 0), (0, pad))), jnp.pad(s2, ((0, 0), (0, pad))))


def _rope(x, c, s1, s2, half):
    return x * c + pltpu.roll(x, LANES - half, 1) * s1 + pltpu.roll(x, half, 1) * s2


def _proj_rows(xn_ref, w_refs, wbf_ref, n_main, tm, tail, chunk, epilogue):
    i = pl.program_id(0)

    @pl.when(i == 0)
    def _():
        off = 0
        for w_ref in w_refs:
            n = w_ref.shape[0]
            wbf_ref[:, off:off + n] = w_ref[...].astype(F32).T.astype(BF16)
            off += n

    def run(rows):
        for r0 in range(0, rows, chunk):
            rs = slice(r0, min(r0 + chunk, rows))
            epilogue(_dot(xn_ref[rs, :], wbf_ref[...]), rs, rows == tail)

    pl.when(i < n_main)(lambda: run(tm))
    pl.when(i == n_main)(lambda: run(tail))


def _u_proj_kernel(xn_ref, w_ref, u_ref, wbf_ref, **kw):
    def epilogue(z, rs, is_tail):
        for j in range(z.shape[1] // LANES):
            u_ref[j, rs, :] = z[:, j * LANES:(j + 1) * LANES]
    _proj_rows(xn_ref, [w_ref], wbf_ref, epilogue=epilogue, **kw)


def _q_proj_kernel(xn_ref, w_ref, g_ref, c_ref, s1_ref, s2_ref, qn_ref, qt_ref, wbf_ref, **kw):
    def epilogue(z, rs, is_tail):
        c, s1, s2, g = c_ref[rs, :], s1_ref[rs, :], s2_ref[rs, :], g_ref[...]
        heads = []
        for h in range(N_HEADS):
            x = _rms(z[:, h * HEAD_DIM:(h + 1) * HEAD_DIM], g)
            heads.append(_rope(x, c, s1, s2, ROT_DIM // 2) * (HEAD_DIM ** -0.5))
        q = jnp.concatenate(heads, axis=1)
        qn_ref[rs, :] = q.astype(BF16)
        qt_ref[:, rs] = q.T.astype(BF16)
    _proj_rows(xn_ref, [w_ref], wbf_ref, epilogue=epilogue, **kw)


def _kv_proj_kernel(xn_ref, wkv_ref, wkw_ref, g_ref, c_ref, s1_ref, s2_ref, ci_ref, si1_ref, si2_ref, scale_ref,
                    kp_ref, ks_ref, vp_ref, vs_ref, kb_ref, vt_ref, kiw_ref, kiwt_ref, kib_ref, wbf_ref, **kw):
    kvw = N_KV_HEADS * HEAD_DIM

    def epilogue(z, rs, is_tail):
        c, s1, s2, g = c_ref[rs, :], s1_ref[rs, :], s2_ref[rs, :], g_ref[...]
        k_ref, v_ref = (ks_ref, vs_ref) if is_tail else (kp_ref, vp_ref)
        n = rs.stop - rs.start
        heads = []
        for h in range(N_KV_HEADS):
            x = _rope(_rms(z[:, h * HEAD_DIM:(h + 1) * HEAD_DIM], g), c, s1, s2, ROT_DIM // 2)
            rows_h = pl.ds(N_KV_HEADS * rs.start + h, n, stride=N_KV_HEADS)
            k_ref[rows_h, :] = x
            v_ref[rows_h, :] = z[:, kvw + h * HEAD_DIM:kvw + (h + 1) * HEAD_DIM]
            heads.append(x)
        kb_ref[rs, :] = jnp.concatenate(heads, axis=1).astype(BF16)
        vt_ref[:, rs] = z[:, kvw:2 * kvw].T.astype(BF16)
        y = _rope(z[:, 2 * kvw:], ci_ref[rs, :], si1_ref[rs, :], si2_ref[rs, :], IDX_ROT_DIM // 2) * scale_ref[...]
        kiw_ref[rs, :] = y
        kiwt_ref[:, rs] = y.T
        kib_ref[rs, :] = y[:, :IDX_DIM].astype(BF16)
    _proj_rows(xn_ref, [wkv_ref, wkw_ref], wbf_ref, epilogue=epilogue, **kw)


def _qi_proj_kernel(xn_ref, wa_ref, wb_ref, c_ref, s1_ref, s2_ref, qn_ref, qt_ref, wbf_ref, **kw):
    def epilogue(z, rs, is_tail):
        c, s1, s2 = c_ref[rs, :], s1_ref[rs, :], s2_ref[rs, :]
        cols = []
        for j in range(z.shape[1] // LANES):
            x = z[:, j * LANES:(j + 1) * LANES]
            cols.append(_rope(x, c, s1, s2, IDX_ROT_DIM // 2) * (IDX_DIM ** -0.5))
        q = jnp.concatenate(cols, axis=1)
        qn_ref[rs, :] = q.astype(BF16)
        qt_ref[:, rs] = q.T.astype(BF16)
    _proj_rows(xn_ref, [wa_ref, wb_ref], wbf_ref, epilogue=epilogue, **kw)


def _gate_proj_kernel(xn_ref, w_ref, o_ref, *, chunk):
    for r0 in range(0, xn_ref.shape[0], chunk):
        rs = slice(r0, r0 + chunk)
        o_ref[rs, :] = jax.nn.sigmoid(_dot(xn_ref[rs, :], w_ref[...])).astype(BF16)


def _row_spec(tm, n):
    return pl.BlockSpec((tm, n), lambda i: (i, 0))


def _full_spec(shape):
    return pl.BlockSpec(shape, lambda i: (0,) * len(shape))


PROJ_CHUNK = 256


def _projections(xn, w_in_t, layer, w_g, q_norm, k_norm, pos, *, n_main_rows, seq, tail, tm, tm_gates):
    m, d = xn.shape
    n_main = n_main_rows // tm
    assert n_main * tm == n_main_rows and n_main_rows + tail == m and tail <= tm
    grid = (n_main + 1,)
    ssm_w, attn_w, kv_w, idx_w = d // 2, N_HEADS * HEAD_DIM, N_KV_HEADS * HEAD_DIM, IDX_HEADS * IDX_DIM
    hd_tabs = _rope_tables(pos, ROT_DIM, HEAD_DIM, True)
    ix_tabs = _rope_tables(pos, IDX_ROT_DIM, IDX_DIM, True)
    kw_tabs = _rope_tables(pos, IDX_ROT_DIM, IDX_DIM, False)
    assert seq % tm == 0 and pos.shape[0] == seq + tail
    tabs_per_seq = seq // tm
    tab_specs = [pl.BlockSpec((tm, LANES), lambda i: (jnp.where(i < n_main, i % tabs_per_seq, tabs_per_seq), 0))] * 3
    xs = _row_spec(tm, d)
    sds = jax.ShapeDtypeStruct
    seq = _cparams("arbitrary")
    kw = dict(n_main=n_main, tm=tm, tail=tail, chunk=PROJ_CHUNK)

    def wcols(start, width):
        assert start % width == 0
        return pl.BlockSpec((None, width, d), lambda i: (layer, start // width, 0), pipeline_mode=pl.Buffered(1))

    def call(kernel, name, w_specs, n_w, extra_specs, extra_args, out_specs, out_shape):
        return pl.pallas_call(
            functools.partial(kernel, **kw), grid=grid,
            in_specs=[xs] + w_specs + extra_specs, out_specs=out_specs, out_shape=out_shape,
            scratch_shapes=[pltpu.VMEM((d, n_w), BF16)], compiler_params=seq, name=name,
        )(xn, *([w_in_t] * len(w_specs)), *extra_args)

    tcol = lambda n: pl.BlockSpec((n, tm), lambda i: (0, i))
    norm_spec = _full_spec((1, HEAD_DIM))
    u8 = call(_u_proj_kernel, "proj_u", [wcols(0, ssm_w)], ssm_w, [], [],
              pl.BlockSpec((ssm_w // LANES, tm, LANES), lambda i: (0, i, 0)), sds((ssm_w // LANES, m, LANES), F32))
    q_nat, q_t = call(_q_proj_kernel, "proj_q", [wcols(ssm_w, attn_w)], attn_w,
                      [norm_spec] + tab_specs, [q_norm.reshape(1, HEAD_DIM), *hd_tabs],
                      [_row_spec(tm, attn_w), tcol(attn_w)], [sds((m, attn_w), BF16), sds((attn_w, m), BF16)])
    off_kv = ssm_w + attn_w
    off_qi = off_kv + 2 * kv_w
    off_kw = off_qi + idx_w
    lane = jnp.arange(LANES)
    kw_scale = jnp.where(lane < IDX_DIM, 1.0, jnp.where(lane < IDX_DIM + IDX_HEADS, IDX_HEADS ** -0.5, 0.0))
    nh = N_KV_HEADS
    head_main = pl.BlockSpec((nh * tm, HEAD_DIM), lambda i: (jnp.minimum(i, n_main - 1), 0))
    head_tail = pl.BlockSpec((nh * tail, HEAD_DIM), lambda i: (0, 0))
    k_p, k_s, v_p, v_s, kb, v_t, kiw, kiw_t, kib = call(
        _kv_proj_kernel, "proj_kv", [wcols(off_kv, 2 * kv_w), wcols(off_kw, LANES)], 2 * kv_w + LANES,
        [norm_spec] + tab_specs + tab_specs + [_full_spec((1, LANES))],
        [k_norm.reshape(1, HEAD_DIM), *hd_tabs, *kw_tabs, kw_scale.astype(F32).reshape(1, LANES)],
        [head_main, head_tail, head_main, head_tail, _row_spec(tm, kv_w), tcol(kv_w),
         _row_spec(tm, LANES), tcol(LANES), _row_spec(tm, IDX_DIM)],
        [sds((nh * n_main_rows, HEAD_DIM), F32), sds((nh * tail, HEAD_DIM), F32),
         sds((nh * n_main_rows, HEAD_DIM), F32), sds((nh * tail, HEAD_DIM), F32),
         sds((m, kv_w), BF16), sds((kv_w, m), BF16),
         sds((m, LANES), F32), sds((LANES, m), F32), sds((m, IDX_DIM), BF16)])
    half = idx_w // 2
    qi_nat, qi_t = call(_qi_proj_kernel, "proj_qi", [wcols(off_qi, half), wcols(off_qi + half, half)], idx_w,
                        tab_specs, ix_tabs,
                        [_row_spec(tm, idx_w), tcol(idx_w)], [sds((m, idx_w), BF16), sds((idx_w, m), BF16)])

    ng = w_g.shape[1]
    tn = 1024
    gates = pl.pallas_call(
        functools.partial(_gate_proj_kernel, chunk=tm_gates // 2), grid=(m // tm_gates, ng // tn),
        in_specs=[pl.BlockSpec((tm_gates, d), lambda i, j: (i, 0)), pl.BlockSpec((d, tn), lambda i, j: (0, j))],
        out_specs=pl.BlockSpec((tm_gates, tn), lambda i, j: (i, j)),
        out_shape=sds((m, ng), BF16), compiler_params=_cparams("parallel", "parallel"), name="proj_gates",
    )(xn, w_g)
    return dict(u8=u8, q=q_nat, q_t=q_t, k_p=k_p, k_s=k_s, kb=kb, v_p=v_p, v_s=v_s, v_t=v_t, qi=qi_nat, qi_t=qi_t,
                kiw=kiw, kiw_t=kiw_t, kib=kib, gates=gates)


def _ssm_prep_kernel(lre_ref, lim_ref, ldt_ref, bre_ref, bim_ref, cre_ref, cim_ref,
                     k_ref, win_ref, wout_ref, lp_ref):
    ns = STATES_PER_BLOCK
    lre, lim = lre_ref[...], lim_ref[...]
    dt = jnp.exp(ldt_ref[...])
    a, th = lre * dt, lim * dt

    def power(l):
        mag = jnp.exp(a * float(l))
        return mag * jnp.cos(th * float(l)), mag * jnp.sin(th * float(l))

    pw = [power(l) for l in range(SSM_T + 1)]
    xr, xi = pw[1][0] - 1.0, pw[1][1]
    den = lre * lre + lim * lim
    cr, ci = (xr * lre + xi * lim) / den, (xi * lre - xr * lim) / den
    bre, bim = bre_ref[...], bim_ref[...]
    bbr, bbi = bre * cr - bim * ci, bre * ci + bim * cr
    cre, cim = cre_ref[...], cim_ref[...]
    hi = lax.Precision.HIGHEST
    nt = (((1,), (1,)), ((), ()))
    lag = []
    for l in range(SSM_T):
        pr, pi = pw[l]
        blr, bli = bbr * pr - bbi * pi, bbr * pi + bbi * pr
        m = (lax.dot_general(blr, cre, nt, precision=hi, preferred_element_type=F32)
             - lax.dot_general(bli, cim, nt, precision=hi, preferred_element_type=F32))
        lag.append(m.astype(BF16))
        t = SSM_T - 1 - l
        win_ref[t * LANES:(t + 1) * LANES, :ns] = blr.astype(BF16)
        win_ref[t * LANES:(t + 1) * LANES, ns:] = bli.astype(BF16)
    zero = jnp.zeros((LANES, LANES), BF16)
    for t in range(SSM_T):
        pr, pi = pw[t + 1]
        wout_ref[t * LANES:(t + 1) * LANES, :ns] = (cre * pr - cim * pi).astype(BF16)
        wout_ref[t * LANES:(t + 1) * LANES, ns:] = (-(cre * pi + cim * pr)).astype(BF16)
        lp_ref[t:t + 1, :ns] = pr
        lp_ref[t:t + 1, ns:] = pi
        for t2 in range(SSM_T):
            k_ref[t * LANES:(t + 1) * LANES, t2 * LANES:(t2 + 1) * LANES] = lag[t2 - t] if t2 >= t else zero


def _block_diag_groups(w):
    g, h, p = w.shape
    nb = g // GROUPS_PER_BLOCK
    w = w.reshape(nb, GROUPS_PER_BLOCK, h, p)
    eye = jnp.eye(GROUPS_PER_BLOCK, dtype=w.dtype)
    out = w[:, :, :, None, :] * eye[None, :, None, :, None]
    return out.reshape(nb, GROUPS_PER_BLOCK * h, GROUPS_PER_BLOCK * p)


def _ssm_prep(lam_re, lam_im, b_re, b_im, c_re, c_im, log_dt):
    g, p = lam_re.shape
    nb = g // GROUPS_PER_BLOCK
    ns = STATES_PER_BLOCK
    tl = SSM_T * LANES
    vec = lambda a: a.reshape(nb, 1, ns)
    ldt = vec(jnp.broadcast_to(log_dt[:, None], (g, p)))
    bt = lambda b: _block_diag_groups(jnp.swapaxes(b, 1, 2))
    vspec = pl.BlockSpec((None, 1, ns), lambda j: (j, 0, 0))
    mspec = pl.BlockSpec((None, LANES, ns), lambda j: (j, 0, 0))
    sds = jax.ShapeDtypeStruct
    return pl.pallas_call(
        _ssm_prep_kernel, grid=(nb,),
        in_specs=[vspec, vspec, vspec, mspec, mspec, mspec, mspec],
        out_specs=[pl.BlockSpec((None, tl, tl), lambda j: (j, 0, 0)),
                   pl.BlockSpec((None, tl, 2 * ns), lambda j: (j, 0, 0)),
                   pl.BlockSpec((None, tl, 2 * ns), lambda j: (j, 0, 0)),
                   pl.BlockSpec((None, SSM_T, 2 * ns), lambda j: (j, 0, 0))],
        out_shape=[sds((nb, tl, tl), BF16), sds((nb, tl, 2 * ns), BF16), sds((nb, tl, 2 * ns), BF16),
                   sds((nb, SSM_T, 2 * ns), F32)],
        compiler_params=_cparams("parallel"), name="ssm_prep",
    )(vec(lam_re), vec(lam_im), ldt, bt(b_re), bt(b_im), _block_diag_groups(c_re), _block_diag_groups(c_im))


def _ssm_kernel(u_ref, k_ref, win_ref, wout_ref, lp_ref, d_ref, h0s_ref, za_ref, hlp_ref, hls_ref, x_scr, hs_scr,
                *, nseq, seq, n_s, t_s):
    ns = STATES_PER_BLOCK
    nk = ns // LANES
    d = d_ref[...]
    split = lambda a: [a[:, k * LANES:(k + 1) * LANES] for k in range(2 * nk)]

    def gather(row0, n_rows, t_steps):
        return [u_ref[pl.ds(row0 + t, n_rows, stride=t_steps), :] for t in range(t_steps)]

    def advance(h, x, t_steps):
        lam = [lp_ref[t_steps - 1:t_steps, k * LANES:(k + 1) * LANES] for k in range(2 * nk)]
        re = [lam[k] * h[k] - lam[nk + k] * h[nk + k] + x[k] for k in range(nk)]
        im = [lam[k] * h[nk + k] + lam[nk + k] * h[k] + x[nk + k] for k in range(nk)]
        return re + im

    def emit(row0, n_rows, t_steps, cols, hs, kmat, wout):
        u = jnp.concatenate(cols, axis=1).astype(BF16)
        y = _dot(u, kmat) + _dot_nt(hs.astype(BF16), wout)
        for t in range(t_steps):
            yt = y[:, t * LANES:(t + 1) * LANES] + d * cols[t]
            za_ref[pl.ds(row0 + t, n_rows, stride=t_steps), :] = jax.nn.gelu(yt, approximate=True)

    c_per = seq // SSM_T
    for b in range(nseq):
        u = jnp.concatenate(gather(b * seq, c_per, SSM_T), axis=1).astype(BF16)
        x = _dot(u, win_ref[...])
        for k in range(2 * nk):
            x_scr[k, b * c_per:(b + 1) * c_per, :] = x[:, k * LANES:(k + 1) * LANES]

    def step(c, h):
        rows = pl.ds(c, nseq, stride=c_per)
        for k in range(2 * nk):
            hs_scr.at[k][rows, :] = h[k]
        return tuple(advance(h, [x_scr.at[k][rows, :] for k in range(2 * nk)], SSM_T))

    h_last = lax.fori_loop(0, c_per, step, tuple(jnp.zeros((nseq, LANES), F32) for _ in range(2 * nk)),
                           unroll=4)
    hlp_ref[...] = jnp.concatenate(list(h_last), axis=1)
    for b in range(nseq):
        hs = jnp.concatenate([hs_scr[k, b * c_per:(b + 1) * c_per, :] for k in range(2 * nk)], axis=1)
        emit(b * seq, c_per, SSM_T, gather(b * seq, c_per, SSM_T), hs, k_ref[...], wout_ref[...])

    row0, tl = nseq * seq, t_s * LANES
    cols = gather(row0, n_s, t_s)
    u = jnp.concatenate(cols, axis=1).astype(BF16)
    x = _dot(u, win_ref[(SSM_T - t_s) * LANES:, :])
    h0 = h0s_ref[...]
    hls_ref[...] = jnp.concatenate(advance(split(h0), split(x), t_s), axis=1)
    emit(row0, n_s, t_s, cols, h0, k_ref[:tl, :tl], wout_ref[:tl, :])


def _ssm(u8, kmat, win, wout, lp, d8, h0s, *, nseq, seq, n_s, t_s):
    nb, m_total, _ = u8.shape
    ns = STATES_PER_BLOCK
    assert m_total == nseq * seq + n_s * t_s and seq % SSM_T == 0 and t_s <= SSM_T
    rows = nseq * seq // SSM_T
    tl = SSM_T * LANES
    sds = jax.ShapeDtypeStruct
    blk = lambda r, c: pl.BlockSpec((None, r, c), lambda j: (j, 0, 0))
    return pl.pallas_call(
        functools.partial(_ssm_kernel, nseq=nseq, seq=seq, n_s=n_s, t_s=t_s),
        grid=(nb,),
        in_specs=[blk(m_total, LANES), blk(tl, tl), blk(tl, 2 * ns), blk(tl, 2 * ns), blk(SSM_T, 2 * ns),
                  blk(1, LANES), blk(n_s, 2 * ns)],
        out_specs=[blk(m_total, LANES), blk(nseq, 2 * ns), blk(n_s, 2 * ns)],
        out_shape=[sds((nb, m_total, LANES), F32), sds((nb, nseq, 2 * ns), F32), sds((nb, n_s, 2 * ns), F32)],
        scratch_shapes=[pltpu.VMEM((2 * ns // LANES, rows, LANES), F32)] * 2,
        compiler_params=_cparams("parallel"), name="ssm",
    )(u8, kmat, win, wout, lp, d8, h0s)


def _state_to_blocks(re, im):
    n, g, p = re.shape
    nb = g // GROUPS_PER_BLOCK
    f = lambda a: jnp.transpose(a.reshape(n, nb, GROUPS_PER_BLOCK * p), (1, 0, 2))
    return jnp.concatenate([f(re), f(im)], axis=-1)


def _blocks_to_state(h):
    nb, n, w = h.shape
    ns = w // 2
    f = lambda a: jnp.transpose(a, (1, 0, 2)).reshape(n, nb * GROUPS_PER_BLOCK, ns // GROUPS_PER_BLOCK)
    return f(h[..., :ns]), f(h[..., ns:])


def _key_to_float(c):
    return lax.bitcast_convert_type(c ^ ((c >> 31) & 0x7FFFFFFF), F32)


def _kth_largest(count_ge, shape, top_k):
    def body(it, carry):
        u, cnt = carry
        bit = jnp.left_shift(jnp.int32(1), 31 - it)
        cand = u | bit
        c = count_ge(_key_to_float(cand ^ INT_MIN))
        take = c >= top_k
        return jnp.where(take, cand, u), jnp.where(take, c, cnt)
    u, cnt = lax.fori_loop(0, 32, body, (jnp.zeros(shape, jnp.int32), jnp.zeros(shape, jnp.int32)))
    return _key_to_float(u ^ INT_MIN), cnt


TIE_CHUNK = 256


def _tri_ones(n, dtype, lower):
    r = lax.broadcasted_iota(jnp.int32, (n, n), 0)
    c = lax.broadcasted_iota(jnp.int32, (n, n), 1)
    return jnp.where((r >= c) if lower else (r <= c), 1.0, 0.0).astype(dtype)


def _attn_prompt_kernel(qt_ref, kb_ref, vt_ref, qit_ref, kib_ref, wt_ref, o_ref, s_scr, bias_scr,
                        *, seq, top_k, key_chunk):
    qb = pl.program_id(1)
    tq = o_ref.shape[0]

    def run(nk):
        kib = kib_ref[:nk]
        s = jnp.zeros((nk, tq), F32)
        for h in range(IDX_HEADS):
            d = _dot(kib, qit_ref[h * IDX_DIM:(h + 1) * IDX_DIM, :])
            s = s + jnp.maximum(d, 0.0) * wt_ref[IDX_DIM + h:IDX_DIM + h + 1, :]
        kpos = lax.broadcasted_iota(jnp.int32, (nk, tq), 0)
        qpos = qb * tq + lax.broadcasted_iota(jnp.int32, (nk, tq), 1)
        allowed = kpos <= qpos
        s_scr[:nk] = jnp.where(allowed, s, -jnp.inf)

        def count_ge(t):
            n_acc = 8
            acc = [jnp.zeros((SUBLANES, tq), jnp.int32) for _ in range(n_acc)]
            for j in range(nk // SUBLANES):
                rows = s_scr[j * SUBLANES:(j + 1) * SUBLANES]
                acc[j % n_acc] = acc[j % n_acc] + (rows >= t).astype(jnp.int32)
            while len(acc) > 1:
                acc = [a + b for a, b in zip(acc[::2], acc[1::2])]
            return jnp.sum(acc[0], axis=0, keepdims=True)

        thr, cnt = _kth_largest(count_ge, (1, tq), top_k)
        few = qpos < top_k - 1
        q_row = qb * tq + lax.broadcasted_iota(jnp.int32, (1, tq), 1)
        tied = jnp.max(jnp.where((cnt > top_k) & (q_row >= top_k - 1), 1, 0)) > 0

        @pl.when(jnp.logical_not(tied))
        def _():
            bias_scr[:nk] = jnp.where(allowed & ((s_scr[:nk] >= thr) | few), 0.0, -jnp.inf)

        @pl.when(tied)
        def _():
            n_gt = jnp.sum((s_scr[:nk] > thr).astype(jnp.int32), axis=0, keepdims=True)
            need = (top_k - n_gt).astype(F32)
            tri = _tri_ones(TIE_CHUNK, BF16, lower=True)
            carry = jnp.zeros((1, tq), F32)
            for c in range(nk // TIE_CHUNK):
                rs = slice(c * TIE_CHUNK, (c + 1) * TIE_CHUNK)
                sc = s_scr[rs]
                eq = sc == thr
                rank = _dot(tri, jnp.where(eq, 1.0, 0.0).astype(BF16)) + carry
                keep = (sc > thr) | (eq & (rank <= need))
                k_row = c * TIE_CHUNK + lax.broadcasted_iota(jnp.int32, (TIE_CHUNK, tq), 0)
                keep_all = (q_row < top_k - 1) & (k_row <= q_row)
                bias_scr[rs] = jnp.where(keep | keep_all, 0.0, -jnp.inf)
                carry = rank[TIE_CHUNK - 1:, :]

        outs = []
        for g in range(N_KV_HEADS):
            kg = kb_ref[:nk, g * HEAD_DIM:(g + 1) * HEAD_DIM]
            vtg = vt_ref[g * HEAD_DIM:(g + 1) * HEAD_DIM, :nk]
            for r in range(N_REP):
                h = g * N_REP + r
                lg = _dot(kg, qt_ref[h * HEAD_DIM:(h + 1) * HEAD_DIM, :]) + bias_scr[:nk]
                p = jnp.exp(lg - jnp.max(lg, axis=0, keepdims=True))
                den = jnp.sum(p, axis=0, keepdims=True)
                outs.append(_dot(vtg, p.astype(BF16)) / den)
        o_ref[...] = jnp.concatenate(outs, axis=0).T.astype(BF16)

    n_var = seq // key_chunk
    need = (qb * tq + tq + key_chunk - 1) // key_chunk
    for v in range(1, n_var + 1):
        pl.when(need == v)(functools.partial(run, v * key_chunk))


def _attn_prompt(q_t, kb, v_t, qi_t, kib, kiw_t, *, batch, seq, m_out, tq=128, key_chunk=256):
    top_k = min(TOPK_MAX, seq // 4)
    nq = seq // tq
    assert seq % key_chunk == 0 and key_chunk % tq == 0
    aw = N_HEADS * HEAD_DIM
    qcol = lambda n: pl.BlockSpec((n, tq), lambda b, i: (0, b * nq + i))
    return pl.pallas_call(
        functools.partial(_attn_prompt_kernel, seq=seq, top_k=top_k, key_chunk=key_chunk),
        grid=(batch, nq),
        in_specs=[qcol(aw),
                  pl.BlockSpec((seq, N_KV_HEADS * HEAD_DIM), lambda b, i: (b, 0)),
                  pl.BlockSpec((N_KV_HEADS * HEAD_DIM, seq), lambda b, i: (0, b)),
                  qcol(IDX_HEADS * IDX_DIM),
                  pl.BlockSpec((seq, IDX_DIM), lambda b, i: (b, 0)),
                  qcol(LANES)],
        out_specs=pl.BlockSpec((tq, aw), lambda b, i: (b * nq + i, 0)),
        out_shape=jax.ShapeDtypeStruct((m_out, aw), BF16),
        scratch_shapes=[pltpu.VMEM((seq, tq), F32), pltpu.VMEM((seq, tq), F32)],
        compiler_params=_cparams("parallel", "parallel"), name="attn_prompt",
    )(q_t, kb, v_t, qi_t, kib, kiw_t)


def _merge_kernel(za_ref, bm_ref, bt_ref, g_ref, x_ref, gw_ref, gb_ref, wa_ref, wb_ref, wo_ref, o_ref,
                  *, n_main, tm, tail):
    d = x_ref.shape[1]

    def run(rows, b_ref):
        za = jnp.concatenate([za_ref[j, :rows] for j in range(za_ref.shape[0])], axis=1)
        a_out = za * jax.nn.sigmoid(_dot(za.astype(BF16), gw_ref[...]) + gb_ref[...])
        merged = (g_ref[:rows, :d] * _dot(a_out.astype(BF16), wa_ref[...])
                  + g_ref[:rows, d:] * _dot(b_ref[:rows], wb_ref[...]))
        o_ref[:rows] = x_ref[:rows] + _dot(merged.astype(BF16), wo_ref[...])

    i = pl.program_id(0)
    pl.when(i < n_main)(lambda: run(tm, bm_ref))
    pl.when(i == n_main)(lambda: run(tail, bt_ref))


def _merge(za8, b_main, b_tail, gates, x, glu_w, glu_b, wa, wb, wo, *, tm):
    m, d = x.shape
    nb = za8.shape[0]
    n_main_rows, aw = b_main.shape
    tail = b_tail.shape[0]
    n_main = n_main_rows // tm
    assert n_main * tm == n_main_rows and n_main_rows + tail == m and tail <= tm
    resident = lambda shape: pl.BlockSpec(shape, lambda i: (0,) * len(shape), pipeline_mode=pl.Buffered(1))
    return pl.pallas_call(
        functools.partial(_merge_kernel, n_main=n_main, tm=tm, tail=tail), grid=(n_main + 1,),
        in_specs=[pl.BlockSpec((nb, tm, LANES), lambda i: (0, i, 0)),
                  pl.BlockSpec((tm, aw), lambda i: (jnp.minimum(i, n_main - 1), 0)),
                  pl.BlockSpec((tail, aw), lambda i: (0, 0)),
                  _row_spec(tm, 2 * d), _row_spec(tm, d), resident(glu_w.shape), resident((1, glu_w.shape[1])),
                  resident(wa.shape), resident(wb.shape), resident(wo.shape)],
        out_specs=_row_spec(tm, d),
        out_shape=jax.ShapeDtypeStruct((m, d), F32),
        compiler_params=_cparams("arbitrary"), name="merge",
    )(za8, b_main, b_tail, gates, x, glu_w, glu_b.reshape(1, -1), wa, wb, wo)


QPAD = SUBLANES
CHUNK_PAGES = 8


def _page_copy(pt_ref, src_ref, buf_ref, sem, seq, slot, p, page_base, rows):
    start = pl.multiple_of((page_base + pt_ref[seq, p]) * rows, rows)
    return pltpu.make_async_copy(src_ref.at[pl.ds(start, rows), :],
                                 buf_ref.at[slot, pl.ds(p * rows, rows), :], sem)


def _prefetch_pages(n_pages, copies):
    s, n_seq = pl.program_id(0), pl.num_programs(0)
    slot = s % 2

    def start_all(seq, sl):
        def body(p, carry):
            for copy in copies:
                copy(seq, sl, p).start()
            return carry
        lax.fori_loop(0, n_pages, body, 0)

    @pl.when(s == 0)
    def _():
        start_all(0, 0)

    @pl.when(s + 1 < n_seq)
    def _():
        start_all(s + 1, 1 - slot)

    def wait_body(p, carry):
        for copy in copies:
            copy(s, slot, p).wait()
        return carry
    lax.fori_loop(0, n_pages, wait_body, 0)
    return slot


def _sample_scores_kernel(pt_ref, qs_ref, ws_ref, kn_ref, cik_ref, sp_ref, sn_ref, buf_ref, sem_ref,
                          *, page_base, n_pages):
    slot = _prefetch_pages(n_pages, [
        lambda seq, sl, p: _page_copy(pt_ref, cik_ref, buf_ref, sem_ref.at[sl], seq, sl, p, page_base, IDX_DIM)])
    qs, ws = qs_ref[...], ws_ref[:, :1]

    def scores(d):
        r = jnp.maximum(d, 0.0) * ws
        return r.reshape(IDX_HEADS, QPAD, r.shape[1]).sum(axis=0)

    nkc = CHUNK_PAGES * PAGE_SIZE
    for c in range(n_pages // CHUNK_PAGES):
        keys_t = jnp.concatenate(
            [buf_ref[slot, (c * CHUNK_PAGES + kk) * IDX_DIM:(c * CHUNK_PAGES + kk + 1) * IDX_DIM, :]
             for kk in range(CHUNK_PAGES)], axis=1).astype(BF16)
        sp_ref[:, c * nkc:(c + 1) * nkc] = scores(_dot(qs, keys_t))
    sn_ref[...] = scores(_dot_nt(qs, kn_ref[...]))


def _sample_thr_kernel(sp_ref, sn_ref, thr_ref, cut_ref, s_scr, *, past, top_k, t_len):
    s = jnp.concatenate([sp_ref[...], sn_ref[...]], axis=1)
    shape = s.shape
    kpos = lax.broadcasted_iota(jnp.int32, shape, 1)
    qidx = lax.broadcasted_iota(jnp.int32, shape, 0) % QPAD
    s_scr[...] = jnp.where(kpos <= past + qidx, s, -jnp.inf)

    def count_ge(t):
        return jnp.sum((s_scr[...] >= t).astype(jnp.int32), axis=1, keepdims=True)

    thr, cnt = _kth_largest(count_ge, (shape[0], 1), top_k)
    thr_ref[...] = jnp.broadcast_to(thr, thr_ref.shape)
    real_row = lax.broadcasted_iota(jnp.int32, (shape[0], 1), 0) % QPAD < t_len
    tied = jnp.max(jnp.where((cnt > top_k) & real_row, 1, 0)) > 0

    @pl.when(jnp.logical_not(tied))
    def _():
        cut_ref[...] = jnp.full(cut_ref.shape, shape[1], jnp.int32)

    @pl.when(tied)
    def _():
        n_gt = jnp.sum((s_scr[...] > thr).astype(jnp.int32), axis=1, keepdims=True)
        need = (top_k - n_gt).astype(F32)
        tri = _tri_ones(LANES, BF16, lower=False)
        carry = jnp.zeros((shape[0], 1), F32)
        cut = jnp.full((shape[0], 1), -1, jnp.int32)
        lane = lax.broadcasted_iota(jnp.int32, (shape[0], LANES), 1)
        for c in range(shape[1] // LANES):
            eq = s_scr[:, c * LANES:(c + 1) * LANES] == thr
            rank = _dot(jnp.where(eq, 1.0, 0.0).astype(BF16), tri) + carry
            last = jnp.max(jnp.where(eq & (rank <= need), lane + c * LANES, -1), axis=1, keepdims=True)
            cut = jnp.maximum(cut, last)
            carry = rank[:, LANES - 1:]
        cut_ref[...] = jnp.broadcast_to(cut, cut_ref.shape)


def _sample_attn_kernel(pt_ref, q_ref, sp_ref, sn_ref, thr_ref, cut_ref, kn_ref, vn_ref, ck_ref, cv_ref, o_ref,
                        kbuf, vbuf, sem_ref, lg_scr, *, page_base, n_pages):
    page_rows = N_KV_HEADS * PAGE_SIZE
    slot = _prefetch_pages(n_pages, [
        lambda seq, sl, p: _page_copy(pt_ref, ck_ref, kbuf, sem_ref.at[0, sl], seq, sl, p, page_base, page_rows),
        lambda seq, sl, p: _page_copy(pt_ref, cv_ref, vbuf, sem_ref.at[1, sl], seq, sl, p, page_base, page_rows)])
    rows_g = N_REP * QPAD
    nkc = CHUNK_PAGES * PAGE_SIZE
    n_chunks = n_pages // CHUNK_PAGES
    past = n_pages * PAGE_SIZE
    thr, cut = thr_ref[:, :1], cut_ref[:, :1]
    q = q_ref[...]
    qg = [q[g * rows_g:(g + 1) * rows_g] for g in range(N_KV_HEADS)]
    head_rows = lambda c, g: pl.ds(c * CHUNK_PAGES * page_rows + g, nkc, stride=N_KV_HEADS)

    def bias(scores, pos0, allowed=None):
        kpos = pos0 + lax.broadcasted_iota(jnp.int32, scores.shape, 1)
        sel = (scores > thr) | ((scores == thr) & (kpos <= cut))
        if allowed is not None:
            sel = sel & allowed
        return jnp.concatenate([jnp.where(sel, 0.0, -jnp.inf)] * N_HEADS, axis=0)

    for c in range(n_chunks):
        lg = jnp.concatenate([_dot_nt(qg[g], kbuf.at[slot][head_rows(c, g), :].astype(BF16))
                              for g in range(N_KV_HEADS)], axis=0)
        lg_scr[:, c * nkc:(c + 1) * nkc] = lg + bias(sp_ref[:, c * nkc:(c + 1) * nkc], c * nkc)
    shape = sn_ref.shape
    allowed = lax.broadcasted_iota(jnp.int32, shape, 1) <= lax.broadcasted_iota(jnp.int32, shape, 0)
    lg = jnp.concatenate([_dot_nt(qg[g], kn_ref[:, g * HEAD_DIM:(g + 1) * HEAD_DIM])
                          for g in range(N_KV_HEADS)], axis=0)
    lg_scr[:, past:] = lg + bias(sn_ref[...], past, allowed)

    lg = lg_scr[...]
    p = jnp.exp(lg - jnp.max(lg, axis=1, keepdims=True))
    den = jnp.sum(p, axis=1, keepdims=True)
    pb = p.astype(BF16)
    acc = jnp.concatenate([_dot(pb[g * rows_g:(g + 1) * rows_g, past:], vn_ref[:, g * HEAD_DIM:(g + 1) * HEAD_DIM])
                           for g in range(N_KV_HEADS)], axis=0)
    for c in range(n_chunks):
        acc = acc + jnp.concatenate(
            [_dot(pb[g * rows_g:(g + 1) * rows_g, c * nkc:(c + 1) * nkc],
                  vbuf.at[slot][head_rows(c, g), :].astype(BF16)) for g in range(N_KV_HEADS)], axis=0)
    o_ref[...] = acc / den


def _pad_queries(x, t_len):
    n, h, d = x.shape
    x = jnp.transpose(x.reshape(n // t_len, t_len, h, d), (0, 2, 1, 3))
    x = jnp.pad(x, ((0, 0), (0, 0), (0, QPAD - t_len), (0, 0)))
    return x.reshape(n // t_len, h * QPAD, d)


def _pad_new_keys(x, t_len):
    n, w = x.shape
    return jnp.pad(x.reshape(n // t_len, t_len, w), ((0, 0), (0, PAGE_SIZE - t_len), (0, 0)))


def _attn_sample(q_s, k_new, v_new, qi_s, wi_s, ki_new, cache_k, cache_v, cache_idx_k, page_table, page_base,
                 *, t_len):
    db, n_pages = page_table.shape
    past = n_pages * PAGE_SIZE
    top_k = min(TOPK_MAX, (past + t_len) // 4)
    kvw = N_KV_HEADS * HEAD_DIM
    sds = jax.ShapeDtypeStruct
    q = _pad_queries(q_s.reshape(-1, N_HEADS, HEAD_DIM), t_len)
    qi = _pad_queries(qi_s.reshape(-1, IDX_HEADS, IDX_DIM), t_len)
    ws = _pad_queries(wi_s.reshape(-1, IDX_HEADS, 1), t_len)
    ws = jnp.broadcast_to(ws, ws.shape[:2] + (PAGE_SIZE,))
    kn, vn, kin = (_pad_new_keys(a, t_len) for a in (k_new, v_new, ki_new))

    assert n_pages % CHUNK_PAGES == 0
    seq_spec = lambda r, c: pl.BlockSpec((None, r, c), lambda s, pt: (s, 0, 0))
    hbm = pl.BlockSpec(memory_space=pl.ANY)
    dma = pltpu.SemaphoreType.DMA
    in_order = _cparams("arbitrary")
    sp, sn = pl.pallas_call(
        functools.partial(_sample_scores_kernel, page_base=page_base, n_pages=n_pages),
        grid_spec=pltpu.PrefetchScalarGridSpec(
            num_scalar_prefetch=1, grid=(db,),
            in_specs=[seq_spec(IDX_HEADS * QPAD, IDX_DIM), seq_spec(IDX_HEADS * QPAD, PAGE_SIZE),
                      seq_spec(PAGE_SIZE, IDX_DIM), hbm],
            out_specs=[seq_spec(QPAD, past), seq_spec(QPAD, PAGE_SIZE)],
            scratch_shapes=[pltpu.VMEM((2, n_pages * IDX_DIM, PAGE_SIZE), F32), dma((2,))]),
        out_shape=[sds((db, QPAD, past), F32), sds((db, QPAD, PAGE_SIZE), F32)],
        compiler_params=in_order, name="sample_scores",
    )(page_table, qi, ws, kin, cache_idx_k.reshape(-1, PAGE_SIZE))

    rows = db * QPAD
    rblk = min(rows, 8 * QPAD)
    assert rows % rblk == 0
    thr, cut = pl.pallas_call(
        functools.partial(_sample_thr_kernel, past=past, top_k=top_k, t_len=t_len),
        grid=(rows // rblk,),
        in_specs=[_row_spec(rblk, past), _row_spec(rblk, PAGE_SIZE)],
        out_specs=[_row_spec(rblk, LANES), _row_spec(rblk, LANES)],
        out_shape=[sds((rows, LANES), F32), sds((rows, LANES), jnp.int32)],
        scratch_shapes=[pltpu.VMEM((rblk, past + PAGE_SIZE), F32)],
        compiler_params=_cparams("parallel"), name="sample_threshold",
    )(sp.reshape(rows, past), sn.reshape(rows, PAGE_SIZE))

    hq = N_HEADS * QPAD
    page_rows = N_KV_HEADS * PAGE_SIZE
    out = pl.pallas_call(
        functools.partial(_sample_attn_kernel, page_base=page_base, n_pages=n_pages),
        grid_spec=pltpu.PrefetchScalarGridSpec(
            num_scalar_prefetch=1, grid=(db,),
            in_specs=[seq_spec(hq, HEAD_DIM), seq_spec(QPAD, past), seq_spec(QPAD, PAGE_SIZE), seq_spec(QPAD, LANES),
                      seq_spec(QPAD, LANES), seq_spec(PAGE_SIZE, kvw), seq_spec(PAGE_SIZE, kvw), hbm, hbm],
            out_specs=seq_spec(hq, HEAD_DIM),
            scratch_shapes=[pltpu.VMEM((2, n_pages * page_rows, HEAD_DIM), F32),
                            pltpu.VMEM((2, n_pages * page_rows, HEAD_DIM), F32),
                            dma((2, 2)), pltpu.VMEM((hq, past + PAGE_SIZE), F32)]),
        out_shape=sds((db, hq, HEAD_DIM), F32),
        compiler_params=in_order, name="sample_attn",
    )(page_table, q, sp, sn, thr.reshape(db, QPAD, LANES), cut.reshape(db, QPAD, LANES), kn, vn, cache_k, cache_v)
    out = out.reshape(db, N_HEADS, QPAD, HEAD_DIM)[:, :, :t_len]
    return jnp.transpose(out, (0, 2, 1, 3)).reshape(db * t_len, N_HEADS * HEAD_DIM)


ROW_TILE = 640
PROJ_ROW_TILE = 512
FFN_ROW_TILE = 512
FF_TILE = 512
MERGE_ROW_TILE = 256


def _layer(x_p, x_s, pos, dims, layer, w_in_all, ck, cv, cik, page_base, s_re, s_im, page_table, p):
    b, s, db, t = dims
    mp, ms = b * s, db * t
    d = x_p.shape[1]
    bf = lambda w: w.astype(BF16)
    ssm_w = d // 2
    w_g = bf(w_in_all[layer, :, w_in_all.shape[2] - 2 * d:])

    x1, xn = _ffn(x_p, x_s, p["ffn1_norm"], bf(p["ffn1_w_gate"]), bf(p["ffn1_w_up"]), bf(p["ffn1_w_down"]),
                  p["mix_norm"], n_main_rows=mp, tail=ms, split_out=False, tm=FFN_ROW_TILE, tf=FF_TILE)
    pr = _projections(xn, jnp.swapaxes(w_in_all, 1, 2), layer, w_g, p["q_norm"], p["k_norm"], pos,
                      n_main_rows=mp, seq=s, tail=ms, tm=PROJ_ROW_TILE, tm_gates=ROW_TILE)

    kmat, win, wout, lp = _ssm_prep(p["ssm_lambda_re"], p["ssm_lambda_im"], p["ssm_b_re"], p["ssm_b_im"],
                                    p["ssm_c_re"], p["ssm_c_im"], p["ssm_log_dt"])
    nblk = ssm_w // LANES
    d8 = p["ssm_d"].reshape(nblk, 1, LANES)
    za8, hl_p, hl_s = _ssm(pr["u8"], kmat, win, wout, lp, d8, _state_to_blocks(s_re, s_im),
                           nseq=b, seq=s, n_s=db, t_s=t)

    b_p = _attn_prompt(pr["q_t"], pr["kb"], pr["v_t"], pr["qi_t"], pr["kib"], pr["kiw_t"],
                       batch=b, seq=s, m_out=mp)
    b_s = _attn_sample(pr["q"][mp:], pr["kb"][mp:], bf(pr["v_s"].reshape(ms, -1)), pr["qi"][mp:],
                       pr["kiw"][mp:, IDX_DIM:IDX_DIM + IDX_HEADS], pr["kib"][mp:],
                       ck, cv, cik, page_table, page_base, t_len=t)

    x2 = _merge(za8, b_p, bf(b_s), pr["gates"], x1, bf(p["glu_w"]), p["glu_b"], bf(p["w_branch_a"]),
                bf(p["w_branch_b"]), bf(p["w_out"]), tm=MERGE_ROW_TILE)
    y_p, y_s = _ffn(x2, None, p["ffn2_norm"], bf(p["ffn2_w_gate"]), bf(p["ffn2_w_up"]), bf(p["ffn2_w_down"]),
                    n_main_rows=mp, tail=ms, split_out=True, tm=FFN_ROW_TILE, tf=FF_TILE)

    hp_re, hp_im = _blocks_to_state(hl_p)
    hs_re, hs_im = _blocks_to_state(hl_s)
    ki = pr["kiw"][:, :IDX_DIM]
    kvs = (N_KV_HEADS, HEAD_DIM)
    rows = (pr["k_p"].reshape(b, s, *kvs), pr["v_p"].reshape(b, s, *kvs), ki[:mp].reshape(b, s, IDX_DIM),
            hp_re, hp_im,
            pr["k_s"].reshape(db, t, *kvs), pr["v_s"].reshape(db, t, *kvs), ki[mp:].reshape(db, t, IDX_DIM),
            hs_re, hs_im)
    return y_p, y_s, rows


def kernel(x_prompt, x_sample, cache_k, cache_v, cache_idx_k, state_ssm_re, state_ssm_im, page_table,
           ffn1_norm, ffn1_w_gate, ffn1_w_up, ffn1_w_down, mix_norm, w_in, q_norm, k_norm,
           ssm_lambda_re, ssm_lambda_im, ssm_b_re, ssm_b_im, ssm_c_re, ssm_c_im, ssm_d, ssm_log_dt,
           glu_w, glu_b, w_branch_a, w_branch_b, w_out, ffn2_norm, ffn2_w_gate, ffn2_w_up, ffn2_w_down):
    b, s, d = x_prompt.shape
    db, t, _ = x_sample.shape
    depth, n_phys = cache_k.shape[:2]
    past = page_table.shape[1] * PAGE_SIZE
    x_p, x_s = x_prompt.reshape(b * s, d), x_sample.reshape(db * t, d)
    pos = jnp.concatenate([jnp.arange(s), jnp.tile(past + jnp.arange(t), db)])
    ck = cache_k.reshape(-1, HEAD_DIM)
    cv = cache_v.reshape(-1, HEAD_DIM)
    cik = jnp.swapaxes(cache_idx_k, 2, 3).reshape(depth * n_phys, IDX_DIM, PAGE_SIZE)
    params = dict(
        ffn1_norm=ffn1_norm, ffn1_w_gate=ffn1_w_gate, ffn1_w_up=ffn1_w_up, ffn1_w_down=ffn1_w_down,
        mix_norm=mix_norm, w_in=w_in, q_norm=q_norm, k_norm=k_norm,
        ssm_lambda_re=ssm_lambda_re, ssm_lambda_im=ssm_lambda_im, ssm_b_re=ssm_b_re, ssm_b_im=ssm_b_im,
        ssm_c_re=ssm_c_re, ssm_c_im=ssm_c_im, ssm_d=ssm_d, ssm_log_dt=ssm_log_dt, glu_w=glu_w, glu_b=glu_b,
        w_branch_a=w_branch_a, w_branch_b=w_branch_b, w_out=w_out,
        ffn2_norm=ffn2_norm, ffn2_w_gate=ffn2_w_gate, ffn2_w_up=ffn2_w_up, ffn2_w_down=ffn2_w_down)
    new = [[] for _ in range(10)]
    for l in range(depth):
        p = {name: w[l] for name, w in params.items()}
        x_p, x_s, rows = _layer(x_p, x_s, pos, (b, s, db, t), l, w_in, ck, cv, cik, l * n_phys,
                                state_ssm_re[l], state_ssm_im[l], page_table, p)
        for lst, r in zip(new, rows):
            lst.append(r)
    return (x_p.reshape(b, s, d), x_s.reshape(db, t, d)) + tuple(jnp.stack(lst) for lst in new)
```

```python
import functools
import math

import jax
import jax.numpy as jnp
from jax import lax
from jax.experimental import pallas as pl
from jax.experimental.pallas import tpu as pltpu

F32 = jnp.float32
BF16 = jnp.bfloat16

SSM_GROUP = 16
SSM_STATE = 64
N_HEADS = 8
HEAD_DIM = 128
N_KV_HEADS = 2
N_REP = N_HEADS // N_KV_HEADS
ROT_DIM = HEAD_DIM // 4
ROPE_THETA = 500000.0
IDX_HEADS = 16
IDX_DIM = 64
IDX_ROT_DIM = IDX_DIM // 4
TOPK_MAX = 256
PAGE_SIZE = 128
FFN_RES = 0.5
EPS = 1e-6

LANES = 128
SUBLANES = 8
VMEM_LIMIT_BYTES = 56 * 1024 * 1024

GROUPS_PER_BLOCK = LANES // SSM_GROUP
STATES_PER_BLOCK = GROUPS_PER_BLOCK * SSM_STATE
SSM_T = 8

INT_MIN = -(2 ** 31)


def _cparams(*sem):
    return pltpu.CompilerParams(dimension_semantics=sem, vmem_limit_bytes=VMEM_LIMIT_BYTES)


def _rms(x, g):
    return x * lax.rsqrt(jnp.mean(x * x, axis=-1, keepdims=True) + EPS) * g


def _dot(a, b):
    return jnp.dot(a, b, preferred_element_type=F32)


def _dot_nt(a, b):
    return lax.dot_general(a, b, (((1,), (1,)), ((), ())), preferred_element_type=F32)


def _ffn_kernel(*refs, nf, n_main, tm, tail, two_src, split_out, with_next_norm):
    refs = list(refs)
    x_main_ref = refs.pop(0)
    x_tail_ref = refs.pop(0) if two_src else x_main_ref
    g_ref, wg_ref, wu_ref, wd_ref = (refs.pop(0) for _ in range(4))
    g2_ref = refs.pop(0) if with_next_norm else None
    y_main_ref = refs.pop(0)
    y_tail_ref = refs.pop(0) if split_out else y_main_ref
    n2_ref = refs.pop(0) if with_next_norm else None
    xn_ref, acc_ref = refs
    i, f = pl.program_id(0), pl.program_id(1)

    def run(rows, x_ref, y_ref):
        @pl.when(f == 0)
        def _():
            xn_ref[:rows] = _rms(x_ref[:rows], g_ref[...]).astype(BF16)
            acc_ref[:rows] = jnp.zeros((rows, acc_ref.shape[1]), F32)

        xn = xn_ref[:rows]
        a = _dot(xn, wg_ref[...])
        b = _dot(xn, wu_ref[...])
        h = (a * jax.nn.sigmoid(a) * b).astype(BF16)
        acc_ref[:rows] += _dot(h, wd_ref[...])

        @pl.when(f == nf - 1)
        def _():
            y = x_ref[:rows] + FFN_RES * acc_ref[:rows]
            y_ref[:rows] = y
            if with_next_norm:
                n2_ref[:rows] = _rms(y, g2_ref[...]).astype(BF16)

    pl.when(i < n_main)(lambda: run(tm, x_main_ref, y_main_ref))
    pl.when(i == n_main)(lambda: run(tail, x_tail_ref, y_tail_ref))


def _ffn(x_main, x_tail, g, wg, wu, wd, g2=None, *, n_main_rows, tail, split_out, tm, tf):
    d = x_main.shape[1]
    nf = wg.shape[1] // tf
    n_main = n_main_rows // tm
    m = n_main_rows + tail
    two_src = x_tail is not None
    with_next = g2 is not None
    clamp = lambda i, f: (jnp.minimum(i, n_main - 1), 0)
    rows = lambda i, f: (i, 0)
    first = lambda i, f: (0, 0)
    sds = jax.ShapeDtypeStruct
    in_specs = [pl.BlockSpec((tm, d), clamp if two_src else rows)]
    args = [x_main]
    if two_src:
        in_specs.append(pl.BlockSpec((tail, d), first))
        args.append(x_tail)
    in_specs += [pl.BlockSpec((1, d), first), pl.BlockSpec((d, tf), lambda i, f: (0, f)),
                 pl.BlockSpec((d, tf), lambda i, f: (0, f)), pl.BlockSpec((tf, d), lambda i, f: (f, 0))]
    args += [g.reshape(1, d), wg, wu, wd]
    if with_next:
        in_specs.append(pl.BlockSpec((1, d), first))
        args.append(g2.reshape(1, d))
    if split_out:
        out_shape = [sds((n_main_rows, d), F32), sds((tail, d), F32)]
        out_specs = [pl.BlockSpec((tm, d), clamp), pl.BlockSpec((tail, d), first)]
    else:
        out_shape = [sds((m, d), F32)]
        out_specs = [pl.BlockSpec((tm, d), rows)]
    if with_next:
        out_shape.append(sds((m, d), BF16))
        out_specs.append(pl.BlockSpec((tm, d), rows))
    return pl.pallas_call(
        functools.partial(_ffn_kernel, nf=nf, n_main=n_main, tm=tm, tail=tail, two_src=two_src,
                          split_out=split_out, with_next_norm=with_next),
        grid=(n_main + 1, nf),
        in_specs=in_specs,
        out_specs=out_specs,
        out_shape=out_shape,
        scratch_shapes=[pltpu.VMEM((tm, d), BF16), pltpu.VMEM((tm, d), F32)],
        compiler_params=_cparams("arbitrary", "arbitrary"),
        name="ffn" + ("_norm" if with_next else ""),
    )(*args)


def _rope_tables(pos, rot_dim, width, tile):
    half = rot_dim // 2
    m = pos.shape[0]
    freqs = ROPE_THETA ** (-jnp.arange(half, dtype=F32) * 2.0 / rot_dim)
    ang = pos.astype(F32)[:, None] * freqs[None, :]
    cos, sin = jnp.cos(ang), jnp.sin(ang)
    zh = jnp.zeros((m, half), F32)
    rest = width - rot_dim
    c = jnp.concatenate([cos, cos, jnp.ones((m, rest), F32)], axis=1)
    s1 = jnp.concatenate([-sin, zh, jnp.zeros((m, rest), F32)], axis=1)
    s2 = jnp.concatenate([zh, sin, jnp.zeros((m, rest), F32)], axis=1)
    if tile:
        reps = LANES // width
        return tuple(jnp.tile(t, (1, reps)) for t in (c, s1, s2))
    pad = LANES - width
    return (jnp.pad(c, ((0, 0), (0, pad)), constant_values=1.0),
            jnp.pad(s1, ((0, 0), (0, pad))), jnp.pad(s2, ((0, 0), (0, pad))))


def _rope(x, c, s1, s2, half):
    return x * c + pltpu.roll(x, LANES - half, 1) * s1 + pltpu.roll(x, half, 1) * s2


def _proj_rows(xn_ref, w_refs, wbf_ref, n_main, tm, tail, chunk, epilogue):
    i = pl.program_id(0)

    @pl.when(i == 0)
    def _():
        off = 0
        for w_ref in w_refs:
            n = w_ref.shape[0]
            wbf_ref[:, off:off + n] = w_ref[...].astype(F32).T.astype(BF16)
            off += n

    def run(rows):
        for r0 in range(0, rows, chunk):
            rs = slice(r0, min(r0 + chunk, rows))
            epilogue(_dot(xn_ref[rs, :], wbf_ref[...]), rs, rows == tail)

    pl.when(i < n_main)(lambda: run(tm))
    pl.when(i == n_main)(lambda: run(tail))


def _u_proj_kernel(xn_ref, w_ref, u_ref, wbf_ref, **kw):
    def epilogue(z, rs, is_tail):
        for j in range(z.shape[1] // LANES):
            u_ref[j, rs, :] = z[:, j * LANES:(j + 1) * LANES]
    _proj_rows(xn_ref, [w_ref], wbf_ref, epilogue=epilogue, **kw)


def _q_proj_kernel(xn_ref, w_ref, g_ref, c_ref, s1_ref, s2_ref, qn_ref, qt_ref, wbf_ref, **kw):
    def epilogue(z, rs, is_tail):
        c, s1, s2, g = c_ref[rs, :], s1_ref[rs, :], s2_ref[rs, :], g_ref[...]
        heads = []
        for h in range(N_HEADS):
            x = _rms(z[:, h * HEAD_DIM:(h + 1) * HEAD_DIM], g)
            heads.append(_rope(x, c, s1, s2, ROT_DIM // 2) * (HEAD_DIM ** -0.5))
        q = jnp.concatenate(heads, axis=1)
        qn_ref[rs, :] = q.astype(BF16)
        qt_ref[:, rs] = q.T.astype(BF16)
    _proj_rows(xn_ref, [w_ref], wbf_ref, epilogue=epilogue, **kw)


def _kv_proj_kernel(xn_ref, wkv_ref, wkw_ref, g_ref, c_ref, s1_ref, s2_ref, ci_ref, si1_ref, si2_ref, scale_ref,
                    kp_ref, ks_ref, vp_ref, vs_ref, kb_ref, vt_ref, kiw_ref, kiwt_ref, kib_ref, wbf_ref, **kw):
    kvw = N_KV_HEADS * HEAD_DIM

    def epilogue(z, rs, is_tail):
        c, s1, s2, g = c_ref[rs, :], s1_ref[rs, :], s2_ref[rs, :], g_ref[...]
        k_ref, v_ref = (ks_ref, vs_ref) if is_tail else (kp_ref, vp_ref)
        n = rs.stop - rs.start
        heads = []
        for h in range(N_KV_HEADS):
            x = _rope(_rms(z[:, h * HEAD_DIM:(h + 1) * HEAD_DIM], g), c, s1, s2, ROT_DIM // 2)
            rows_h = pl.ds(N_KV_HEADS * rs.start + h, n, stride=N_KV_HEADS)
            k_ref[rows_h, :] = x
            v_ref[rows_h, :] = z[:, kvw + h * HEAD_DIM:kvw + (h + 1) * HEAD_DIM]
            heads.append(x)
        kb_ref[rs, :] = jnp.concatenate(heads, axis=1).astype(BF16)
        vt_ref[:, rs] = z[:, kvw:2 * kvw].T.astype(BF16)
        y = _rope(z[:, 2 * kvw:], ci_ref[rs, :], si1_ref[rs, :], si2_ref[rs, :], IDX_ROT_DIM // 2) * scale_ref[...]
        kiw_ref[rs, :] = y
        kiwt_ref[:, rs] = y.T
        kib_ref[rs, :] = y[:, :IDX_DIM].astype(BF16)
    _proj_rows(xn_ref, [wkv_ref, wkw_ref], wbf_ref, epilogue=epilogue, **kw)


def _qi_proj_kernel(xn_ref, wa_ref, wb_ref, c_ref, s1_ref, s2_ref, qn_ref, qt_ref, wbf_ref, **kw):
    def epilogue(z, rs, is_tail):
        c, s1, s2 = c_ref[rs, :], s1_ref[rs, :], s2_ref[rs, :]
        cols = []
        for j in range(z.shape[1] // LANES):
            x = z[:, j * LANES:(j + 1) * LANES]
            cols.append(_rope(x, c, s1, s2, IDX_ROT_DIM // 2) * (IDX_DIM ** -0.5))
        q = jnp.concatenate(cols, axis=1)
        qn_ref[rs, :] = q.astype(BF16)
        qt_ref[:, rs] = q.T.astype(BF16)
    _proj_rows(xn_ref, [wa_ref, wb_ref], wbf_ref, epilogue=epilogue, **kw)


def _gate_proj_kernel(xn_ref, w_ref, o_ref, *, chunk):
    for r0 in range(0, xn_ref.shape[0], chunk):
        rs = slice(r0, r0 + chunk)
        o_ref[rs, :] = jax.nn.sigmoid(_dot(xn_ref[rs, :], w_ref[...])).astype(BF16)


def _row_spec(tm, n):
    return pl.BlockSpec((tm, n), lambda i: (i, 0))


def _full_spec(shape):
    return pl.BlockSpec(shape, lambda i: (0,) * len(shape))


PROJ_CHUNK = 256


def _projections(xn, w_in_t, layer, w_g, q_norm, k_norm, pos, *, n_main_rows, seq, tail, tm, tm_gates):
    m, d = xn.shape
    n_main = n_main_rows // tm
    assert n_main * tm == n_main_rows and n_main_rows + tail == m and tail <= tm
    grid = (n_main + 1,)
    ssm_w, attn_w, kv_w, idx_w = d // 2, N_HEADS * HEAD_DIM, N_KV_HEADS * HEAD_DIM, IDX_HEADS * IDX_DIM
    hd_tabs = _rope_tables(pos, ROT_DIM, HEAD_DIM, True)
    ix_tabs = _rope_tables(pos, IDX_ROT_DIM, IDX_DIM, True)
    kw_tabs = _rope_tables(pos, IDX_ROT_DIM, IDX_DIM, False)
    assert seq % tm == 0 and pos.shape[0] == seq + tail
    tabs_per_seq = seq // tm
    tab_specs = [pl.BlockSpec((tm, LANES), lambda i: (jnp.where(i < n_main, i % tabs_per_seq, tabs_per_seq), 0))] * 3
    xs = _row_spec(tm, d)
    sds = jax.ShapeDtypeStruct
    seq = _cparams("arbitrary")
    kw = dict(n_main=n_main, tm=tm, tail=tail, chunk=PROJ_CHUNK)

    def wcols(start, width):
        assert start % width == 0
        return pl.BlockSpec((None, width, d), lambda i: (layer, start // width, 0), pipeline_mode=pl.Buffered(1))

    def call(kernel, name, w_specs, n_w, extra_specs, extra_args, out_specs, out_shape):
        return pl.pallas_call(
            functools.partial(kernel, **kw), grid=grid,
            in_specs=[xs] + w_specs + extra_specs, out_specs=out_specs, out_shape=out_shape,
            scratch_shapes=[pltpu.VMEM((d, n_w), BF16)], compiler_params=seq, name=name,
        )(xn, *([w_in_t] * len(w_specs)), *extra_args)

    tcol = lambda n: pl.BlockSpec((n, tm), lambda i: (0, i))
    norm_spec = _full_spec((1, HEAD_DIM))
    u8 = call(_u_proj_kernel, "proj_u", [wcols(0, ssm_w)], ssm_w, [], [],
              pl.BlockSpec((ssm_w // LANES, tm, LANES), lambda i: (0, i, 0)), sds((ssm_w // LANES, m, LANES), F32))
    q_nat, q_t = call(_q_proj_kernel, "proj_q", [wcols(ssm_w, attn_w)], attn_w,
                      [norm_spec] + tab_specs, [q_norm.reshape(1, HEAD_DIM), *hd_tabs],
                      [_row_spec(tm, attn_w), tcol(attn_w)], [sds((m, attn_w), BF16), sds((attn_w, m), BF16)])
    off_kv = ssm_w + attn_w
    off_qi = off_kv + 2 * kv_w
    off_kw = off_qi + idx_w
    lane = jnp.arange(LANES)
    kw_scale = jnp.where(lane < IDX_DIM, 1.0, jnp.where(lane < IDX_DIM + IDX_HEADS, IDX_HEADS ** -0.5, 0.0))
    nh = N_KV_HEADS
    head_main = pl.BlockSpec((nh * tm, HEAD_DIM), lambda i: (jnp.minimum(i, n_main - 1), 0))
    head_tail = pl.BlockSpec((nh * tail, HEAD_DIM), lambda i: (0, 0))
    k_p, k_s, v_p, v_s, kb, v_t, kiw, kiw_t, kib = call(
        _kv_proj_kernel, "proj_kv", [wcols(off_kv, 2 * kv_w), wcols(off_kw, LANES)], 2 * kv_w + LANES,
        [norm_spec] + tab_specs + tab_specs + [_full_spec((1, LANES))],
        [k_norm.reshape(1, HEAD_DIM), *hd_tabs, *kw_tabs, kw_scale.astype(F32).reshape(1, LANES)],
        [head_main, head_tail, head_main, head_tail, _row_spec(tm, kv_w), tcol(kv_w),
         _row_spec(tm, LANES), tcol(LANES), _row_spec(tm, IDX_DIM)],
        [sds((nh * n_main_rows, HEAD_DIM), F32), sds((nh * tail, HEAD_DIM), F32),
         sds((nh * n_main_rows, HEAD_DIM), F32), sds((nh * tail, HEAD_DIM), F32),
         sds((m, kv_w), BF16), sds((kv_w, m), BF16),
         sds((m, LANES), F32), sds((LANES, m), F32), sds((m, IDX_DIM), BF16)])
    half = idx_w // 2
    qi_nat, qi_t = call(_qi_proj_kernel, "proj_qi", [wcols(off_qi, half), wcols(off_qi + half, half)], idx_w,
                        tab_specs, ix_tabs,
                        [_row_spec(tm, idx_w), tcol(idx_w)], [sds((m, idx_w), BF16), sds((idx_w, m), BF16)])

    ng = w_g.shape[1]
    tn = 1024
    gates = pl.pallas_call(
        functools.partial(_gate_proj_kernel, chunk=tm_gates // 2), grid=(m // tm_gates, ng // tn),
        in_specs=[pl.BlockSpec((tm_gates, d), lambda i, j: (i, 0)), pl.BlockSpec((d, tn), lambda i, j: (0, j))],
        out_specs=pl.BlockSpec((tm_gates, tn), lambda i, j: (i, j)),
        out_shape=sds((m, ng), BF16), compiler_params=_cparams("parallel", "parallel"), name="proj_gates",
    )(xn, w_g)
    return dict(u8=u8, q=q_nat, q_t=q_t, k_p=k_p, k_s=k_s, kb=kb, v_p=v_p, v_s=v_s, v_t=v_t, qi=qi_nat, qi_t=qi_t,
                kiw=kiw, kiw_t=kiw_t, kib=kib, gates=gates)


def _ssm_prep_kernel(lre_ref, lim_ref, ldt_ref, bre_ref, bim_ref, cre_ref, cim_ref,
                     k_ref, win_ref, wout_ref, lp_ref):
    ns = STATES_PER_BLOCK
    lre, lim = lre_ref[...], lim_ref[...]
    dt = jnp.exp(ldt_ref[...])
    a, th = lre * dt, lim * dt

    def power(l):
        mag = jnp.exp(a * float(l))
        return mag * jnp.cos(th * float(l)), mag * jnp.sin(th * float(l))

    pw = [power(l) for l in range(SSM_T + 1)]
    xr, xi = pw[1][0] - 1.0, pw[1][1]
    den = lre * lre + lim * lim
    cr, ci = (xr * lre + xi * lim) / den, (xi * lre - xr * lim) / den
    bre, bim = bre_ref[...], bim_ref[...]
    bbr, bbi = bre * cr - bim * ci, bre * ci + bim * cr
    cre, cim = cre_ref[...], cim_ref[...]
    hi = lax.Precision.HIGHEST
    nt = (((1,), (1,)), ((), ()))
    lag = []
    for l in range(SSM_T):
        pr, pi = pw[l]
        blr, bli = bbr * pr - bbi * pi, bbr * pi + bbi * pr
        m = (lax.dot_general(blr, cre, nt, precision=hi, preferred_element_type=F32)
             - lax.dot_general(bli, cim, nt, precision=hi, preferred_element_type=F32))
        lag.append(m.astype(BF16))
        t = SSM_T - 1 - l
        win_ref[t * LANES:(t + 1) * LANES, :ns] = blr.astype(BF16)
        win_ref[t * LANES:(t + 1) * LANES, ns:] = bli.astype(BF16)
    zero = jnp.zeros((LANES, LANES), BF16)
    for t in range(SSM_T):
        pr, pi = pw[t + 1]
        wout_ref[t * LANES:(t + 1) * LANES, :ns] = (cre * pr - cim * pi).astype(BF16)
        wout_ref[t * LANES:(t + 1) * LANES, ns:] = (-(cre * pi + cim * pr)).astype(BF16)
        lp_ref[t:t + 1, :ns] = pr
        lp_ref[t:t + 1, ns:] = pi
        for t2 in range(SSM_T):
            k_ref[t * LANES:(t + 1) * LANES, t2 * LANES:(t2 + 1) * LANES] = lag[t2 - t] if t2 >= t else zero


def _block_diag_groups(w):
    g, h, p = w.shape
    nb = g // GROUPS_PER_BLOCK
    w = w.reshape(nb, GROUPS_PER_BLOCK, h, p)
    eye = jnp.eye(GROUPS_PER_BLOCK, dtype=w.dtype)
    out = w[:, :, :, None, :] * eye[None, :, None, :, None]
    return out.reshape(nb, GROUPS_PER_BLOCK * h, GROUPS_PER_BLOCK * p)


def _ssm_prep(lam_re, lam_im, b_re, b_im, c_re, c_im, log_dt):
    g, p = lam_re.shape
    nb = g // GROUPS_PER_BLOCK
    ns = STATES_PER_BLOCK
    tl = SSM_T * LANES
    vec = lambda a: a.reshape(nb, 1, ns)
    ldt = vec(jnp.broadcast_to(log_dt[:, None], (g, p)))
    bt = lambda b: _block_diag_groups(jnp.swapaxes(b, 1, 2))
    vspec = pl.BlockSpec((None, 1, ns), lambda j: (j, 0, 0))
    mspec = pl.BlockSpec((None, LANES, ns), lambda j: (j, 0, 0))
    sds = jax.ShapeDtypeStruct
    return pl.pallas_call(
        _ssm_prep_kernel, grid=(nb,),
        in_specs=[vspec, vspec, vspec, mspec, mspec, mspec, mspec],
        out_specs=[pl.BlockSpec((None, tl, tl), lambda j: (j, 0, 0)),
                   pl.BlockSpec((None, tl, 2 * ns), lambda j: (j, 0, 0)),
                   pl.BlockSpec((None, tl, 2 * ns), lambda j: (j, 0, 0)),
                   pl.BlockSpec((None, SSM_T, 2 * ns), lambda j: (j, 0, 0))],
        out_shape=[sds((nb, tl, tl), BF16), sds((nb, tl, 2 * ns), BF16), sds((nb, tl, 2 * ns), BF16),
                   sds((nb, SSM_T, 2 * ns), F32)],
        compiler_params=_cparams("parallel"), name="ssm_prep",
    )(vec(lam_re), vec(lam_im), ldt, bt(b_re), bt(b_im), _block_diag_groups(c_re), _block_diag_groups(c_im))


def _ssm_kernel(u_ref, k_ref, win_ref, wout_ref, lp_ref, d_ref, h0s_ref, za_ref, hlp_ref, hls_ref, x_scr, hs_scr,
                *, nseq, seq, n_s, t_s):
    ns = STATES_PER_BLOCK
    nk = ns // LANES
    d = d_ref[...]
    split = lambda a: [a[:, k * LANES:(k + 1) * LANES] for k in range(2 * nk)]

    def gather(row0, n_rows, t_steps):
        return [u_ref[pl.ds(row0 + t, n_rows, stride=t_steps), :] for t in range(t_steps)]

    def advance(h, x, t_steps):
        lam = [lp_ref[t_steps - 1:t_steps, k * LANES:(k + 1) * LANES] for k in range(2 * nk)]
        re = [lam[k] * h[k] - lam[nk + k] * h[nk + k] + x[k] for k in range(nk)]
        im = [lam[k] * h[nk + k] + lam[nk + k] * h[k] + x[nk + k] for k in range(nk)]
        return re + im

    def emit(row0, n_rows, t_steps, cols, hs, kmat, wout):
        u = jnp.concatenate(cols, axis=1).astype(BF16)
        y = _dot(u, kmat) + _dot_nt(hs.astype(BF16), wout)
        for t in range(t_steps):
            yt = y[:, t * LANES:(t + 1) * LANES] + d * cols[t]
            za_ref[pl.ds(row0 + t, n_rows, stride=t_steps), :] = jax.nn.gelu(yt, approximate=True)

    c_per = seq // SSM_T
    for b in range(nseq):
        u = jnp.concatenate(gather(b * seq, c_per, SSM_T), axis=1).astype(BF16)
        x = _dot(u, win_ref[...])
        for k in range(2 * nk):
            x_scr[k, b * c_per:(b + 1) * c_per, :] = x[:, k * LANES:(k + 1) * LANES]

    def step(c, h):
        rows = pl.ds(c, nseq, stride=c_per)
        for k in range(2 * nk):
            hs_scr.at[k][rows, :] = h[k]
        return tuple(advance(h, [x_scr.at[k][rows, :] for k in range(2 * nk)], SSM_T))

    h_last = lax.fori_loop(0, c_per, step, tuple(jnp.zeros((nseq, LANES), F32) for _ in range(2 * nk)),
                           unroll=4)
    hlp_ref[...] = jnp.concatenate(list(h_last), axis=1)
    for b in range(nseq):
        hs = jnp.concatenate([hs_scr[k, b * c_per:(b + 1) * c_per, :] for k in range(2 * nk)], axis=1)
        emit(b * seq, c_per, SSM_T, gather(b * seq, c_per, SSM_T), hs, k_ref[...], wout_ref[...])

    row0, tl = nseq * seq, t_s * LANES
    cols = gather(row0, n_s, t_s)
    u = jnp.concatenate(cols, axis=1).astype(BF16)
    x = _dot(u, win_ref[(SSM_T - t_s) * LANES:, :])
    h0 = h0s_ref[...]
    hls_ref[...] = jnp.concatenate(advance(split(h0), split(x), t_s), axis=1)
    emit(row0, n_s, t_s, cols, h0, k_ref[:tl, :tl], wout_ref[:tl, :])


def _ssm(u8, kmat, win, wout, lp, d8, h0s, *, nseq, seq, n_s, t_s):
    nb, m_total, _ = u8.shape
    ns = STATES_PER_BLOCK
    assert m_total == nseq * seq + n_s * t_s and seq % SSM_T == 0 and t_s <= SSM_T
    rows = nseq * seq // SSM_T
    tl = SSM_T * LANES
    sds = jax.ShapeDtypeStruct
    blk = lambda r, c: pl.BlockSpec((None, r, c), lambda j: (j, 0, 0))
    return pl.pallas_call(
        functools.partial(_ssm_kernel, nseq=nseq, seq=seq, n_s=n_s, t_s=t_s),
        grid=(nb,),
        in_specs=[blk(m_total, LANES), blk(tl, tl), blk(tl, 2 * ns), blk(tl, 2 * ns), blk(SSM_T, 2 * ns),
                  blk(1, LANES), blk(n_s, 2 * ns)],
        out_specs=[blk(m_total, LANES), blk(nseq, 2 * ns), blk(n_s, 2 * ns)],
        out_shape=[sds((nb, m_total, LANES), F32), sds((nb, nseq, 2 * ns), F32), sds((nb, n_s, 2 * ns), F32)],
        scratch_shapes=[pltpu.VMEM((2 * ns // LANES, rows, LANES), F32)] * 2,
        compiler_params=_cparams("parallel"), name="ssm",
    )(u8, kmat, win, wout, lp, d8, h0s)


def _state_to_blocks(re, im):
    n, g, p = re.shape
    nb = g // GROUPS_PER_BLOCK
    f = lambda a: jnp.transpose(a.reshape(n, nb, GROUPS_PER_BLOCK * p), (1, 0, 2))
    return jnp.concatenate([f(re), f(im)], axis=-1)


def _blocks_to_state(h):
    nb, n, w = h.shape
    ns = w // 2
    f = lambda a: jnp.transpose(a, (1, 0, 2)).reshape(n, nb * GROUPS_PER_BLOCK, ns // GROUPS_PER_BLOCK)
    return f(h[..., :ns]), f(h[..., ns:])


def _key_to_float(c):
    return lax.bitcast_convert_type(c ^ ((c >> 31) & 0x7FFFFFFF), F32)


def _kth_largest(count_ge, shape, top_k):
    def body(it, carry):
        u, cnt = carry
        bit = jnp.left_shift(jnp.int32(1), 31 - it)
        cand = u | bit
        c = count_ge(_key_to_float(cand ^ INT_MIN))
        take = c >= top_k
        return jnp.where(take, cand, u), jnp.where(take, c, cnt)
    u, cnt = lax.fori_loop(0, 32, body, (jnp.zeros(shape, jnp.int32), jnp.zeros(shape, jnp.int32)))
    return _key_to_float(u ^ INT_MIN), cnt


TIE_CHUNK = 256


def _tri_ones(n, dtype, lower):
    r = lax.broadcasted_iota(jnp.int32, (n, n), 0)
    c = lax.broadcasted_iota(jnp.int32, (n, n), 1)
    return jnp.where((r >= c) if lower else (r <= c), 1.0, 0.0).astype(dtype)


def _attn_prompt_kernel(qt_ref, kb_ref, vt_ref, qit_ref, kib_ref, wt_ref, o_ref, s_scr, bias_scr,
                        *, seq, top_k, key_chunk):
    qb = pl.program_id(1)
    tq = o_ref.shape[0]

    def run(nk):
        kib = kib_ref[:nk]
        s = jnp.zeros((nk, tq), F32)
        for h in range(IDX_HEADS):
            d = _dot(kib, qit_ref[h * IDX_DIM:(h + 1) * IDX_DIM, :])
            s = s + jnp.maximum(d, 0.0) * wt_ref[IDX_DIM + h:IDX_DIM + h + 1, :]
        kpos = lax.broadcasted_iota(jnp.int32, (nk, tq), 0)
        qpos = qb * tq + lax.broadcasted_iota(jnp.int32, (nk, tq), 1)
        allowed = kpos <= qpos
        s_scr[:nk] = jnp.where(allowed, s, -jnp.inf)

        def count_ge(t):
            n_acc = 8
            acc = [jnp.zeros((SUBLANES, tq), jnp.int32) for _ in range(n_acc)]
            for j in range(nk // SUBLANES):
                rows = s_scr[j * SUBLANES:(j + 1) * SUBLANES]
                acc[j % n_acc] = acc[j % n_acc] + (rows >= t).astype(jnp.int32)
            while len(acc) > 1:
                acc = [a + b for a, b in zip(acc[::2], acc[1::2])]
            return jnp.sum(acc[0], axis=0, keepdims=True)

        thr, cnt = _kth_largest(count_ge, (1, tq), top_k)
        few = qpos < top_k - 1
        q_row = qb * tq + lax.broadcasted_iota(jnp.int32, (1, tq), 1)
        tied = jnp.max(jnp.where((cnt > top_k) & (q_row >= top_k - 1), 1, 0)) > 0

        @pl.when(jnp.logical_not(tied))
        def _():
            bias_scr[:nk] = jnp.where(allowed & ((s_scr[:nk] >= thr) | few), 0.0, -jnp.inf)

        @pl.when(tied)
        def _():
            n_gt = jnp.sum((s_scr[:nk] > thr).astype(jnp.int32), axis=0, keepdims=True)
            need = (top_k - n_gt).astype(F32)
            tri = _tri_ones(TIE_CHUNK, BF16, lower=True)
            carry = jnp.zeros((1, tq), F32)
            for c in range(nk // TIE_CHUNK):
                rs = slice(c * TIE_CHUNK, (c + 1) * TIE_CHUNK)
                sc = s_scr[rs]
                eq = sc == thr
                rank = _dot(tri, jnp.where(eq, 1.0, 0.0).astype(BF16)) + carry
                keep = (sc > thr) | (eq & (rank <= need))
                k_row = c * TIE_CHUNK + lax.broadcasted_iota(jnp.int32, (TIE_CHUNK, tq), 0)
                keep_all = (q_row < top_k - 1) & (k_row <= q_row)
                bias_scr[rs] = jnp.where(keep | keep_all, 0.0, -jnp.inf)
                carry = rank[TIE_CHUNK - 1:, :]

        outs = []
        for g in range(N_KV_HEADS):
            kg = kb_ref[:nk, g * HEAD_DIM:(g + 1) * HEAD_DIM]
            vtg = vt_ref[g * HEAD_DIM:(g + 1) * HEAD_DIM, :nk]
            for r in range(N_REP):
                h = g * N_REP + r
                lg = _dot(kg, qt_ref[h * HEAD_DIM:(h + 1) * HEAD_DIM, :]) + bias_scr[:nk]
                p = jnp.exp(lg - jnp.max(lg, axis=0, keepdims=True))
                den = jnp.sum(p, axis=0, keepdims=True)
                outs.append(_dot(vtg, p.astype(BF16)) / den)
        o_ref[...] = jnp.concatenate(outs, axis=0).T.astype(BF16)

    n_var = seq // key_chunk
    need = (qb * tq + tq + key_chunk - 1) // key_chunk
    for v in range(1, n_var + 1):
        pl.when(need == v)(functools.partial(run, v * key_chunk))


def _attn_prompt(q_t, kb, v_t, qi_t, kib, kiw_t, *, batch, seq, m_out, tq=128, key_chunk=256):
    top_k = min(TOPK_MAX, seq // 4)
    nq = seq // tq
    assert seq % key_chunk == 0 and key_chunk % tq == 0
    aw = N_HEADS * HEAD_DIM
    qcol = lambda n: pl.BlockSpec((n, tq), lambda b, i: (0, b * nq + i))
    return pl.pallas_call(
        functools.partial(_attn_prompt_kernel, seq=seq, top_k=top_k, key_chunk=key_chunk),
        grid=(batch, nq),
        in_specs=[qcol(aw),
                  pl.BlockSpec((seq, N_KV_HEADS * HEAD_DIM), lambda b, i: (b, 0)),
                  pl.BlockSpec((N_KV_HEADS * HEAD_DIM, seq), lambda b, i: (0, b)),
                  qcol(IDX_HEADS * IDX_DIM),
                  pl.BlockSpec((seq, IDX_DIM), lambda b, i: (b, 0)),
                  qcol(LANES)],
        out_specs=pl.BlockSpec((tq, aw), lambda b, i: (b * nq + i, 0)),
        out_shape=jax.ShapeDtypeStruct((m_out, aw), BF16),
        scratch_shapes=[pltpu.VMEM((seq, tq), F32), pltpu.VMEM((seq, tq), F32)],
        compiler_params=_cparams("parallel", "parallel"), name="attn_prompt",
    )(q_t, kb, v_t, qi_t, kib, kiw_t)


def _merge_kernel(za_ref, bm_ref, bt_ref, g_ref, x_ref, gw_ref, gb_ref, wa_ref, wb_ref, wo_ref, o_ref,
                  *, n_main, tm, tail):
    d = x_ref.shape[1]

    def run(rows, b_ref):
        za = jnp.concatenate([za_ref[j, :rows] for j in range(za_ref.shape[0])], axis=1)
        a_out = za * jax.nn.sigmoid(_dot(za.astype(BF16), gw_ref[...]) + gb_ref[...])
        merged = (g_ref[:rows, :d] * _dot(a_out.astype(BF16), wa_ref[...])
                  + g_ref[:rows, d:] * _dot(b_ref[:rows], wb_ref[...]))
        o_ref[:rows] = x_ref[:rows] + _dot(merged.astype(BF16), wo_ref[...])

    i = pl.program_id(0)
    pl.when(i < n_main)(lambda: run(tm, bm_ref))
    pl.when(i == n_main)(lambda: run(tail, bt_ref))


def _merge(za8, b_main, b_tail, gates, x, glu_w, glu_b, wa, wb, wo, *, tm):
    m, d = x.shape
    nb = za8.shape[0]
    n_main_rows, aw = b_main.shape
    tail = b_tail.shape[0]
    n_main = n_main_rows // tm
    assert n_main * tm == n_main_rows and n_main_rows + tail == m and tail <= tm
    resident = lambda shape: pl.BlockSpec(shape, lambda i: (0,) * len(shape), pipeline_mode=pl.Buffered(1))
    return pl.pallas_call(
        functools.partial(_merge_kernel, n_main=n_main, tm=tm, tail=tail), grid=(n_main + 1,),
        in_specs=[pl.BlockSpec((nb, tm, LANES), lambda i: (0, i, 0)),
                  pl.BlockSpec((tm, aw), lambda i: (jnp.minimum(i, n_main - 1), 0)),
                  pl.BlockSpec((tail, aw), lambda i: (0, 0)),
                  _row_spec(tm, 2 * d), _row_spec(tm, d), resident(glu_w.shape), resident((1, glu_w.shape[1])),
                  resident(wa.shape), resident(wb.shape), resident(wo.shape)],
        out_specs=_row_spec(tm, d),
        out_shape=jax.ShapeDtypeStruct((m, d), F32),
        compiler_params=_cparams("arbitrary"), name="merge",
    )(za8, b_main, b_tail, gates, x, glu_w, glu_b.reshape(1, -1), wa, wb, wo)


QPAD = SUBLANES
CHUNK_PAGES = 8


def _page_copy(pt_ref, src_ref, buf_ref, sem, seq, slot, p, page_base, rows):
    start = pl.multiple_of((page_base + pt_ref[seq, p]) * rows, rows)
    return pltpu.make_async_copy(src_ref.at[pl.ds(start, rows), :],
                                 buf_ref.at[slot, pl.ds(p * rows, rows), :], sem)


def _prefetch_pages(n_pages, copies):
    s, n_seq = pl.program_id(0), pl.num_programs(0)
    slot = s % 2

    def start_all(seq, sl):
        def body(p, carry):
            for copy in copies:
                copy(seq, sl, p).start()
            return carry
        lax.fori_loop(0, n_pages, body, 0, unroll=8)

    @pl.when(s == 0)
    def _():
        start_all(0, 0)

    @pl.when(s + 1 < n_seq)
    def _():
        start_all(s + 1, 1 - slot)

    def wait_body(p, carry):
        for copy in copies:
            copy(s, slot, p).wait()
        return carry
    lax.fori_loop(0, n_pages, wait_body, 0, unroll=8)
    return slot


def _sample_scores_kernel(pt_ref, qs_ref, ws_ref, kn_ref, cik_ref, sp_ref, sn_ref, buf_ref, sem_ref,
                          *, page_base, n_pages):
    slot = _prefetch_pages(n_pages, [
        lambda seq, sl, p: _page_copy(pt_ref, cik_ref, buf_ref, sem_ref.at[sl], seq, sl, p, page_base, IDX_DIM)])
    qs, ws = qs_ref[...], ws_ref[:, :1]

    def scores(d):
        r = jnp.maximum(d, 0.0) * ws
        return r.reshape(IDX_HEADS, QPAD, r.shape[1]).sum(axis=0)

    nkc = CHUNK_PAGES * PAGE_SIZE
    for c in range(n_pages // CHUNK_PAGES):
        keys_t = jnp.concatenate(
            [buf_ref[slot, (c * CHUNK_PAGES + kk) * IDX_DIM:(c * CHUNK_PAGES + kk + 1) * IDX_DIM, :]
             for kk in range(CHUNK_PAGES)], axis=1).astype(BF16)
        sp_ref[:, c * nkc:(c + 1) * nkc] = scores(_dot(qs, keys_t))
    sn_ref[...] = scores(_dot_nt(qs, kn_ref[...]))


def _sample_thr_kernel(sp_ref, sn_ref, thr_ref, cut_ref, s_scr, *, past, top_k, t_len):
    s = jnp.concatenate([sp_ref[...], sn_ref[...]], axis=1)
    shape = s.shape
    kpos = lax.broadcasted_iota(jnp.int32, shape, 1)
    qidx = lax.broadcasted_iota(jnp.int32, shape, 0) % t_len
    s_scr[...] = jnp.where(kpos <= past + qidx, s, -jnp.inf)

    def count_ge(t):
        return jnp.sum((s_scr[...] >= t).astype(jnp.int32), axis=1, keepdims=True)

    thr, cnt = _kth_largest(count_ge, (shape[0], 1), top_k)
    thr_ref[...] = jnp.broadcast_to(thr, thr_ref.shape)
    tied = jnp.max(jnp.where(cnt > top_k, 1, 0)) > 0

    @pl.when(jnp.logical_not(tied))
    def _():
        cut_ref[...] = jnp.full(cut_ref.shape, shape[1], jnp.int32)

    @pl.when(tied)
    def _():
        n_gt = jnp.sum((s_scr[...] > thr).astype(jnp.int32), axis=1, keepdims=True)
        need = (top_k - n_gt).astype(F32)
        tri = _tri_ones(LANES, BF16, lower=False)
        carry = jnp.zeros((shape[0], 1), F32)
        cut = jnp.full((shape[0], 1), -1, jnp.int32)
        lane = lax.broadcasted_iota(jnp.int32, (shape[0], LANES), 1)
        for c in range(shape[1] // LANES):
            eq = s_scr[:, c * LANES:(c + 1) * LANES] == thr
            rank = _dot(jnp.where(eq, 1.0, 0.0).astype(BF16), tri) + carry
            last = jnp.max(jnp.where(eq & (rank <= need), lane + c * LANES, -1), axis=1, keepdims=True)
            cut = jnp.maximum(cut, last)
            carry = rank[:, LANES - 1:]
        cut_ref[...] = jnp.broadcast_to(cut, cut_ref.shape)


def _sample_attn_kernel(pt_ref, q_ref, sp_ref, sn_ref, thr_ref, cut_ref, kn_ref, vn_ref, ck_ref, cv_ref, o_ref,
                        kbuf, vbuf, sem_ref, lg_scr, *, page_base, n_pages):
    page_rows = N_KV_HEADS * PAGE_SIZE
    slot = _prefetch_pages(n_pages, [
        lambda seq, sl, p: _page_copy(pt_ref, ck_ref, kbuf, sem_ref.at[0, sl], seq, sl, p, page_base, page_rows),
        lambda seq, sl, p: _page_copy(pt_ref, cv_ref, vbuf, sem_ref.at[1, sl], seq, sl, p, page_base, page_rows)])
    rows_g = N_REP * QPAD
    nkc = CHUNK_PAGES * PAGE_SIZE
    n_chunks = n_pages // CHUNK_PAGES
    past = n_pages * PAGE_SIZE
    thr, cut = thr_ref[:, :1], cut_ref[:, :1]
    q = q_ref[...]
    qg = [q[g * rows_g:(g + 1) * rows_g] for g in range(N_KV_HEADS)]
    head_rows = lambda c, g: pl.ds(c * CHUNK_PAGES * page_rows + g, nkc, stride=N_KV_HEADS)

    def bias(scores, pos0, allowed=None):
        kpos = pos0 + lax.broadcasted_iota(jnp.int32, scores.shape, 1)
        sel = (scores > thr) | ((scores == thr) & (kpos <= cut))
        if allowed is not None:
            sel = sel & allowed
        return jnp.concatenate([jnp.where(sel, 0.0, -jnp.inf)] * N_HEADS, axis=0)

    for c in range(n_chunks):
        lg = jnp.concatenate([_dot_nt(qg[g], kbuf.at[slot][head_rows(c, g), :].astype(BF16))
                              for g in range(N_KV_HEADS)], axis=0)
        lg_scr[:, c * nkc:(c + 1) * nkc] = lg + bias(sp_ref[:, c * nkc:(c + 1) * nkc], c * nkc)
    shape = sn_ref.shape
    allowed = lax.broadcasted_iota(jnp.int32, shape, 1) <= lax.broadcasted_iota(jnp.int32, shape, 0)
    lg = jnp.concatenate([_dot_nt(qg[g], kn_ref[:, g * HEAD_DIM:(g + 1) * HEAD_DIM])
                          for g in range(N_KV_HEADS)], axis=0)
    lg_scr[:, past:] = lg + bias(sn_ref[...], past, allowed)

    lg = lg_scr[...]
    p = jnp.exp(lg - jnp.max(lg, axis=1, keepdims=True))
    den = jnp.sum(p, axis=1, keepdims=True)
    pb = p.astype(BF16)
    acc = jnp.concatenate([_dot(pb[g * rows_g:(g + 1) * rows_g, past:], vn_ref[:, g * HEAD_DIM:(g + 1) * HEAD_DIM])
                           for g in range(N_KV_HEADS)], axis=0)
    for c in range(n_chunks):
        acc = acc + jnp.concatenate(
            [_dot(pb[g * rows_g:(g + 1) * rows_g, c * nkc:(c + 1) * nkc],
                  vbuf.at[slot][head_rows(c, g), :].astype(BF16)) for g in range(N_KV_HEADS)], axis=0)
    o_ref[...] = acc / den


def _pad_queries(x, t_len):
    n, h, d = x.shape
    x = jnp.transpose(x.reshape(n // t_len, t_len, h, d), (0, 2, 1, 3))
    x = jnp.pad(x, ((0, 0), (0, 0), (0, QPAD - t_len), (0, 0)))
    return x.reshape(n // t_len, h * QPAD, d)


def _pad_new_keys(x, t_len):
    n, w = x.shape
    return jnp.pad(x.reshape(n // t_len, t_len, w), ((0, 0), (0, PAGE_SIZE - t_len), (0, 0)))


def _attn_sample(q_s, k_new, v_new, qi_s, wi_s, ki_new, cache_k, cache_v, cache_idx_k, page_table, page_base,
                 *, t_len):
    db, n_pages = page_table.shape
    past = n_pages * PAGE_SIZE
    top_k = min(TOPK_MAX, (past + t_len) // 4)
    kvw = N_KV_HEADS * HEAD_DIM
    sds = jax.ShapeDtypeStruct
    q = _pad_queries(q_s.reshape(-1, N_HEADS, HEAD_DIM), t_len)
    qi = _pad_queries(qi_s.reshape(-1, IDX_HEADS, IDX_DIM), t_len)
    ws = _pad_queries(wi_s.reshape(-1, IDX_HEADS, 1), t_len)
    ws = jnp.broadcast_to(ws, ws.shape[:2] + (PAGE_SIZE,))
    kn, vn, kin = (_pad_new_keys(a, t_len) for a in (k_new, v_new, ki_new))

    assert n_pages % CHUNK_PAGES == 0
    seq_spec = lambda r, c: pl.BlockSpec((None, r, c), lambda s, pt: (s, 0, 0))
    hbm = pl.BlockSpec(memory_space=pl.ANY)
    dma = pltpu.SemaphoreType.DMA
    in_order = _cparams("arbitrary")
    sp, sn = pl.pallas_call(
        functools.partial(_sample_scores_kernel, page_base=page_base, n_pages=n_pages),
        grid_spec=pltpu.PrefetchScalarGridSpec(
            num_scalar_prefetch=1, grid=(db,),
            in_specs=[seq_spec(IDX_HEADS * QPAD, IDX_DIM), seq_spec(IDX_HEADS * QPAD, PAGE_SIZE),
                      seq_spec(PAGE_SIZE, IDX_DIM), hbm],
            out_specs=[seq_spec(QPAD, past), seq_spec(QPAD, PAGE_SIZE)],
            scratch_shapes=[pltpu.VMEM((2, n_pages * IDX_DIM, PAGE_SIZE), F32), dma((2,))]),
        out_shape=[sds((db, QPAD, past), F32), sds((db, QPAD, PAGE_SIZE), F32)],
        compiler_params=in_order, name="sample_scores",
    )(page_table, qi, ws, kin, cache_idx_k.reshape(-1, PAGE_SIZE))

    rows = db * t_len
    rblk = min(rows, 8 * SUBLANES)
    assert rows % rblk == 0
    real = lambda a: a[:, :t_len].reshape(rows, a.shape[2])
    padded = lambda a: jnp.pad(a.reshape(db, t_len, LANES), ((0, 0), (0, QPAD - t_len), (0, 0)))
    thr, cut = pl.pallas_call(
        functools.partial(_sample_thr_kernel, past=past, top_k=top_k, t_len=t_len),
        grid=(rows // rblk,),
        in_specs=[_row_spec(rblk, past), _row_spec(rblk, PAGE_SIZE)],
        out_specs=[_row_spec(rblk, LANES), _row_spec(rblk, LANES)],
        out_shape=[sds((rows, LANES), F32), sds((rows, LANES), jnp.int32)],
        scratch_shapes=[pltpu.VMEM((rblk, past + PAGE_SIZE), F32)],
        compiler_params=_cparams("parallel"), name="sample_threshold",
    )(real(sp), real(sn))

    hq = N_HEADS * QPAD
    page_rows = N_KV_HEADS * PAGE_SIZE
    out = pl.pallas_call(
        functools.partial(_sample_attn_kernel, page_base=page_base, n_pages=n_pages),
        grid_spec=pltpu.PrefetchScalarGridSpec(
            num_scalar_prefetch=1, grid=(db,),
            in_specs=[seq_spec(hq, HEAD_DIM), seq_spec(QPAD, past), seq_spec(QPAD, PAGE_SIZE), seq_spec(QPAD, LANES),
                      seq_spec(QPAD, LANES), seq_spec(PAGE_SIZE, kvw), seq_spec(PAGE_SIZE, kvw), hbm, hbm],
            out_specs=seq_spec(hq, HEAD_DIM),
            scratch_shapes=[pltpu.VMEM((2, n_pages * page_rows, HEAD_DIM), F32),
                            pltpu.VMEM((2, n_pages * page_rows, HEAD_DIM), F32),
                            dma((2, 2)), pltpu.VMEM((hq, past + PAGE_SIZE), F32)]),
        out_shape=sds((db, hq, HEAD_DIM), F32),
        compiler_params=in_order, name="sample_attn",
    )(page_table, q, sp, sn, padded(thr), padded(cut), kn, vn, cache_k, cache_v)
    out = out.reshape(db, N_HEADS, QPAD, HEAD_DIM)[:, :, :t_len]
    return jnp.transpose(out, (0, 2, 1, 3)).reshape(db * t_len, N_HEADS * HEAD_DIM)


ROW_TILE = 640
PROJ_ROW_TILE = 512
FFN_ROW_TILE = 512
FF_TILE = 512
MERGE_ROW_TILE = 256


def _layer(x_p, x_s, pos, dims, layer, w_in_all, ck, cv, cik, page_base, s_re, s_im, page_table, p):
    b, s, db, t = dims
    mp, ms = b * s, db * t
    d = x_p.shape[1]
    bf = lambda w: w.astype(BF16)
    ssm_w = d // 2
    w_g = bf(w_in_all[layer, :, w_in_all.shape[2] - 2 * d:])

    x1, xn = _ffn(x_p, x_s, p["ffn1_norm"], bf(p["ffn1_w_gate"]), bf(p["ffn1_w_up"]), bf(p["ffn1_w_down"]),
                  p["mix_norm"], n_main_rows=mp, tail=ms, split_out=False, tm=FFN_ROW_TILE, tf=FF_TILE)
    pr = _projections(xn, jnp.swapaxes(w_in_all, 1, 2), layer, w_g, p["q_norm"], p["k_norm"], pos,
                      n_main_rows=mp, seq=s, tail=ms, tm=PROJ_ROW_TILE, tm_gates=ROW_TILE)

    kmat, win, wout, lp = _ssm_prep(p["ssm_lambda_re"], p["ssm_lambda_im"], p["ssm_b_re"], p["ssm_b_im"],
                                    p["ssm_c_re"], p["ssm_c_im"], p["ssm_log_dt"])
    nblk = ssm_w // LANES
    d8 = p["ssm_d"].reshape(nblk, 1, LANES)
    za8, hl_p, hl_s = _ssm(pr["u8"], kmat, win, wout, lp, d8, _state_to_blocks(s_re, s_im),
                           nseq=b, seq=s, n_s=db, t_s=t)

    b_p = _attn_prompt(pr["q_t"], pr["kb"], pr["v_t"], pr["qi_t"], pr["kib"], pr["kiw_t"],
                       batch=b, seq=s, m_out=mp)
    b_s = _attn_sample(pr["q"][mp:], pr["kb"][mp:], bf(pr["v_s"].reshape(ms, -1)), pr["qi"][mp:],
                       pr["kiw"][mp:, IDX_DIM:IDX_DIM + IDX_HEADS], pr["kib"][mp:],
                       ck, cv, cik, page_table, page_base, t_len=t)

    x2 = _merge(za8, b_p, bf(b_s), pr["gates"], x1, bf(p["glu_w"]), p["glu_b"], bf(p["w_branch_a"]),
                bf(p["w_branch_b"]), bf(p["w_out"]), tm=MERGE_ROW_TILE)
    y_p, y_s = _ffn(x2, None, p["ffn2_norm"], bf(p["ffn2_w_gate"]), bf(p["ffn2_w_up"]), bf(p["ffn2_w_down"]),
                    n_main_rows=mp, tail=ms, split_out=True, tm=FFN_ROW_TILE, tf=FF_TILE)

    hp_re, hp_im = _blocks_to_state(hl_p)
    hs_re, hs_im = _blocks_to_state(hl_s)
    ki = pr["kiw"][:, :IDX_DIM]
    kvs = (N_KV_HEADS, HEAD_DIM)
    rows = (pr["k_p"].reshape(b, s, *kvs), pr["v_p"].reshape(b, s, *kvs), ki[:mp].reshape(b, s, IDX_DIM),
            hp_re, hp_im,
            pr["k_s"].reshape(db, t, *kvs), pr["v_s"].reshape(db, t, *kvs), ki[mp:].reshape(db, t, IDX_DIM),
            hs_re, hs_im)
    return y_p, y_s, rows


def kernel(x_prompt, x_sample, cache_k, cache_v, cache_idx_k, state_ssm_re, state_ssm_im, page_table,
           ffn1_norm, ffn1_w_gate, ffn1_w_up, ffn1_w_down, mix_norm, w_in, q_norm, k_norm,
           ssm_lambda_re, ssm_lambda_im, ssm_b_re, ssm_b_im, ssm_c_re, ssm_c_im, ssm_d, ssm_log_dt,
           glu_w, glu_b, w_branch_a, w_branch_b, w_out, ffn2_norm, ffn2_w_gate, ffn2_w_up, ffn2_w_down):
    b, s, d = x_prompt.shape
    db, t, _ = x_sample.shape
    depth, n_phys = cache_k.shape[:2]
    past = page_table.shape[1] * PAGE_SIZE
    x_p, x_s = x_prompt.reshape(b * s, d), x_sample.reshape(db * t, d)
    pos = jnp.concatenate([jnp.arange(s), jnp.tile(past + jnp.arange(t), db)])
    ck = cache_k.reshape(-1, HEAD_DIM)
    cv = cache_v.reshape(-1, HEAD_DIM)
    cik = jnp.swapaxes(cache_idx_k, 2, 3).reshape(depth * n_phys, IDX_DIM, PAGE_SIZE)
    params = dict(
        ffn1_norm=ffn1_norm, ffn1_w_gate=ffn1_w_gate, ffn1_w_up=ffn1_w_up, ffn1_w_down=ffn1_w_down,
        mix_norm=mix_norm, w_in=w_in, q_norm=q_norm, k_norm=k_norm,
        ssm_lambda_re=ssm_lambda_re, ssm_lambda_im=ssm_lambda_im, ssm_b_re=ssm_b_re, ssm_b_im=ssm_b_im,
        ssm_c_re=ssm_c_re, ssm_c_im=ssm_c_im, ssm_d=ssm_d, ssm_log_dt=ssm_log_dt, glu_w=glu_w, glu_b=glu_b,
        w_branch_a=w_branch_a, w_branch_b=w_branch_b, w_out=w_out,
        ffn2_norm=ffn2_norm, ffn2_w_gate=ffn2_w_gate, ffn2_w_up=ffn2_w_up, ffn2_w_down=ffn2_w_down)
    new = [[] for _ in range(10)]
    for l in range(depth):
        p = {name: w[l] for name, w in params.items()}
        x_p, x_s, rows = _layer(x_p, x_s, pos, (b, s, db, t), l, w_in, ck, cv, cik, l * n_phys,
                                state_ssm_re[l], state_ssm_im[l], page_table, p)
        for lst, r in zip(new, rows):
            lst.append(r)
    return (x_p.reshape(b, s, d), x_s.reshape(db, t, d)) + tuple(jnp.stack(lst) for lst in new)
```

```python
import functools

import jax
import jax.numpy as jnp
from jax import lax
from jax.experimental import pallas as pl
from jax.experimental.pallas import tpu as pltpu

F32 = jnp.float32
BF16 = jnp.bfloat16

SSM_GROUP = 16
SSM_STATE = 64
N_HEADS = 8
HEAD_DIM = 128
N_KV_HEADS = 2
N_REP = N_HEADS // N_KV_HEADS
ROT_DIM = HEAD_DIM // 4
ROPE_THETA = 500000.0
IDX_HEADS = 16
IDX_DIM = 64
IDX_ROT_DIM = IDX_DIM // 4
TOPK_MAX = 256
PAGE_SIZE = 128
FFN_RES = 0.5
EPS = 1e-6

LANES = 128
SUBLANES = 8
VMEM_LIMIT_BYTES = 56 * 1024 * 1024

GROUPS_PER_BLOCK = LANES // SSM_GROUP
STATES_PER_BLOCK = GROUPS_PER_BLOCK * SSM_STATE
SSM_T = 8

INT_MIN = -(2 ** 31)


def _cparams(*sem):
    return pltpu.CompilerParams(dimension_semantics=sem, vmem_limit_bytes=VMEM_LIMIT_BYTES)


def _rms(x, g):
    return x * lax.rsqrt(jnp.mean(x * x, axis=-1, keepdims=True) + EPS) * g


def _dot(a, b):
    return jnp.dot(a, b, preferred_element_type=F32)


def _dot_nt(a, b):
    return lax.dot_general(a, b, (((1,), (1,)), ((), ())), preferred_element_type=F32)


def _ffn_kernel(*refs, nf, n_main, tm, tail, two_src, split_out, with_next_norm):
    refs = list(refs)
    x_main_ref = refs.pop(0)
    x_tail_ref = refs.pop(0) if two_src else x_main_ref
    g_ref, wg_ref, wu_ref, wd_ref = (refs.pop(0) for _ in range(4))
    g2_ref = refs.pop(0) if with_next_norm else None
    y_main_ref = refs.pop(0)
    y_tail_ref = refs.pop(0) if split_out else y_main_ref
    n2_ref = refs.pop(0) if with_next_norm else None
    xn_ref, acc_ref = refs
    i, f = pl.program_id(0), pl.program_id(1)

    def run(rows, x_ref, y_ref):
        @pl.when(f == 0)
        def _():
            xn_ref[:rows] = _rms(x_ref[:rows], g_ref[...]).astype(BF16)
            acc_ref[:rows] = jnp.zeros((rows, acc_ref.shape[1]), F32)

        xn = xn_ref[:rows]
        a = _dot(xn, wg_ref[...])
        b = _dot(xn, wu_ref[...])
        h = (a * jax.nn.sigmoid(a) * b).astype(BF16)
        acc_ref[:rows] += _dot(h, wd_ref[...])

        @pl.when(f == nf - 1)
        def _():
            y = x_ref[:rows] + FFN_RES * acc_ref[:rows]
            y_ref[:rows] = y
            if with_next_norm:
                n2_ref[:rows] = _rms(y, g2_ref[...]).astype(BF16)

    pl.when(i < n_main)(lambda: run(tm, x_main_ref, y_main_ref))
    pl.when(i == n_main)(lambda: run(tail, x_tail_ref, y_tail_ref))


def _ffn(x_main, x_tail, g, wg, wu, wd, g2=None, *, n_main_rows, tail, split_out, tm, tf):
    d = x_main.shape[1]
    nf = wg.shape[1] // tf
    n_main = n_main_rows // tm
    m = n_main_rows + tail
    two_src = x_tail is not None
    with_next = g2 is not None
    clamp = lambda i, f: (jnp.minimum(i, n_main - 1), 0)
    rows = lambda i, f: (i, 0)
    first = lambda i, f: (0, 0)
    sds = jax.ShapeDtypeStruct
    in_specs = [pl.BlockSpec((tm, d), clamp if two_src else rows)]
    args = [x_main]
    if two_src:
        in_specs.append(pl.BlockSpec((tail, d), first))
        args.append(x_tail)
    in_specs += [pl.BlockSpec((1, d), first), pl.BlockSpec((d, tf), lambda i, f: (0, f)),
                 pl.BlockSpec((d, tf), lambda i, f: (0, f)), pl.BlockSpec((tf, d), lambda i, f: (f, 0))]
    args += [g.reshape(1, d), wg, wu, wd]
    if with_next:
        in_specs.append(pl.BlockSpec((1, d), first))
        args.append(g2.reshape(1, d))
    if split_out:
        out_shape = [sds((n_main_rows, d), F32), sds((tail, d), F32)]
        out_specs = [pl.BlockSpec((tm, d), clamp), pl.BlockSpec((tail, d), first)]
    else:
        out_shape = [sds((m, d), F32)]
        out_specs = [pl.BlockSpec((tm, d), rows)]
    if with_next:
        out_shape.append(sds((m, d), BF16))
        out_specs.append(pl.BlockSpec((tm, d), rows))
    return pl.pallas_call(
        functools.partial(_ffn_kernel, nf=nf, n_main=n_main, tm=tm, tail=tail, two_src=two_src,
                          split_out=split_out, with_next_norm=with_next),
        grid=(n_main + 1, nf),
        in_specs=in_specs,
        out_specs=out_specs,
        out_shape=out_shape,
        scratch_shapes=[pltpu.VMEM((tm, d), BF16), pltpu.VMEM((tm, d), F32)],
        compiler_params=_cparams("arbitrary", "arbitrary"),
        name="ffn" + ("_norm" if with_next else ""),
    )(*args)


def _rope_tables(pos, rot_dim, width, tile):
    half = rot_dim // 2
    m = pos.shape[0]
    freqs = ROPE_THETA ** (-jnp.arange(half, dtype=F32) * 2.0 / rot_dim)
    ang = pos.astype(F32)[:, None] * freqs[None, :]
    cos, sin = jnp.cos(ang), jnp.sin(ang)
    zh = jnp.zeros((m, half), F32)
    rest = width - rot_dim
    c = jnp.concatenate([cos, cos, jnp.ones((m, rest), F32)], axis=1)
    s1 = jnp.concatenate([-sin, zh, jnp.zeros((m, rest), F32)], axis=1)
    s2 = jnp.concatenate([zh, sin, jnp.zeros((m, rest), F32)], axis=1)
    if tile:
        reps = LANES // width
        return tuple(jnp.tile(t, (1, reps)) for t in (c, s1, s2))
    pad = LANES - width
    return (jnp.pad(c, ((0, 0), (0, pad)), constant_values=1.0),
            jnp.pad(s1, ((0, 0), (0, pad))), jnp.pad(s2, ((0, 0), (0, pad))))


def _rope(x, c, s1, s2, half):
    return x * c + pltpu.roll(x, LANES - half, 1) * s1 + pltpu.roll(x, half, 1) * s2


def _proj_rows(xn_ref, w_refs, wbf_ref, n_main, tm, tail, chunk, epilogue):
    i = pl.program_id(0)

    @pl.when(i == 0)
    def _():
        off = 0
        for w_ref in w_refs:
            n = w_ref.shape[0]
            wbf_ref[:, off:off + n] = w_ref[...].astype(F32).T.astype(BF16)
            off += n

    def run(rows):
        for r0 in range(0, rows, chunk):
            rs = slice(r0, min(r0 + chunk, rows))
            epilogue(_dot(xn_ref[rs, :], wbf_ref[...]), rs, rows == tail)

    pl.when(i < n_main)(lambda: run(tm))
    pl.when(i == n_main)(lambda: run(tail))


def _u_proj_kernel(xn_ref, w_ref, u_ref, wbf_ref, **kw):
    def epilogue(z, rs, is_tail):
        for j in range(z.shape[1] // LANES):
            u_ref[j, rs, :] = z[:, j * LANES:(j + 1) * LANES]
    _proj_rows(xn_ref, [w_ref], wbf_ref, epilogue=epilogue, **kw)


def _q_proj_kernel(xn_ref, w_ref, g_ref, c_ref, s1_ref, s2_ref, qn_ref, qt_ref, wbf_ref, **kw):
    def epilogue(z, rs, is_tail):
        c, s1, s2, g = c_ref[rs, :], s1_ref[rs, :], s2_ref[rs, :], g_ref[...]
        heads = []
        for h in range(N_HEADS):
            x = _rms(z[:, h * HEAD_DIM:(h + 1) * HEAD_DIM], g)
            heads.append(_rope(x, c, s1, s2, ROT_DIM // 2) * (HEAD_DIM ** -0.5))
        q = jnp.concatenate(heads, axis=1)
        qn_ref[rs, :] = q.astype(BF16)
        qt_ref[:, rs] = q.T.astype(BF16)
    _proj_rows(xn_ref, [w_ref], wbf_ref, epilogue=epilogue, **kw)


def _kv_proj_kernel(xn_ref, wkv_ref, wkw_ref, g_ref, c_ref, s1_ref, s2_ref, ci_ref, si1_ref, si2_ref, scale_ref,
                    kp_ref, ks_ref, vp_ref, vs_ref, kb_ref, vt_ref, kiw_ref, kiwt_ref, kib_ref, wbf_ref, **kw):
    kvw = N_KV_HEADS * HEAD_DIM

    def epilogue(z, rs, is_tail):
        c, s1, s2, g = c_ref[rs, :], s1_ref[rs, :], s2_ref[rs, :], g_ref[...]
        k_ref, v_ref = (ks_ref, vs_ref) if is_tail else (kp_ref, vp_ref)
        n = rs.stop - rs.start
        heads = []
        for h in range(N_KV_HEADS):
            x = _rope(_rms(z[:, h * HEAD_DIM:(h + 1) * HEAD_DIM], g), c, s1, s2, ROT_DIM // 2)
            rows_h = pl.ds(N_KV_HEADS * rs.start + h, n, stride=N_KV_HEADS)
            k_ref[rows_h, :] = x
            v_ref[rows_h, :] = z[:, kvw + h * HEAD_DIM:kvw + (h + 1) * HEAD_DIM]
            heads.append(x)
        kb_ref[rs, :] = jnp.concatenate(heads, axis=1).astype(BF16)
        vt_ref[:, rs] = z[:, kvw:2 * kvw].T.astype(BF16)
        y = _rope(z[:, 2 * kvw:], ci_ref[rs, :], si1_ref[rs, :], si2_ref[rs, :], IDX_ROT_DIM // 2) * scale_ref[...]
        kiw_ref[rs, :] = y
        kiwt_ref[:, rs] = y.T
        kib_ref[rs, :] = y[:, :IDX_DIM].astype(BF16)
    _proj_rows(xn_ref, [wkv_ref, wkw_ref], wbf_ref, epilogue=epilogue, **kw)


def _qi_proj_kernel(xn_ref, wa_ref, wb_ref, c_ref, s1_ref, s2_ref, qn_ref, qt_ref, wbf_ref, **kw):
    def epilogue(z, rs, is_tail):
        c, s1, s2 = c_ref[rs, :], s1_ref[rs, :], s2_ref[rs, :]
        cols = []
        for j in range(z.shape[1] // LANES):
            x = z[:, j * LANES:(j + 1) * LANES]
            cols.append(_rope(x, c, s1, s2, IDX_ROT_DIM // 2) * (IDX_DIM ** -0.5))
        q = jnp.concatenate(cols, axis=1)
        qn_ref[rs, :] = q.astype(BF16)
        qt_ref[:, rs] = q.T.astype(BF16)
    _proj_rows(xn_ref, [wa_ref, wb_ref], wbf_ref, epilogue=epilogue, **kw)


def _gate_proj_kernel(xn_ref, w_ref, o_ref, *, chunk):
    for r0 in range(0, xn_ref.shape[0], chunk):
        rs = slice(r0, r0 + chunk)
        o_ref[rs, :] = jax.nn.sigmoid(_dot(xn_ref[rs, :], w_ref[...])).astype(BF16)


def _row_spec(tm, n):
    return pl.BlockSpec((tm, n), lambda i: (i, 0))


def _full_spec(shape):
    return pl.BlockSpec(shape, lambda i: (0,) * len(shape))


PROJ_CHUNK = 256


def _projections(xn, w_in_t, layer, w_g, q_norm, k_norm, pos, *, n_main_rows, seq, tail, tm, tm_gates):
    m, d = xn.shape
    n_main = n_main_rows // tm
    assert n_main * tm == n_main_rows and n_main_rows + tail == m and tail <= tm
    grid = (n_main + 1,)
    ssm_w, attn_w, kv_w, idx_w = d // 2, N_HEADS * HEAD_DIM, N_KV_HEADS * HEAD_DIM, IDX_HEADS * IDX_DIM
    hd_tabs = _rope_tables(pos, ROT_DIM, HEAD_DIM, True)
    ix_tabs = _rope_tables(pos, IDX_ROT_DIM, IDX_DIM, True)
    kw_tabs = _rope_tables(pos, IDX_ROT_DIM, IDX_DIM, False)
    assert seq % tm == 0 and pos.shape[0] == seq + tail
    tabs_per_seq = seq // tm
    tab_specs = [pl.BlockSpec((tm, LANES), lambda i: (jnp.where(i < n_main, i % tabs_per_seq, tabs_per_seq), 0))] * 3
    xs = _row_spec(tm, d)
    sds = jax.ShapeDtypeStruct
    seq = _cparams("arbitrary")
    kw = dict(n_main=n_main, tm=tm, tail=tail, chunk=PROJ_CHUNK)

    def wcols(start, width):
        assert start % width == 0
        return pl.BlockSpec((None, width, d), lambda i: (layer, start // width, 0), pipeline_mode=pl.Buffered(1))

    def call(kernel, name, w_specs, n_w, extra_specs, extra_args, out_specs, out_shape):
        return pl.pallas_call(
            functools.partial(kernel, **kw), grid=grid,
            in_specs=[xs] + w_specs + extra_specs, out_specs=out_specs, out_shape=out_shape,
            scratch_shapes=[pltpu.VMEM((d, n_w), BF16)], compiler_params=seq, name=name,
        )(xn, *([w_in_t] * len(w_specs)), *extra_args)

    tcol = lambda n: pl.BlockSpec((n, tm), lambda i: (0, i))
    norm_spec = _full_spec((1, HEAD_DIM))
    u8 = call(_u_proj_kernel, "proj_u", [wcols(0, ssm_w)], ssm_w, [], [],
              pl.BlockSpec((ssm_w // LANES, tm, LANES), lambda i: (0, i, 0)), sds((ssm_w // LANES, m, LANES), F32))
    q_nat, q_t = call(_q_proj_kernel, "proj_q", [wcols(ssm_w, attn_w)], attn_w,
                      [norm_spec] + tab_specs, [q_norm.reshape(1, HEAD_DIM), *hd_tabs],
                      [_row_spec(tm, attn_w), tcol(attn_w)], [sds((m, attn_w), BF16), sds((attn_w, m), BF16)])
    off_kv = ssm_w + attn_w
    off_qi = off_kv + 2 * kv_w
    off_kw = off_qi + idx_w
    lane = jnp.arange(LANES)
    kw_scale = jnp.where(lane < IDX_DIM, 1.0, jnp.where(lane < IDX_DIM + IDX_HEADS, IDX_HEADS ** -0.5, 0.0))
    nh = N_KV_HEADS
    head_main = pl.BlockSpec((nh * tm, HEAD_DIM), lambda i: (jnp.minimum(i, n_main - 1), 0))
    head_tail = pl.BlockSpec((nh * tail, HEAD_DIM), lambda i: (0, 0))
    k_p, k_s, v_p, v_s, kb, v_t, kiw, kiw_t, kib = call(
        _kv_proj_kernel, "proj_kv", [wcols(off_kv, 2 * kv_w), wcols(off_kw, LANES)], 2 * kv_w + LANES,
        [norm_spec] + tab_specs + tab_specs + [_full_spec((1, LANES))],
        [k_norm.reshape(1, HEAD_DIM), *hd_tabs, *kw_tabs, kw_scale.astype(F32).reshape(1, LANES)],
        [head_main, head_tail, head_main, head_tail, _row_spec(tm, kv_w), tcol(kv_w),
         _row_spec(tm, LANES), tcol(LANES), _row_spec(tm, IDX_DIM)],
        [sds((nh * n_main_rows, HEAD_DIM), F32), sds((nh * tail, HEAD_DIM), F32),
         sds((nh * n_main_rows, HEAD_DIM), F32), sds((nh * tail, HEAD_DIM), F32),
         sds((m, kv_w), BF16), sds((kv_w, m), BF16),
         sds((m, LANES), F32), sds((LANES, m), F32), sds((m, IDX_DIM), BF16)])
    half = idx_w // 2
    qi_nat, qi_t = call(_qi_proj_kernel, "proj_qi", [wcols(off_qi, half), wcols(off_qi + half, half)], idx_w,
                        tab_specs, ix_tabs,
                        [_row_spec(tm, idx_w), tcol(idx_w)], [sds((m, idx_w), BF16), sds((idx_w, m), BF16)])

    ng = w_g.shape[1]
    tn = 1024
    gates = pl.pallas_call(
        functools.partial(_gate_proj_kernel, chunk=tm_gates // 2), grid=(m // tm_gates, ng // tn),
        in_specs=[pl.BlockSpec((tm_gates, d), lambda i, j: (i, 0)), pl.BlockSpec((d, tn), lambda i, j: (0, j))],
        out_specs=pl.BlockSpec((tm_gates, tn), lambda i, j: (i, j)),
        out_shape=sds((m, ng), BF16), compiler_params=_cparams("parallel", "parallel"), name="proj_gates",
    )(xn, w_g)
    return dict(u8=u8, q=q_nat, q_t=q_t, k_p=k_p, k_s=k_s, kb=kb, v_p=v_p, v_s=v_s, v_t=v_t, qi=qi_nat, qi_t=qi_t,
                kiw=kiw, kiw_t=kiw_t, kib=kib, gates=gates)


def _ssm_prep_kernel(lre_ref, lim_ref, ldt_ref, bre_ref, bim_ref, cre_ref, cim_ref,
                     k_ref, win_ref, wout_ref, lp_ref):
    ns = STATES_PER_BLOCK
    lre, lim = lre_ref[...], lim_ref[...]
    dt = jnp.exp(ldt_ref[...])
    a, th = lre * dt, lim * dt

    def power(l):
        mag = jnp.exp(a * float(l))
        return mag * jnp.cos(th * float(l)), mag * jnp.sin(th * float(l))

    pw = [power(l) for l in range(SSM_T + 1)]
    xr, xi = pw[1][0] - 1.0, pw[1][1]
    den = lre * lre + lim * lim
    cr, ci = (xr * lre + xi * lim) / den, (xi * lre - xr * lim) / den
    bre, bim = bre_ref[...], bim_ref[...]
    bbr, bbi = bre * cr - bim * ci, bre * ci + bim * cr
    cre, cim = cre_ref[...], cim_ref[...]
    split = lambda a: (a.astype(BF16), (a - a.astype(BF16).astype(F32)).astype(BF16))
    c_hi, c_lo = split(jnp.concatenate([cre, cim], axis=1))
    lag = []
    for l in range(SSM_T):
        pr, pi = pw[l]
        blr, bli = bbr * pr - bbi * pi, bbr * pi + bbi * pr
        b_hi, b_lo = split(jnp.concatenate([blr, -bli], axis=1))
        m = _dot_nt(b_hi, c_hi) + _dot_nt(b_hi, c_lo) + _dot_nt(b_lo, c_hi)
        lag.append(m.astype(BF16))
        t = SSM_T - 1 - l
        win_ref[t * LANES:(t + 1) * LANES, :ns] = blr.astype(BF16)
        win_ref[t * LANES:(t + 1) * LANES, ns:] = bli.astype(BF16)
    zero = jnp.zeros((LANES, LANES), BF16)
    for t in range(SSM_T):
        pr, pi = pw[t + 1]
        wout_ref[t * LANES:(t + 1) * LANES, :ns] = (cre * pr - cim * pi).astype(BF16)
        wout_ref[t * LANES:(t + 1) * LANES, ns:] = (-(cre * pi + cim * pr)).astype(BF16)
        lp_ref[t:t + 1, :ns] = pr
        lp_ref[t:t + 1, ns:] = pi
        for t2 in range(SSM_T):
            k_ref[t * LANES:(t + 1) * LANES, t2 * LANES:(t2 + 1) * LANES] = lag[t2 - t] if t2 >= t else zero


def _block_diag_groups(w):
    g, h, p = w.shape
    nb = g // GROUPS_PER_BLOCK
    w = w.reshape(nb, GROUPS_PER_BLOCK, h, p)
    eye = jnp.eye(GROUPS_PER_BLOCK, dtype=w.dtype)
    out = w[:, :, :, None, :] * eye[None, :, None, :, None]
    return out.reshape(nb, GROUPS_PER_BLOCK * h, GROUPS_PER_BLOCK * p)


def _ssm_prep(lam_re, lam_im, b_re, b_im, c_re, c_im, log_dt):
    g, p = lam_re.shape
    nb = g // GROUPS_PER_BLOCK
    ns = STATES_PER_BLOCK
    tl = SSM_T * LANES
    vec = lambda a: a.reshape(nb, 1, ns)
    ldt = vec(jnp.broadcast_to(log_dt[:, None], (g, p)))
    bt = lambda b: _block_diag_groups(jnp.swapaxes(b, 1, 2))
    vspec = pl.BlockSpec((None, 1, ns), lambda j: (j, 0, 0))
    mspec = pl.BlockSpec((None, LANES, ns), lambda j: (j, 0, 0))
    sds = jax.ShapeDtypeStruct
    return pl.pallas_call(
        _ssm_prep_kernel, grid=(nb,),
        in_specs=[vspec, vspec, vspec, mspec, mspec, mspec, mspec],
        out_specs=[pl.BlockSpec((None, tl, tl), lambda j: (j, 0, 0)),
                   pl.BlockSpec((None, tl, 2 * ns), lambda j: (j, 0, 0)),
                   pl.BlockSpec((None, tl, 2 * ns), lambda j: (j, 0, 0)),
                   pl.BlockSpec((None, SSM_T, 2 * ns), lambda j: (j, 0, 0))],
        out_shape=[sds((nb, tl, tl), BF16), sds((nb, tl, 2 * ns), BF16), sds((nb, tl, 2 * ns), BF16),
                   sds((nb, SSM_T, 2 * ns), F32)],
        compiler_params=_cparams("parallel"), name="ssm_prep",
    )(vec(lam_re), vec(lam_im), ldt, bt(b_re), bt(b_im), _block_diag_groups(c_re), _block_diag_groups(c_im))


def _ssm_kernel(u_ref, k_ref, win_ref, wout_ref, lp_ref, d_ref, h0s_ref, za_ref, hlp_ref, hls_ref, x_scr, hs_scr,
                *, nseq, seq, n_s, t_s):
    ns = STATES_PER_BLOCK
    nk = ns // LANES
    d = d_ref[...]
    split = lambda a: [a[:, k * LANES:(k + 1) * LANES] for k in range(2 * nk)]

    def gather(row0, n_rows, t_steps):
        return [u_ref[pl.ds(row0 + t, n_rows, stride=t_steps), :] for t in range(t_steps)]

    def advance(h, x, t_steps):
        lam = [lp_ref[t_steps - 1:t_steps, k * LANES:(k + 1) * LANES] for k in range(2 * nk)]
        re = [lam[k] * h[k] - lam[nk + k] * h[nk + k] + x[k] for k in range(nk)]
        im = [lam[k] * h[nk + k] + lam[nk + k] * h[k] + x[nk + k] for k in range(nk)]
        return re + im

    def emit(row0, n_rows, t_steps, cols, hs, kmat, wout):
        u = jnp.concatenate(cols, axis=1).astype(BF16)
        y = _dot(u, kmat) + _dot_nt(hs.astype(BF16), wout)
        for t in range(t_steps):
            yt = y[:, t * LANES:(t + 1) * LANES] + d * cols[t]
            za_ref[pl.ds(row0 + t, n_rows, stride=t_steps), :] = jax.nn.gelu(yt, approximate=True)

    c_per = seq // SSM_T
    for b in range(nseq):
        u = jnp.concatenate(gather(b * seq, c_per, SSM_T), axis=1).astype(BF16)
        x = _dot(u, win_ref[...])
        for k in range(2 * nk):
            x_scr[k, b * c_per:(b + 1) * c_per, :] = x[:, k * LANES:(k + 1) * LANES]

    def step(c, h):
        rows = pl.ds(c, nseq, stride=c_per)
        for k in range(2 * nk):
            hs_scr.at[k][rows, :] = h[k]
        return tuple(advance(h, [x_scr.at[k][rows, :] for k in range(2 * nk)], SSM_T))

    h_last = lax.fori_loop(0, c_per, step, tuple(jnp.zeros((nseq, LANES), F32) for _ in range(2 * nk)),
                           unroll=4)
    hlp_ref[...] = jnp.concatenate(list(h_last), axis=1)
    for b in range(nseq):
        hs = jnp.concatenate([hs_scr[k, b * c_per:(b + 1) * c_per, :] for k in range(2 * nk)], axis=1)
        emit(b * seq, c_per, SSM_T, gather(b * seq, c_per, SSM_T), hs, k_ref[...], wout_ref[...])

    row0, tl = nseq * seq, t_s * LANES
    cols = gather(row0, n_s, t_s)
    u = jnp.concatenate(cols, axis=1).astype(BF16)
    x = _dot(u, win_ref[(SSM_T - t_s) * LANES:, :])
    h0 = h0s_ref[...]
    hls_ref[...] = jnp.concatenate(advance(split(h0), split(x), t_s), axis=1)
    emit(row0, n_s, t_s, cols, h0, k_ref[:tl, :tl], wout_ref[:tl, :])


def _ssm(u8, kmat, win, wout, lp, d8, h0s, *, nseq, seq, n_s, t_s):
    nb, m_total, _ = u8.shape
    ns = STATES_PER_BLOCK
    assert m_total == nseq * seq + n_s * t_s and seq % SSM_T == 0 and t_s <= SSM_T
    rows = nseq * seq // SSM_T
    tl = SSM_T * LANES
    sds = jax.ShapeDtypeStruct
    blk = lambda r, c: pl.BlockSpec((None, r, c), lambda j: (j, 0, 0))
    return pl.pallas_call(
        functools.partial(_ssm_kernel, nseq=nseq, seq=seq, n_s=n_s, t_s=t_s),
        grid=(nb,),
        in_specs=[blk(m_total, LANES), blk(tl, tl), blk(tl, 2 * ns), blk(tl, 2 * ns), blk(SSM_T, 2 * ns),
                  blk(1, LANES), blk(n_s, 2 * ns)],
        out_specs=[blk(m_total, LANES), blk(nseq, 2 * ns), blk(n_s, 2 * ns)],
        out_shape=[sds((nb, m_total, LANES), F32), sds((nb, nseq, 2 * ns), F32), sds((nb, n_s, 2 * ns), F32)],
        scratch_shapes=[pltpu.VMEM((2 * ns // LANES, rows, LANES), F32)] * 2,
        compiler_params=_cparams("parallel"), name="ssm",
    )(u8, kmat, win, wout, lp, d8, h0s)


def _state_to_blocks(re, im):
    n, g, p = re.shape
    nb = g // GROUPS_PER_BLOCK
    f = lambda a: jnp.transpose(a.reshape(n, nb, GROUPS_PER_BLOCK * p), (1, 0, 2))
    return jnp.concatenate([f(re), f(im)], axis=-1)


def _blocks_to_state(h):
    nb, n, w = h.shape
    ns = w // 2
    f = lambda a: jnp.transpose(a, (1, 0, 2)).reshape(n, nb * GROUPS_PER_BLOCK, ns // GROUPS_PER_BLOCK)
    return f(h[..., :ns]), f(h[..., ns:])


def _key_to_float(c):
    return lax.bitcast_convert_type(c ^ ((c >> 31) & 0x7FFFFFFF), F32)


def _kth_largest(count_ge, shape, top_k):
    def body(it, carry):
        u, cnt = carry
        bit = jnp.left_shift(jnp.int32(1), 31 - it)
        cand = u | bit
        c = count_ge(_key_to_float(cand ^ INT_MIN))
        take = c >= top_k
        return jnp.where(take, cand, u), jnp.where(take, c, cnt)
    u, cnt = lax.fori_loop(0, 32, body, (jnp.zeros(shape, jnp.int32), jnp.zeros(shape, jnp.int32)))
    return _key_to_float(u ^ INT_MIN), cnt


TIE_CHUNK = 256


def _tri_ones(n, dtype, lower):
    r = lax.broadcasted_iota(jnp.int32, (n, n), 0)
    c = lax.broadcasted_iota(jnp.int32, (n, n), 1)
    return jnp.where((r >= c) if lower else (r <= c), 1.0, 0.0).astype(dtype)


def _attn_prompt_kernel(qt_ref, kb_ref, vt_ref, qit_ref, kib_ref, wt_ref, o_ref, s_scr, bias_scr,
                        *, seq, top_k, key_chunk):
    qb = pl.program_id(1)
    tq = o_ref.shape[0]

    def run(nk):
        kib = kib_ref[:nk]
        s = jnp.zeros((nk, tq), F32)
        for h in range(IDX_HEADS):
            d = _dot(kib, qit_ref[h * IDX_DIM:(h + 1) * IDX_DIM, :])
            s = s + jnp.maximum(d, 0.0) * wt_ref[IDX_DIM + h:IDX_DIM + h + 1, :]
        kpos = lax.broadcasted_iota(jnp.int32, (nk, tq), 0)
        qpos = qb * tq + lax.broadcasted_iota(jnp.int32, (nk, tq), 1)
        allowed = kpos <= qpos
        s_scr[:nk] = jnp.where(allowed, s, -jnp.inf)

        def count_ge(t):
            n_acc = 8
            acc = [jnp.zeros((SUBLANES, tq), jnp.int32) for _ in range(n_acc)]
            for j in range(nk // SUBLANES):
                rows = s_scr[j * SUBLANES:(j + 1) * SUBLANES]
                acc[j % n_acc] = acc[j % n_acc] + (rows >= t).astype(jnp.int32)
            while len(acc) > 1:
                acc = [a + b for a, b in zip(acc[::2], acc[1::2])]
            return jnp.sum(acc[0], axis=0, keepdims=True)

        thr, cnt = _kth_largest(count_ge, (1, tq), top_k)
        few = qpos < top_k - 1
        q_row = qb * tq + lax.broadcasted_iota(jnp.int32, (1, tq), 1)
        tied = jnp.max(jnp.where((cnt > top_k) & (q_row >= top_k - 1), 1, 0)) > 0

        @pl.when(jnp.logical_not(tied))
        def _():
            bias_scr[:nk] = jnp.where(allowed & ((s_scr[:nk] >= thr) | few), 0.0, -jnp.inf)

        @pl.when(tied)
        def _():
            n_gt = jnp.sum((s_scr[:nk] > thr).astype(jnp.int32), axis=0, keepdims=True)
            need = (top_k - n_gt).astype(F32)
            tri = _tri_ones(TIE_CHUNK, BF16, lower=True)
            carry = jnp.zeros((1, tq), F32)
            for c in range(nk // TIE_CHUNK):
                rs = slice(c * TIE_CHUNK, (c + 1) * TIE_CHUNK)
                sc = s_scr[rs]
                eq = sc == thr
                rank = _dot(tri, jnp.where(eq, 1.0, 0.0).astype(BF16)) + carry
                keep = (sc > thr) | (eq & (rank <= need))
                k_row = c * TIE_CHUNK + lax.broadcasted_iota(jnp.int32, (TIE_CHUNK, tq), 0)
                keep_all = (q_row < top_k - 1) & (k_row <= q_row)
                bias_scr[rs] = jnp.where(keep | keep_all, 0.0, -jnp.inf)
                carry = rank[TIE_CHUNK - 1:, :]

        outs = []
        for g in range(N_KV_HEADS):
            kg = kb_ref[:nk, g * HEAD_DIM:(g + 1) * HEAD_DIM]
            vtg = vt_ref[g * HEAD_DIM:(g + 1) * HEAD_DIM, :nk]
            for r in range(N_REP):
                h = g * N_REP + r
                lg = _dot(kg, qt_ref[h * HEAD_DIM:(h + 1) * HEAD_DIM, :]) + bias_scr[:nk]
                p = jnp.exp(lg - jnp.max(lg, axis=0, keepdims=True))
                den = jnp.sum(p, axis=0, keepdims=True)
                outs.append(_dot(vtg, p.astype(BF16)) / den)
        o_ref[...] = jnp.concatenate(outs, axis=0).T.astype(BF16)

    n_var = seq // key_chunk
    need = (qb * tq + tq + key_chunk - 1) // key_chunk
    for v in range(1, n_var + 1):
        pl.when(need == v)(functools.partial(run, v * key_chunk))


def _attn_prompt(q_t, kb, v_t, qi_t, kib, kiw_t, *, batch, seq, m_out, tq=128, key_chunk=256):
    top_k = min(TOPK_MAX, seq // 4)
    nq = seq // tq
    assert seq % key_chunk == 0 and key_chunk % tq == 0
    aw = N_HEADS * HEAD_DIM
    qcol = lambda n: pl.BlockSpec((n, tq), lambda b, i: (0, b * nq + i))
    return pl.pallas_call(
        functools.partial(_attn_prompt_kernel, seq=seq, top_k=top_k, key_chunk=key_chunk),
        grid=(batch, nq),
        in_specs=[qcol(aw),
                  pl.BlockSpec((seq, N_KV_HEADS * HEAD_DIM), lambda b, i: (b, 0)),
                  pl.BlockSpec((N_KV_HEADS * HEAD_DIM, seq), lambda b, i: (0, b)),
                  qcol(IDX_HEADS * IDX_DIM),
                  pl.BlockSpec((seq, IDX_DIM), lambda b, i: (b, 0)),
                  qcol(LANES)],
        out_specs=pl.BlockSpec((tq, aw), lambda b, i: (b * nq + i, 0)),
        out_shape=jax.ShapeDtypeStruct((m_out, aw), BF16),
        scratch_shapes=[pltpu.VMEM((seq, tq), F32), pltpu.VMEM((seq, tq), F32)],
        compiler_params=_cparams("parallel", "parallel"), name="attn_prompt",
    )(q_t, kb, v_t, qi_t, kib, kiw_t)


def _merge_kernel(za_ref, bm_ref, bt_ref, g_ref, x_ref, gw_ref, gb_ref, wa_ref, wb_ref, wo_ref, o_ref,
                  *, n_main, tm, tail):
    d = x_ref.shape[1]

    def run(rows, b_ref):
        za = jnp.concatenate([za_ref[j, :rows] for j in range(za_ref.shape[0])], axis=1)
        a_out = za * jax.nn.sigmoid(_dot(za.astype(BF16), gw_ref[...]) + gb_ref[...])
        merged = (g_ref[:rows, :d] * _dot(a_out.astype(BF16), wa_ref[...])
                  + g_ref[:rows, d:] * _dot(b_ref[:rows], wb_ref[...]))
        o_ref[:rows] = x_ref[:rows] + _dot(merged.astype(BF16), wo_ref[...])

    i = pl.program_id(0)
    pl.when(i < n_main)(lambda: run(tm, bm_ref))
    pl.when(i == n_main)(lambda: run(tail, bt_ref))


def _merge(za8, b_main, b_tail, gates, x, glu_w, glu_b, wa, wb, wo, *, tm):
    m, d = x.shape
    nb = za8.shape[0]
    n_main_rows, aw = b_main.shape
    tail = b_tail.shape[0]
    n_main = n_main_rows // tm
    assert n_main * tm == n_main_rows and n_main_rows + tail == m and tail <= tm
    resident = lambda shape: pl.BlockSpec(shape, lambda i: (0,) * len(shape), pipeline_mode=pl.Buffered(1))
    return pl.pallas_call(
        functools.partial(_merge_kernel, n_main=n_main, tm=tm, tail=tail), grid=(n_main + 1,),
        in_specs=[pl.BlockSpec((nb, tm, LANES), lambda i: (0, i, 0)),
                  pl.BlockSpec((tm, aw), lambda i: (jnp.minimum(i, n_main - 1), 0)),
                  pl.BlockSpec((tail, aw), lambda i: (0, 0)),
                  _row_spec(tm, 2 * d), _row_spec(tm, d), resident(glu_w.shape), resident((1, glu_w.shape[1])),
                  resident(wa.shape), resident(wb.shape), resident(wo.shape)],
        out_specs=_row_spec(tm, d),
        out_shape=jax.ShapeDtypeStruct((m, d), F32),
        compiler_params=_cparams("arbitrary"), name="merge",
    )(za8, b_main, b_tail, gates, x, glu_w, glu_b.reshape(1, -1), wa, wb, wo)


QPAD = SUBLANES
CHUNK_PAGES = 8


def _page_copy(pt_ref, src_ref, buf_ref, sem, seq, slot, p, page_base, rows):
    start = pl.multiple_of((page_base + pt_ref[seq, p]) * rows, rows)
    return pltpu.make_async_copy(src_ref.at[pl.ds(start, rows), :],
                                 buf_ref.at[slot, pl.ds(p * rows, rows), :], sem)


def _prefetch_pages(n_pages, copies):
    s, n_seq = pl.program_id(0), pl.num_programs(0)
    slot = s % 2

    def start_all(seq, sl):
        def body(p, carry):
            for copy in copies:
                copy(seq, sl, p).start()
            return carry
        lax.fori_loop(0, n_pages, body, 0, unroll=8)

    @pl.when(s == 0)
    def _():
        start_all(0, 0)

    @pl.when(s + 1 < n_seq)
    def _():
        start_all(s + 1, 1 - slot)

    def wait_body(p, carry):
        for copy in copies:
            copy(s, slot, p).wait()
        return carry
    lax.fori_loop(0, n_pages, wait_body, 0, unroll=8)
    return slot


def _sample_scores_kernel(pt_ref, qs_ref, ws_ref, kn_ref, cik_ref, sp_ref, sn_ref, buf_ref, sem_ref,
                          *, page_base, n_pages):
    slot = _prefetch_pages(n_pages, [
        lambda seq, sl, p: _page_copy(pt_ref, cik_ref, buf_ref, sem_ref.at[sl], seq, sl, p, page_base, IDX_DIM)])
    qs, ws = qs_ref[...], ws_ref[:, :1]

    def scores(d):
        r = jnp.maximum(d, 0.0) * ws
        return r.reshape(IDX_HEADS, QPAD, r.shape[1]).sum(axis=0)

    nkc = CHUNK_PAGES * PAGE_SIZE
    for c in range(n_pages // CHUNK_PAGES):
        keys_t = jnp.concatenate(
            [buf_ref[slot, (c * CHUNK_PAGES + kk) * IDX_DIM:(c * CHUNK_PAGES + kk + 1) * IDX_DIM, :]
             for kk in range(CHUNK_PAGES)], axis=1).astype(BF16)
        sp_ref[:, c * nkc:(c + 1) * nkc] = scores(_dot(qs, keys_t))
    sn_ref[...] = scores(_dot_nt(qs, kn_ref[...]))


def _sample_thr_kernel(sp_ref, sn_ref, thr_ref, cut_ref, s_scr, *, past, top_k, t_len):
    s = jnp.concatenate([sp_ref[...], sn_ref[...]], axis=1)
    shape = s.shape
    kpos = lax.broadcasted_iota(jnp.int32, shape, 1)
    qidx = lax.broadcasted_iota(jnp.int32, shape, 0) % t_len
    s_scr[...] = jnp.where(kpos <= past + qidx, s, -jnp.inf)

    def count_ge(t):
        return jnp.sum((s_scr[...] >= t).astype(jnp.int32), axis=1, keepdims=True)

    thr, cnt = _kth_largest(count_ge, (shape[0], 1), top_k)
    thr_ref[...] = jnp.broadcast_to(thr, thr_ref.shape)
    tied = jnp.max(jnp.where(cnt > top_k, 1, 0)) > 0

    @pl.when(jnp.logical_not(tied))
    def _():
        cut_ref[...] = jnp.full(cut_ref.shape, shape[1], jnp.int32)

    @pl.when(tied)
    def _():
        n_gt = jnp.sum((s_scr[...] > thr).astype(jnp.int32), axis=1, keepdims=True)
        need = (top_k - n_gt).astype(F32)
        tri = _tri_ones(LANES, BF16, lower=False)
        carry = jnp.zeros((shape[0], 1), F32)
        cut = jnp.full((shape[0], 1), -1, jnp.int32)
        lane = lax.broadcasted_iota(jnp.int32, (shape[0], LANES), 1)
        for c in range(shape[1] // LANES):
            eq = s_scr[:, c * LANES:(c + 1) * LANES] == thr
            rank = _dot(jnp.where(eq, 1.0, 0.0).astype(BF16), tri) + carry
            last = jnp.max(jnp.where(eq & (rank <= need), lane + c * LANES, -1), axis=1, keepdims=True)
            cut = jnp.maximum(cut, last)
            carry = rank[:, LANES - 1:]
        cut_ref[...] = jnp.broadcast_to(cut, cut_ref.shape)


def _sample_attn_kernel(pt_ref, q_ref, sp_ref, sn_ref, thr_ref, cut_ref, kn_ref, vn_ref, ck_ref, cv_ref, o_ref,
                        kbuf, vbuf, sem_ref, lg_scr, *, page_base, n_pages):
    page_rows = N_KV_HEADS * PAGE_SIZE
    slot = _prefetch_pages(n_pages, [
        lambda seq, sl, p: _page_copy(pt_ref, ck_ref, kbuf, sem_ref.at[0, sl], seq, sl, p, page_base, page_rows),
        lambda seq, sl, p: _page_copy(pt_ref, cv_ref, vbuf, sem_ref.at[1, sl], seq, sl, p, page_base, page_rows)])
    rows_g = N_REP * QPAD
    nkc = CHUNK_PAGES * PAGE_SIZE
    n_chunks = n_pages // CHUNK_PAGES
    past = n_pages * PAGE_SIZE
    thr, cut = thr_ref[:, :1], cut_ref[:, :1]
    q = q_ref[...]
    qg = [q[g * rows_g:(g + 1) * rows_g] for g in range(N_KV_HEADS)]
    head_rows = lambda c, g: pl.ds(c * CHUNK_PAGES * page_rows + g, nkc, stride=N_KV_HEADS)

    def bias(scores, pos0, allowed=None):
        kpos = pos0 + lax.broadcasted_iota(jnp.int32, scores.shape, 1)
        sel = (scores > thr) | ((scores == thr) & (kpos <= cut))
        if allowed is not None:
            sel = sel & allowed
        return jnp.concatenate([jnp.where(sel, 0.0, -jnp.inf)] * N_HEADS, axis=0)

    for c in range(n_chunks):
        lg = jnp.concatenate([_dot_nt(qg[g], kbuf.at[slot][head_rows(c, g), :].astype(BF16))
                              for g in range(N_KV_HEADS)], axis=0)
        lg_scr[:, c * nkc:(c + 1) * nkc] = lg + bias(sp_ref[:, c * nkc:(c + 1) * nkc], c * nkc)
    shape = sn_ref.shape
    allowed = lax.broadcasted_iota(jnp.int32, shape, 1) <= lax.broadcasted_iota(jnp.int32, shape, 0)
    lg = jnp.concatenate([_dot_nt(qg[g], kn_ref[:, g * HEAD_DIM:(g + 1) * HEAD_DIM])
                          for g in range(N_KV_HEADS)], axis=0)
    lg_scr[:, past:] = lg + bias(sn_ref[...], past, allowed)

    lg = lg_scr[...]
    p = jnp.exp(lg - jnp.max(lg, axis=1, keepdims=True))
    den = jnp.sum(p, axis=1, keepdims=True)
    pb = p.astype(BF16)
    acc = jnp.concatenate([_dot(pb[g * rows_g:(g + 1) * rows_g, past:], vn_ref[:, g * HEAD_DIM:(g + 1) * HEAD_DIM])
                           for g in range(N_KV_HEADS)], axis=0)
    for c in range(n_chunks):
        acc = acc + jnp.concatenate(
            [_dot(pb[g * rows_g:(g + 1) * rows_g, c * nkc:(c + 1) * nkc],
                  vbuf.at[slot][head_rows(c, g), :].astype(BF16)) for g in range(N_KV_HEADS)], axis=0)
    o_ref[...] = acc / den


def _pad_queries(x, t_len):
    n, h, d = x.shape
    x = jnp.transpose(x.reshape(n // t_len, t_len, h, d), (0, 2, 1, 3))
    x = jnp.pad(x, ((0, 0), (0, 0), (0, QPAD - t_len), (0, 0)))
    return x.reshape(n // t_len, h * QPAD, d)


def _pad_new_keys(x, t_len):
    n, w = x.shape
    return jnp.pad(x.reshape(n // t_len, t_len, w), ((0, 0), (0, PAGE_SIZE - t_len), (0, 0)))


def _attn_sample(q_s, k_new, v_new, qi_s, wi_s, ki_new, cache_k, cache_v, cache_idx_k, page_table, page_base,
                 *, t_len):
    db, n_pages = page_table.shape
    past = n_pages * PAGE_SIZE
    top_k = min(TOPK_MAX, (past + t_len) // 4)
    kvw = N_KV_HEADS * HEAD_DIM
    sds = jax.ShapeDtypeStruct
    q = _pad_queries(q_s.reshape(-1, N_HEADS, HEAD_DIM), t_len)
    qi = _pad_queries(qi_s.reshape(-1, IDX_HEADS, IDX_DIM), t_len)
    ws = _pad_queries(wi_s.reshape(-1, IDX_HEADS, 1), t_len)
    ws = jnp.broadcast_to(ws, ws.shape[:2] + (PAGE_SIZE,))
    kn, vn, kin = (_pad_new_keys(a, t_len) for a in (k_new, v_new, ki_new))

    assert n_pages % CHUNK_PAGES == 0
    seq_spec = lambda r, c: pl.BlockSpec((None, r, c), lambda s, pt: (s, 0, 0))
    hbm = pl.BlockSpec(memory_space=pl.ANY)
    dma = pltpu.SemaphoreType.DMA
    in_order = _cparams("arbitrary")
    sp, sn = pl.pallas_call(
        functools.partial(_sample_scores_kernel, page_base=page_base, n_pages=n_pages),
        grid_spec=pltpu.PrefetchScalarGridSpec(
            num_scalar_prefetch=1, grid=(db,),
            in_specs=[seq_spec(IDX_HEADS * QPAD, IDX_DIM), seq_spec(IDX_HEADS * QPAD, PAGE_SIZE),
                      seq_spec(PAGE_SIZE, IDX_DIM), hbm],
            out_specs=[seq_spec(QPAD, past), seq_spec(QPAD, PAGE_SIZE)],
            scratch_shapes=[pltpu.VMEM((2, n_pages * IDX_DIM, PAGE_SIZE), F32), dma((2,))]),
        out_shape=[sds((db, QPAD, past), F32), sds((db, QPAD, PAGE_SIZE), F32)],
        compiler_params=in_order, name="sample_scores",
    )(page_table, qi, ws, kin, cache_idx_k.reshape(-1, PAGE_SIZE))

    rows = db * t_len
    rblk = min(rows, 8 * SUBLANES)
    assert rows % rblk == 0
    real = lambda a: a[:, :t_len].reshape(rows, a.shape[2])
    padded = lambda a: jnp.pad(a.reshape(db, t_len, LANES), ((0, 0), (0, QPAD - t_len), (0, 0)))
    thr, cut = pl.pallas_call(
        functools.partial(_sample_thr_kernel, past=past, top_k=top_k, t_len=t_len),
        grid=(rows // rblk,),
        in_specs=[_row_spec(rblk, past), _row_spec(rblk, PAGE_SIZE)],
        out_specs=[_row_spec(rblk, LANES), _row_spec(rblk, LANES)],
        out_shape=[sds((rows, LANES), F32), sds((rows, LANES), jnp.int32)],
        scratch_shapes=[pltpu.VMEM((rblk, past + PAGE_SIZE), F32)],
        compiler_params=_cparams("parallel"), name="sample_threshold",
    )(real(sp), real(sn))

    hq = N_HEADS * QPAD
    page_rows = N_KV_HEADS * PAGE_SIZE
    out = pl.pallas_call(
        functools.partial(_sample_attn_kernel, page_base=page_base, n_pages=n_pages),
        grid_spec=pltpu.PrefetchScalarGridSpec(
            num_scalar_prefetch=1, grid=(db,),
            in_specs=[seq_spec(hq, HEAD_DIM), seq_spec(QPAD, past), seq_spec(QPAD, PAGE_SIZE), seq_spec(QPAD, LANES),
                      seq_spec(QPAD, LANES), seq_spec(PAGE_SIZE, kvw), seq_spec(PAGE_SIZE, kvw), hbm, hbm],
            out_specs=seq_spec(hq, HEAD_DIM),
            scratch_shapes=[pltpu.VMEM((2, n_pages * page_rows, HEAD_DIM), F32),
                            pltpu.VMEM((2, n_pages * page_rows, HEAD_DIM), F32),
                            dma((2, 2)), pltpu.VMEM((hq, past + PAGE_SIZE), F32)]),
        out_shape=sds((db, hq, HEAD_DIM), F32),
        compiler_params=in_order, name="sample_attn",
    )(page_table, q, sp, sn, padded(thr), padded(cut), kn, vn, cache_k, cache_v)
    out = out.reshape(db, N_HEADS, QPAD, HEAD_DIM)[:, :, :t_len]
    return jnp.transpose(out, (0, 2, 1, 3)).reshape(db * t_len, N_HEADS * HEAD_DIM)


GATE_ROW_TILE = 640
PROJ_ROW_TILE = 512
FFN_ROW_TILE = 512
FF_TILE = 512
MERGE_ROW_TILE = 256


def _layer(x_p, x_s, pos, dims, layer, w_in_all, ck, cv, cik, page_base, s_re, s_im, page_table, p):
    b, s, db, t = dims
    mp, ms = b * s, db * t
    d = x_p.shape[1]
    bf = lambda w: w.astype(BF16)
    ssm_w = d // 2
    w_g = bf(w_in_all[layer, :, w_in_all.shape[2] - 2 * d:])

    x1, xn = _ffn(x_p, x_s, p["ffn1_norm"], bf(p["ffn1_w_gate"]), bf(p["ffn1_w_up"]), bf(p["ffn1_w_down"]),
                  p["mix_norm"], n_main_rows=mp, tail=ms, split_out=False, tm=FFN_ROW_TILE, tf=FF_TILE)
    pr = _projections(xn, jnp.swapaxes(w_in_all, 1, 2), layer, w_g, p["q_norm"], p["k_norm"], pos,
                      n_main_rows=mp, seq=s, tail=ms, tm=PROJ_ROW_TILE, tm_gates=GATE_ROW_TILE)

    kmat, win, wout, lp = _ssm_prep(p["ssm_lambda_re"], p["ssm_lambda_im"], p["ssm_b_re"], p["ssm_b_im"],
                                    p["ssm_c_re"], p["ssm_c_im"], p["ssm_log_dt"])
    nblk = ssm_w // LANES
    d8 = p["ssm_d"].reshape(nblk, 1, LANES)
    za8, hl_p, hl_s = _ssm(pr["u8"], kmat, win, wout, lp, d8, _state_to_blocks(s_re, s_im),
                           nseq=b, seq=s, n_s=db, t_s=t)

    b_p = _attn_prompt(pr["q_t"], pr["kb"], pr["v_t"], pr["qi_t"], pr["kib"], pr["kiw_t"],
                       batch=b, seq=s, m_out=mp)
    b_s = _attn_sample(pr["q"][mp:], pr["kb"][mp:], bf(pr["v_s"].reshape(ms, -1)), pr["qi"][mp:],
                       pr["kiw"][mp:, IDX_DIM:IDX_DIM + IDX_HEADS], pr["kib"][mp:],
                       ck, cv, cik, page_table, page_base, t_len=t)

    x2 = _merge(za8, b_p, bf(b_s), pr["gates"], x1, bf(p["glu_w"]), p["glu_b"], bf(p["w_branch_a"]),
                bf(p["w_branch_b"]), bf(p["w_out"]), tm=MERGE_ROW_TILE)
    y_p, y_s = _ffn(x2, None, p["ffn2_norm"], bf(p["ffn2_w_gate"]), bf(p["ffn2_w_up"]), bf(p["ffn2_w_down"]),
                    n_main_rows=mp, tail=ms, split_out=True, tm=FFN_ROW_TILE, tf=FF_TILE)

    hp_re, hp_im = _blocks_to_state(hl_p)
    hs_re, hs_im = _blocks_to_state(hl_s)
    ki = pr["kiw"][:, :IDX_DIM]
    kvs = (N_KV_HEADS, HEAD_DIM)
    rows = (pr["k_p"].reshape(b, s, *kvs), pr["v_p"].reshape(b, s, *kvs), ki[:mp].reshape(b, s, IDX_DIM),
            hp_re, hp_im,
            pr["k_s"].reshape(db, t, *kvs), pr["v_s"].reshape(db, t, *kvs), ki[mp:].reshape(db, t, IDX_DIM),
            hs_re, hs_im)
    return y_p, y_s, rows


def kernel(x_prompt, x_sample, cache_k, cache_v, cache_idx_k, state_ssm_re, state_ssm_im, page_table,
           ffn1_norm, ffn1_w_gate, ffn1_w_up, ffn1_w_down, mix_norm, w_in, q_norm, k_norm,
           ssm_lambda_re, ssm_lambda_im, ssm_b_re, ssm_b_im, ssm_c_re, ssm_c_im, ssm_d, ssm_log_dt,
           glu_w, glu_b, w_branch_a, w_branch_b, w_out, ffn2_norm, ffn2_w_gate, ffn2_w_up, ffn2_w_down):
    b, s, d = x_prompt.shape
    db, t, _ = x_sample.shape
    depth, n_phys = cache_k.shape[:2]
    past = page_table.shape[1] * PAGE_SIZE
    x_p, x_s = x_prompt.reshape(b * s, d), x_sample.reshape(db * t, d)
    pos = jnp.concatenate([jnp.arange(s), jnp.tile(past + jnp.arange(t), db)])
    ck = cache_k.reshape(-1, HEAD_DIM)
    cv = cache_v.reshape(-1, HEAD_DIM)
    cik = jnp.swapaxes(cache_idx_k, 2, 3).reshape(depth * n_phys, IDX_DIM, PAGE_SIZE)
    params = dict(
        ffn1_norm=ffn1_norm, ffn1_w_gate=ffn1_w_gate, ffn1_w_up=ffn1_w_up, ffn1_w_down=ffn1_w_down,
        mix_norm=mix_norm, w_in=w_in, q_norm=q_norm, k_norm=k_norm,
        ssm_lambda_re=ssm_lambda_re, ssm_lambda_im=ssm_lambda_im, ssm_b_re=ssm_b_re, ssm_b_im=ssm_b_im,
        ssm_c_re=ssm_c_re, ssm_c_im=ssm_c_im, ssm_d=ssm_d, ssm_log_dt=ssm_log_dt, glu_w=glu_w, glu_b=glu_b,
        w_branch_a=w_branch_a, w_branch_b=w_branch_b, w_out=w_out,
        ffn2_norm=ffn2_norm, ffn2_w_gate=ffn2_w_gate, ffn2_w_up=ffn2_w_up, ffn2_w_down=ffn2_w_down)
    new = [[] for _ in range(10)]
    for l in range(depth):
        p = {name: w[l] for name, w in params.items()}
        x_p, x_s, rows = _layer(x_p, x_s, pos, (b, s, db, t), l, w_in, ck, cv, cik, l * n_phys,
                                state_ssm_re[l], state_ssm_im[l], page_table, p)
        for lst, r in zip(new, rows):
            lst.append(r)
    return (x_p.reshape(b, s, d), x_s.reshape(db, t, d)) + tuple(jnp.stack(lst) for lst in new)
```

```python
import functools

import jax
import jax.numpy as jnp
from jax import lax
from jax.experimental import pallas as pl
from jax.experimental.pallas import tpu as pltpu

F32 = jnp.float32
BF16 = jnp.bfloat16

SSM_GROUP = 16
SSM_STATE = 64
N_HEADS = 8
HEAD_DIM = 128
N_KV_HEADS = 2
N_REP = N_HEADS // N_KV_HEADS
ROT_DIM = HEAD_DIM // 4
ROPE_THETA = 500000.0
IDX_HEADS = 16
IDX_DIM = 64
IDX_ROT_DIM = IDX_DIM // 4
TOPK_MAX = 256
PAGE_SIZE = 128
FFN_RES = 0.5
EPS = 1e-6

LANES = 128
SUBLANES = 8
VMEM_LIMIT_BYTES = 56 * 1024 * 1024

GROUPS_PER_BLOCK = LANES // SSM_GROUP
STATES_PER_BLOCK = GROUPS_PER_BLOCK * SSM_STATE
SSM_T = 8

INT_MIN = -(2 ** 31)


def _cparams(*sem):
    return pltpu.CompilerParams(dimension_semantics=sem, vmem_limit_bytes=VMEM_LIMIT_BYTES)


def _rms(x, g):
    return x * lax.rsqrt(jnp.mean(x * x, axis=-1, keepdims=True) + EPS) * g


def _dot(a, b):
    return jnp.dot(a, b, preferred_element_type=F32)


def _dot_nt(a, b):
    return lax.dot_general(a, b, (((1,), (1,)), ((), ())), preferred_element_type=F32)


def _ffn_kernel(*refs, nf, n_main, tm, tail, two_src, split_out, with_next_norm):
    refs = list(refs)
    x_main_ref = refs.pop(0)
    x_tail_ref = refs.pop(0) if two_src else x_main_ref
    g_ref, wg_ref, wu_ref, wd_ref = (refs.pop(0) for _ in range(4))
    g2_ref = refs.pop(0) if with_next_norm else None
    y_main_ref = refs.pop(0)
    y_tail_ref = refs.pop(0) if split_out else y_main_ref
    n2_ref = refs.pop(0) if with_next_norm else None
    xn_ref, acc_ref = refs
    i, f = pl.program_id(0), pl.program_id(1)

    def run(rows, x_ref, y_ref):
        @pl.when(f == 0)
        def _():
            xn_ref[:rows] = _rms(x_ref[:rows], g_ref[...]).astype(BF16)
            acc_ref[:rows] = jnp.zeros((rows, acc_ref.shape[1]), F32)

        xn = xn_ref[:rows]
        a = _dot(xn, wg_ref[...])
        b = _dot(xn, wu_ref[...])
        h = (a * jax.nn.sigmoid(a) * b).astype(BF16)
        acc_ref[:rows] += _dot(h, wd_ref[...])

        @pl.when(f == nf - 1)
        def _():
            y = x_ref[:rows] + FFN_RES * acc_ref[:rows]
            y_ref[:rows] = y
            if with_next_norm:
                n2_ref[:rows] = _rms(y, g2_ref[...]).astype(BF16)

    pl.when(i < n_main)(lambda: run(tm, x_main_ref, y_main_ref))
    pl.when(i == n_main)(lambda: run(tail, x_tail_ref, y_tail_ref))


def _ffn(x_main, x_tail, g, wg, wu, wd, g2=None, *, n_main_rows, tail, split_out, tm, tf):
    d = x_main.shape[1]
    nf = wg.shape[1] // tf
    n_main = n_main_rows // tm
    m = n_main_rows + tail
    two_src = x_tail is not None
    with_next = g2 is not None
    clamp = lambda i, f: (jnp.minimum(i, n_main - 1), 0)
    rows = lambda i, f: (i, 0)
    first = lambda i, f: (0, 0)
    sds = jax.ShapeDtypeStruct
    in_specs = [pl.BlockSpec((tm, d), clamp if two_src else rows)]
    args = [x_main]
    if two_src:
        in_specs.append(pl.BlockSpec((tail, d), first))
        args.append(x_tail)
    in_specs += [pl.BlockSpec((1, d), first), pl.BlockSpec((d, tf), lambda i, f: (0, f)),
                 pl.BlockSpec((d, tf), lambda i, f: (0, f)), pl.BlockSpec((tf, d), lambda i, f: (f, 0))]
    args += [g.reshape(1, d), wg, wu, wd]
    if with_next:
        in_specs.append(pl.BlockSpec((1, d), first))
        args.append(g2.reshape(1, d))
    if split_out:
        out_shape = [sds((n_main_rows, d), F32), sds((tail, d), F32)]
        out_specs = [pl.BlockSpec((tm, d), clamp), pl.BlockSpec((tail, d), first)]
    else:
        out_shape = [sds((m, d), F32)]
        out_specs = [pl.BlockSpec((tm, d), rows)]
    if with_next:
        out_shape.append(sds((m, d), BF16))
        out_specs.append(pl.BlockSpec((tm, d), rows))
    return pl.pallas_call(
        functools.partial(_ffn_kernel, nf=nf, n_main=n_main, tm=tm, tail=tail, two_src=two_src,
                          split_out=split_out, with_next_norm=with_next),
        grid=(n_main + 1, nf),
        in_specs=in_specs,
        out_specs=out_specs,
        out_shape=out_shape,
        scratch_shapes=[pltpu.VMEM((tm, d), BF16), pltpu.VMEM((tm, d), F32)],
        compiler_params=_cparams("arbitrary", "arbitrary"),
        name="ffn" + ("_norm" if with_next else ""),
    )(*args)


def _rope_tables(pos, rot_dim, width, tile):
    half = rot_dim // 2
    m = pos.shape[0]
    freqs = ROPE_THETA ** (-jnp.arange(half, dtype=F32) * 2.0 / rot_dim)
    ang = pos.astype(F32)[:, None] * freqs[None, :]
    cos, sin = jnp.cos(ang), jnp.sin(ang)
    zh = jnp.zeros((m, half), F32)
    rest = width - rot_dim
    c = jnp.concatenate([cos, cos, jnp.ones((m, rest), F32)], axis=1)
    s1 = jnp.concatenate([-sin, zh, jnp.zeros((m, rest), F32)], axis=1)
    s2 = jnp.concatenate([zh, sin, jnp.zeros((m, rest), F32)], axis=1)
    if tile:
        reps = LANES // width
        return tuple(jnp.tile(t, (1, reps)) for t in (c, s1, s2))
    pad = LANES - width
    return (jnp.pad(c, ((0, 0), (0, pad)), constant_values=1.0),
            jnp.pad(s1, ((0, 0), (0, pad))), jnp.pad(s2, ((0, 0), (0, pad))))


def _rope(x, c, s1, s2, half):
    return x * c + pltpu.roll(x, LANES - half, 1) * s1 + pltpu.roll(x, half, 1) * s2


def _proj_rows(xn_ref, w_refs, wbf_ref, n_main, tm, tail, chunk, epilogue):
    i = pl.program_id(0)

    @pl.when(i == 0)
    def _():
        off = 0
        for w_ref in w_refs:
            n = w_ref.shape[0]
            wbf_ref[:, off:off + n] = w_ref[...].astype(F32).T.astype(BF16)
            off += n

    def run(rows):
        for r0 in range(0, rows, chunk):
            rs = slice(r0, min(r0 + chunk, rows))
            epilogue(_dot(xn_ref[rs, :], wbf_ref[...]), rs, rows == tail)

    pl.when(i < n_main)(lambda: run(tm))
    pl.when(i == n_main)(lambda: run(tail))


def _u_proj_kernel(xn_ref, w_ref, u_ref, wbf_ref, **kw):
    def epilogue(z, rs, is_tail):
        for j in range(z.shape[1] // LANES):
            u_ref[j, rs, :] = z[:, j * LANES:(j + 1) * LANES]
    _proj_rows(xn_ref, [w_ref], wbf_ref, epilogue=epilogue, **kw)


def _q_proj_kernel(xn_ref, w_ref, g_ref, c_ref, s1_ref, s2_ref, qn_ref, qt_ref, wbf_ref, **kw):
    def epilogue(z, rs, is_tail):
        c, s1, s2, g = c_ref[rs, :], s1_ref[rs, :], s2_ref[rs, :], g_ref[...]
        heads = []
        for h in range(N_HEADS):
            x = _rms(z[:, h * HEAD_DIM:(h + 1) * HEAD_DIM], g)
            heads.append(_rope(x, c, s1, s2, ROT_DIM // 2) * (HEAD_DIM ** -0.5))
        q = jnp.concatenate(heads, axis=1)
        qn_ref[rs, :] = q.astype(BF16)
        qt_ref[:, rs] = q.T.astype(BF16)
    _proj_rows(xn_ref, [w_ref], wbf_ref, epilogue=epilogue, **kw)


def _kv_proj_kernel(xn_ref, wkv_ref, wkw_ref, g_ref, c_ref, s1_ref, s2_ref, ci_ref, si1_ref, si2_ref, scale_ref,
                    kp_ref, ks_ref, vp_ref, vs_ref, kb_ref, vt_ref, kiw_ref, kiwt_ref, kib_ref, wbf_ref, **kw):
    kvw = N_KV_HEADS * HEAD_DIM

    def epilogue(z, rs, is_tail):
        c, s1, s2, g = c_ref[rs, :], s1_ref[rs, :], s2_ref[rs, :], g_ref[...]
        k_ref, v_ref = (ks_ref, vs_ref) if is_tail else (kp_ref, vp_ref)
        n = rs.stop - rs.start
        heads = []
        for h in range(N_KV_HEADS):
            x = _rope(_rms(z[:, h * HEAD_DIM:(h + 1) * HEAD_DIM], g), c, s1, s2, ROT_DIM // 2)
            rows_h = pl.ds(N_KV_HEADS * rs.start + h, n, stride=N_KV_HEADS)
            k_ref[rows_h, :] = x
            v_ref[rows_h, :] = z[:, kvw + h * HEAD_DIM:kvw + (h + 1) * HEAD_DIM]
            heads.append(x)
        kb_ref[rs, :] = jnp.concatenate(heads, axis=1).astype(BF16)
        vt_ref[:, rs] = z[:, kvw:2 * kvw].T.astype(BF16)
        y = _rope(z[:, 2 * kvw:], ci_ref[rs, :], si1_ref[rs, :], si2_ref[rs, :], IDX_ROT_DIM // 2) * scale_ref[...]
        kiw_ref[rs, :] = y
        kiwt_ref[:, rs] = y.T
        kib_ref[rs, :] = y[:, :IDX_DIM].astype(BF16)
    _proj_rows(xn_ref, [wkv_ref, wkw_ref], wbf_ref, epilogue=epilogue, **kw)


def _qi_proj_kernel(xn_ref, wa_ref, wb_ref, c_ref, s1_ref, s2_ref, qn_ref, qt_ref, wbf_ref, **kw):
    def epilogue(z, rs, is_tail):
        c, s1, s2 = c_ref[rs, :], s1_ref[rs, :], s2_ref[rs, :]
        cols = []
        for j in range(z.shape[1] // LANES):
            x = z[:, j * LANES:(j + 1) * LANES]
            cols.append(_rope(x, c, s1, s2, IDX_ROT_DIM // 2) * (IDX_DIM ** -0.5))
        q = jnp.concatenate(cols, axis=1)
        qn_ref[rs, :] = q.astype(BF16)
        qt_ref[:, rs] = q.T.astype(BF16)
    _proj_rows(xn_ref, [wa_ref, wb_ref], wbf_ref, epilogue=epilogue, **kw)


def _gate_proj_kernel(xn_ref, w_ref, o_ref, *, chunk):
    for r0 in range(0, xn_ref.shape[0], chunk):
        rs = slice(r0, r0 + chunk)
        o_ref[rs, :] = jax.nn.sigmoid(_dot(xn_ref[rs, :], w_ref[...])).astype(BF16)


def _row_spec(tm, n):
    return pl.BlockSpec((tm, n), lambda i: (i, 0))


def _full_spec(shape):
    return pl.BlockSpec(shape, lambda i: (0,) * len(shape))


PROJ_CHUNK = 256


def _projections(xn, w_in_t, layer, w_g, q_norm, k_norm, pos, *, n_main_rows, seq, tail, tm, tm_gates):
    m, d = xn.shape
    n_main = n_main_rows // tm
    assert n_main * tm == n_main_rows and n_main_rows + tail == m and tail <= tm
    grid = (n_main + 1,)
    ssm_w, attn_w, kv_w, idx_w = d // 2, N_HEADS * HEAD_DIM, N_KV_HEADS * HEAD_DIM, IDX_HEADS * IDX_DIM
    hd_tabs = _rope_tables(pos, ROT_DIM, HEAD_DIM, True)
    ix_tabs = _rope_tables(pos, IDX_ROT_DIM, IDX_DIM, True)
    kw_tabs = _rope_tables(pos, IDX_ROT_DIM, IDX_DIM, False)
    assert seq % tm == 0 and pos.shape[0] == seq + tail
    tabs_per_seq = seq // tm
    tab_specs = [pl.BlockSpec((tm, LANES), lambda i: (jnp.where(i < n_main, i % tabs_per_seq, tabs_per_seq), 0))] * 3
    xs = _row_spec(tm, d)
    sds = jax.ShapeDtypeStruct
    seq = _cparams("arbitrary")
    kw = dict(n_main=n_main, tm=tm, tail=tail, chunk=PROJ_CHUNK)

    def wcols(start, width):
        assert start % width == 0
        return pl.BlockSpec((None, width, d), lambda i: (layer, start // width, 0), pipeline_mode=pl.Buffered(1))

    def call(kernel, name, w_specs, n_w, extra_specs, extra_args, out_specs, out_shape):
        return pl.pallas_call(
            functools.partial(kernel, **kw), grid=grid,
            in_specs=[xs] + w_specs + extra_specs, out_specs=out_specs, out_shape=out_shape,
            scratch_shapes=[pltpu.VMEM((d, n_w), BF16)], compiler_params=seq, name=name,
        )(xn, *([w_in_t] * len(w_specs)), *extra_args)

    tcol = lambda n: pl.BlockSpec((n, tm), lambda i: (0, i))
    norm_spec = _full_spec((1, HEAD_DIM))
    u8 = call(_u_proj_kernel, "proj_u", [wcols(0, ssm_w)], ssm_w, [], [],
              pl.BlockSpec((ssm_w // LANES, tm, LANES), lambda i: (0, i, 0)), sds((ssm_w // LANES, m, LANES), F32))
    q_nat, q_t = call(_q_proj_kernel, "proj_q", [wcols(ssm_w, attn_w)], attn_w,
                      [norm_spec] + tab_specs, [q_norm.reshape(1, HEAD_DIM), *hd_tabs],
                      [_row_spec(tm, attn_w), tcol(attn_w)], [sds((m, attn_w), BF16), sds((attn_w, m), BF16)])
    off_kv = ssm_w + attn_w
    off_qi = off_kv + 2 * kv_w
    off_kw = off_qi + idx_w
    lane = jnp.arange(LANES)
    kw_scale = jnp.where(lane < IDX_DIM, 1.0, jnp.where(lane < IDX_DIM + IDX_HEADS, IDX_HEADS ** -0.5, 0.0))
    nh = N_KV_HEADS
    head_main = pl.BlockSpec((nh * tm, HEAD_DIM), lambda i: (jnp.minimum(i, n_main - 1), 0))
    head_tail = pl.BlockSpec((nh * tail, HEAD_DIM), lambda i: (0, 0))
    k_p, k_s, v_p, v_s, kb, v_t, kiw, kiw_t, kib = call(
        _kv_proj_kernel, "proj_kv", [wcols(off_kv, 2 * kv_w), wcols(off_kw, LANES)], 2 * kv_w + LANES,
        [norm_spec] + tab_specs + tab_specs + [_full_spec((1, LANES))],
        [k_norm.reshape(1, HEAD_DIM), *hd_tabs, *kw_tabs, kw_scale.astype(F32).reshape(1, LANES)],
        [head_main, head_tail, head_main, head_tail, _row_spec(tm, kv_w), tcol(kv_w),
         _row_spec(tm, LANES), tcol(LANES), _row_spec(tm, IDX_DIM)],
        [sds((nh * n_main_rows, HEAD_DIM), F32), sds((nh * tail, HEAD_DIM), F32),
         sds((nh * n_main_rows, HEAD_DIM), F32), sds((nh * tail, HEAD_DIM), F32),
         sds((m, kv_w), BF16), sds((kv_w, m), BF16),
         sds((m, LANES), F32), sds((LANES, m), F32), sds((m, IDX_DIM), BF16)])
    half = idx_w // 2
    qi_nat, qi_t = call(_qi_proj_kernel, "proj_qi", [wcols(off_qi, half), wcols(off_qi + half, half)], idx_w,
                        tab_specs, ix_tabs,
                        [_row_spec(tm, idx_w), tcol(idx_w)], [sds((m, idx_w), BF16), sds((idx_w, m), BF16)])

    ng = w_g.shape[1]
    tn = 1024
    gates = pl.pallas_call(
        functools.partial(_gate_proj_kernel, chunk=tm_gates // 2), grid=(m // tm_gates, ng // tn),
        in_specs=[pl.BlockSpec((tm_gates, d), lambda i, j: (i, 0)), pl.BlockSpec((d, tn), lambda i, j: (0, j))],
        out_specs=pl.BlockSpec((tm_gates, tn), lambda i, j: (i, j)),
        out_shape=sds((m, ng), BF16), compiler_params=_cparams("parallel", "parallel"), name="proj_gates",
    )(xn, w_g)
    return dict(u8=u8, q=q_nat, q_t=q_t, k_p=k_p, k_s=k_s, kb=kb, v_p=v_p, v_s=v_s, v_t=v_t, qi=qi_nat, qi_t=qi_t,
                kiw=kiw, kiw_t=kiw_t, kib=kib, gates=gates)


def _ssm_prep_kernel(lre_ref, lim_ref, ldt_ref, bre_ref, bim_ref, cre_ref, cim_ref,
                     k_ref, win_ref, wout_ref, lp_ref):
    ns = STATES_PER_BLOCK
    lre, lim = lre_ref[...], lim_ref[...]
    dt = jnp.exp(ldt_ref[...])
    a, th = lre * dt, lim * dt

    def power(l):
        mag = jnp.exp(a * float(l))
        return mag * jnp.cos(th * float(l)), mag * jnp.sin(th * float(l))

    pw = [power(l) for l in range(SSM_T + 1)]
    xr, xi = pw[1][0] - 1.0, pw[1][1]
    den = lre * lre + lim * lim
    cr, ci = (xr * lre + xi * lim) / den, (xi * lre - xr * lim) / den
    bre, bim = bre_ref[...], bim_ref[...]
    bbr, bbi = bre * cr - bim * ci, bre * ci + bim * cr
    cre, cim = cre_ref[...], cim_ref[...]
    split = lambda a: (a.astype(BF16), (a - a.astype(BF16).astype(F32)).astype(BF16))
    c_hi, c_lo = split(jnp.concatenate([cre, cim], axis=1))
    lag = []
    for l in range(SSM_T):
        pr, pi = pw[l]
        blr, bli = bbr * pr - bbi * pi, bbr * pi + bbi * pr
        b_hi, b_lo = split(jnp.concatenate([blr, -bli], axis=1))
        m = _dot_nt(b_hi, c_hi) + _dot_nt(b_hi, c_lo) + _dot_nt(b_lo, c_hi)
        lag.append(m.astype(BF16))
        t = SSM_T - 1 - l
        win_ref[t * LANES:(t + 1) * LANES, :ns] = blr.astype(BF16)
        win_ref[t * LANES:(t + 1) * LANES, ns:] = bli.astype(BF16)
    zero = jnp.zeros((LANES, LANES), BF16)
    for t in range(SSM_T):
        pr, pi = pw[t + 1]
        wout_ref[t * LANES:(t + 1) * LANES, :ns] = (cre * pr - cim * pi).astype(BF16)
        wout_ref[t * LANES:(t + 1) * LANES, ns:] = (-(cre * pi + cim * pr)).astype(BF16)
        lp_ref[t:t + 1, :ns] = pr
        lp_ref[t:t + 1, ns:] = pi
        for t2 in range(SSM_T):
            k_ref[t * LANES:(t + 1) * LANES, t2 * LANES:(t2 + 1) * LANES] = lag[t2 - t] if t2 >= t else zero


def _block_diag_groups(w):
    g, h, p = w.shape
    nb = g // GROUPS_PER_BLOCK
    w = w.reshape(nb, GROUPS_PER_BLOCK, h, p)
    eye = jnp.eye(GROUPS_PER_BLOCK, dtype=w.dtype)
    out = w[:, :, :, None, :] * eye[None, :, None, :, None]
    return out.reshape(nb, GROUPS_PER_BLOCK * h, GROUPS_PER_BLOCK * p)


def _ssm_prep(lam_re, lam_im, b_re, b_im, c_re, c_im, log_dt):
    g, p = lam_re.shape
    nb = g // GROUPS_PER_BLOCK
    ns = STATES_PER_BLOCK
    tl = SSM_T * LANES
    vec = lambda a: a.reshape(nb, 1, ns)
    ldt = vec(jnp.broadcast_to(log_dt[:, None], (g, p)))
    bt = lambda b: _block_diag_groups(jnp.swapaxes(b, 1, 2))
    vspec = pl.BlockSpec((None, 1, ns), lambda j: (j, 0, 0))
    mspec = pl.BlockSpec((None, LANES, ns), lambda j: (j, 0, 0))
    sds = jax.ShapeDtypeStruct
    return pl.pallas_call(
        _ssm_prep_kernel, grid=(nb,),
        in_specs=[vspec, vspec, vspec, mspec, mspec, mspec, mspec],
        out_specs=[pl.BlockSpec((None, tl, tl), lambda j: (j, 0, 0)),
                   pl.BlockSpec((None, tl, 2 * ns), lambda j: (j, 0, 0)),
                   pl.BlockSpec((None, tl, 2 * ns), lambda j: (j, 0, 0)),
                   pl.BlockSpec((None, SSM_T, 2 * ns), lambda j: (j, 0, 0))],
        out_shape=[sds((nb, tl, tl), BF16), sds((nb, tl, 2 * ns), BF16), sds((nb, tl, 2 * ns), BF16),
                   sds((nb, SSM_T, 2 * ns), F32)],
        compiler_params=_cparams("parallel"), name="ssm_prep",
    )(vec(lam_re), vec(lam_im), ldt, bt(b_re), bt(b_im), _block_diag_groups(c_re), _block_diag_groups(c_im))


def _ssm_kernel(u_ref, k_ref, win_ref, wout_ref, lp_ref, d_ref, h0s_ref, za_ref, hlp_ref, hls_ref, x_scr, hs_scr,
                *, nseq, seq, n_s, t_s):
    ns = STATES_PER_BLOCK
    nk = ns // LANES
    d = d_ref[...]
    split = lambda a: [a[:, k * LANES:(k + 1) * LANES] for k in range(2 * nk)]

    def gather(row0, n_rows, t_steps):
        return [u_ref[pl.ds(row0 + t, n_rows, stride=t_steps), :] for t in range(t_steps)]

    def advance(h, x, t_steps):
        lam = [lp_ref[t_steps - 1:t_steps, k * LANES:(k + 1) * LANES] for k in range(2 * nk)]
        re = [lam[k] * h[k] - lam[nk + k] * h[nk + k] + x[k] for k in range(nk)]
        im = [lam[k] * h[nk + k] + lam[nk + k] * h[k] + x[nk + k] for k in range(nk)]
        return re + im

    def emit(row0, n_rows, t_steps, cols, hs, kmat, wout):
        u = jnp.concatenate(cols, axis=1).astype(BF16)
        y = _dot(u, kmat) + _dot_nt(hs.astype(BF16), wout)
        for t in range(t_steps):
            yt = y[:, t * LANES:(t + 1) * LANES] + d * cols[t]
            za_ref[pl.ds(row0 + t, n_rows, stride=t_steps), :] = jax.nn.gelu(yt, approximate=True)

    c_per = seq // SSM_T
    for b in range(nseq):
        u = jnp.concatenate(gather(b * seq, c_per, SSM_T), axis=1).astype(BF16)
        x = _dot(u, win_ref[...])
        for k in range(2 * nk):
            x_scr[k, b * c_per:(b + 1) * c_per, :] = x[:, k * LANES:(k + 1) * LANES]

    def step(c, h):
        rows = pl.ds(c, nseq, stride=c_per)
        for k in range(2 * nk):
            hs_scr.at[k][rows, :] = h[k]
        return tuple(advance(h, [x_scr.at[k][rows, :] for k in range(2 * nk)], SSM_T))

    h_last = lax.fori_loop(0, c_per, step, tuple(jnp.zeros((nseq, LANES), F32) for _ in range(2 * nk)),
                           unroll=4)
    hlp_ref[...] = jnp.concatenate(list(h_last), axis=1)
    for b in range(nseq):
        hs = jnp.concatenate([hs_scr[k, b * c_per:(b + 1) * c_per, :] for k in range(2 * nk)], axis=1)
        emit(b * seq, c_per, SSM_T, gather(b * seq, c_per, SSM_T), hs, k_ref[...], wout_ref[...])

    row0, tl = nseq * seq, t_s * LANES
    cols = gather(row0, n_s, t_s)
    u = jnp.concatenate(cols, axis=1).astype(BF16)
    x = _dot(u, win_ref[(SSM_T - t_s) * LANES:, :])
    h0 = h0s_ref[...]
    hls_ref[...] = jnp.concatenate(advance(split(h0), split(x), t_s), axis=1)
    emit(row0, n_s, t_s, cols, h0, k_ref[:tl, :tl], wout_ref[:tl, :])


def _ssm(u8, kmat, win, wout, lp, d8, h0s, *, nseq, seq, n_s, t_s):
    nb, m_total, _ = u8.shape
    ns = STATES_PER_BLOCK
    assert m_total == nseq * seq + n_s * t_s and seq % SSM_T == 0 and t_s <= SSM_T
    rows = nseq * seq // SSM_T
    tl = SSM_T * LANES
    sds = jax.ShapeDtypeStruct
    blk = lambda r, c: pl.BlockSpec((None, r, c), lambda j: (j, 0, 0))
    return pl.pallas_call(
        functools.partial(_ssm_kernel, nseq=nseq, seq=seq, n_s=n_s, t_s=t_s),
        grid=(nb,),
        in_specs=[blk(m_total, LANES), blk(tl, tl), blk(tl, 2 * ns), blk(tl, 2 * ns), blk(SSM_T, 2 * ns),
                  blk(1, LANES), blk(n_s, 2 * ns)],
        out_specs=[blk(m_total, LANES), blk(nseq, 2 * ns), blk(n_s, 2 * ns)],
        out_shape=[sds((nb, m_total, LANES), F32), sds((nb, nseq, 2 * ns), F32), sds((nb, n_s, 2 * ns), F32)],
        scratch_shapes=[pltpu.VMEM((2 * ns // LANES, rows, LANES), F32)] * 2,
        compiler_params=_cparams("parallel"), name="ssm",
    )(u8, kmat, win, wout, lp, d8, h0s)


def _state_to_blocks(re, im):
    n, g, p = re.shape
    nb = g // GROUPS_PER_BLOCK
    f = lambda a: jnp.transpose(a.reshape(n, nb, GROUPS_PER_BLOCK * p), (1, 0, 2))
    return jnp.concatenate([f(re), f(im)], axis=-1)


def _blocks_to_state(h):
    nb, n, w = h.shape
    ns = w // 2
    f = lambda a: jnp.transpose(a, (1, 0, 2)).reshape(n, nb * GROUPS_PER_BLOCK, ns // GROUPS_PER_BLOCK)
    return f(h[..., :ns]), f(h[..., ns:])


def _key_to_float(c):
    return lax.bitcast_convert_type(c ^ ((c >> 31) & 0x7FFFFFFF), F32)


def _kth_largest(count_ge, shape, top_k):
    def body(it, carry):
        u, cnt = carry
        bit = jnp.left_shift(jnp.int32(1), 31 - it)
        cand = u | bit
        c = count_ge(_key_to_float(cand ^ INT_MIN))
        take = c >= top_k
        return jnp.where(take, cand, u), jnp.where(take, c, cnt)
    u, cnt = lax.fori_loop(0, 32, body, (jnp.zeros(shape, jnp.int32), jnp.zeros(shape, jnp.int32)))
    return _key_to_float(u ^ INT_MIN), cnt


TIE_CHUNK = 128


def _tri_ones(n, dtype, lower):
    r = lax.broadcasted_iota(jnp.int32, (n, n), 0)
    c = lax.broadcasted_iota(jnp.int32, (n, n), 1)
    return jnp.where((r >= c) if lower else (r <= c), 1.0, 0.0).astype(dtype)


def _attn_prompt_kernel(qt_ref, kb_ref, vt_ref, qit_ref, kib_ref, wt_ref, o_ref, s_scr, bias_scr,
                        *, seq, top_k, key_chunk):
    qb = pl.program_id(1)
    tq = o_ref.shape[0]

    def run(nk):
        kib = kib_ref[:nk]
        s = jnp.zeros((nk, tq), F32)
        for h in range(IDX_HEADS):
            d = _dot(kib, qit_ref[h * IDX_DIM:(h + 1) * IDX_DIM, :])
            s = s + jnp.maximum(d, 0.0) * wt_ref[IDX_DIM + h:IDX_DIM + h + 1, :]
        kpos = lax.broadcasted_iota(jnp.int32, (nk, tq), 0)
        qpos = qb * tq + lax.broadcasted_iota(jnp.int32, (nk, tq), 1)
        allowed = kpos <= qpos
        s_scr[:nk] = jnp.where(allowed, s, -jnp.inf)

        def count_ge(t):
            n_acc = 8
            acc = [jnp.zeros((SUBLANES, tq), jnp.int32) for _ in range(n_acc)]
            for j in range(nk // SUBLANES):
                rows = s_scr[j * SUBLANES:(j + 1) * SUBLANES]
                acc[j % n_acc] = acc[j % n_acc] + (rows >= t).astype(jnp.int32)
            while len(acc) > 1:
                acc = [a + b for a, b in zip(acc[::2], acc[1::2])]
            return jnp.sum(acc[0], axis=0, keepdims=True)

        thr, cnt = _kth_largest(count_ge, (1, tq), top_k)
        few = qpos < top_k - 1
        q_row = qb * tq + lax.broadcasted_iota(jnp.int32, (1, tq), 1)
        tied = jnp.max(jnp.where((cnt > top_k) & (q_row >= top_k - 1), 1, 0)) > 0

        @pl.when(jnp.logical_not(tied))
        def _():
            bias_scr[:nk] = jnp.where(allowed & ((s_scr[:nk] >= thr) | few), 0.0, -jnp.inf)

        @pl.when(tied)
        def _():
            n_gt = jnp.sum((s_scr[:nk] > thr).astype(jnp.int32), axis=0, keepdims=True)
            need = (top_k - n_gt).astype(F32)
            tri = _tri_ones(TIE_CHUNK, BF16, lower=True)
            carry = jnp.zeros((1, tq), F32)
            for c in range(nk // TIE_CHUNK):
                rs = slice(c * TIE_CHUNK, (c + 1) * TIE_CHUNK)
                sc = s_scr[rs]
                eq = sc == thr
                rank = _dot(tri, jnp.where(eq, 1.0, 0.0).astype(BF16)) + carry
                keep = (sc > thr) | (eq & (rank <= need))
                k_row = c * TIE_CHUNK + lax.broadcasted_iota(jnp.int32, (TIE_CHUNK, tq), 0)
                keep_all = (q_row < top_k - 1) & (k_row <= q_row)
                bias_scr[rs] = jnp.where(keep | keep_all, 0.0, -jnp.inf)
                carry = rank[TIE_CHUNK - 1:, :]

        outs = []
        for g in range(N_KV_HEADS):
            kg = kb_ref[:nk, g * HEAD_DIM:(g + 1) * HEAD_DIM]
            vtg = vt_ref[g * HEAD_DIM:(g + 1) * HEAD_DIM, :nk]
            for r in range(N_REP):
                h = g * N_REP + r
                lg = _dot(kg, qt_ref[h * HEAD_DIM:(h + 1) * HEAD_DIM, :]) + bias_scr[:nk]
                p = jnp.exp(lg - jnp.max(lg, axis=0, keepdims=True))
                den = jnp.sum(p, axis=0, keepdims=True)
                outs.append(_dot(vtg, p.astype(BF16)) / den)
        o_ref[...] = jnp.concatenate(outs, axis=0).T.astype(BF16)

    n_var = seq // key_chunk
    need = (qb * tq + tq + key_chunk - 1) // key_chunk
    for v in range(1, n_var + 1):
        pl.when(need == v)(functools.partial(run, v * key_chunk))


def _attn_prompt(q_t, kb, v_t, qi_t, kib, kiw_t, *, batch, seq, m_out, tq=128, key_chunk=128):
    top_k = min(TOPK_MAX, seq // 4)
    nq = seq // tq
    assert seq % key_chunk == 0 and key_chunk % tq == 0
    aw = N_HEADS * HEAD_DIM
    qcol = lambda n: pl.BlockSpec((n, tq), lambda b, i: (0, b * nq + i))
    return pl.pallas_call(
        functools.partial(_attn_prompt_kernel, seq=seq, top_k=top_k, key_chunk=key_chunk),
        grid=(batch, nq),
        in_specs=[qcol(aw),
                  pl.BlockSpec((seq, N_KV_HEADS * HEAD_DIM), lambda b, i: (b, 0)),
                  pl.BlockSpec((N_KV_HEADS * HEAD_DIM, seq), lambda b, i: (0, b)),
                  qcol(IDX_HEADS * IDX_DIM),
                  pl.BlockSpec((seq, IDX_DIM), lambda b, i: (b, 0)),
                  qcol(LANES)],
        out_specs=pl.BlockSpec((tq, aw), lambda b, i: (b * nq + i, 0)),
        out_shape=jax.ShapeDtypeStruct((m_out, aw), BF16),
        scratch_shapes=[pltpu.VMEM((seq, tq), F32), pltpu.VMEM((seq, tq), F32)],
        compiler_params=_cparams("parallel", "parallel"), name="attn_prompt",
    )(q_t, kb, v_t, qi_t, kib, kiw_t)


def _merge_kernel(za_ref, bm_ref, bt_ref, g_ref, x_ref, gw_ref, gb_ref, wa_ref, wb_ref, wo_ref, o_ref,
                  *, n_main, tm, tail):
    d = x_ref.shape[1]

    def run(rows, b_ref):
        za = jnp.concatenate([za_ref[j, :rows] for j in range(za_ref.shape[0])], axis=1)
        a_out = za * jax.nn.sigmoid(_dot(za.astype(BF16), gw_ref[...]) + gb_ref[...])
        merged = (g_ref[:rows, :d] * _dot(a_out.astype(BF16), wa_ref[...])
                  + g_ref[:rows, d:] * _dot(b_ref[:rows], wb_ref[...]))
        o_ref[:rows] = x_ref[:rows] + _dot(merged.astype(BF16), wo_ref[...])

    i = pl.program_id(0)
    pl.when(i < n_main)(lambda: run(tm, bm_ref))
    pl.when(i == n_main)(lambda: run(tail, bt_ref))


def _merge(za8, b_main, b_tail, gates, x, glu_w, glu_b, wa, wb, wo, *, tm):
    m, d = x.shape
    nb = za8.shape[0]
    n_main_rows, aw = b_main.shape
    tail = b_tail.shape[0]
    n_main = n_main_rows // tm
    assert n_main * tm == n_main_rows and n_main_rows + tail == m and tail <= tm
    resident = lambda shape: pl.BlockSpec(shape, lambda i: (0,) * len(shape), pipeline_mode=pl.Buffered(1))
    return pl.pallas_call(
        functools.partial(_merge_kernel, n_main=n_main, tm=tm, tail=tail), grid=(n_main + 1,),
        in_specs=[pl.BlockSpec((nb, tm, LANES), lambda i: (0, i, 0)),
                  pl.BlockSpec((tm, aw), lambda i: (jnp.minimum(i, n_main - 1), 0)),
                  pl.BlockSpec((tail, aw), lambda i: (0, 0)),
                  _row_spec(tm, 2 * d), _row_spec(tm, d), resident(glu_w.shape), resident((1, glu_w.shape[1])),
                  resident(wa.shape), resident(wb.shape), resident(wo.shape)],
        out_specs=_row_spec(tm, d),
        out_shape=jax.ShapeDtypeStruct((m, d), F32),
        compiler_params=_cparams("arbitrary"), name="merge",
    )(za8, b_main, b_tail, gates, x, glu_w, glu_b.reshape(1, -1), wa, wb, wo)


QPAD = SUBLANES
CHUNK_PAGES = 8


def _page_copy(pt_ref, src_ref, buf_ref, sem, seq, slot, p, page_base, rows):
    start = pl.multiple_of((page_base + pt_ref[seq, p]) * rows, rows)
    return pltpu.make_async_copy(src_ref.at[pl.ds(start, rows), :],
                                 buf_ref.at[slot, pl.ds(p * rows, rows), :], sem)


def _prefetch_pages(n_pages, copies):
    s, n_seq = pl.program_id(0), pl.num_programs(0)
    slot = s % 2

    def start_all(seq, sl):
        def body(p, carry):
            for copy in copies:
                copy(seq, sl, p).start()
            return carry
        lax.fori_loop(0, n_pages, body, 0, unroll=8)

    @pl.when(s == 0)
    def _():
        start_all(0, 0)

    @pl.when(s + 1 < n_seq)
    def _():
        start_all(s + 1, 1 - slot)

    def wait_body(p, carry):
        for copy in copies:
            copy(s, slot, p).wait()
        return carry
    lax.fori_loop(0, n_pages, wait_body, 0, unroll=8)
    return slot


def _sample_scores_kernel(pt_ref, qs_ref, ws_ref, kn_ref, cik_ref, sp_ref, sn_ref, buf_ref, sem_ref,
                          *, page_base, n_pages):
    slot = _prefetch_pages(n_pages, [
        lambda seq, sl, p: _page_copy(pt_ref, cik_ref, buf_ref, sem_ref.at[sl], seq, sl, p, page_base, IDX_DIM)])
    qs, ws = qs_ref[...], ws_ref[:, :1]

    def scores(d):
        r = jnp.maximum(d, 0.0) * ws
        return r.reshape(IDX_HEADS, QPAD, r.shape[1]).sum(axis=0)

    nkc = CHUNK_PAGES * PAGE_SIZE
    for c in range(n_pages // CHUNK_PAGES):
        keys_t = jnp.concatenate(
            [buf_ref[slot, (c * CHUNK_PAGES + kk) * IDX_DIM:(c * CHUNK_PAGES + kk + 1) * IDX_DIM, :]
             for kk in range(CHUNK_PAGES)], axis=1).astype(BF16)
        sp_ref[:, c * nkc:(c + 1) * nkc] = scores(_dot(qs, keys_t))
    sn_ref[...] = scores(_dot_nt(qs, kn_ref[...]))


def _sample_thr_kernel(sp_ref, sn_ref, thr_ref, cut_ref, s_scr, *, past, top_k, t_len):
    s = jnp.concatenate([sp_ref[...], sn_ref[...]], axis=1)
    shape = s.shape
    kpos = lax.broadcasted_iota(jnp.int32, shape, 1)
    qidx = lax.broadcasted_iota(jnp.int32, shape, 0) % t_len
    s_scr[...] = jnp.where(kpos <= past + qidx, s, -jnp.inf)

    def count_ge(t):
        return jnp.sum((s_scr[...] >= t).astype(jnp.int32), axis=1, keepdims=True)

    thr, cnt = _kth_largest(count_ge, (shape[0], 1), top_k)
    thr_ref[...] = jnp.broadcast_to(thr, thr_ref.shape)
    tied = jnp.max(jnp.where(cnt > top_k, 1, 0)) > 0

    @pl.when(jnp.logical_not(tied))
    def _():
        cut_ref[...] = jnp.full(cut_ref.shape, shape[1], jnp.int32)

    @pl.when(tied)
    def _():
        n_gt = jnp.sum((s_scr[...] > thr).astype(jnp.int32), axis=1, keepdims=True)
        need = (top_k - n_gt).astype(F32)
        tri = _tri_ones(LANES, BF16, lower=False)
        carry = jnp.zeros((shape[0], 1), F32)
        cut = jnp.full((shape[0], 1), -1, jnp.int32)
        lane = lax.broadcasted_iota(jnp.int32, (shape[0], LANES), 1)
        for c in range(shape[1] // LANES):
            eq = s_scr[:, c * LANES:(c + 1) * LANES] == thr
            rank = _dot(jnp.where(eq, 1.0, 0.0).astype(BF16), tri) + carry
            last = jnp.max(jnp.where(eq & (rank <= need), lane + c * LANES, -1), axis=1, keepdims=True)
            cut = jnp.maximum(cut, last)
            carry = rank[:, LANES - 1:]
        cut_ref[...] = jnp.broadcast_to(cut, cut_ref.shape)


def _sample_attn_kernel(pt_ref, q_ref, sp_ref, sn_ref, thr_ref, cut_ref, kn_ref, vn_ref, ck_ref, cv_ref, o_ref,
                        kbuf, vbuf, sem_ref, lg_scr, *, page_base, n_pages):
    page_rows = N_KV_HEADS * PAGE_SIZE
    slot = _prefetch_pages(n_pages, [
        lambda seq, sl, p: _page_copy(pt_ref, ck_ref, kbuf, sem_ref.at[0, sl], seq, sl, p, page_base, page_rows),
        lambda seq, sl, p: _page_copy(pt_ref, cv_ref, vbuf, sem_ref.at[1, sl], seq, sl, p, page_base, page_rows)])
    rows_g = N_REP * QPAD
    nkc = CHUNK_PAGES * PAGE_SIZE
    n_chunks = n_pages // CHUNK_PAGES
    past = n_pages * PAGE_SIZE
    thr, cut = thr_ref[:, :1], cut_ref[:, :1]
    q = q_ref[...]
    qg = [q[g * rows_g:(g + 1) * rows_g] for g in range(N_KV_HEADS)]
    head_rows = lambda c, g: pl.ds(c * CHUNK_PAGES * page_rows + g, nkc, stride=N_KV_HEADS)

    def bias(scores, pos0, allowed=None):
        kpos = pos0 + lax.broadcasted_iota(jnp.int32, scores.shape, 1)
        sel = (scores > thr) | ((scores == thr) & (kpos <= cut))
        if allowed is not None:
            sel = sel & allowed
        return jnp.concatenate([jnp.where(sel, 0.0, -jnp.inf)] * N_HEADS, axis=0)

    for c in range(n_chunks):
        lg = jnp.concatenate([_dot_nt(qg[g], kbuf.at[slot][head_rows(c, g), :].astype(BF16))
                              for g in range(N_KV_HEADS)], axis=0)
        lg_scr[:, c * nkc:(c + 1) * nkc] = lg + bias(sp_ref[:, c * nkc:(c + 1) * nkc], c * nkc)
    shape = sn_ref.shape
    allowed = lax.broadcasted_iota(jnp.int32, shape, 1) <= lax.broadcasted_iota(jnp.int32, shape, 0)
    lg = jnp.concatenate([_dot_nt(qg[g], kn_ref[:, g * HEAD_DIM:(g + 1) * HEAD_DIM])
                          for g in range(N_KV_HEADS)], axis=0)
    lg_scr[:, past:] = lg + bias(sn_ref[...], past, allowed)

    lg = lg_scr[...]
    p = jnp.exp(lg - jnp.max(lg, axis=1, keepdims=True))
    den = jnp.sum(p, axis=1, keepdims=True)
    pb = p.astype(BF16)
    acc = jnp.concatenate([_dot(pb[g * rows_g:(g + 1) * rows_g, past:], vn_ref[:, g * HEAD_DIM:(g + 1) * HEAD_DIM])
                           for g in range(N_KV_HEADS)], axis=0)
    for c in range(n_chunks):
        acc = acc + jnp.concatenate(
            [_dot(pb[g * rows_g:(g + 1) * rows_g, c * nkc:(c + 1) * nkc],
                  vbuf.at[slot][head_rows(c, g), :].astype(BF16)) for g in range(N_KV_HEADS)], axis=0)
    o_ref[...] = acc / den


def _pad_queries(x, t_len):
    n, h, d = x.shape
    x = jnp.transpose(x.reshape(n // t_len, t_len, h, d), (0, 2, 1, 3))
    x = jnp.pad(x, ((0, 0), (0, 0), (0, QPAD - t_len), (0, 0)))
    return x.reshape(n // t_len, h * QPAD, d)


def _pad_new_keys(x, t_len):
    n, w = x.shape
    return jnp.pad(x.reshape(n // t_len, t_len, w), ((0, 0), (0, PAGE_SIZE - t_len), (0, 0)))


def _attn_sample(q_s, k_new, v_new, qi_s, wi_s, ki_new, cache_k, cache_v, cache_idx_k, page_table, page_base,
                 *, t_len):
    db, n_pages = page_table.shape
    past = n_pages * PAGE_SIZE
    top_k = min(TOPK_MAX, (past + t_len) // 4)
    kvw = N_KV_HEADS * HEAD_DIM
    sds = jax.ShapeDtypeStruct
    q = _pad_queries(q_s.reshape(-1, N_HEADS, HEAD_DIM), t_len)
    qi = _pad_queries(qi_s.reshape(-1, IDX_HEADS, IDX_DIM), t_len)
    ws = _pad_queries(wi_s.reshape(-1, IDX_HEADS, 1), t_len)
    ws = jnp.broadcast_to(ws, ws.shape[:2] + (PAGE_SIZE,))
    kn, vn, kin = (_pad_new_keys(a, t_len) for a in (k_new, v_new, ki_new))

    assert n_pages % CHUNK_PAGES == 0
    seq_spec = lambda r, c: pl.BlockSpec((None, r, c), lambda s, pt: (s, 0, 0))
    hbm = pl.BlockSpec(memory_space=pl.ANY)
    dma = pltpu.SemaphoreType.DMA
    in_order = _cparams("arbitrary")
    sp, sn = pl.pallas_call(
        functools.partial(_sample_scores_kernel, page_base=page_base, n_pages=n_pages),
        grid_spec=pltpu.PrefetchScalarGridSpec(
            num_scalar_prefetch=1, grid=(db,),
            in_specs=[seq_spec(IDX_HEADS * QPAD, IDX_DIM), seq_spec(IDX_HEADS * QPAD, PAGE_SIZE),
                      seq_spec(PAGE_SIZE, IDX_DIM), hbm],
            out_specs=[seq_spec(QPAD, past), seq_spec(QPAD, PAGE_SIZE)],
            scratch_shapes=[pltpu.VMEM((2, n_pages * IDX_DIM, PAGE_SIZE), F32), dma((2,))]),
        out_shape=[sds((db, QPAD, past), F32), sds((db, QPAD, PAGE_SIZE), F32)],
        compiler_params=in_order, name="sample_scores",
    )(page_table, qi, ws, kin, cache_idx_k.reshape(-1, PAGE_SIZE))

    rows = db * t_len
    rblk = min(rows, 8 * SUBLANES)
    assert rows % rblk == 0
    real = lambda a: a[:, :t_len].reshape(rows, a.shape[2])
    padded = lambda a: jnp.pad(a.reshape(db, t_len, LANES), ((0, 0), (0, QPAD - t_len), (0, 0)))
    thr, cut = pl.pallas_call(
        functools.partial(_sample_thr_kernel, past=past, top_k=top_k, t_len=t_len),
        grid=(rows // rblk,),
        in_specs=[_row_spec(rblk, past), _row_spec(rblk, PAGE_SIZE)],
        out_specs=[_row_spec(rblk, LANES), _row_spec(rblk, LANES)],
        out_shape=[sds((rows, LANES), F32), sds((rows, LANES), jnp.int32)],
        scratch_shapes=[pltpu.VMEM((rblk, past + PAGE_SIZE), F32)],
        compiler_params=_cparams("parallel"), name="sample_threshold",
    )(real(sp), real(sn))

    hq = N_HEADS * QPAD
    page_rows = N_KV_HEADS * PAGE_SIZE
    out = pl.pallas_call(
        functools.partial(_sample_attn_kernel, page_base=page_base, n_pages=n_pages),
        grid_spec=pltpu.PrefetchScalarGridSpec(
            num_scalar_prefetch=1, grid=(db,),
            in_specs=[seq_spec(hq, HEAD_DIM), seq_spec(QPAD, past), seq_spec(QPAD, PAGE_SIZE), seq_spec(QPAD, LANES),
                      seq_spec(QPAD, LANES), seq_spec(PAGE_SIZE, kvw), seq_spec(PAGE_SIZE, kvw), hbm, hbm],
            out_specs=seq_spec(hq, HEAD_DIM),
            scratch_shapes=[pltpu.VMEM((2, n_pages * page_rows, HEAD_DIM), F32),
                            pltpu.VMEM((2, n_pages * page_rows, HEAD_DIM), F32),
                            dma((2, 2)), pltpu.VMEM((hq, past + PAGE_SIZE), F32)]),
        out_shape=sds((db, hq, HEAD_DIM), F32),
        compiler_params=in_order, name="sample_attn",
    )(page_table, q, sp, sn, padded(thr), padded(cut), kn, vn, cache_k, cache_v)
    out = out.reshape(db, N_HEADS, QPAD, HEAD_DIM)[:, :, :t_len]
    return jnp.transpose(out, (0, 2, 1, 3)).reshape(db * t_len, N_HEADS * HEAD_DIM)


GATE_ROW_TILE = 640
PROJ_ROW_TILE = 512
FFN_ROW_TILE = 512
FF_TILE = 512
MERGE_ROW_TILE = 256


def _layer(x_p, x_s, pos, dims, layer, w_in_all, ck, cv, cik, page_base, s_re, s_im, page_table, p):
    b, s, db, t = dims
    mp, ms = b * s, db * t
    d = x_p.shape[1]
    bf = lambda w: w.astype(BF16)
    ssm_w = d // 2
    w_g = bf(w_in_all[layer, :, w_in_all.shape[2] - 2 * d:])

    x1, xn = _ffn(x_p, x_s, p["ffn1_norm"], bf(p["ffn1_w_gate"]), bf(p["ffn1_w_up"]), bf(p["ffn1_w_down"]),
                  p["mix_norm"], n_main_rows=mp, tail=ms, split_out=False, tm=FFN_ROW_TILE, tf=FF_TILE)
    pr = _projections(xn, jnp.swapaxes(w_in_all, 1, 2), layer, w_g, p["q_norm"], p["k_norm"], pos,
                      n_main_rows=mp, seq=s, tail=ms, tm=PROJ_ROW_TILE, tm_gates=GATE_ROW_TILE)

    kmat, win, wout, lp = _ssm_prep(p["ssm_lambda_re"], p["ssm_lambda_im"], p["ssm_b_re"], p["ssm_b_im"],
                                    p["ssm_c_re"], p["ssm_c_im"], p["ssm_log_dt"])
    nblk = ssm_w // LANES
    d8 = p["ssm_d"].reshape(nblk, 1, LANES)
    za8, hl_p, hl_s = _ssm(pr["u8"], kmat, win, wout, lp, d8, _state_to_blocks(s_re, s_im),
                           nseq=b, seq=s, n_s=db, t_s=t)

    b_p = _attn_prompt(pr["q_t"], pr["kb"], pr["v_t"], pr["qi_t"], pr["kib"], pr["kiw_t"],
                       batch=b, seq=s, m_out=mp)
    b_s = _attn_sample(pr["q"][mp:], pr["kb"][mp:], bf(pr["v_s"].reshape(ms, -1)), pr["qi"][mp:],
                       pr["kiw"][mp:, IDX_DIM:IDX_DIM + IDX_HEADS], pr["kib"][mp:],
                       ck, cv, cik, page_table, page_base, t_len=t)

    x2 = _merge(za8, b_p, bf(b_s), pr["gates"], x1, bf(p["glu_w"]), p["glu_b"], bf(p["w_branch_a"]),
                bf(p["w_branch_b"]), bf(p["w_out"]), tm=MERGE_ROW_TILE)
    y_p, y_s = _ffn(x2, None, p["ffn2_norm"], bf(p["ffn2_w_gate"]), bf(p["ffn2_w_up"]), bf(p["ffn2_w_down"]),
                    n_main_rows=mp, tail=ms, split_out=True, tm=FFN_ROW_TILE, tf=FF_TILE)

    hp_re, hp_im = _blocks_to_state(hl_p)
    hs_re, hs_im = _blocks_to_state(hl_s)
    ki = pr["kiw"][:, :IDX_DIM]
    kvs = (N_KV_HEADS, HEAD_DIM)
    rows = (pr["k_p"].reshape(b, s, *kvs), pr["v_p"].reshape(b, s, *kvs), ki[:mp].reshape(b, s, IDX_DIM),
            hp_re, hp_im,
            pr["k_s"].reshape(db, t, *kvs), pr["v_s"].reshape(db, t, *kvs), ki[mp:].reshape(db, t, IDX_DIM),
            hs_re, hs_im)
    return y_p, y_s, rows


def kernel(x_prompt, x_sample, cache_k, cache_v, cache_idx_k, state_ssm_re, state_ssm_im, page_table,
           ffn1_norm, ffn1_w_gate, ffn1_w_up, ffn1_w_down, mix_norm, w_in, q_norm, k_norm,
           ssm_lambda_re, ssm_lambda_im, ssm_b_re, ssm_b_im, ssm_c_re, ssm_c_im, ssm_d, ssm_log_dt,
           glu_w, glu_b, w_branch_a, w_branch_b, w_out, ffn2_norm, ffn2_w_gate, ffn2_w_up, ffn2_w_down):
    b, s, d = x_prompt.shape
    db, t, _ = x_sample.shape
    depth, n_phys = cache_k.shape[:2]
    past = page_table.shape[1] * PAGE_SIZE
    x_p, x_s = x_prompt.reshape(b * s, d), x_sample.reshape(db * t, d)
    pos = jnp.concatenate([jnp.arange(s), jnp.tile(past + jnp.arange(t), db)])
    ck = cache_k.reshape(-1, HEAD_DIM)
    cv = cache_v.reshape(-1, HEAD_DIM)
    cik = jnp.swapaxes(cache_idx_k, 2, 3).reshape(depth * n_phys, IDX_DIM, PAGE_SIZE)
    params = dict(
        ffn1_norm=ffn1_norm, ffn1_w_gate=ffn1_w_gate, ffn1_w_up=ffn1_w_up, ffn1_w_down=ffn1_w_down,
        mix_norm=mix_norm, w_in=w_in, q_norm=q_norm, k_norm=k_norm,
        ssm_lambda_re=ssm_lambda_re, ssm_lambda_im=ssm_lambda_im, ssm_b_re=ssm_b_re, ssm_b_im=ssm_b_im,
        ssm_c_re=ssm_c_re, ssm_c_im=ssm_c_im, ssm_d=ssm_d, ssm_log_dt=ssm_log_dt, glu_w=glu_w, glu_b=glu_b,
        w_branch_a=w_branch_a, w_branch_b=w_branch_b, w_out=w_out,
        ffn2_norm=ffn2_norm, ffn2_w_gate=ffn2_w_gate, ffn2_w_up=ffn2_w_up, ffn2_w_down=ffn2_w_down)
    new = [[] for _ in range(10)]
    for l in range(depth):
        p = {name: w[l] for name, w in params.items()}
        x_p, x_s, rows = _layer(x_p, x_s, pos, (b, s, db, t), l, w_in, ck, cv, cik, l * n_phys,
                                state_ssm_re[l], state_ssm_im[l], page_table, p)
        for lst, r in zip(new, rows):
            lst.append(r)
    return (x_p.reshape(b, s, d), x_s.reshape(db, t, d)) + tuple(jnp.stack(lst) for lst in new)
```

```python
import functools

import jax
import jax.numpy as jnp
from jax import lax
from jax.experimental import pallas as pl
from jax.experimental.pallas import tpu as pltpu

F32 = jnp.float32
BF16 = jnp.bfloat16

SSM_GROUP = 16
SSM_STATE = 64
N_HEADS = 8
HEAD_DIM = 128
N_KV_HEADS = 2
N_REP = N_HEADS // N_KV_HEADS
ROT_DIM = HEAD_DIM // 4
ROPE_THETA = 500000.0
IDX_HEADS = 16
IDX_DIM = 64
IDX_ROT_DIM = IDX_DIM // 4
TOPK_MAX = 256
PAGE_SIZE = 128
FFN_RES = 0.5
EPS = 1e-6

LANES = 128
SUBLANES = 8
VMEM_LIMIT_BYTES = 56 * 1024 * 1024

GROUPS_PER_BLOCK = LANES // SSM_GROUP
STATES_PER_BLOCK = GROUPS_PER_BLOCK * SSM_STATE
SSM_T = 8

INT_MIN = -(2 ** 31)


def _cparams(*sem):
    return pltpu.CompilerParams(dimension_semantics=sem, vmem_limit_bytes=VMEM_LIMIT_BYTES)


def _rms(x, g):
    return x * lax.rsqrt(jnp.mean(x * x, axis=-1, keepdims=True) + EPS) * g


def _dot(a, b):
    return jnp.dot(a, b, preferred_element_type=F32)


def _dot_nt(a, b):
    return lax.dot_general(a, b, (((1,), (1,)), ((), ())), preferred_element_type=F32)


def _ffn_kernel(*refs, nf, n_main, tm, tail, two_src, split_out, with_next_norm):
    refs = list(refs)
    x_main_ref = refs.pop(0)
    x_tail_ref = refs.pop(0) if two_src else x_main_ref
    g_ref, wg_ref, wu_ref, wd_ref = (refs.pop(0) for _ in range(4))
    g2_ref = refs.pop(0) if with_next_norm else None
    y_main_ref = refs.pop(0)
    y_tail_ref = refs.pop(0) if split_out else y_main_ref
    n2_ref = refs.pop(0) if with_next_norm else None
    xn_ref, acc_ref = refs
    i, f = pl.program_id(0), pl.program_id(1)

    def run(rows, x_ref, y_ref):
        @pl.when(f == 0)
        def _():
            xn_ref[:rows] = _rms(x_ref[:rows], g_ref[...]).astype(BF16)
            acc_ref[:rows] = jnp.zeros((rows, acc_ref.shape[1]), F32)

        xn = xn_ref[:rows]
        a = _dot(xn, wg_ref[...])
        b = _dot(xn, wu_ref[...])
        h = (a * jax.nn.sigmoid(a) * b).astype(BF16)
        acc_ref[:rows] += _dot(h, wd_ref[...])

        @pl.when(f == nf - 1)
        def _():
            y = x_ref[:rows] + FFN_RES * acc_ref[:rows]
            y_ref[:rows] = y
            if with_next_norm:
                n2_ref[:rows] = _rms(y, g2_ref[...]).astype(BF16)

    pl.when(i < n_main)(lambda: run(tm, x_main_ref, y_main_ref))
    pl.when(i == n_main)(lambda: run(tail, x_tail_ref, y_tail_ref))


def _ffn(x_main, x_tail, g, wg, wu, wd, g2=None, *, n_main_rows, tail, split_out, tm, tf):
    d = x_main.shape[1]
    nf = wg.shape[1] // tf
    n_main = n_main_rows // tm
    m = n_main_rows + tail
    two_src = x_tail is not None
    with_next = g2 is not None
    clamp = lambda i, f: (jnp.minimum(i, n_main - 1), 0)
    rows = lambda i, f: (i, 0)
    first = lambda i, f: (0, 0)
    sds = jax.ShapeDtypeStruct
    in_specs = [pl.BlockSpec((tm, d), clamp if two_src else rows)]
    args = [x_main]
    if two_src:
        in_specs.append(pl.BlockSpec((tail, d), first))
        args.append(x_tail)
    in_specs += [pl.BlockSpec((1, d), first), pl.BlockSpec((d, tf), lambda i, f: (0, f)),
                 pl.BlockSpec((d, tf), lambda i, f: (0, f)), pl.BlockSpec((tf, d), lambda i, f: (f, 0))]
    args += [g.reshape(1, d), wg, wu, wd]
    if with_next:
        in_specs.append(pl.BlockSpec((1, d), first))
        args.append(g2.reshape(1, d))
    if split_out:
        out_shape = [sds((n_main_rows, d), F32), sds((tail, d), F32)]
        out_specs = [pl.BlockSpec((tm, d), clamp), pl.BlockSpec((tail, d), first)]
    else:
        out_shape = [sds((m, d), F32)]
        out_specs = [pl.BlockSpec((tm, d), rows)]
    if with_next:
        out_shape.append(sds((m, d), BF16))
        out_specs.append(pl.BlockSpec((tm, d), rows))
    return pl.pallas_call(
        functools.partial(_ffn_kernel, nf=nf, n_main=n_main, tm=tm, tail=tail, two_src=two_src,
                          split_out=split_out, with_next_norm=with_next),
        grid=(n_main + 1, nf),
        in_specs=in_specs,
        out_specs=out_specs,
        out_shape=out_shape,
        scratch_shapes=[pltpu.VMEM((tm, d), BF16), pltpu.VMEM((tm, d), F32)],
        compiler_params=_cparams("arbitrary", "arbitrary"),
        name="ffn" + ("_norm" if with_next else ""),
    )(*args)


def _rope_tables(pos, rot_dim, width, tile):
    half = rot_dim // 2
    m = pos.shape[0]
    freqs = ROPE_THETA ** (-jnp.arange(half, dtype=F32) * 2.0 / rot_dim)
    ang = pos.astype(F32)[:, None] * freqs[None, :]
    cos, sin = jnp.cos(ang), jnp.sin(ang)
    zh = jnp.zeros((m, half), F32)
    rest = width - rot_dim
    c = jnp.concatenate([cos, cos, jnp.ones((m, rest), F32)], axis=1)
    s1 = jnp.concatenate([-sin, zh, jnp.zeros((m, rest), F32)], axis=1)
    s2 = jnp.concatenate([zh, sin, jnp.zeros((m, rest), F32)], axis=1)
    if tile:
        reps = LANES // width
        return tuple(jnp.tile(t, (1, reps)) for t in (c, s1, s2))
    pad = LANES - width
    return (jnp.pad(c, ((0, 0), (0, pad)), constant_values=1.0),
            jnp.pad(s1, ((0, 0), (0, pad))), jnp.pad(s2, ((0, 0), (0, pad))))


def _rope(x, c, s1, s2, half):
    return x * c + pltpu.roll(x, LANES - half, 1) * s1 + pltpu.roll(x, half, 1) * s2


def _proj_rows(xn_ref, w_refs, wbf_ref, n_main, tm, tail, chunk, epilogue):
    i = pl.program_id(0)

    @pl.when(i == 0)
    def _():
        off = 0
        for w_ref in w_refs:
            n = w_ref.shape[0]
            wbf_ref[:, off:off + n] = w_ref[...].astype(F32).T.astype(BF16)
            off += n

    def run(rows):
        for r0 in range(0, rows, chunk):
            rs = slice(r0, min(r0 + chunk, rows))
            epilogue(_dot(xn_ref[rs, :], wbf_ref[...]), rs, rows == tail)

    pl.when(i < n_main)(lambda: run(tm))
    pl.when(i == n_main)(lambda: run(tail))


def _u_proj_kernel(xn_ref, w_ref, u_ref, wbf_ref, **kw):
    def epilogue(z, rs, is_tail):
        for j in range(z.shape[1] // LANES):
            u_ref[j, rs, :] = z[:, j * LANES:(j + 1) * LANES]
    _proj_rows(xn_ref, [w_ref], wbf_ref, epilogue=epilogue, **kw)


def _q_proj_kernel(xn_ref, w_ref, g_ref, c_ref, s1_ref, s2_ref, qn_ref, qt_ref, wbf_ref, **kw):
    def epilogue(z, rs, is_tail):
        c, s1, s2, g = c_ref[rs, :], s1_ref[rs, :], s2_ref[rs, :], g_ref[...]
        heads = []
        for h in range(N_HEADS):
            x = _rms(z[:, h * HEAD_DIM:(h + 1) * HEAD_DIM], g)
            heads.append(_rope(x, c, s1, s2, ROT_DIM // 2) * (HEAD_DIM ** -0.5))
        q = jnp.concatenate(heads, axis=1)
        qn_ref[rs, :] = q.astype(BF16)
        qt_ref[:, rs] = q.T.astype(BF16)
    _proj_rows(xn_ref, [w_ref], wbf_ref, epilogue=epilogue, **kw)


def _kv_proj_kernel(xn_ref, wkv_ref, wkw_ref, g_ref, c_ref, s1_ref, s2_ref, ci_ref, si1_ref, si2_ref, scale_ref,
                    kp_ref, ks_ref, vp_ref, vs_ref, kb_ref, vt_ref, kiw_ref, kiwt_ref, kib_ref, wbf_ref, **kw):
    kvw = N_KV_HEADS * HEAD_DIM

    def epilogue(z, rs, is_tail):
        c, s1, s2, g = c_ref[rs, :], s1_ref[rs, :], s2_ref[rs, :], g_ref[...]
        k_ref, v_ref = (ks_ref, vs_ref) if is_tail else (kp_ref, vp_ref)
        n = rs.stop - rs.start
        heads = []
        for h in range(N_KV_HEADS):
            x = _rope(_rms(z[:, h * HEAD_DIM:(h + 1) * HEAD_DIM], g), c, s1, s2, ROT_DIM // 2)
            rows_h = pl.ds(N_KV_HEADS * rs.start + h, n, stride=N_KV_HEADS)
            k_ref[rows_h, :] = x
            v_ref[rows_h, :] = z[:, kvw + h * HEAD_DIM:kvw + (h + 1) * HEAD_DIM]
            heads.append(x)
        kb_ref[rs, :] = jnp.concatenate(heads, axis=1).astype(BF16)
        vt_ref[:, rs] = z[:, kvw:2 * kvw].T.astype(BF16)
        y = _rope(z[:, 2 * kvw:], ci_ref[rs, :], si1_ref[rs, :], si2_ref[rs, :], IDX_ROT_DIM // 2) * scale_ref[...]
        kiw_ref[rs, :] = y
        kiwt_ref[:, rs] = y.T
        kib_ref[rs, :] = y[:, :IDX_DIM].astype(BF16)
    _proj_rows(xn_ref, [wkv_ref, wkw_ref], wbf_ref, epilogue=epilogue, **kw)


def _qi_proj_kernel(xn_ref, wa_ref, wb_ref, c_ref, s1_ref, s2_ref, qn_ref, qt_ref, wbf_ref, **kw):
    def epilogue(z, rs, is_tail):
        c, s1, s2 = c_ref[rs, :], s1_ref[rs, :], s2_ref[rs, :]
        cols = []
        for j in range(z.shape[1] // LANES):
            x = z[:, j * LANES:(j + 1) * LANES]
            cols.append(_rope(x, c, s1, s2, IDX_ROT_DIM // 2) * (IDX_DIM ** -0.5))
        q = jnp.concatenate(cols, axis=1)
        qn_ref[rs, :] = q.astype(BF16)
        qt_ref[:, rs] = q.T.astype(BF16)
    _proj_rows(xn_ref, [wa_ref, wb_ref], wbf_ref, epilogue=epilogue, **kw)


def _gate_proj_kernel(xn_ref, w_ref, o_ref, *, chunk):
    for r0 in range(0, xn_ref.shape[0], chunk):
        rs = slice(r0, r0 + chunk)
        o_ref[rs, :] = jax.nn.sigmoid(_dot(xn_ref[rs, :], w_ref[...])).astype(BF16)


def _row_spec(tm, n):
    return pl.BlockSpec((tm, n), lambda i: (i, 0))


def _full_spec(shape):
    return pl.BlockSpec(shape, lambda i: (0,) * len(shape))


PROJ_CHUNK = 256


def _projections(xn, w_in_t, layer, w_g, q_norm, k_norm, pos, *, n_main_rows, seq, tail, tm, tm_gates):
    m, d = xn.shape
    n_main = n_main_rows // tm
    assert n_main * tm == n_main_rows and n_main_rows + tail == m and tail <= tm
    grid = (n_main + 1,)
    ssm_w, attn_w, kv_w, idx_w = d // 2, N_HEADS * HEAD_DIM, N_KV_HEADS * HEAD_DIM, IDX_HEADS * IDX_DIM
    hd_tabs = _rope_tables(pos, ROT_DIM, HEAD_DIM, True)
    ix_tabs = _rope_tables(pos, IDX_ROT_DIM, IDX_DIM, True)
    kw_tabs = _rope_tables(pos, IDX_ROT_DIM, IDX_DIM, False)
    assert seq % tm == 0 and pos.shape[0] == seq + tail
    tabs_per_seq = seq // tm
    tab_specs = [pl.BlockSpec((tm, LANES), lambda i: (jnp.where(i < n_main, i % tabs_per_seq, tabs_per_seq), 0))] * 3
    xs = _row_spec(tm, d)
    sds = jax.ShapeDtypeStruct
    seq = _cparams("arbitrary")
    kw = dict(n_main=n_main, tm=tm, tail=tail, chunk=PROJ_CHUNK)

    def wcols(start, width):
        assert start % width == 0
        return pl.BlockSpec((None, width, d), lambda i: (layer, start // width, 0), pipeline_mode=pl.Buffered(1))

    def call(kernel, name, w_specs, n_w, extra_specs, extra_args, out_specs, out_shape):
        return pl.pallas_call(
            functools.partial(kernel, **kw), grid=grid,
            in_specs=[xs] + w_specs + extra_specs, out_specs=out_specs, out_shape=out_shape,
            scratch_shapes=[pltpu.VMEM((d, n_w), BF16)], compiler_params=seq, name=name,
        )(xn, *([w_in_t] * len(w_specs)), *extra_args)

    tcol = lambda n: pl.BlockSpec((n, tm), lambda i: (0, i))
    norm_spec = _full_spec((1, HEAD_DIM))
    u8 = call(_u_proj_kernel, "proj_u", [wcols(0, ssm_w)], ssm_w, [], [],
              pl.BlockSpec((ssm_w // LANES, tm, LANES), lambda i: (0, i, 0)), sds((ssm_w // LANES, m, LANES), F32))
    q_nat, q_t = call(_q_proj_kernel, "proj_q", [wcols(ssm_w, attn_w)], attn_w,
                      [norm_spec] + tab_specs, [q_norm.reshape(1, HEAD_DIM), *hd_tabs],
                      [_row_spec(tm, attn_w), tcol(attn_w)], [sds((m, attn_w), BF16), sds((attn_w, m), BF16)])
    off_kv = ssm_w + attn_w
    off_qi = off_kv + 2 * kv_w
    off_kw = off_qi + idx_w
    lane = jnp.arange(LANES)
    kw_scale = jnp.where(lane < IDX_DIM, 1.0, jnp.where(lane < IDX_DIM + IDX_HEADS, IDX_HEADS ** -0.5, 0.0))
    nh = N_KV_HEADS
    head_main = pl.BlockSpec((nh * tm, HEAD_DIM), lambda i: (jnp.minimum(i, n_main - 1), 0))
    head_tail = pl.BlockSpec((nh * tail, HEAD_DIM), lambda i: (0, 0))
    k_p, k_s, v_p, v_s, kb, v_t, kiw, kiw_t, kib = call(
        _kv_proj_kernel, "proj_kv", [wcols(off_kv, 2 * kv_w), wcols(off_kw, LANES)], 2 * kv_w + LANES,
        [norm_spec] + tab_specs + tab_specs + [_full_spec((1, LANES))],
        [k_norm.reshape(1, HEAD_DIM), *hd_tabs, *kw_tabs, kw_scale.astype(F32).reshape(1, LANES)],
        [head_main, head_tail, head_main, head_tail, _row_spec(tm, kv_w), tcol(kv_w),
         _row_spec(tm, LANES), tcol(LANES), _row_spec(tm, IDX_DIM)],
        [sds((nh * n_main_rows, HEAD_DIM), F32), sds((nh * tail, HEAD_DIM), F32),
         sds((nh * n_main_rows, HEAD_DIM), F32), sds((nh * tail, HEAD_DIM), F32),
         sds((m, kv_w), BF16), sds((kv_w, m), BF16),
         sds((m, LANES), F32), sds((LANES, m), F32), sds((m, IDX_DIM), BF16)])
    half = idx_w // 2
    qi_nat, qi_t = call(_qi_proj_kernel, "proj_qi", [wcols(off_qi, half), wcols(off_qi + half, half)], idx_w,
                        tab_specs, ix_tabs,
                        [_row_spec(tm, idx_w), tcol(idx_w)], [sds((m, idx_w), BF16), sds((idx_w, m), BF16)])

    ng = w_g.shape[1]
    tn = 1024
    gates = pl.pallas_call(
        functools.partial(_gate_proj_kernel, chunk=tm_gates // 2), grid=(m // tm_gates, ng // tn),
        in_specs=[pl.BlockSpec((tm_gates, d), lambda i, j: (i, 0)), pl.BlockSpec((d, tn), lambda i, j: (0, j))],
        out_specs=pl.BlockSpec((tm_gates, tn), lambda i, j: (i, j)),
        out_shape=sds((m, ng), BF16), compiler_params=_cparams("parallel", "parallel"), name="proj_gates",
    )(xn, w_g)
    return dict(u8=u8, q=q_nat, q_t=q_t, k_p=k_p, k_s=k_s, kb=kb, v_p=v_p, v_s=v_s, v_t=v_t, qi=qi_nat, qi_t=qi_t,
                kiw=kiw, kiw_t=kiw_t, kib=kib, gates=gates)


def _ssm_prep_kernel(lre_ref, lim_ref, ldt_ref, bre_ref, bim_ref, cre_ref, cim_ref,
                     k_ref, win_ref, wout_ref, lp_ref):
    ns = STATES_PER_BLOCK
    lre, lim = lre_ref[...], lim_ref[...]
    dt = jnp.exp(ldt_ref[...])
    a, th = lre * dt, lim * dt

    def power(l):
        mag = jnp.exp(a * float(l))
        return mag * jnp.cos(th * float(l)), mag * jnp.sin(th * float(l))

    pw = [power(l) for l in range(SSM_T + 1)]
    xr, xi = pw[1][0] - 1.0, pw[1][1]
    den = lre * lre + lim * lim
    cr, ci = (xr * lre + xi * lim) / den, (xi * lre - xr * lim) / den
    bre, bim = bre_ref[...], bim_ref[...]
    bbr, bbi = bre * cr - bim * ci, bre * ci + bim * cr
    cre, cim = cre_ref[...], cim_ref[...]
    split = lambda a: (a.astype(BF16), (a - a.astype(BF16).astype(F32)).astype(BF16))
    c_hi, c_lo = split(jnp.concatenate([cre, cim], axis=1))
    lag = []
    for l in range(SSM_T):
        pr, pi = pw[l]
        blr, bli = bbr * pr - bbi * pi, bbr * pi + bbi * pr
        b_hi, b_lo = split(jnp.concatenate([blr, -bli], axis=1))
        m = _dot_nt(b_hi, c_hi) + _dot_nt(b_hi, c_lo) + _dot_nt(b_lo, c_hi)
        lag.append(m.astype(BF16))
        t = SSM_T - 1 - l
        win_ref[t * LANES:(t + 1) * LANES, :ns] = blr.astype(BF16)
        win_ref[t * LANES:(t + 1) * LANES, ns:] = bli.astype(BF16)
    zero = jnp.zeros((LANES, LANES), BF16)
    for t in range(SSM_T):
        pr, pi = pw[t + 1]
        wout_ref[t * LANES:(t + 1) * LANES, :ns] = (cre * pr - cim * pi).astype(BF16)
        wout_ref[t * LANES:(t + 1) * LANES, ns:] = (-(cre * pi + cim * pr)).astype(BF16)
        lp_ref[t:t + 1, :ns] = pr
        lp_ref[t:t + 1, ns:] = pi
        for t2 in range(SSM_T):
            k_ref[t * LANES:(t + 1) * LANES, t2 * LANES:(t2 + 1) * LANES] = lag[t2 - t] if t2 >= t else zero


def _block_diag_groups(w):
    g, h, p = w.shape
    nb = g // GROUPS_PER_BLOCK
    w = w.reshape(nb, GROUPS_PER_BLOCK, h, p)
    eye = jnp.eye(GROUPS_PER_BLOCK, dtype=w.dtype)
    out = w[:, :, :, None, :] * eye[None, :, None, :, None]
    return out.reshape(nb, GROUPS_PER_BLOCK * h, GROUPS_PER_BLOCK * p)


def _ssm_prep(lam_re, lam_im, b_re, b_im, c_re, c_im, log_dt):
    g, p = lam_re.shape
    nb = g // GROUPS_PER_BLOCK
    ns = STATES_PER_BLOCK
    tl = SSM_T * LANES
    vec = lambda a: a.reshape(nb, 1, ns)
    ldt = vec(jnp.broadcast_to(log_dt[:, None], (g, p)))
    bt = lambda b: _block_diag_groups(jnp.swapaxes(b, 1, 2))
    vspec = pl.BlockSpec((None, 1, ns), lambda j: (j, 0, 0))
    mspec = pl.BlockSpec((None, LANES, ns), lambda j: (j, 0, 0))
    sds = jax.ShapeDtypeStruct
    return pl.pallas_call(
        _ssm_prep_kernel, grid=(nb,),
        in_specs=[vspec, vspec, vspec, mspec, mspec, mspec, mspec],
        out_specs=[pl.BlockSpec((None, tl, tl), lambda j: (j, 0, 0)),
                   pl.BlockSpec((None, tl, 2 * ns), lambda j: (j, 0, 0)),
                   pl.BlockSpec((None, tl, 2 * ns), lambda j: (j, 0, 0)),
                   pl.BlockSpec((None, SSM_T, 2 * ns), lambda j: (j, 0, 0))],
        out_shape=[sds((nb, tl, tl), BF16), sds((nb, tl, 2 * ns), BF16), sds((nb, tl, 2 * ns), BF16),
                   sds((nb, SSM_T, 2 * ns), F32)],
        compiler_params=_cparams("parallel"), name="ssm_prep",
    )(vec(lam_re), vec(lam_im), ldt, bt(b_re), bt(b_im), _block_diag_groups(c_re), _block_diag_groups(c_im))


def _ssm_kernel(u_ref, k_ref, win_ref, wout_ref, lp_ref, d_ref, h0s_ref, za_ref, hlp_ref, hls_ref, x_scr, hs_scr,
                *, nseq, seq, n_s, t_s):
    ns = STATES_PER_BLOCK
    nk = ns // LANES
    d = d_ref[...]
    split = lambda a: [a[:, k * LANES:(k + 1) * LANES] for k in range(2 * nk)]

    def gather(row0, n_rows, t_steps):
        return [u_ref[pl.ds(row0 + t, n_rows, stride=t_steps), :] for t in range(t_steps)]

    def advance(h, x, t_steps):
        lam = [lp_ref[t_steps - 1:t_steps, k * LANES:(k + 1) * LANES] for k in range(2 * nk)]
        re = [lam[k] * h[k] - lam[nk + k] * h[nk + k] + x[k] for k in range(nk)]
        im = [lam[k] * h[nk + k] + lam[nk + k] * h[k] + x[nk + k] for k in range(nk)]
        return re + im

    def emit(row0, n_rows, t_steps, cols, hs, kmat, wout):
        u = jnp.concatenate(cols, axis=1).astype(BF16)
        y = _dot(u, kmat) + _dot_nt(hs.astype(BF16), wout)
        for t in range(t_steps):
            yt = y[:, t * LANES:(t + 1) * LANES] + d * cols[t]
            za_ref[pl.ds(row0 + t, n_rows, stride=t_steps), :] = jax.nn.gelu(yt, approximate=True)

    c_per = seq // SSM_T
    for b in range(nseq):
        u = jnp.concatenate(gather(b * seq, c_per, SSM_T), axis=1).astype(BF16)
        x = _dot(u, win_ref[...])
        for k in range(2 * nk):
            x_scr[k, b * c_per:(b + 1) * c_per, :] = x[:, k * LANES:(k + 1) * LANES]

    def step(c, h):
        rows = pl.ds(c, nseq, stride=c_per)
        for k in range(2 * nk):
            hs_scr.at[k][rows, :] = h[k]
        return tuple(advance(h, [x_scr.at[k][rows, :] for k in range(2 * nk)], SSM_T))

    h_last = lax.fori_loop(0, c_per, step, tuple(jnp.zeros((nseq, LANES), F32) for _ in range(2 * nk)),
                           unroll=4)
    hlp_ref[...] = jnp.concatenate(list(h_last), axis=1)
    for b in range(nseq):
        hs = jnp.concatenate([hs_scr[k, b * c_per:(b + 1) * c_per, :] for k in range(2 * nk)], axis=1)
        emit(b * seq, c_per, SSM_T, gather(b * seq, c_per, SSM_T), hs, k_ref[...], wout_ref[...])

    row0, tl = nseq * seq, t_s * LANES
    cols = gather(row0, n_s, t_s)
    u = jnp.concatenate(cols, axis=1).astype(BF16)
    x = _dot(u, win_ref[(SSM_T - t_s) * LANES:, :])
    h0 = h0s_ref[...]
    hls_ref[...] = jnp.concatenate(advance(split(h0), split(x), t_s), axis=1)
    emit(row0, n_s, t_s, cols, h0, k_ref[:tl, :tl], wout_ref[:tl, :])


def _ssm(u8, kmat, win, wout, lp, d8, h0s, *, nseq, seq, n_s, t_s):
    nb, m_total, _ = u8.shape
    ns = STATES_PER_BLOCK
    assert m_total == nseq * seq + n_s * t_s and seq % SSM_T == 0 and t_s <= SSM_T
    rows = nseq * seq // SSM_T
    tl = SSM_T * LANES
    sds = jax.ShapeDtypeStruct
    blk = lambda r, c: pl.BlockSpec((None, r, c), lambda j: (j, 0, 0))
    return pl.pallas_call(
        functools.partial(_ssm_kernel, nseq=nseq, seq=seq, n_s=n_s, t_s=t_s),
        grid=(nb,),
        in_specs=[blk(m_total, LANES), blk(tl, tl), blk(tl, 2 * ns), blk(tl, 2 * ns), blk(SSM_T, 2 * ns),
                  blk(1, LANES), blk(n_s, 2 * ns)],
        out_specs=[blk(m_total, LANES), blk(nseq, 2 * ns), blk(n_s, 2 * ns)],
        out_shape=[sds((nb, m_total, LANES), F32), sds((nb, nseq, 2 * ns), F32), sds((nb, n_s, 2 * ns), F32)],
        scratch_shapes=[pltpu.VMEM((2 * ns // LANES, rows, LANES), F32)] * 2,
        compiler_params=_cparams("parallel"), name="ssm",
    )(u8, kmat, win, wout, lp, d8, h0s)


def _state_to_blocks(re, im):
    n, g, p = re.shape
    nb = g // GROUPS_PER_BLOCK
    f = lambda a: jnp.transpose(a.reshape(n, nb, GROUPS_PER_BLOCK * p), (1, 0, 2))
    return jnp.concatenate([f(re), f(im)], axis=-1)


def _blocks_to_state(h):
    nb, n, w = h.shape
    ns = w // 2
    f = lambda a: jnp.transpose(a, (1, 0, 2)).reshape(n, nb * GROUPS_PER_BLOCK, ns // GROUPS_PER_BLOCK)
    return f(h[..., :ns]), f(h[..., ns:])


def _key_to_float(c):
    return lax.bitcast_convert_type(c ^ ((c >> 31) & 0x7FFFFFFF), F32)


def _kth_largest(count_ge, shape, top_k):
    def body(it, carry):
        u, cnt = carry
        bit = jnp.left_shift(jnp.int32(1), 31 - it)
        cand = u | bit
        c = count_ge(_key_to_float(cand ^ INT_MIN))
        take = c >= top_k
        return jnp.where(take, cand, u), jnp.where(take, c, cnt)
    u, cnt = lax.fori_loop(0, 32, body, (jnp.zeros(shape, jnp.int32), jnp.zeros(shape, jnp.int32)))
    return _key_to_float(u ^ INT_MIN), cnt


TIE_CHUNK = 256


def _tri_ones(n, dtype, lower):
    r = lax.broadcasted_iota(jnp.int32, (n, n), 0)
    c = lax.broadcasted_iota(jnp.int32, (n, n), 1)
    return jnp.where((r >= c) if lower else (r <= c), 1.0, 0.0).astype(dtype)


def _attn_prompt_kernel(qt_ref, kb_ref, vt_ref, qit_ref, kib_ref, wt_ref, o_ref, s_scr, bias_scr,
                        *, seq, top_k, key_chunk):
    qb = pl.program_id(1)
    tq = o_ref.shape[0]

    def run(nk):
        kib = kib_ref[:nk]
        s = jnp.zeros((nk, tq), F32)
        for h in range(IDX_HEADS):
            d = _dot(kib, qit_ref[h * IDX_DIM:(h + 1) * IDX_DIM, :])
            s = s + jnp.maximum(d, 0.0) * wt_ref[IDX_DIM + h:IDX_DIM + h + 1, :]
        kpos = lax.broadcasted_iota(jnp.int32, (nk, tq), 0)
        qpos = qb * tq + lax.broadcasted_iota(jnp.int32, (nk, tq), 1)
        allowed = kpos <= qpos
        s_scr[:nk] = jnp.where(allowed, s, -jnp.inf)

        def count_ge(t):
            n_acc = 8
            acc = [jnp.zeros((SUBLANES, tq), jnp.int32) for _ in range(n_acc)]
            for j in range(nk // SUBLANES):
                rows = s_scr[j * SUBLANES:(j + 1) * SUBLANES]
                acc[j % n_acc] = acc[j % n_acc] + (rows >= t).astype(jnp.int32)
            while len(acc) > 1:
                acc = [a + b for a, b in zip(acc[::2], acc[1::2])]
            return jnp.sum(acc[0], axis=0, keepdims=True)

        thr, cnt = _kth_largest(count_ge, (1, tq), top_k)
        few = qpos < top_k - 1
        q_row = qb * tq + lax.broadcasted_iota(jnp.int32, (1, tq), 1)
        tied = jnp.max(jnp.where((cnt > top_k) & (q_row >= top_k - 1), 1, 0)) > 0

        @pl.when(jnp.logical_not(tied))
        def _():
            bias_scr[:nk] = jnp.where(allowed & ((s_scr[:nk] >= thr) | few), 0.0, -jnp.inf)

        @pl.when(tied)
        def _():
            n_gt = jnp.sum((s_scr[:nk] > thr).astype(jnp.int32), axis=0, keepdims=True)
            need = (top_k - n_gt).astype(F32)
            tri = _tri_ones(TIE_CHUNK, BF16, lower=True)
            carry = jnp.zeros((1, tq), F32)
            for c in range(nk // TIE_CHUNK):
                rs = slice(c * TIE_CHUNK, (c + 1) * TIE_CHUNK)
                sc = s_scr[rs]
                eq = sc == thr
                rank = _dot(tri, jnp.where(eq, 1.0, 0.0).astype(BF16)) + carry
                keep = (sc > thr) | (eq & (rank <= need))
                k_row = c * TIE_CHUNK + lax.broadcasted_iota(jnp.int32, (TIE_CHUNK, tq), 0)
                keep_all = (q_row < top_k - 1) & (k_row <= q_row)
                bias_scr[rs] = jnp.where(keep | keep_all, 0.0, -jnp.inf)
                carry = rank[TIE_CHUNK - 1:, :]

        outs = []
        for g in range(N_KV_HEADS):
            kg = kb_ref[:nk, g * HEAD_DIM:(g + 1) * HEAD_DIM]
            vtg = vt_ref[g * HEAD_DIM:(g + 1) * HEAD_DIM, :nk]
            for r in range(N_REP):
                h = g * N_REP + r
                lg = _dot(kg, qt_ref[h * HEAD_DIM:(h + 1) * HEAD_DIM, :]) + bias_scr[:nk]
                p = jnp.exp(lg - jnp.max(lg, axis=0, keepdims=True))
                den = jnp.sum(p, axis=0, keepdims=True)
                outs.append(_dot(vtg, p.astype(BF16)) / den)
        o_ref[...] = jnp.concatenate(outs, axis=0).T.astype(BF16)

    n_var = seq // key_chunk
    need = (qb * tq + tq + key_chunk - 1) // key_chunk
    for v in range(1, n_var + 1):
        pl.when(need == v)(functools.partial(run, v * key_chunk))


def _attn_prompt(q_t, kb, v_t, qi_t, kib, kiw_t, *, batch, seq, m_out, tq=128, key_chunk=256):
    top_k = min(TOPK_MAX, seq // 4)
    nq = seq // tq
    assert seq % key_chunk == 0 and key_chunk % tq == 0
    aw = N_HEADS * HEAD_DIM
    qcol = lambda n: pl.BlockSpec((n, tq), lambda b, i: (0, b * nq + i))
    return pl.pallas_call(
        functools.partial(_attn_prompt_kernel, seq=seq, top_k=top_k, key_chunk=key_chunk),
        grid=(batch, nq),
        in_specs=[qcol(aw),
                  pl.BlockSpec((seq, N_KV_HEADS * HEAD_DIM), lambda b, i: (b, 0)),
                  pl.BlockSpec((N_KV_HEADS * HEAD_DIM, seq), lambda b, i: (0, b)),
                  qcol(IDX_HEADS * IDX_DIM),
                  pl.BlockSpec((seq, IDX_DIM), lambda b, i: (b, 0)),
                  qcol(LANES)],
        out_specs=pl.BlockSpec((tq, aw), lambda b, i: (b * nq + i, 0)),
        out_shape=jax.ShapeDtypeStruct((m_out, aw), BF16),
        scratch_shapes=[pltpu.VMEM((seq, tq), F32), pltpu.VMEM((seq, tq), F32)],
        compiler_params=_cparams("parallel", "parallel"), name="attn_prompt",
    )(q_t, kb, v_t, qi_t, kib, kiw_t)


def _merge_kernel(za_ref, bm_ref, bt_ref, g_ref, x_ref, gw_ref, gb_ref, wa_ref, wb_ref, wo_ref, o_ref,
                  *, n_main, tm, tail):
    d = x_ref.shape[1]

    def run(rows, b_ref):
        za = jnp.concatenate([za_ref[j, :rows] for j in range(za_ref.shape[0])], axis=1)
        a_out = za * jax.nn.sigmoid(_dot(za.astype(BF16), gw_ref[...]) + gb_ref[...])
        merged = (g_ref[:rows, :d] * _dot(a_out.astype(BF16), wa_ref[...])
                  + g_ref[:rows, d:] * _dot(b_ref[:rows], wb_ref[...]))
        o_ref[:rows] = x_ref[:rows] + _dot(merged.astype(BF16), wo_ref[...])

    i = pl.program_id(0)
    pl.when(i < n_main)(lambda: run(tm, bm_ref))
    pl.when(i == n_main)(lambda: run(tail, bt_ref))


def _merge(za8, b_main, b_tail, gates, x, glu_w, glu_b, wa, wb, wo, *, tm):
    m, d = x.shape
    nb = za8.shape[0]
    n_main_rows, aw = b_main.shape
    tail = b_tail.shape[0]
    n_main = n_main_rows // tm
    assert n_main * tm == n_main_rows and n_main_rows + tail == m and tail <= tm
    resident = lambda shape: pl.BlockSpec(shape, lambda i: (0,) * len(shape), pipeline_mode=pl.Buffered(1))
    return pl.pallas_call(
        functools.partial(_merge_kernel, n_main=n_main, tm=tm, tail=tail), grid=(n_main + 1,),
        in_specs=[pl.BlockSpec((nb, tm, LANES), lambda i: (0, i, 0)),
                  pl.BlockSpec((tm, aw), lambda i: (jnp.minimum(i, n_main - 1), 0)),
                  pl.BlockSpec((tail, aw), lambda i: (0, 0)),
                  _row_spec(tm, 2 * d), _row_spec(tm, d), resident(glu_w.shape), resident((1, glu_w.shape[1])),
                  resident(wa.shape), resident(wb.shape), resident(wo.shape)],
        out_specs=_row_spec(tm, d),
        out_shape=jax.ShapeDtypeStruct((m, d), F32),
        compiler_params=_cparams("arbitrary"), name="merge",
    )(za8, b_main, b_tail, gates, x, glu_w, glu_b.reshape(1, -1), wa, wb, wo)


QPAD = SUBLANES
CHUNK_PAGES = 8


def _page_copy(pt_ref, src_ref, buf_ref, sem, seq, slot, p, page_base, rows):
    start = pl.multiple_of((page_base + pt_ref[seq, p]) * rows, rows)
    return pltpu.make_async_copy(src_ref.at[pl.ds(start, rows), :],
                                 buf_ref.at[slot, pl.ds(p * rows, rows), :], sem)


def _prefetch_pages(n_pages, copies):
    s, n_seq = pl.program_id(0), pl.num_programs(0)
    slot = s % 2

    def start_all(seq, sl):
        def body(i, carry):
            for par in range(2):
                for j, copy in enumerate(copies):
                    copy(seq, sl, 2 * i + par).start(priority=(j + par) % 2)
            return carry
        lax.fori_loop(0, n_pages // 2, body, 0, unroll=4)

    @pl.when(s == 0)
    def _():
        start_all(0, 0)

    @pl.when(s + 1 < n_seq)
    def _():
        start_all(s + 1, 1 - slot)

    def wait_body(p, carry):
        for copy in copies:
            copy(s, slot, p).wait()
        return carry
    lax.fori_loop(0, n_pages, wait_body, 0, unroll=8)
    return slot


def _sample_scores_kernel(pt_ref, qs_ref, ws_ref, kn_ref, cik_ref, sp_ref, sn_ref, buf_ref, sem_ref,
                          *, page_base, n_pages):
    slot = _prefetch_pages(n_pages, [
        lambda seq, sl, p: _page_copy(pt_ref, cik_ref, buf_ref, sem_ref.at[sl], seq, sl, p, page_base, IDX_DIM)])
    qs, ws = qs_ref[...], ws_ref[:, :1]

    def scores(d):
        r = jnp.maximum(d, 0.0) * ws
        return r.reshape(IDX_HEADS, QPAD, r.shape[1]).sum(axis=0)

    nkc = CHUNK_PAGES * PAGE_SIZE
    for c in range(n_pages // CHUNK_PAGES):
        keys_t = jnp.concatenate(
            [buf_ref[slot, (c * CHUNK_PAGES + kk) * IDX_DIM:(c * CHUNK_PAGES + kk + 1) * IDX_DIM, :]
             for kk in range(CHUNK_PAGES)], axis=1).astype(BF16)
        sp_ref[:, c * nkc:(c + 1) * nkc] = scores(_dot(qs, keys_t))
    sn_ref[...] = scores(_dot_nt(qs, kn_ref[...]))


def _sample_thr_kernel(sp_ref, sn_ref, thr_ref, cut_ref, s_scr, *, past, top_k, t_len):
    s = jnp.concatenate([sp_ref[...], sn_ref[...]], axis=1)
    shape = s.shape
    kpos = lax.broadcasted_iota(jnp.int32, shape, 1)
    qidx = lax.broadcasted_iota(jnp.int32, shape, 0) % t_len
    s_scr[...] = jnp.where(kpos <= past + qidx, s, -jnp.inf)

    def count_ge(t):
        return jnp.sum((s_scr[...] >= t).astype(jnp.int32), axis=1, keepdims=True)

    thr, cnt = _kth_largest(count_ge, (shape[0], 1), top_k)
    thr_ref[...] = jnp.broadcast_to(thr, thr_ref.shape)
    tied = jnp.max(jnp.where(cnt > top_k, 1, 0)) > 0

    @pl.when(jnp.logical_not(tied))
    def _():
        cut_ref[...] = jnp.full(cut_ref.shape, shape[1], jnp.int32)

    @pl.when(tied)
    def _():
        n_gt = jnp.sum((s_scr[...] > thr).astype(jnp.int32), axis=1, keepdims=True)
        need = (top_k - n_gt).astype(F32)
        tri = _tri_ones(LANES, BF16, lower=False)
        carry = jnp.zeros((shape[0], 1), F32)
        cut = jnp.full((shape[0], 1), -1, jnp.int32)
        lane = lax.broadcasted_iota(jnp.int32, (shape[0], LANES), 1)
        for c in range(shape[1] // LANES):
            eq = s_scr[:, c * LANES:(c + 1) * LANES] == thr
            rank = _dot(jnp.where(eq, 1.0, 0.0).astype(BF16), tri) + carry
            last = jnp.max(jnp.where(eq & (rank <= need), lane + c * LANES, -1), axis=1, keepdims=True)
            cut = jnp.maximum(cut, last)
            carry = rank[:, LANES - 1:]
        cut_ref[...] = jnp.broadcast_to(cut, cut_ref.shape)


def _sample_attn_kernel(pt_ref, q_ref, sp_ref, sn_ref, thr_ref, cut_ref, kn_ref, vn_ref, ck_ref, cv_ref, o_ref,
                        kbuf, vbuf, sem_ref, lg_scr, *, page_base, n_pages):
    page_rows = N_KV_HEADS * PAGE_SIZE
    slot = _prefetch_pages(n_pages, [
        lambda seq, sl, p: _page_copy(pt_ref, ck_ref, kbuf, sem_ref.at[0, sl], seq, sl, p, page_base, page_rows),
        lambda seq, sl, p: _page_copy(pt_ref, cv_ref, vbuf, sem_ref.at[1, sl], seq, sl, p, page_base, page_rows)])
    rows_g = N_REP * QPAD
    nkc = CHUNK_PAGES * PAGE_SIZE
    n_chunks = n_pages // CHUNK_PAGES
    past = n_pages * PAGE_SIZE
    thr, cut = thr_ref[:, :1], cut_ref[:, :1]
    q = q_ref[...]
    qg = [q[g * rows_g:(g + 1) * rows_g] for g in range(N_KV_HEADS)]
    head_rows = lambda c, g: pl.ds(c * CHUNK_PAGES * page_rows + g, nkc, stride=N_KV_HEADS)

    def bias(scores, pos0, allowed=None):
        kpos = pos0 + lax.broadcasted_iota(jnp.int32, scores.shape, 1)
        sel = (scores > thr) | ((scores == thr) & (kpos <= cut))
        if allowed is not None:
            sel = sel & allowed
        return jnp.concatenate([jnp.where(sel, 0.0, -jnp.inf)] * N_HEADS, axis=0)

    for c in range(n_chunks):
        lg = jnp.concatenate([_dot_nt(qg[g], kbuf.at[slot][head_rows(c, g), :].astype(BF16))
                              for g in range(N_KV_HEADS)], axis=0)
        lg_scr[:, c * nkc:(c + 1) * nkc] = lg + bias(sp_ref[:, c * nkc:(c + 1) * nkc], c * nkc)
    shape = sn_ref.shape
    allowed = lax.broadcasted_iota(jnp.int32, shape, 1) <= lax.broadcasted_iota(jnp.int32, shape, 0)
    lg = jnp.concatenate([_dot_nt(qg[g], kn_ref[:, g * HEAD_DIM:(g + 1) * HEAD_DIM])
                          for g in range(N_KV_HEADS)], axis=0)
    lg_scr[:, past:] = lg + bias(sn_ref[...], past, allowed)

    lg = lg_scr[...]
    p = jnp.exp(lg - jnp.max(lg, axis=1, keepdims=True))
    den = jnp.sum(p, axis=1, keepdims=True)
    pb = p.astype(BF16)
    acc = jnp.concatenate([_dot(pb[g * rows_g:(g + 1) * rows_g, past:], vn_ref[:, g * HEAD_DIM:(g + 1) * HEAD_DIM])
                           for g in range(N_KV_HEADS)], axis=0)
    for c in range(n_chunks):
        acc = acc + jnp.concatenate(
            [_dot(pb[g * rows_g:(g + 1) * rows_g, c * nkc:(c + 1) * nkc],
                  vbuf.at[slot][head_rows(c, g), :].astype(BF16)) for g in range(N_KV_HEADS)], axis=0)
    o_ref[...] = acc / den


def _pad_queries(x, t_len):
    n, h, d = x.shape
    x = jnp.transpose(x.reshape(n // t_len, t_len, h, d), (0, 2, 1, 3))
    x = jnp.pad(x, ((0, 0), (0, 0), (0, QPAD - t_len), (0, 0)))
    return x.reshape(n // t_len, h * QPAD, d)


def _pad_new_keys(x, t_len):
    n, w = x.shape
    return jnp.pad(x.reshape(n // t_len, t_len, w), ((0, 0), (0, PAGE_SIZE - t_len), (0, 0)))


def _attn_sample(q_s, k_new, v_new, qi_s, wi_s, ki_new, cache_k, cache_v, cache_idx_k, page_table, page_base,
                 *, t_len):
    db, n_pages = page_table.shape
    past = n_pages * PAGE_SIZE
    top_k = min(TOPK_MAX, (past + t_len) // 4)
    kvw = N_KV_HEADS * HEAD_DIM
    sds = jax.ShapeDtypeStruct
    q = _pad_queries(q_s.reshape(-1, N_HEADS, HEAD_DIM), t_len)
    qi = _pad_queries(qi_s.reshape(-1, IDX_HEADS, IDX_DIM), t_len)
    ws = _pad_queries(wi_s.reshape(-1, IDX_HEADS, 1), t_len)
    ws = jnp.broadcast_to(ws, ws.shape[:2] + (PAGE_SIZE,))
    kn, vn, kin = (_pad_new_keys(a, t_len) for a in (k_new, v_new, ki_new))

    assert n_pages % CHUNK_PAGES == 0
    seq_spec = lambda r, c: pl.BlockSpec((None, r, c), lambda s, pt: (s, 0, 0))
    hbm = pl.BlockSpec(memory_space=pl.ANY)
    dma = pltpu.SemaphoreType.DMA
    in_order = _cparams("arbitrary")
    sp, sn = pl.pallas_call(
        functools.partial(_sample_scores_kernel, page_base=page_base, n_pages=n_pages),
        grid_spec=pltpu.PrefetchScalarGridSpec(
            num_scalar_prefetch=1, grid=(db,),
            in_specs=[seq_spec(IDX_HEADS * QPAD, IDX_DIM), seq_spec(IDX_HEADS * QPAD, PAGE_SIZE),
                      seq_spec(PAGE_SIZE, IDX_DIM), hbm],
            out_specs=[seq_spec(QPAD, past), seq_spec(QPAD, PAGE_SIZE)],
            scratch_shapes=[pltpu.VMEM((2, n_pages * IDX_DIM, PAGE_SIZE), F32), dma((2,))]),
        out_shape=[sds((db, QPAD, past), F32), sds((db, QPAD, PAGE_SIZE), F32)],
        compiler_params=in_order, name="sample_scores",
    )(page_table, qi, ws, kin, cache_idx_k.reshape(-1, PAGE_SIZE))

    rows = db * t_len
    rblk = min(rows, 8 * SUBLANES)
    assert rows % rblk == 0
    real = lambda a: a[:, :t_len].reshape(rows, a.shape[2])
    padded = lambda a: jnp.pad(a.reshape(db, t_len, LANES), ((0, 0), (0, QPAD - t_len), (0, 0)))
    thr, cut = pl.pallas_call(
        functools.partial(_sample_thr_kernel, past=past, top_k=top_k, t_len=t_len),
        grid=(rows // rblk,),
        in_specs=[_row_spec(rblk, past), _row_spec(rblk, PAGE_SIZE)],
        out_specs=[_row_spec(rblk, LANES), _row_spec(rblk, LANES)],
        out_shape=[sds((rows, LANES), F32), sds((rows, LANES), jnp.int32)],
        scratch_shapes=[pltpu.VMEM((rblk, past + PAGE_SIZE), F32)],
        compiler_params=_cparams("parallel"), name="sample_threshold",
    )(real(sp), real(sn))

    hq = N_HEADS * QPAD
    page_rows = N_KV_HEADS * PAGE_SIZE
    out = pl.pallas_call(
        functools.partial(_sample_attn_kernel, page_base=page_base, n_pages=n_pages),
        grid_spec=pltpu.PrefetchScalarGridSpec(
            num_scalar_prefetch=1, grid=(db,),
            in_specs=[seq_spec(hq, HEAD_DIM), seq_spec(QPAD, past), seq_spec(QPAD, PAGE_SIZE), seq_spec(QPAD, LANES),
                      seq_spec(QPAD, LANES), seq_spec(PAGE_SIZE, kvw), seq_spec(PAGE_SIZE, kvw), hbm, hbm],
            out_specs=seq_spec(hq, HEAD_DIM),
            scratch_shapes=[pltpu.VMEM((2, n_pages * page_rows, HEAD_DIM), F32),
                            pltpu.VMEM((2, n_pages * page_rows, HEAD_DIM), F32),
                            dma((2, 2)), pltpu.VMEM((hq, past + PAGE_SIZE), F32)]),
        out_shape=sds((db, hq, HEAD_DIM), F32),
        compiler_params=in_order, name="sample_attn",
    )(page_table, q, sp, sn, padded(thr), padded(cut), kn, vn, cache_k, cache_v)
    out = out.reshape(db, N_HEADS, QPAD, HEAD_DIM)[:, :, :t_len]
    return jnp.transpose(out, (0, 2, 1, 3)).reshape(db * t_len, N_HEADS * HEAD_DIM)


GATE_ROW_TILE = 640
PROJ_ROW_TILE = 512
FFN_ROW_TILE = 512
FF_TILE = 512
MERGE_ROW_TILE = 256


def _layer(x_p, x_s, pos, dims, layer, w_in_all, ck, cv, cik, page_base, s_re, s_im, page_table, p):
    b, s, db, t = dims
    mp, ms = b * s, db * t
    d = x_p.shape[1]
    bf = lambda w: w.astype(BF16)
    ssm_w = d // 2
    w_g = bf(w_in_all[layer, :, w_in_all.shape[2] - 2 * d:])

    x1, xn = _ffn(x_p, x_s, p["ffn1_norm"], bf(p["ffn1_w_gate"]), bf(p["ffn1_w_up"]), bf(p["ffn1_w_down"]),
                  p["mix_norm"], n_main_rows=mp, tail=ms, split_out=False, tm=FFN_ROW_TILE, tf=FF_TILE)
    pr = _projections(xn, jnp.swapaxes(w_in_all, 1, 2), layer, w_g, p["q_norm"], p["k_norm"], pos,
                      n_main_rows=mp, seq=s, tail=ms, tm=PROJ_ROW_TILE, tm_gates=GATE_ROW_TILE)

    kmat, win, wout, lp = _ssm_prep(p["ssm_lambda_re"], p["ssm_lambda_im"], p["ssm_b_re"], p["ssm_b_im"],
                                    p["ssm_c_re"], p["ssm_c_im"], p["ssm_log_dt"])
    nblk = ssm_w // LANES
    d8 = p["ssm_d"].reshape(nblk, 1, LANES)
    za8, hl_p, hl_s = _ssm(pr["u8"], kmat, win, wout, lp, d8, _state_to_blocks(s_re, s_im),
                           nseq=b, seq=s, n_s=db, t_s=t)

    b_p = _attn_prompt(pr["q_t"], pr["kb"], pr["v_t"], pr["qi_t"], pr["kib"], pr["kiw_t"],
                       batch=b, seq=s, m_out=mp)
    b_s = _attn_sample(pr["q"][mp:], pr["kb"][mp:], bf(pr["v_s"].reshape(ms, -1)), pr["qi"][mp:],
                       pr["kiw"][mp:, IDX_DIM:IDX_DIM + IDX_HEADS], pr["kib"][mp:],
                       ck, cv, cik, page_table, page_base, t_len=t)

    x2 = _merge(za8, b_p, bf(b_s), pr["gates"], x1, bf(p["glu_w"]), p["glu_b"], bf(p["w_branch_a"]),
                bf(p["w_branch_b"]), bf(p["w_out"]), tm=MERGE_ROW_TILE)
    y_p, y_s = _ffn(x2, None, p["ffn2_norm"], bf(p["ffn2_w_gate"]), bf(p["ffn2_w_up"]), bf(p["ffn2_w_down"]),
                    n_main_rows=mp, tail=ms, split_out=True, tm=FFN_ROW_TILE, tf=FF_TILE)

    hp_re, hp_im = _blocks_to_state(hl_p)
    hs_re, hs_im = _blocks_to_state(hl_s)
    ki = pr["kiw"][:, :IDX_DIM]
    kvs = (N_KV_HEADS, HEAD_DIM)
    rows = (pr["k_p"].reshape(b, s, *kvs), pr["v_p"].reshape(b, s, *kvs), ki[:mp].reshape(b, s, IDX_DIM),
            hp_re, hp_im,
            pr["k_s"].reshape(db, t, *kvs), pr["v_s"].reshape(db, t, *kvs), ki[mp:].reshape(db, t, IDX_DIM),
            hs_re, hs_im)
    return y_p, y_s, rows


def kernel(x_prompt, x_sample, cache_k, cache_v, cache_idx_k, state_ssm_re, state_ssm_im, page_table,
           ffn1_norm, ffn1_w_gate, ffn1_w_up, ffn1_w_down, mix_norm, w_in, q_norm, k_norm,
           ssm_lambda_re, ssm_lambda_im, ssm_b_re, ssm_b_im, ssm_c_re, ssm_c_im, ssm_d, ssm_log_dt,
           glu_w, glu_b, w_branch_a, w_branch_b, w_out, ffn2_norm, ffn2_w_gate, ffn2_w_up, ffn2_w_down):
    b, s, d = x_prompt.shape
    db, t, _ = x_sample.shape
    depth, n_phys = cache_k.shape[:2]
    past = page_table.shape[1] * PAGE_SIZE
    x_p, x_s = x_prompt.reshape(b * s, d), x_sample.reshape(db * t, d)
    pos = jnp.concatenate([jnp.arange(s), jnp.tile(past + jnp.arange(t), db)])
    ck = cache_k.reshape(-1, HEAD_DIM)
    cv = cache_v.reshape(-1, HEAD_DIM)
    cik = jnp.swapaxes(cache_idx_k, 2, 3).reshape(depth * n_phys, IDX_DIM, PAGE_SIZE)
    params = dict(
        ffn1_norm=ffn1_norm, ffn1_w_gate=ffn1_w_gate, ffn1_w_up=ffn1_w_up, ffn1_w_down=ffn1_w_down,
        mix_norm=mix_norm, w_in=w_in, q_norm=q_norm, k_norm=k_norm,
        ssm_lambda_re=ssm_lambda_re, ssm_lambda_im=ssm_lambda_im, ssm_b_re=ssm_b_re, ssm_b_im=ssm_b_im,
        ssm_c_re=ssm_c_re, ssm_c_im=ssm_c_im, ssm_d=ssm_d, ssm_log_dt=ssm_log_dt, glu_w=glu_w, glu_b=glu_b,
        w_branch_a=w_branch_a, w_branch_b=w_branch_b, w_out=w_out,
        ffn2_norm=ffn2_norm, ffn2_w_gate=ffn2_w_gate, ffn2_w_up=ffn2_w_up, ffn2_w_down=ffn2_w_down)
    new = [[] for _ in range(10)]
    for l in range(depth):
        p = {name: w[l] for name, w in params.items()}
        x_p, x_s, rows = _layer(x_p, x_s, pos, (b, s, db, t), l, w_in, ck, cv, cik, l * n_phys,
                                state_ssm_re[l], state_ssm_im[l], page_table, p)
        for lst, r in zip(new, rows):
            lst.append(r)
    return (x_p.reshape(b, s, d), x_s.reshape(db, t, d)) + tuple(jnp.stack(lst) for lst in new)
```
